```python
import math
import jax
import jax.numpy as jnp
from jax import lax
import numpy as np

D_MODEL = 1024
BATCH = 4
SEQ = 4096
DEPTH = 1

GRID_W = 64
CTX_LEN = 256
D_MIX = D_MODEL
EPS = 1e-6
CONV_W = 5
NEG_BIG = -1e30

SSD_WIDTH = D_MIX // 2
SSD_HEADDIM = 64
SSD_HEADS = SSD_WIDTH // SSD_HEADDIM
SSD_GROUPS = 2
SSD_STATE = 128
SSD_CHUNK = 128
SSD_BC = SSD_GROUPS * SSD_STATE
SSD_CONV_DIM = SSD_WIDTH + 2 * SSD_BC
SSD_IN = SSD_WIDTH + SSD_CONV_DIM + 2 * SSD_HEADS

ML_WIDTH = D_MIX - SSD_WIDTH
ML_HEADS = 4
ML_HEADDIM = ML_WIDTH // ML_HEADS
ML_QK_BLOCK = 4
ML_CHUNK = 128
ML_IN = 3 * ML_WIDTH + 4 * ML_HEADS

D_IN = SSD_IN + ML_IN

MOE_GROUPS = 4
MOE_EXPERTS_PER_GROUP = 8
MOE_EXPERTS = MOE_GROUPS * MOE_EXPERTS_PER_GROUP
MOE_TOP_K = 2
D_EXPERT = D_MODEL // 4

kernel_name = "hymba_ssd_mlstm_hmoe_dit_block"


def rms_norm(t, w):
    tf = t.astype(jnp.float32)
    y = tf * lax.rsqrt(jnp.mean(tf * tf, axis=-1, keepdims=True) + EPS)
    return (y * w.astype(jnp.float32)).astype(t.dtype)


def modulate(t, shift, scale):
    return t * (1 + scale) + shift


def dwconv_centred(t, w, bias):
    pad = CONV_W // 2
    length = t.shape[1]
    tp = jnp.pad(t, ((0, 0), (pad, pad), (0, 0)))
    out = bias
    for tap in range(CONV_W):
        out = out + w[tap] * tp[:, tap:tap + length]
    return out


def to_col_major(t, rows):
    b, length, ch = t.shape
    return t.reshape(b, rows, GRID_W, ch).transpose(0, 2, 1, 3).reshape(b, length, ch)


def from_col_major(t, rows):
    b, length, ch = t.shape
    return t.reshape(b, GRID_W, rows, ch).transpose(0, 2, 1, 3).reshape(b, length, ch)


def segsum(a):
    size = a.shape[-1]
    cs = jnp.cumsum(a, axis=-1)
    mask = jnp.tril(jnp.ones((size, size), dtype=bool))
    return jnp.where(mask, cs[..., :, None] - cs[..., None, :], -jnp.inf)


def ssd_scan(xs, dt, a_neg, bm, cm, init_state):
    b, length, nh, hp = xs.shape
    nc = length // SSD_CHUNK
    rep = nh // SSD_GROUPS
    bh = jnp.repeat(bm, rep, axis=2)
    ch = jnp.repeat(cm, rep, axis=2)

    def chunk(t):
        return t.reshape((b, nc, SSD_CHUNK) + t.shape[2:])

    xc, bc, cc = chunk(xs * dt[..., None]), chunk(bh), chunk(ch)
    ac = chunk(dt * a_neg).transpose(0, 3, 1, 2)
    a_cs = jnp.cumsum(ac, axis=-1)
    l_mat = jnp.exp(segsum(ac))
    y_diag = jnp.einsum("bclhn,bcshn,bhcls,bcshp->bclhp", cc, bc, l_mat, xc)
    decay_states = jnp.exp(a_cs[..., -1:] - a_cs)
    states = jnp.einsum("bclhn,bhcl,bclhp->bchpn", bc, decay_states, xc)
    states = jnp.concatenate([init_state[:, None], states], axis=1)
    chunk_decay = jnp.exp(segsum(jnp.pad(a_cs[..., -1], ((0, 0), (0, 0), (1, 0)))))
    new_states = jnp.einsum("bhzc,bchpn->bzhpn", chunk_decay, states)
    prev_states, final_state = new_states[:, :-1], new_states[:, -1]
    y_off = jnp.einsum("bclhn,bchpn,bhcl->bclhp", cc, prev_states, jnp.exp(a_cs))
    return (y_diag + y_off).reshape(b, length, nh, hp), final_state


def ssd_group(p_lat, p_ctx, conv_w, conv_b, dt_bias, a_log, d_skip, norm_w):
    a_neg = -jnp.exp(a_log.astype(jnp.float32))

    def prep(p):
        b, length, _ = p.shape
        z = p[..., :SSD_WIDTH]
        xbc = jax.nn.silu(dwconv_centred(p[..., SSD_WIDTH:SSD_WIDTH + SSD_CONV_DIM], conv_w, conv_b)).astype(jnp.float32)
        dt_raw = p[..., SSD_WIDTH + SSD_CONV_DIM:].astype(jnp.float32).reshape(b, length, 2, SSD_HEADS)
        dt = jax.nn.softplus(dt_raw + dt_bias.astype(jnp.float32))
        xs = xbc[..., :SSD_WIDTH].reshape(b, length, SSD_HEADS, SSD_HEADDIM)
        bm = xbc[..., SSD_WIDTH:SSD_WIDTH + SSD_BC].reshape(b, length, SSD_GROUPS, SSD_STATE)
        cm = xbc[..., SSD_WIDTH + SSD_BC:].reshape(b, length, SSD_GROUPS, SSD_STATE)
        return z, xs, bm, cm, dt

    zl, xl, bl, cl, dtl = prep(p_lat)
    zc, xc, bc, cc, dtc = prep(p_ctx)
    s0 = jnp.zeros((xl.shape[0], SSD_HEADS, SSD_HEADDIM, SSD_STATE), jnp.float32)
    yc_f, s_f = ssd_scan(xc, dtc[:, :, 0], a_neg[0], bc, cc, s0)
    yl_f, _ = ssd_scan(xl, dtl[:, :, 0], a_neg[0], bl, cl, s_f)
    yc_b, s_b = ssd_scan(xc[:, ::-1], dtc[:, ::-1, 1], a_neg[1], bc[:, ::-1], cc[:, ::-1], s0)
    yl_b, _ = ssd_scan(xl[:, ::-1], dtl[:, ::-1, 1], a_neg[1], bl[:, ::-1], cl[:, ::-1], s_b)

    def finish(yf, yb, xs, z):
        b, length = xs.shape[:2]
        y = yf + yb[:, ::-1] + d_skip.astype(jnp.float32)[:, None] * xs
        y = y.reshape(b, length, SSD_WIDTH) * jax.nn.silu(z.astype(jnp.float32))
        return rms_norm(y, norm_w)

    return finish(yl_f, yl_b, xl, zl), finish(yc_f, yc_b, xc, zc)


def mlstm_scan(q, k, v, log_i, log_f, state):
    b, nh, length, dh = q.shape
    nc = length // ML_CHUNK

    def chunk(t):
        return t.reshape((b, nh, nc, ML_CHUNK) + t.shape[3:])

    qc, kc, vc = chunk(q), chunk(k), chunk(v)
    ic, fc = chunk(log_i), chunk(log_f)
    bcum = jnp.cumsum(fc, axis=-1)
    b_last = bcum[..., -1]
    a = b_last[..., None] - bcum + ic
    m_loc = jnp.max(a, axis=-1)
    wgt = jnp.exp(a - m_loc[..., None])
    c_loc = jnp.einsum("bhcs,bhcsd,bhcse->bhcde", wgt, vc, kc)
    n_loc = jnp.einsum("bhcs,bhcse->bhce", wgt, kc)

    def step(carry, inp):
        c_st, n_st, m_st = carry
        c_l, n_l, m_l, b_l = inp
        m_new = jnp.maximum(b_l + m_st, m_l)
        s_prev = jnp.exp(b_l + m_st - m_new)
        s_loc = jnp.exp(m_l - m_new)
        c_new = s_prev[..., None, None] * c_st + s_loc[..., None, None] * c_l
        n_new = s_prev[..., None] * n_st + s_loc[..., None] * n_l
        return (c_new, n_new, m_new), (c_st, n_st, m_st)

    final, prev = lax.scan(step, state, (jnp.moveaxis(c_loc, 2, 0), jnp.moveaxis(n_loc, 2, 0), jnp.moveaxis(m_loc, 2, 0), jnp.moveaxis(b_last, 2, 0)))
    c_prev = jnp.moveaxis(prev[0], 0, 2)
    n_prev = jnp.moveaxis(prev[1], 0, 2)
    m_prev = jnp.moveaxis(prev[2], 0, 2)
    mask = jnp.tril(jnp.ones((ML_CHUNK, ML_CHUNK), dtype=bool))
    d_log = jnp.where(mask, bcum[..., :, None] - bcum[..., None, :] + ic[..., None, :], -jnp.inf)
    g_inter = bcum + m_prev[..., None]
    m_out = jnp.maximum(g_inter, jnp.max(d_log, axis=-1))
    scores = jnp.einsum("bhctd,bhcsd->bhcts", qc, kc) * jnp.exp(d_log - m_out[..., None])
    w_inter = jnp.exp(g_inter - m_out)
    num = jnp.einsum("bhcts,bhcsd->bhctd", scores, vc) + w_inter[..., None] * jnp.einsum("bhcde,bhcte->bhctd", c_prev, qc)
    den = jnp.sum(scores, axis=-1) + w_inter * jnp.einsum("bhce,bhcte->bhct", n_prev, qc)
    h = num / jnp.maximum(jnp.abs(den), jnp.exp(-m_out))[..., None]
    return h.reshape(b, nh, length, dh), final


def mlstm_group(p_lat, p_ctx, conv_w, conv_b, w_qk, gate_b, norm_w, skip):
    def prep(p):
        b, length, _ = p.shape
        x_m = p[..., :ML_WIDTH]
        v = p[..., ML_WIDTH:2 * ML_WIDTH]
        o = p[..., 2 * ML_WIDTH:3 * ML_WIDTH]
        g = p[..., 3 * ML_WIDTH:]
        xconv = jax.nn.silu(dwconv_centred(x_m, conv_w, conv_b)).astype(jnp.float32)
        blocks = xconv.reshape(b, length, ML_WIDTH // ML_QK_BLOCK, ML_QK_BLOCK)
        qk = jnp.einsum("blnj,snjk->sblnk", blocks, w_qk.astype(jnp.float32))

        def heads(t):
            return t.reshape(b, length, ML_HEADS, ML_HEADDIM).transpose(0, 2, 1, 3)

        q = heads(qk[0])
        k = heads(qk[1]) * (ML_HEADDIM ** -0.5)
        vh = heads(v.astype(jnp.float32))
        gates = (g.astype(jnp.float32).reshape(b, length, 2, 2, ML_HEADS) + gate_b.astype(jnp.float32)).transpose(2, 3, 0, 4, 1)
        log_i = gates[0]
        log_f = jax.nn.log_sigmoid(gates[1])
        return xconv, o, q, k, vh, log_i, log_f

    xl, ol, ql, kl, vl, il, fl = prep(p_lat)
    xc, oc, qc, kc, vc, ic, fc = prep(p_ctx)
    b = ql.shape[0]
    state0 = (jnp.zeros((b, ML_HEADS, ML_HEADDIM, ML_HEADDIM), jnp.float32), jnp.zeros((b, ML_HEADS, ML_HEADDIM), jnp.float32), jnp.full((b, ML_HEADS), NEG_BIG, jnp.float32))
    hc_f, st_f = mlstm_scan(qc, kc, vc, ic[0], fc[0], state0)
    hl_f, _ = mlstm_scan(ql, kl, vl, il[0], fl[0], st_f)
    hc_b, st_b = mlstm_scan(qc[:, :, ::-1], kc[:, :, ::-1], vc[:, :, ::-1], ic[1][..., ::-1], fc[1][..., ::-1], state0)
    hl_b, _ = mlstm_scan(ql[:, :, ::-1], kl[:, :, ::-1], vl[:, :, ::-1], il[1][..., ::-1], fl[1][..., ::-1], st_b)

    def finish(hf, hb, xconv, o):
        bb, length = xconv.shape[:2]
        h = (hf + hb[:, :, ::-1]).transpose(0, 2, 1, 3)
        h = jax.nn.sigmoid(o.astype(jnp.float32)).reshape(bb, length, ML_HEADS, ML_HEADDIM) * h
        h = h * lax.rsqrt(jnp.mean(h * h, axis=-1, keepdims=True) + EPS)
        return h.reshape(bb, length, ML_WIDTH) * norm_w.astype(jnp.float32) + skip.astype(jnp.float32) * xconv

    return finish(hl_f, hl_b, xl, ol), finish(hc_f, hc_b, xc, oc)


def hmoe(t_in, rg_w, rg_b, re_w, re_b, w_gate, w_up, w_down):
    b, length, d = t_in.shape
    t = t_in.reshape(b * length, d)
    g_logits = (t @ rg_w + rg_b).astype(jnp.float32)
    g_prob = jax.nn.softmax(g_logits, axis=-1)
    g_sel = jnp.argmax(g_logits, axis=-1)
    p_group = jnp.take_along_axis(g_prob, g_sel[:, None], axis=-1)
    e_logits = (t @ re_w + re_b).astype(jnp.float32).reshape(-1, MOE_GROUPS, MOE_EXPERTS_PER_GROUP)
    e_logits = jnp.take_along_axis(e_logits, g_sel[:, None, None], axis=1)[:, 0]
    top_logit, top_idx = lax.top_k(e_logits, MOE_TOP_K)
    top_w = jax.nn.softmax(top_logit, axis=-1) * p_group
    w_in_group = jnp.einsum("tk,tke->te", top_w, jax.nn.one_hot(top_idx, MOE_EXPERTS_PER_GROUP, dtype=jnp.float32))
    combine = jax.nn.one_hot(g_sel, MOE_GROUPS, dtype=jnp.float32)[:, :, None] * w_in_group[:, None, :]
    y = jnp.zeros_like(t)
    for grp in range(MOE_GROUPS):
        lo = grp * MOE_EXPERTS_PER_GROUP
        hi = lo + MOE_EXPERTS_PER_GROUP
        hidden = jax.nn.silu(jnp.einsum("td,edf->tef", t, w_gate[lo:hi])) * jnp.einsum("td,edf->tef", t, w_up[lo:hi])
        hidden = hidden * combine[:, grp, :, None].astype(hidden.dtype)
        y = y + jnp.einsum("tef,efd->td", hidden, w_down[lo:hi])
    return y.reshape(b, length, d)


def setup_inputs(seed: int = 0) -> dict:
    key = jax.random.key(seed)
    ks = iter(jax.random.split(key, 40))
    f32 = jnp.float32

    def nrm(shape, scale):
        return scale * jax.random.normal(next(ks), shape, f32)

    def gain(shape):
        return 1.0 + 0.02 * jax.random.normal(next(ks), shape, f32)

    nl = DEPTH
    x = nrm((BATCH, SEQ, D_MODEL), 1.0)
    c = nrm((BATCH, D_MODEL), 1.0)
    ctx = nrm((BATCH, CTX_LEN, D_MODEL), 1.0)
    c_ctx = nrm((D_MODEL,), 1.0)
    w_mod = nrm((nl, D_MODEL, 6 * D_MODEL), 0.5 * D_MODEL ** -0.5)
    b_mod = nrm((nl, 6 * D_MODEL), 0.01)
    norm1_w = gain((nl, D_MODEL))
    w_in = nrm((nl, D_MODEL, D_IN), D_MODEL ** -0.5)
    ssd_conv_w = nrm((nl, CONV_W, SSD_CONV_DIM), CONV_W ** -0.5)
    ssd_conv_b = nrm((nl, SSD_CONV_DIM), 0.01)
    dt0 = jnp.exp(jax.random.uniform(next(ks), (nl, 2, SSD_HEADS), f32, math.log(1e-3), math.log(1e-1)))
    ssd_dt_bias = dt0 + jnp.log(-jnp.expm1(-dt0))
    ssd_a_log = jnp.log(jax.random.uniform(next(ks), (nl, 2, SSD_HEADS), f32, 1.0, 16.0))
    ssd_d = gain((nl, SSD_HEADS))
    ssd_norm_w = gain((nl, SSD_WIDTH))
    ml_conv_w = nrm((nl, CONV_W, ML_WIDTH), CONV_W ** -0.5)
    ml_conv_b = nrm((nl, ML_WIDTH), 0.01)
    ml_w_qk = nrm((nl, 2, ML_WIDTH // ML_QK_BLOCK, ML_QK_BLOCK, ML_QK_BLOCK), ML_QK_BLOCK ** -0.5)
    ml_gate_b = jnp.stack([nrm((nl, 2, ML_HEADS), 0.1), 3.0 + nrm((nl, 2, ML_HEADS), 0.5)], axis=1)
    ml_norm_w = gain((nl, ML_WIDTH))
    ml_skip = gain((nl, ML_WIDTH))
    w_out = nrm((nl, D_MIX, D_MODEL), D_MIX ** -0.5)
    norm2_w = gain((nl, D_MODEL))
    moe_rg_w = nrm((nl, D_MODEL, MOE_GROUPS), D_MODEL ** -0.5)
    moe_rg_b = nrm((nl, MOE_GROUPS), 0.01)
    moe_re_w = nrm((nl, D_MODEL, MOE_EXPERTS), D_MODEL ** -0.5)
    moe_re_b = nrm((nl, MOE_EXPERTS), 0.01)
    moe_w_gate = nrm((nl, MOE_EXPERTS, D_MODEL, D_EXPERT), D_MODEL ** -0.5)
    moe_w_up = nrm((nl, MOE_EXPERTS, D_MODEL, D_EXPERT), D_MODEL ** -0.5)
    moe_w_down = nrm((nl, MOE_EXPERTS, D_EXPERT, D_MODEL), D_EXPERT ** -0.5)
    final_norm_w = gain((D_MODEL,))
    return {"x": x, "c": c, "ctx": ctx, "c_ctx": c_ctx, "w_mod": w_mod, "b_mod": b_mod, "norm1_w": norm1_w, "w_in": w_in, "ssd_conv_w": ssd_conv_w, "ssd_conv_b": ssd_conv_b, "ssd_dt_bias": ssd_dt_bias, "ssd_a_log": ssd_a_log, "ssd_d": ssd_d, "ssd_norm_w": ssd_norm_w, "ml_conv_w": ml_conv_w, "ml_conv_b": ml_conv_b, "ml_w_qk": ml_w_qk, "ml_gate_b": ml_gate_b, "ml_norm_w": ml_norm_w, "ml_skip": ml_skip, "w_out": w_out, "norm2_w": norm2_w, "moe_rg_w": moe_rg_w, "moe_rg_b": moe_rg_b, "moe_re_w": moe_re_w, "moe_re_b": moe_re_b, "moe_w_gate": moe_w_gate, "moe_w_up": moe_w_up, "moe_w_down": moe_w_down, "final_norm_w": final_norm_w}


def reference(x, c, ctx, c_ctx, w_mod, b_mod, norm1_w, w_in, ssd_conv_w, ssd_conv_b, ssd_dt_bias, ssd_a_log, ssd_d, ssd_norm_w, ml_conv_w, ml_conv_b, ml_w_qk, ml_gate_b, ml_norm_w, ml_skip, w_out, norm2_w, moe_rg_w, moe_rg_b, moe_re_w, moe_re_b, moe_w_gate, moe_w_up, moe_w_down, final_norm_w):
    rows = x.shape[1] // GRID_W
    for layer in range(DEPTH):
        mod_lat = (jax.nn.silu(c) @ w_mod[layer] + b_mod[layer])[:, None, :]
        mod_ctx = jax.nn.silu(c_ctx) @ w_mod[layer] + b_mod[layer]
        sh1, sc1, g1, sh2, sc2, g2 = jnp.split(mod_lat, 6, axis=-1)
        csh1, csc1, cg1, csh2, csc2, cg2 = jnp.split(mod_ctx, 6, axis=-1)
        h_lat = modulate(rms_norm(x, norm1_w[layer]), sh1, sc1)
        h_ctx = modulate(rms_norm(ctx, norm1_w[layer]), csh1, csc1)
        p_lat = h_lat @ w_in[layer]
        p_ctx = h_ctx @ w_in[layer]
        y_ssd_lat, y_ssd_ctx = ssd_group(p_lat[..., :SSD_IN], p_ctx[..., :SSD_IN], ssd_conv_w[layer], ssd_conv_b[layer], ssd_dt_bias[layer], ssd_a_log[layer], ssd_d[layer], ssd_norm_w[layer])
        y_ml_lat, y_ml_ctx = mlstm_group(to_col_major(p_lat[..., SSD_IN:], rows), p_ctx[..., SSD_IN:], ml_conv_w[layer], ml_conv_b[layer], ml_w_qk[layer], ml_gate_b[layer], ml_norm_w[layer], ml_skip[layer])
        y_lat = jnp.concatenate([y_ssd_lat, from_col_major(y_ml_lat, rows)], axis=-1).astype(x.dtype)
        x = x + g1 * (y_lat @ w_out[layer])
        x = x + g2 * hmoe(modulate(rms_norm(x, norm2_w[layer]), sh2, sc2), moe_rg_w[layer], moe_rg_b[layer], moe_re_w[layer], moe_re_b[layer], moe_w_gate[layer], moe_w_up[layer], moe_w_down[layer])
        if layer < DEPTH - 1:
            y_ctx = jnp.concatenate([y_ssd_ctx, y_ml_ctx], axis=-1).astype(ctx.dtype)
            ctx = ctx + cg1 * (y_ctx @ w_out[layer])
            ctx = ctx + cg2 * hmoe(modulate(rms_norm(ctx, norm2_w[layer]), csh2, csc2), moe_rg_w[layer], moe_rg_b[layer], moe_re_w[layer], moe_re_b[layer], moe_w_gate[layer], moe_w_up[layer], moe_w_down[layer])
    return rms_norm(x, final_norm_w)
```

```python
import functools

import jax
import jax.numpy as jnp
from jax import lax
from jax.experimental import pallas as pl
from jax.experimental.pallas import tpu as pltpu

F32 = jnp.float32
BF16 = jnp.bfloat16
HIGHEST = lax.Precision.HIGHEST

D_MODEL = 1024
GRID_W = 64
EPS = 1e-6
CONV_W = 5
NEG_BIG = -1e30
CHUNK = 128
LANES = 128
HALO = 16

SSD_WIDTH = 512
SSD_HEADS = 8
SSD_HEADDIM = 64
SSD_GROUPS = 2
SSD_STATE = 128
SSD_XBC = SSD_WIDTH + 2 * SSD_GROUPS * SSD_STATE

ML_WIDTH = 512
ML_HEADS = 4
ML_HEADDIM = 128
ML_QK_BLOCK = 4

MOE_GROUPS = 4
MOE_EPG = 8
MOE_EXPERTS = 32
D_EXPERT = 256
ROUTE_LANE0 = MOE_GROUPS

TM_IN = 256
TM_MOE = 1024
VMEM_LIMIT = 48 * 1024 * 1024


def _silu(v):
    return v * jax.nn.sigmoid(v)


def _softplus(v):
    return jnp.maximum(v, 0.0) + jnp.log1p(jnp.exp(-jnp.abs(v)))


def _dot(a, b):
    return jnp.dot(a, b, preferred_element_type=F32)


def _dot_nt(a, b):
    return lax.dot_general(a, b, (((1,), (1,)), ((), ())), preferred_element_type=F32)


def _dot_hi(a, b):
    return jnp.dot(a, b, preferred_element_type=F32, precision=HIGHEST)


def _mod_kernel(c_ref, w_ref, b_ref, o_ref):
    c = c_ref[...]
    o_ref[...] = _dot_hi(_silu(c), w_ref[...]) + b_ref[...]


def _modulation(c_all, w_mod, b_mod):
    n = w_mod.shape[1]
    bn = 1536
    return pl.pallas_call(
        _mod_kernel,
        grid=(n // bn,),
        in_specs=[
            pl.BlockSpec((8, D_MODEL), lambda j: (0, 0)),
            pl.BlockSpec((D_MODEL, bn), lambda j: (0, j)),
            pl.BlockSpec((1, bn), lambda j: (0, j)),
        ],
        out_specs=pl.BlockSpec((8, bn), lambda j: (0, j)),
        out_shape=jax.ShapeDtypeStruct((8, n), F32),
        compiler_params=pltpu.CompilerParams(vmem_limit_bytes=VMEM_LIMIT),
        name="modulation",
    )(c_all, w_mod, b_mod.reshape(1, n))


def _in_kernel(nct, x_ref, ctx_ref, sh_ref, sc_ref, nw_ref, w_ref, z_ref, xbc_ref, ml_ref, g_ref):
    t = pl.program_id(1)
    xin = jnp.where(t < nct, ctx_ref[...], x_ref[...])
    ms = jnp.mean(xin * xin, axis=-1, keepdims=True)
    y = xin * lax.rsqrt(ms + EPS) * nw_ref[...]
    h = (y * (1.0 + sc_ref[...]) + sh_ref[...]).astype(BF16)
    z_ref[...] = _dot(h, w_ref[:, 0:512]).astype(BF16)
    for j in range(2):
        xbc_ref[:, j * 512:(j + 1) * 512] = _dot(h, w_ref[:, 512 + j * 512:1024 + j * 512]).astype(BF16)
    for j in range(3):
        ml_ref[:, j * 512:(j + 1) * 512] = _dot(h, w_ref[:, 1536 + j * 512:2048 + j * 512]).astype(BF16)
    g_ref[...] = _dot(h, w_ref[:, 3072:3200])


def _in_proj(x, ctx, mod3, norm_w, w_cat):
    b, seq, d = x.shape
    ctx_len = ctx.shape[1]
    nct, nlt = ctx_len // TM_IN, seq // TM_IN
    tot = ctx_len + seq
    n_ctx_row = b

    def mod_map(j):
        return lambda bi, t: (jnp.where(t < nct, n_ctx_row, bi) * 6 + j, 0, 0)

    out_shapes = (
        jax.ShapeDtypeStruct((b, tot, SSD_WIDTH), BF16),
        jax.ShapeDtypeStruct((b, tot, SSD_XBC), BF16),
        jax.ShapeDtypeStruct((b, tot, 3 * ML_WIDTH), BF16),
        jax.ShapeDtypeStruct((b, tot, LANES), F32),
    )

    def out_spec(width):
        return pl.BlockSpec((None, TM_IN, width), lambda bi, t: (bi, t, 0))

    return pl.pallas_call(
        functools.partial(_in_kernel, nct),
        grid=(b, nct + nlt),
        in_specs=[
            pl.BlockSpec((None, TM_IN, d), lambda bi, t: (bi, jnp.maximum(t - nct, 0), 0)),
            pl.BlockSpec((None, TM_IN, d), lambda bi, t: (bi, jnp.minimum(t, nct - 1), 0)),
            pl.BlockSpec((None, 1, d), mod_map(0)),
            pl.BlockSpec((None, 1, d), mod_map(1)),
            pl.BlockSpec((1, d), lambda bi, t: (0, 0)),
            pl.BlockSpec(w_cat.shape, lambda bi, t: (0, 0)),
        ],
        out_specs=[out_spec(SSD_WIDTH), out_spec(SSD_XBC), out_spec(3 * ML_WIDTH), out_spec(LANES)],
        out_shape=out_shapes,
        compiler_params=pltpu.CompilerParams(
            dimension_semantics=("arbitrary", "arbitrary"), vmem_limit_bytes=VMEM_LIMIT),
        name="in_proj",
    )(x, ctx, mod3, mod3, norm_w.reshape(1, d), w_cat)


def _chunk_eff(rev, ncc, nc, c):
    if not rev:
        return c
    return jnp.where(c < ncc, ncc - 1 - c, nc - 1 - (c - ncc))


def _lat_block(rev, ncc, nc, c):
    nl = nc - ncc
    if not rev:
        return jnp.maximum(c - ncc, 0)
    return jnp.where(c < ncc, nl - 1, nl - 1 - (c - ncc))


def _conv_silu(xm, xp_ref, xn_ref, cw_ref, cb_ref, first, last):
    xp = xp_ref[...].astype(F32)[HALO - 8:HALO] * jnp.where(first, 0.0, 1.0)
    xn = xn_ref[...].astype(F32)[0:8] * jnp.where(last, 0.0, 1.0)
    ext = jnp.concatenate([xp, xm, xn], axis=0)
    acc = cb_ref[...]
    for tap in range(CONV_W):
        off = 8 - CONV_W // 2 + tap
        acc = acc + cw_ref[tap:tap + 1, :] * ext[off:off + CHUNK]
    return _silu(acc)


def _tri(rev, transposed=False):
    row = lax.broadcasted_iota(jnp.int32, (CHUNK, CHUNK), 0)
    col = lax.broadcasted_iota(jnp.int32, (CHUNK, CHUNK), 1)
    if transposed:
        row, col = col, row
    keep = (col >= row) if rev else (col <= row)
    return keep


def _scan_specs(rev, ncc, nc, width_main, width_halo):
    ceff = functools.partial(_chunk_eff, rev, ncc, nc)
    blocks_per_chunk = CHUNK // HALO
    last_halo = nc * blocks_per_chunk - 1
    main = pl.BlockSpec((None, CHUNK, width_main), lambda bi, c: (bi, ceff(c), 0))
    prev = pl.BlockSpec((None, HALO, width_halo),
                        lambda bi, c: (bi, jnp.maximum(ceff(c) * blocks_per_chunk - 1, 0), 0))
    nxt = pl.BlockSpec((None, HALO, width_halo),
                       lambda bi, c: (bi, jnp.minimum((ceff(c) + 1) * blocks_per_chunk, last_halo), 0))
    gates = pl.BlockSpec((None, CHUNK, LANES), lambda bi, c: (bi, ceff(c), 0))
    return main, prev, nxt, gates


def _full(shape):
    return pl.BlockSpec(shape, lambda bi, c: (0,) * len(shape))


def _ssd_kernel(rev, ncc, nc, *refs):
    if rev:
        (xm_ref, xp_ref, xn_ref, g_ref, cw_ref, cb_ref, dtbr_ref, dtbc_ref, alr_ref, alc_ref, e_ref,
         o_ref, s_ref) = refs
    else:
        (xm_ref, xp_ref, xn_ref, g_ref, cw_ref, cb_ref, dtbr_ref, dtbc_ref, alr_ref, alc_ref, e_ref,
         dsk_ref, yb_ref, z_ref, nw_ref, o_ref, s_ref) = refs
    c = pl.program_id(1)
    ceff = _chunk_eff(rev, ncc, nc, c)
    first = (ceff == 0) | (ceff == ncc)
    last = (ceff == ncc - 1) | (ceff == nc - 1)

    @pl.when(c == 0)
    def _():
        s_ref[...] = jnp.zeros_like(s_ref)

    xbc = _conv_silu(xm_ref[...].astype(F32), xp_ref, xn_ref, cw_ref, cb_ref, first, last)
    xs = xbc[:, :SSD_WIDTH]

    lane0 = SSD_HEADS * int(rev)
    g = g_ref[...]
    lane = lax.broadcasted_iota(jnp.int32, (CHUNK, LANES), 1)
    lmask = (lane >= lane0) & (lane < lane0 + SSD_HEADS)
    dt = jnp.where(lmask, _softplus(g + dtbr_ref[...]), 0.0)
    a = dt * (-jnp.exp(alr_ref[...]))
    gt = g.T
    dt_t = _softplus(gt + dtbc_ref[...])[lane0:lane0 + SSD_HEADS]
    a_t = dt_t * (-jnp.exp(alc_ref[...][lane0:lane0 + SSD_HEADS]))

    cs = _dot_hi(_tri(rev).astype(F32), a)
    cs_t = _dot_hi(a_t, _tri(rev, transposed=True).astype(F32))
    e = e_ref[...]
    dtx = _dot_hi(dt, e)
    csx = _dot_hi(cs, e)
    end = 0 if rev else CHUNK - 1
    totx = csx[end:end + 1, :]
    ecsx = jnp.exp(csx)
    decx = jnp.exp(totx - csx)
    etotx = jnp.exp(totx)

    xdt = xs * dtx
    xdt_b = xdt.astype(BF16)
    xd_b = (xdt * decx).astype(BF16)
    keep = _tri(rev)
    half = lax.broadcasted_iota(jnp.int32, (CHUNK, LANES), 1) // SSD_HEADDIM

    y_blocks = []
    gs = SSD_GROUPS * SSD_STATE
    hpg = SSD_HEADS // SSD_GROUPS
    gw = hpg * SSD_HEADDIM
    for grp in range(SSD_GROUPS):
        bm = xbc[:, SSD_WIDTH + grp * SSD_STATE:SSD_WIDTH + (grp + 1) * SSD_STATE]
        cm = xbc[:, SSD_WIDTH + gs + grp * SSD_STATE:SSD_WIDTH + gs + (grp + 1) * SSD_STATE]
        bm_b, cm_b = bm.astype(BF16), cm.astype(BF16)
        cb = _dot_nt(cm_b, bm_b)
        s_old = s_ref[:, grp * gw:(grp + 1) * gw]
        y_off = _dot(cm_b, s_old.astype(BF16)) * ecsx[:, grp * gw:(grp + 1) * gw]
        for pair in range(hpg // 2):
            blk = grp * (hpg // 2) + pair
            xj = xdt_b[:, blk * LANES:(blk + 1) * LANES]
            acc = y_off[:, pair * LANES:(pair + 1) * LANES]
            for q in range(2):
                h = blk * 2 + q
                dl = cs[:, lane0 + h:lane0 + h + 1] - cs_t[h:h + 1, :]
                m = (cb * jnp.exp(jnp.where(keep, dl, NEG_BIG))).astype(BF16)
                acc = acc + _dot(m, jnp.where(half == q, xj, jnp.zeros_like(xj)))
            y_blocks.append(acc)
        s_new = s_old * etotx[:, grp * gw:(grp + 1) * gw] + _dot(bm.T.astype(BF16), xd_b[:, grp * gw:(grp + 1) * gw])
        s_ref[:, grp * gw:(grp + 1) * gw] = s_new

    y = jnp.concatenate(y_blocks, axis=1)
    if rev:
        o_ref[...] = y.astype(o_ref.dtype)
    else:
        y = y + yb_ref[...].astype(F32) + dsk_ref[...] * xs
        y = y * _silu(z_ref[...].astype(F32))
        y = y * lax.rsqrt(jnp.mean(y * y, axis=-1, keepdims=True) + EPS) * nw_ref[...]
        o_ref[...] = y.astype(o_ref.dtype)


def _ssd_scan(rev, ncc, xbc, gates, params, extra):
    b, tot, _ = xbc.shape
    nc = tot // CHUNK
    nl = nc - ncc
    main, prev, nxt, gspec = _scan_specs(rev, ncc, nc, SSD_XBC, SSD_XBC)
    in_specs = [main, prev, nxt, gspec] + [_full(p.shape) for p in params]
    args = [xbc, xbc, xbc, gates] + list(params)
    lat = functools.partial(_lat_block, rev, ncc, nc)
    out_spec = pl.BlockSpec((None, CHUNK, SSD_WIDTH), lambda bi, c: (bi, lat(c), 0))
    if not rev:
        dsk, yb, z, nw = extra
        ceff = functools.partial(_chunk_eff, rev, ncc, nc)
        in_specs += [_full(dsk.shape), out_spec,
                     pl.BlockSpec((None, CHUNK, SSD_WIDTH), lambda bi, c: (bi, ceff(c), 0)), _full(nw.shape)]
        args += [dsk, yb, z, nw]
    return pl.pallas_call(
        functools.partial(_ssd_kernel, rev, ncc, nc),
        grid=(b, nc),
        in_specs=in_specs,
        out_specs=out_spec,
        out_shape=jax.ShapeDtypeStruct((b, nl * CHUNK, SSD_WIDTH), BF16),
        scratch_shapes=[pltpu.VMEM((SSD_STATE, SSD_WIDTH), F32)],
        compiler_params=pltpu.CompilerParams(
            dimension_semantics=("arbitrary", "arbitrary"), vmem_limit_bytes=VMEM_LIMIT),
        name="ssd_rev" if rev else "ssd_fwd",
    )(*args)


def _ml_kernel(rev, ncc, nc, *refs):
    if rev:
        (mm_ref, mp_ref, mn_ref, g_ref, cw_ref, cb_ref, wq_ref, wk_ref, gbr_ref, gbc_ref,
         o_ref, c_ref, n_ref, mx_ref) = refs
    else:
        (mm_ref, mp_ref, mn_ref, g_ref, cw_ref, cb_ref, wq_ref, wk_ref, gbr_ref, gbc_ref,
         hb_ref, nw_ref, sk_ref, o_ref, c_ref, n_ref, mx_ref) = refs
    c = pl.program_id(1)
    ceff = _chunk_eff(rev, ncc, nc, c)
    first = (ceff == 0) | (ceff == ncc)
    last = (ceff == ncc - 1) | (ceff == nc - 1)

    @pl.when(c == 0)
    def _():
        c_ref[...] = jnp.zeros_like(c_ref)
        n_ref[...] = jnp.zeros_like(n_ref)
        mx_ref[...] = jnp.full(mx_ref.shape, NEG_BIG, F32)

    xconv = _conv_silu(mm_ref[:, 0:ML_WIDTH].astype(F32), mp_ref, mn_ref, cw_ref, cb_ref, first, last)
    xc_b = xconv.astype(BF16)
    q = _dot(xc_b, wq_ref[...])
    k = _dot(xc_b, wk_ref[...])

    i_lane0 = 16 + ML_HEADS * int(rev)
    f_lane0 = 24 + ML_HEADS * int(rev)
    ga = g_ref[...] + gbr_ref[...]
    lane = lax.broadcasted_iota(jnp.int32, (CHUNK, LANES), 1)
    logf = jnp.where((lane >= f_lane0) & (lane < f_lane0 + ML_HEADS), -_softplus(-ga), 0.0)
    cs = _dot_hi(_tri(rev).astype(F32), logf)
    gt = g_ref[...].T + gbc_ref[...]
    i_t = gt[16:24]
    logf_t = -_softplus(-gt[24:32])
    cs_t = _dot_hi(logf_t, _tri(rev, transposed=True).astype(F32))
    keep = _tri(rev)
    end = 0 if rev else CHUNK - 1

    outs = []
    for h in range(ML_HEADS):
        r = ML_HEADS * int(rev) + h
        sl = slice(h * ML_HEADDIM, (h + 1) * ML_HEADDIM)
        qh, kh = q[:, sl], k[:, sl]
        vh = mm_ref[:, ML_WIDTH + h * ML_HEADDIM:ML_WIDTH + (h + 1) * ML_HEADDIM]
        qh_b, kh_b = qh.astype(BF16), kh.astype(BF16)
        csc = cs[:, f_lane0 + h:f_lane0 + h + 1]
        csr = cs_t[r:r + 1, :]
        ic = ga[:, i_lane0 + h:i_lane0 + h + 1]
        ir = i_t[r:r + 1, :]
        tot = csr[:, end:end + 1]
        m_prev = mx_ref[h:h + 1, 0:1]
        c_prev = c_ref[h]
        n_prev = n_ref[h:h + 1, :]

        dlog = jnp.where(keep, csc - csr + ir, NEG_BIG)
        g_inter = csc + m_prev
        m_out = jnp.maximum(g_inter, jnp.max(dlog, axis=1, keepdims=True))
        scores = _dot_nt(qh_b, kh_b) * jnp.exp(dlog - m_out)
        w_inter = jnp.exp(g_inter - m_out)
        num = _dot(scores.astype(BF16), vh) + w_inter * _dot_nt(qh_b, c_prev.astype(BF16))
        den = jnp.sum(scores, axis=1, keepdims=True) + w_inter * jnp.sum(qh * n_prev, axis=1, keepdims=True)
        outs.append(num / jnp.maximum(jnp.abs(den), jnp.exp(-m_out)))

        a_r = tot - csr + ir
        m_loc = jnp.max(a_r, axis=1, keepdims=True)
        kw = kh * jnp.exp(tot - csc + ic - m_loc)
        c_loc = _dot(vh.astype(F32).T.astype(BF16), kw.astype(BF16))
        n_loc = jnp.sum(kw, axis=0, keepdims=True)
        m_new = jnp.maximum(tot + m_prev, m_loc)
        s_prev = jnp.exp(tot + m_prev - m_new)
        s_loc = jnp.exp(m_loc - m_new)
        c_ref[h] = s_prev * c_prev + s_loc * c_loc
        n_ref[h:h + 1, :] = s_prev * n_prev + s_loc * n_loc
        mx_ref[h:h + 1, :] = jnp.broadcast_to(m_new, (1, LANES))

    if rev:
        o_ref[...] = jnp.concatenate(outs, axis=1).astype(o_ref.dtype)
    else:
        normed = []
        for h in range(ML_HEADS):
            sl = slice(h * ML_HEADDIM, (h + 1) * ML_HEADDIM)
            o_gate = mm_ref[:, 2 * ML_WIDTH + h * ML_HEADDIM:2 * ML_WIDTH + (h + 1) * ML_HEADDIM].astype(F32)
            hh = jax.nn.sigmoid(o_gate) * (outs[h] + hb_ref[:, sl].astype(F32))
            normed.append(hh * lax.rsqrt(jnp.mean(hh * hh, axis=-1, keepdims=True) + EPS))
        y = jnp.concatenate(normed, axis=1) * nw_ref[...] + sk_ref[...] * xconv
        o_ref[...] = y.astype(o_ref.dtype)


def _ml_scan(rev, ncc, ml, gates, params, extra):
    b, tot, _ = ml.shape
    nc = tot // CHUNK
    nl = nc - ncc
    main, prev, nxt, gspec = _scan_specs(rev, ncc, nc, 3 * ML_WIDTH, ML_WIDTH)
    in_specs = [main, prev, nxt, gspec] + [_full(p.shape) for p in params]
    args = [ml, ml, ml, gates] + list(params)
    lat = functools.partial(_lat_block, rev, ncc, nc)
    out_spec = pl.BlockSpec((None, CHUNK, ML_WIDTH), lambda bi, c: (bi, lat(c), 0))
    if not rev:
        hb, nw, sk = extra
        in_specs += [out_spec, _full(nw.shape), _full(sk.shape)]
        args += [hb, nw, sk]
    return pl.pallas_call(
        functools.partial(_ml_kernel, rev, ncc, nc),
        grid=(b, nc),
        in_specs=in_specs,
        out_specs=out_spec,
        out_shape=jax.ShapeDtypeStruct((b, nl * CHUNK, ML_WIDTH), BF16),
        scratch_shapes=[pltpu.VMEM((ML_HEADS, ML_HEADDIM, ML_HEADDIM), F32),
                        pltpu.VMEM((8, ML_HEADDIM), F32),
                        pltpu.VMEM((8, LANES), F32)],
        compiler_params=pltpu.CompilerParams(
            dimension_semantics=("arbitrary", "arbitrary"), vmem_limit_bytes=VMEM_LIMIT),
        name="mlstm_rev" if rev else "mlstm_fwd",
    )(*args)


def _out_kernel(x_ref, ys_ref, ym_ref, wo_ref, g1_ref, sh_ref, sc_ref, nw_ref, rwh_ref, rwl_ref, rb_ref,
                x1_ref, t_ref, comb_ref):
    mix = _dot(ys_ref[...], wo_ref[0:SSD_WIDTH, :]) + _dot(ym_ref[...], wo_ref[SSD_WIDTH:, :])
    x1 = x_ref[...] + g1_ref[...] * mix
    x1_ref[...] = x1
    y = x1 * lax.rsqrt(jnp.mean(x1 * x1, axis=-1, keepdims=True) + EPS) * nw_ref[...]
    t = y * (1.0 + sc_ref[...]) + sh_ref[...]
    t_hi = t.astype(BF16)
    t_ref[...] = t_hi
    t_lo = (t - t_hi.astype(F32)).astype(BF16)
    lg = _dot(t_hi, rwh_ref[...]) + _dot(t_lo, rwh_ref[...]) + _dot(t_hi, rwl_ref[...]) + rb_ref[...]

    lane = lax.broadcasted_iota(jnp.int32, lg.shape, 1).astype(F32)
    gmask = lane < MOE_GROUPS
    gl = jnp.where(gmask, lg, NEG_BIG)
    gmax = jnp.max(gl, axis=1, keepdims=True)
    g_sel = jnp.min(jnp.where(gmask & (gl == gmax), lane, 1e9), axis=1, keepdims=True)
    p_group = 1.0 / jnp.sum(jnp.where(gmask, jnp.exp(gl - gmax), 0.0), axis=1, keepdims=True)
    lo = ROUTE_LANE0 + MOE_EPG * g_sel
    emask = (lane >= lo) & (lane < lo + MOE_EPG)
    l1 = jnp.max(jnp.where(emask, lg, NEG_BIG), axis=1, keepdims=True)
    i1 = jnp.min(jnp.where(emask & (lg == l1), lane, 1e9), axis=1, keepdims=True)
    emask2 = emask & (lane != i1)
    l2 = jnp.max(jnp.where(emask2, lg, NEG_BIG), axis=1, keepdims=True)
    i2 = jnp.min(jnp.where(emask2 & (lg == l2), lane, 1e9), axis=1, keepdims=True)
    r = jnp.exp(l2 - l1)
    w1 = p_group / (1.0 + r)
    w2 = p_group * r / (1.0 + r)
    comb_ref[...] = jnp.where(lane == i1, w1, 0.0) + jnp.where(lane == i2, w2, 0.0)


def _out_proj(x, y_ssd, y_ml, w_out_b, mod3, norm_w, rw_hi, rw_lo, rb):
    b, seq, d = x.shape
    nt = seq // TM_IN

    def row(j):
        return pl.BlockSpec((None, 1, d), lambda bi, t: (bi * 6 + j, 0, 0))

    def tile(width):
        return pl.BlockSpec((None, TM_IN, width), lambda bi, t: (bi, t, 0))

    return pl.pallas_call(
        _out_kernel,
        grid=(b, nt),
        in_specs=[tile(d), tile(SSD_WIDTH), tile(ML_WIDTH), _full(w_out_b.shape),
                  row(2), row(3), row(4), _full((1, d)), _full(rw_hi.shape), _full(rw_lo.shape), _full(rb.shape)],
        out_specs=[tile(d), tile(d), tile(LANES)],
        out_shape=(jax.ShapeDtypeStruct((b, seq, d), F32), jax.ShapeDtypeStruct((b, seq, d), BF16),
                   jax.ShapeDtypeStruct((b, seq, LANES), F32)),
        compiler_params=pltpu.CompilerParams(
            dimension_semantics=("arbitrary", "arbitrary"), vmem_limit_bytes=VMEM_LIMIT),
        name="out_proj_router",
    )(x, y_ssd, y_ml, w_out_b, mod3, mod3, mod3, norm_w.reshape(1, d), rw_hi, rw_lo, rb)


def _moe_kernel(t_ref, comb_ref, wg_ref, wu_ref, wd_ref, x1_ref, g2_ref, fw_ref, o_ref, acc_ref):
    e = pl.program_id(2)

    @pl.when(e == 0)
    def _():
        acc_ref[...] = jnp.zeros_like(acc_ref)

    t = t_ref[...]
    comb = comb_ref[...]
    lane = lax.broadcasted_iota(jnp.int32, comb.shape, 1)
    w = jnp.sum(jnp.where(lane == e + ROUTE_LANE0, comb, 0.0), axis=1, keepdims=True)
    hidden = _silu(_dot(t, wg_ref[...])) * _dot(t, wu_ref[...]) * w
    acc_ref[...] += _dot(hidden.astype(BF16), wd_ref[...])

    @pl.when(e == MOE_EXPERTS - 1)
    def _():
        x2 = x1_ref[...] + g2_ref[...] * acc_ref[...]
        o_ref[...] = x2 * lax.rsqrt(jnp.mean(x2 * x2, axis=-1, keepdims=True) + EPS) * fw_ref[...]


def _moe(t, comb, wg_b, wu_b, wd_b, x1, mod3, final_w):
    b, seq, d = x1.shape
    tm = min(TM_MOE, seq)
    nt = seq // tm

    def tile(width):
        return pl.BlockSpec((None, tm, width), lambda bi, ti, e: (bi, ti, 0))

    return pl.pallas_call(
        _moe_kernel,
        grid=(b, nt, MOE_EXPERTS),
        in_specs=[tile(d), tile(LANES),
                  pl.BlockSpec((None, d, D_EXPERT), lambda bi, ti, e: (e, 0, 0)),
                  pl.BlockSpec((None, d, D_EXPERT), lambda bi, ti, e: (e, 0, 0)),
                  pl.BlockSpec((None, D_EXPERT, d), lambda bi, ti, e: (e, 0, 0)),
                  tile(d),
                  pl.BlockSpec((None, 1, d), lambda bi, ti, e: (bi * 6 + 5, 0, 0)),
                  pl.BlockSpec((1, d), lambda bi, ti, e: (0, 0))],
        out_specs=tile(d),
        out_shape=jax.ShapeDtypeStruct((b, seq, d), F32),
        scratch_shapes=[pltpu.VMEM((tm, d), F32)],
        compiler_params=pltpu.CompilerParams(
            dimension_semantics=("arbitrary", "arbitrary", "arbitrary"), vmem_limit_bytes=VMEM_LIMIT),
        name="experts_final_norm",
    )(t, comb, wg_b, wu_b, wd_b, x1, mod3, final_w.reshape(1, d))


def _lane_pad(v, offset=0):
    v = v.reshape(-1).astype(F32)
    return jnp.zeros((1, LANES), F32).at[0, offset:offset + v.shape[0]].set(v)


def _to_col_major(t, rows):
    b, length, ch = t.shape
    return t.reshape(b, rows, GRID_W, ch).transpose(0, 2, 1, 3).reshape(b, length, ch)


def _from_col_major(t, rows):
    b, length, ch = t.shape
    return t.reshape(b, GRID_W, rows, ch).transpose(0, 2, 1, 3).reshape(b, length, ch)


def kernel(x, c, ctx, c_ctx, w_mod, b_mod, norm1_w, w_in, ssd_conv_w, ssd_conv_b, ssd_dt_bias, ssd_a_log, ssd_d, ssd_norm_w, ml_conv_w, ml_conv_b, ml_w_qk, ml_gate_b, ml_norm_w, ml_skip, w_out, norm2_w, moe_rg_w, moe_rg_b, moe_re_w, moe_re_b, moe_w_gate, moe_w_up, moe_w_down, final_norm_w):
    b, seq, d = x.shape
    ctx_len = ctx.shape[1]
    rows = seq // GRID_W
    ncc = ctx_len // CHUNK
    assert w_mod.shape[0] == 1 and d == D_MODEL and b + 1 <= 8
    assert seq % TM_IN == 0 and ctx_len % TM_IN == 0 and seq % min(TM_MOE, seq) == 0

    c_all = jnp.zeros((8, d), F32).at[:b].set(c).at[b].set(c_ctx)
    mod = _modulation(c_all, w_mod[0], b_mod[0])
    mod3 = mod.reshape(8 * 6, 1, d)

    w = w_in[0]
    ssd_in = SSD_WIDTH + SSD_XBC + 2 * SSD_HEADS
    ml_main = 3 * ML_WIDTH
    n_gate = 2 * SSD_HEADS + 4 * ML_HEADS
    w_cat = jnp.concatenate([
        w[:, :SSD_WIDTH + SSD_XBC], w[:, ssd_in:ssd_in + ml_main],
        w[:, SSD_WIDTH + SSD_XBC:ssd_in], w[:, ssd_in + ml_main:],
        jnp.zeros((d, LANES - n_gate), F32)], axis=1).astype(BF16)
    z, xbc, ml, gates = _in_proj(x, ctx, mod3, norm1_w[0], w_cat)

    e_mats = []
    for direction in range(2):
        lane = jnp.arange(LANES)[:, None]
        head = (jnp.arange(SSD_WIDTH) // SSD_HEADDIM)[None, :]
        e_mats.append((lane == direction * SSD_HEADS + head).astype(F32))
    cw = jnp.zeros((8, SSD_XBC), F32).at[:CONV_W].set(ssd_conv_w[0])
    dtb = _lane_pad(ssd_dt_bias[0])
    alog = _lane_pad(ssd_a_log[0])
    ssd_params = [cw, ssd_conv_b[0].reshape(1, -1), dtb, dtb.reshape(LANES, 1), alog, alog.reshape(LANES, 1)]
    yb = _ssd_scan(True, ncc, xbc, gates, ssd_params + [e_mats[1]], None)
    dsk = jnp.repeat(ssd_d[0], SSD_HEADDIM).reshape(1, SSD_WIDTH)
    y_ssd = _ssd_scan(False, ncc, xbc, gates, ssd_params + [e_mats[0]],
                      (dsk, yb, z, ssd_norm_w[0].reshape(1, -1)))

    ml_cm = jnp.concatenate([ml[:, :ctx_len], _to_col_major(ml[:, ctx_len:], rows)], axis=1)
    g_cm = jnp.concatenate([gates[:, :ctx_len], _to_col_major(gates[:, ctx_len:], rows)], axis=1)
    eye = jnp.eye(ML_WIDTH // ML_QK_BLOCK, dtype=F32)
    w_bd = jnp.einsum("snjk,nm->snjmk", ml_w_qk[0], eye).reshape(2, ML_WIDTH, ML_WIDTH)
    wq = w_bd[0].astype(BF16)
    wk = (w_bd[1] * (ML_HEADDIM ** -0.5)).astype(BF16)
    mcw = jnp.zeros((8, ML_WIDTH), F32).at[:CONV_W].set(ml_conv_w[0])
    gb = _lane_pad(ml_gate_b[0], offset=2 * SSD_HEADS)
    ml_params = [mcw, ml_conv_b[0].reshape(1, -1), wq, wk, gb, gb.reshape(LANES, 1)]
    hb = _ml_scan(True, ncc, ml_cm, g_cm, ml_params, None)
    y_ml_cm = _ml_scan(False, ncc, ml_cm, g_cm, ml_params,
                       (hb, ml_norm_w[0].reshape(1, -1), ml_skip[0].reshape(1, -1)))
    y_ml = _from_col_major(y_ml_cm, rows)

    rw = jnp.concatenate([moe_rg_w[0], moe_re_w[0],
                          jnp.zeros((d, LANES - MOE_GROUPS - MOE_EXPERTS), F32)], axis=1)
    rw_hi = rw.astype(BF16)
    rw_lo = (rw - rw_hi.astype(F32)).astype(BF16)
    rb = _lane_pad(jnp.concatenate([moe_rg_b[0], moe_re_b[0]]))
    x1, t, comb = _out_proj(x, y_ssd, y_ml, w_out[0].astype(BF16), mod3, norm2_w[0], rw_hi, rw_lo, rb)

    return _moe(t, comb, moe_w_gate[0].astype(BF16), moe_w_up[0].astype(BF16), moe_w_down[0].astype(BF16),
                x1, mod3, final_norm_w)
```

```python
import functools

import jax
import jax.numpy as jnp
from jax import lax
from jax.experimental import pallas as pl
from jax.experimental.pallas import tpu as pltpu

F32 = jnp.float32
BF16 = jnp.bfloat16
HIGHEST = lax.Precision.HIGHEST

D_MODEL = 1024
GRID_W = 64
EPS = 1e-6
CONV_W = 5
NEG_BIG = -1e30
CHUNK = 128
LANES = 128
HALO = 16

SSD_WIDTH = 512
SSD_HEADS = 8
SSD_HEADDIM = 64
SSD_GROUPS = 2
SSD_STATE = 128
SSD_XBC = SSD_WIDTH + 2 * SSD_GROUPS * SSD_STATE

ML_WIDTH = 512
ML_HEADS = 4
ML_HEADDIM = 128
ML_QK_BLOCK = 4

MOE_GROUPS = 4
MOE_EPG = 8
MOE_EXPERTS = 32
D_EXPERT = 256
ROUTE_LANE0 = MOE_GROUPS

TM_IN = 256
TM_FIN = 256
SLAB = D_MODEL // LANES
BM = 128
TILE_PITCH = BM + 8
SCATTER_GROUP = 4
VMEM_LIMIT = 48 * 1024 * 1024
VMEM_LIMIT_MOE = 56 * 1024 * 1024


def _silu(v):
    return v * jax.nn.sigmoid(v)


def _softplus(v):
    return jnp.maximum(v, 0.0) + jnp.log1p(jnp.exp(-jnp.abs(v)))


def _dot(a, b):
    return jnp.dot(a, b, preferred_element_type=F32)


def _dot_nt(a, b):
    return lax.dot_general(a, b, (((1,), (1,)), ((), ())), preferred_element_type=F32)


def _dot_hi(a, b):
    return jnp.dot(a, b, preferred_element_type=F32, precision=HIGHEST)


def _mod_kernel(c_ref, w_ref, b_ref, o_ref):
    c = c_ref[...]
    o_ref[...] = _dot_hi(_silu(c), w_ref[...]) + b_ref[...]


def _modulation(c_all, w_mod, b_mod):
    n = w_mod.shape[1]
    bn = 1536
    return pl.pallas_call(
        _mod_kernel,
        grid=(n // bn,),
        in_specs=[
            pl.BlockSpec((8, D_MODEL), lambda j: (0, 0)),
            pl.BlockSpec((D_MODEL, bn), lambda j: (0, j)),
            pl.BlockSpec((1, bn), lambda j: (0, j)),
        ],
        out_specs=pl.BlockSpec((8, bn), lambda j: (0, j)),
        out_shape=jax.ShapeDtypeStruct((8, n), F32),
        compiler_params=pltpu.CompilerParams(vmem_limit_bytes=VMEM_LIMIT),
        name="modulation",
    )(c_all, w_mod, b_mod.reshape(1, n))


def _in_kernel(nct, x_ref, ctx_ref, sh_ref, sc_ref, nw_ref, w_ref, z_ref, xbc_ref, ml_ref, g_ref):
    t = pl.program_id(1)
    xin = jnp.where(t < nct, ctx_ref[...], x_ref[...])
    ms = jnp.mean(xin * xin, axis=-1, keepdims=True)
    y = xin * lax.rsqrt(ms + EPS) * nw_ref[...]
    h = (y * (1.0 + sc_ref[...]) + sh_ref[...]).astype(BF16)
    z_ref[...] = _dot(h, w_ref[:, 0:512]).astype(BF16)
    for j in range(2):
        xbc_ref[:, j * 512:(j + 1) * 512] = _dot(h, w_ref[:, 512 + j * 512:1024 + j * 512]).astype(BF16)
    for j in range(3):
        ml_ref[:, j * 512:(j + 1) * 512] = _dot(h, w_ref[:, 1536 + j * 512:2048 + j * 512]).astype(BF16)
    g_ref[...] = _dot(h, w_ref[:, 3072:3200])


def _in_proj(x, ctx, mod3, norm_w, w_cat):
    b, seq, d = x.shape
    ctx_len = ctx.shape[1]
    nct, nlt = ctx_len // TM_IN, seq // TM_IN
    tot = ctx_len + seq
    n_ctx_row = b

    def mod_map(j):
        return lambda bi, t: (jnp.where(t < nct, n_ctx_row, bi) * 6 + j, 0, 0)

    out_shapes = (
        jax.ShapeDtypeStruct((b, tot, SSD_WIDTH), BF16),
        jax.ShapeDtypeStruct((b, tot, SSD_XBC), BF16),
        jax.ShapeDtypeStruct((b, tot, 3 * ML_WIDTH), BF16),
        jax.ShapeDtypeStruct((b, tot, LANES), F32),
    )

    def out_spec(width):
        return pl.BlockSpec((None, TM_IN, width), lambda bi, t: (bi, t, 0))

    return pl.pallas_call(
        functools.partial(_in_kernel, nct),
        grid=(b, nct + nlt),
        in_specs=[
            pl.BlockSpec((None, TM_IN, d), lambda bi, t: (bi, jnp.maximum(t - nct, 0), 0)),
            pl.BlockSpec((None, TM_IN, d), lambda bi, t: (bi, jnp.minimum(t, nct - 1), 0)),
            pl.BlockSpec((None, 1, d), mod_map(0)),
            pl.BlockSpec((None, 1, d), mod_map(1)),
            pl.BlockSpec((1, d), lambda bi, t: (0, 0)),
            pl.BlockSpec(w_cat.shape, lambda bi, t: (0, 0)),
        ],
        out_specs=[out_spec(SSD_WIDTH), out_spec(SSD_XBC), out_spec(3 * ML_WIDTH), out_spec(LANES)],
        out_shape=out_shapes,
        compiler_params=pltpu.CompilerParams(
            dimension_semantics=("arbitrary", "arbitrary"), vmem_limit_bytes=VMEM_LIMIT),
        name="in_proj",
    )(x, ctx, mod3, mod3, norm_w.reshape(1, d), w_cat)


def _chunk_eff(rev, ncc, nc, c):
    if not rev:
        return c
    return jnp.where(c < ncc, ncc - 1 - c, nc - 1 - (c - ncc))


def _lat_block(rev, ncc, nc, c):
    nl = nc - ncc
    if not rev:
        return jnp.maximum(c - ncc, 0)
    return jnp.where(c < ncc, nl - 1, nl - 1 - (c - ncc))


def _conv_silu(xm, xp_ref, xn_ref, cw_ref, cb_ref, first, last):
    xp = xp_ref[...].astype(F32)[HALO - 8:HALO] * jnp.where(first, 0.0, 1.0)
    xn = xn_ref[...].astype(F32)[0:8] * jnp.where(last, 0.0, 1.0)
    ext = jnp.concatenate([xp, xm, xn], axis=0)
    acc = cb_ref[...]
    for tap in range(CONV_W):
        off = 8 - CONV_W // 2 + tap
        acc = acc + cw_ref[tap:tap + 1, :] * ext[off:off + CHUNK]
    return _silu(acc)


def _tri(rev, transposed=False):
    row = lax.broadcasted_iota(jnp.int32, (CHUNK, CHUNK), 0)
    col = lax.broadcasted_iota(jnp.int32, (CHUNK, CHUNK), 1)
    if transposed:
        row, col = col, row
    keep = (col >= row) if rev else (col <= row)
    return keep


def _scan_specs(rev, ncc, nc, width_main, width_halo):
    ceff = functools.partial(_chunk_eff, rev, ncc, nc)
    blocks_per_chunk = CHUNK // HALO
    last_halo = nc * blocks_per_chunk - 1
    main = pl.BlockSpec((None, CHUNK, width_main), lambda bi, c: (bi, ceff(c), 0))
    prev = pl.BlockSpec((None, HALO, width_halo),
                        lambda bi, c: (bi, jnp.maximum(ceff(c) * blocks_per_chunk - 1, 0), 0))
    nxt = pl.BlockSpec((None, HALO, width_halo),
                       lambda bi, c: (bi, jnp.minimum((ceff(c) + 1) * blocks_per_chunk, last_halo), 0))
    gates = pl.BlockSpec((None, CHUNK, LANES), lambda bi, c: (bi, ceff(c), 0))
    return main, prev, nxt, gates


def _full(shape):
    return pl.BlockSpec(shape, lambda bi, c: (0,) * len(shape))


def _ssd_kernel(rev, ncc, nc, *refs):
    if rev:
        (xm_ref, xp_ref, xn_ref, g_ref, cw_ref, cb_ref, dtbr_ref, dtbc_ref, alr_ref, alc_ref, e_ref,
         o_ref, s_ref) = refs
    else:
        (xm_ref, xp_ref, xn_ref, g_ref, cw_ref, cb_ref, dtbr_ref, dtbc_ref, alr_ref, alc_ref, e_ref,
         dsk_ref, yb_ref, z_ref, nw_ref, o_ref, s_ref) = refs
    c = pl.program_id(1)
    ceff = _chunk_eff(rev, ncc, nc, c)
    first = (ceff == 0) | (ceff == ncc)
    last = (ceff == ncc - 1) | (ceff == nc - 1)

    @pl.when(c == 0)
    def _():
        s_ref[...] = jnp.zeros_like(s_ref)

    xbc = _conv_silu(xm_ref[...].astype(F32), xp_ref, xn_ref, cw_ref, cb_ref, first, last)
    xs = xbc[:, :SSD_WIDTH]

    lane0 = SSD_HEADS * int(rev)
    g = g_ref[...]
    lane = lax.broadcasted_iota(jnp.int32, (CHUNK, LANES), 1)
    lmask = (lane >= lane0) & (lane < lane0 + SSD_HEADS)
    dt = jnp.where(lmask, _softplus(g + dtbr_ref[...]), 0.0)
    a = dt * (-jnp.exp(alr_ref[...]))
    gt = g.T
    dt_t = _softplus(gt + dtbc_ref[...])[lane0:lane0 + SSD_HEADS]
    a_t = dt_t * (-jnp.exp(alc_ref[...][lane0:lane0 + SSD_HEADS]))

    cs = _dot_hi(_tri(rev).astype(F32), a)
    cs_t = _dot_hi(a_t, _tri(rev, transposed=True).astype(F32))
    e = e_ref[...]
    dtx = _dot_hi(dt, e)
    csx = _dot_hi(cs, e)
    end = 0 if rev else CHUNK - 1
    totx = csx[end:end + 1, :]
    ecsx = jnp.exp(csx)
    decx = jnp.exp(totx - csx)
    etotx = jnp.exp(totx)

    xdt = xs * dtx
    xdt_b = xdt.astype(BF16)
    xd_b = (xdt * decx).astype(BF16)
    keep = _tri(rev)
    half = lax.broadcasted_iota(jnp.int32, (CHUNK, LANES), 1) // SSD_HEADDIM

    y_blocks = []
    gs = SSD_GROUPS * SSD_STATE
    hpg = SSD_HEADS // SSD_GROUPS
    gw = hpg * SSD_HEADDIM
    for grp in range(SSD_GROUPS):
        bm = xbc[:, SSD_WIDTH + grp * SSD_STATE:SSD_WIDTH + (grp + 1) * SSD_STATE]
        cm = xbc[:, SSD_WIDTH + gs + grp * SSD_STATE:SSD_WIDTH + gs + (grp + 1) * SSD_STATE]
        bm_b, cm_b = bm.astype(BF16), cm.astype(BF16)
        cb = _dot_nt(cm_b, bm_b)
        s_old = s_ref[:, grp * gw:(grp + 1) * gw]
        y_off = _dot(cm_b, s_old.astype(BF16)) * ecsx[:, grp * gw:(grp + 1) * gw]
        for pair in range(hpg // 2):
            blk = grp * (hpg // 2) + pair
            xj = xdt_b[:, blk * LANES:(blk + 1) * LANES]
            acc = y_off[:, pair * LANES:(pair + 1) * LANES]
            for q in range(2):
                h = blk * 2 + q
                dl = cs[:, lane0 + h:lane0 + h + 1] - cs_t[h:h + 1, :]
                m = (cb * jnp.exp(jnp.where(keep, dl, NEG_BIG))).astype(BF16)
                acc = acc + _dot(m, jnp.where(half == q, xj, jnp.zeros_like(xj)))
            y_blocks.append(acc)
        s_new = s_old * etotx[:, grp * gw:(grp + 1) * gw] + _dot(bm.T.astype(BF16), xd_b[:, grp * gw:(grp + 1) * gw])
        s_ref[:, grp * gw:(grp + 1) * gw] = s_new

    y = jnp.concatenate(y_blocks, axis=1)
    if rev:
        o_ref[...] = y.astype(o_ref.dtype)
    else:
        y = y + yb_ref[...].astype(F32) + dsk_ref[...] * xs
        y = y * _silu(z_ref[...].astype(F32))
        y = y * lax.rsqrt(jnp.mean(y * y, axis=-1, keepdims=True) + EPS) * nw_ref[...]
        o_ref[...] = y.astype(o_ref.dtype)


def _ssd_scan(rev, ncc, xbc, gates, params, extra):
    b, tot, _ = xbc.shape
    nc = tot // CHUNK
    nl = nc - ncc
    main, prev, nxt, gspec = _scan_specs(rev, ncc, nc, SSD_XBC, SSD_XBC)
    in_specs = [main, prev, nxt, gspec] + [_full(p.shape) for p in params]
    args = [xbc, xbc, xbc, gates] + list(params)
    lat = functools.partial(_lat_block, rev, ncc, nc)
    out_spec = pl.BlockSpec((None, CHUNK, SSD_WIDTH), lambda bi, c: (bi, lat(c), 0))
    if not rev:
        dsk, yb, z, nw = extra
        ceff = functools.partial(_chunk_eff, rev, ncc, nc)
        in_specs += [_full(dsk.shape), out_spec,
                     pl.BlockSpec((None, CHUNK, SSD_WIDTH), lambda bi, c: (bi, ceff(c), 0)), _full(nw.shape)]
        args += [dsk, yb, z, nw]
    return pl.pallas_call(
        functools.partial(_ssd_kernel, rev, ncc, nc),
        grid=(b, nc),
        in_specs=in_specs,
        out_specs=out_spec,
        out_shape=jax.ShapeDtypeStruct((b, nl * CHUNK, SSD_WIDTH), BF16),
        scratch_shapes=[pltpu.VMEM((SSD_STATE, SSD_WIDTH), F32)],
        compiler_params=pltpu.CompilerParams(
            dimension_semantics=("arbitrary", "arbitrary"), vmem_limit_bytes=VMEM_LIMIT),
        name="ssd_rev" if rev else "ssd_fwd",
    )(*args)


def _ml_kernel(rev, ncc, nc, *refs):
    if rev:
        (mm_ref, mp_ref, mn_ref, g_ref, cw_ref, cb_ref, wq_ref, wk_ref, gbr_ref, gbc_ref,
         o_ref, c_ref, n_ref, mx_ref) = refs
    else:
        (mm_ref, mp_ref, mn_ref, g_ref, cw_ref, cb_ref, wq_ref, wk_ref, gbr_ref, gbc_ref,
         hb_ref, nw_ref, sk_ref, o_ref, c_ref, n_ref, mx_ref) = refs
    c = pl.program_id(1)
    ceff = _chunk_eff(rev, ncc, nc, c)
    first = (ceff == 0) | (ceff == ncc)
    last = (ceff == ncc - 1) | (ceff == nc - 1)

    @pl.when(c == 0)
    def _():
        c_ref[...] = jnp.zeros_like(c_ref)
        n_ref[...] = jnp.zeros_like(n_ref)
        mx_ref[...] = jnp.full(mx_ref.shape, NEG_BIG, F32)

    xconv = _conv_silu(mm_ref[:, 0:ML_WIDTH].astype(F32), mp_ref, mn_ref, cw_ref, cb_ref, first, last)
    xc_b = xconv.astype(BF16)
    q = _dot(xc_b, wq_ref[...])
    k = _dot(xc_b, wk_ref[...])

    i_lane0 = 16 + ML_HEADS * int(rev)
    f_lane0 = 24 + ML_HEADS * int(rev)
    ga = g_ref[...] + gbr_ref[...]
    lane = lax.broadcasted_iota(jnp.int32, (CHUNK, LANES), 1)
    logf = jnp.where((lane >= f_lane0) & (lane < f_lane0 + ML_HEADS), -_softplus(-ga), 0.0)
    cs = _dot_hi(_tri(rev).astype(F32), logf)
    gt = g_ref[...].T + gbc_ref[...]
    i_t = gt[16:24]
    logf_t = -_softplus(-gt[24:32])
    cs_t = _dot_hi(logf_t, _tri(rev, transposed=True).astype(F32))
    keep = _tri(rev)
    end = 0 if rev else CHUNK - 1

    outs = []
    for h in range(ML_HEADS):
        r = ML_HEADS * int(rev) + h
        sl = slice(h * ML_HEADDIM, (h + 1) * ML_HEADDIM)
        qh, kh = q[:, sl], k[:, sl]
        vh = mm_ref[:, ML_WIDTH + h * ML_HEADDIM:ML_WIDTH + (h + 1) * ML_HEADDIM]
        qh_b, kh_b = qh.astype(BF16), kh.astype(BF16)
        csc = cs[:, f_lane0 + h:f_lane0 + h + 1]
        csr = cs_t[r:r + 1, :]
        ic = ga[:, i_lane0 + h:i_lane0 + h + 1]
        ir = i_t[r:r + 1, :]
        tot = csr[:, end:end + 1]
        m_prev = mx_ref[h:h + 1, 0:1]
        c_prev = c_ref[h]
        n_prev = n_ref[h:h + 1, :]

        dlog = jnp.where(keep, csc - csr + ir, NEG_BIG)
        g_inter = csc + m_prev
        m_out = jnp.maximum(g_inter, jnp.max(dlog, axis=1, keepdims=True))
        scores = _dot_nt(qh_b, kh_b) * jnp.exp(dlog - m_out)
        w_inter = jnp.exp(g_inter - m_out)
        num = _dot(scores.astype(BF16), vh) + w_inter * _dot_nt(qh_b, c_prev.astype(BF16))
        den = jnp.sum(scores, axis=1, keepdims=True) + w_inter * jnp.sum(qh * n_prev, axis=1, keepdims=True)
        outs.append(num / jnp.maximum(jnp.abs(den), jnp.exp(-m_out)))

        a_r = tot - csr + ir
        m_loc = jnp.max(a_r, axis=1, keepdims=True)
        kw = kh * jnp.exp(tot - csc + ic - m_loc)
        c_loc = _dot(vh.astype(F32).T.astype(BF16), kw.astype(BF16))
        n_loc = jnp.sum(kw, axis=0, keepdims=True)
        m_new = jnp.maximum(tot + m_prev, m_loc)
        s_prev = jnp.exp(tot + m_prev - m_new)
        s_loc = jnp.exp(m_loc - m_new)
        c_ref[h] = s_prev * c_prev + s_loc * c_loc
        n_ref[h:h + 1, :] = s_prev * n_prev + s_loc * n_loc
        mx_ref[h:h + 1, :] = jnp.broadcast_to(m_new, (1, LANES))

    if rev:
        o_ref[...] = jnp.concatenate(outs, axis=1).astype(o_ref.dtype)
    else:
        normed = []
        for h in range(ML_HEADS):
            sl = slice(h * ML_HEADDIM, (h + 1) * ML_HEADDIM)
            o_gate = mm_ref[:, 2 * ML_WIDTH + h * ML_HEADDIM:2 * ML_WIDTH + (h + 1) * ML_HEADDIM].astype(F32)
            hh = jax.nn.sigmoid(o_gate) * (outs[h] + hb_ref[:, sl].astype(F32))
            normed.append(hh * lax.rsqrt(jnp.mean(hh * hh, axis=-1, keepdims=True) + EPS))
        y = jnp.concatenate(normed, axis=1) * nw_ref[...] + sk_ref[...] * xconv
        o_ref[...] = y.astype(o_ref.dtype)


def _ml_scan(rev, ncc, ml, gates, params, extra):
    b, tot, _ = ml.shape
    nc = tot // CHUNK
    nl = nc - ncc
    main, prev, nxt, gspec = _scan_specs(rev, ncc, nc, 3 * ML_WIDTH, ML_WIDTH)
    in_specs = [main, prev, nxt, gspec] + [_full(p.shape) for p in params]
    args = [ml, ml, ml, gates] + list(params)
    lat = functools.partial(_lat_block, rev, ncc, nc)
    out_spec = pl.BlockSpec((None, CHUNK, ML_WIDTH), lambda bi, c: (bi, lat(c), 0))
    if not rev:
        hb, nw, sk = extra
        in_specs += [out_spec, _full(nw.shape), _full(sk.shape)]
        args += [hb, nw, sk]
    return pl.pallas_call(
        functools.partial(_ml_kernel, rev, ncc, nc),
        grid=(b, nc),
        in_specs=in_specs,
        out_specs=out_spec,
        out_shape=jax.ShapeDtypeStruct((b, nl * CHUNK, ML_WIDTH), BF16),
        scratch_shapes=[pltpu.VMEM((ML_HEADS, ML_HEADDIM, ML_HEADDIM), F32),
                        pltpu.VMEM((8, ML_HEADDIM), F32),
                        pltpu.VMEM((8, LANES), F32)],
        compiler_params=pltpu.CompilerParams(
            dimension_semantics=("arbitrary", "arbitrary"), vmem_limit_bytes=VMEM_LIMIT),
        name="mlstm_rev" if rev else "mlstm_fwd",
    )(*args)


def _out_kernel(x_ref, ys_ref, ym_ref, wo_ref, g1_ref, sh_ref, sc_ref, nw_ref, rwh_ref, rwl_ref, rb_ref,
                x1s_ref, ts_ref, route_ref):
    mix = _dot(ys_ref[...], wo_ref[0:SSD_WIDTH, :]) + _dot(ym_ref[...], wo_ref[SSD_WIDTH:, :])
    x1 = x_ref[...] + g1_ref[...] * mix
    y = x1 * lax.rsqrt(jnp.mean(x1 * x1, axis=-1, keepdims=True) + EPS) * nw_ref[...]
    t = y * (1.0 + sc_ref[...]) + sh_ref[...]
    for j in range(SLAB):
        x1s_ref[pl.ds(j, TM_IN, stride=SLAB), :] = x1[:, j * LANES:(j + 1) * LANES]
        ts_ref[pl.ds(j, TM_IN, stride=SLAB), :] = t[:, j * LANES:(j + 1) * LANES]
    t_hi = t.astype(BF16)
    t_lo = (t - t_hi.astype(F32)).astype(BF16)
    lg = _dot(t_hi, rwh_ref[...]) + _dot(t_lo, rwh_ref[...]) + _dot(t_hi, rwl_ref[...]) + rb_ref[...]

    lane = lax.broadcasted_iota(jnp.int32, lg.shape, 1).astype(F32)
    gmask = lane < MOE_GROUPS
    gl = jnp.where(gmask, lg, NEG_BIG)
    gmax = jnp.max(gl, axis=1, keepdims=True)
    g_sel = jnp.min(jnp.where(gmask & (gl == gmax), lane, 1e9), axis=1, keepdims=True)
    p_group = 1.0 / jnp.sum(jnp.where(gmask, jnp.exp(gl - gmax), 0.0), axis=1, keepdims=True)
    lo = ROUTE_LANE0 + MOE_EPG * g_sel
    emask = (lane >= lo) & (lane < lo + MOE_EPG)
    l1 = jnp.max(jnp.where(emask, lg, NEG_BIG), axis=1, keepdims=True)
    i1 = jnp.min(jnp.where(emask & (lg == l1), lane, 1e9), axis=1, keepdims=True)
    emask2 = emask & (lane != i1)
    l2 = jnp.max(jnp.where(emask2, lg, NEG_BIG), axis=1, keepdims=True)
    i2 = jnp.min(jnp.where(emask2 & (lg == l2), lane, 1e9), axis=1, keepdims=True)
    r = jnp.exp(l2 - l1)
    w1 = p_group / (1.0 + r)
    w2 = p_group * r / (1.0 + r)
    route_ref[...] = (jnp.where(lane == 0, i1, 0.0) + jnp.where(lane == 1, i2, 0.0)
                      + jnp.where(lane == 2, w1, 0.0) + jnp.where(lane == 3, w2, 0.0))


def _out_proj(x, y_ssd, y_ml, w_out_b, mod3, norm_w, rw_hi, rw_lo, rb):
    b, seq, d = x.shape
    nt = seq // TM_IN

    def row(j):
        return pl.BlockSpec((None, 1, d), lambda bi, t: (bi * 6 + j, 0, 0))

    def tile(width):
        return pl.BlockSpec((None, TM_IN, width), lambda bi, t: (bi, t, 0))

    return pl.pallas_call(
        _out_kernel,
        grid=(b, nt),
        in_specs=[tile(d), tile(SSD_WIDTH), tile(ML_WIDTH), _full(w_out_b.shape),
                  row(2), row(3), row(4), _full((1, d)), _full(rw_hi.shape), _full(rw_lo.shape), _full(rb.shape)],
        out_specs=[pl.BlockSpec((None, TM_IN * SLAB, LANES), lambda bi, t: (bi, t, 0)),
                   pl.BlockSpec((None, TM_IN * SLAB, LANES), lambda bi, t: (bi, t, 0)), tile(LANES)],
        out_shape=(jax.ShapeDtypeStruct((b, seq * SLAB, LANES), F32),
                   jax.ShapeDtypeStruct((b, seq * SLAB, LANES), F32),
                   jax.ShapeDtypeStruct((b, seq, LANES), F32)),
        compiler_params=pltpu.CompilerParams(
            dimension_semantics=("arbitrary", "arbitrary"), vmem_limit_bytes=VMEM_LIMIT),
        name="out_proj_router",
    )(x, y_ssd, y_ml, w_out_b, mod3, mod3, mod3, norm_w.reshape(1, d), rw_hi, rw_lo, rb)


def _route_tables(route, seq):
    b = route.shape[0]
    n_inst = 2 * seq
    nblk = n_inst // BM + MOE_EXPERTS
    e_flat = (route[..., 0:2].astype(jnp.int32) - ROUTE_LANE0).reshape(b, n_inst)
    w_flat = route[..., 2:4].reshape(b, n_inst)
    order = jnp.argsort(e_flat, axis=1, stable=True)
    counts = jnp.sum(e_flat[:, :, None] == jnp.arange(MOE_EXPERTS)[None, None, :], axis=1).astype(jnp.int32)
    nblk_e = (counts + BM - 1) // BM
    blk_end = jnp.cumsum(nblk_e, axis=1)
    blk_start = blk_end - nblk_e
    cnt_start = jnp.cumsum(counts, axis=1) - counts
    nb = blk_end[:, -1:]
    j = jnp.arange(nblk)[None, :]
    valid_blk = j < nb
    jj = jnp.minimum(j, nb - 1)
    e_j = jnp.sum(jj[:, :, None] >= blk_end[:, None, :], axis=2).astype(jnp.int32)
    take = lambda tbl: jnp.take_along_axis(tbl, e_j, axis=1)
    r = jnp.arange(BM)[None, None, :]
    rank = ((jj - take(blk_start)) * BM)[:, :, None] + r
    valid = valid_blk[:, :, None] & (rank < take(counts)[:, :, None])
    sidx = jnp.clip(take(cnt_start)[:, :, None] + rank, 0, n_inst - 1).reshape(b, nblk * BM)
    inst = jnp.take_along_axis(order, sidx, axis=1)
    valid = valid.reshape(b, nblk * BM)
    tok = jnp.where(valid, inst // 2, seq + jnp.tile(jnp.arange(BM), nblk)[None, :]).astype(jnp.int32)
    wslot = jnp.where(valid, jnp.take_along_axis(w_flat, inst, axis=1), 0.0)
    return (e_j.reshape(-1), valid_blk.astype(jnp.int32).reshape(-1),
            tok.reshape(b, 1, nblk * BM), wslot.reshape(b, nblk * BM, 1))


def _moe_kernel(seq, nblk, be_ref, bv_ref, tok_ref, ws_ref, wg_ref, wu_ref, wd_ref, g2_ref, fw_ref,
                t_hbm, x1_hbm, o_hbm, t_scr, y_scr, wgb, wub, wdb, xt, ot, stage, sems):
    b = pl.program_id(0)
    j = pl.program_id(1)
    idx = b * nblk + j
    rows = seq * SLAB

    @pl.when(j == 0)
    def _():
        cp_t = pltpu.make_async_copy(t_hbm.at[b], t_scr.at[pl.ds(0, rows)], sems.at[0])
        cp_x = pltpu.make_async_copy(x1_hbm.at[b], y_scr.at[pl.ds(0, rows)], sems.at[1])
        cp_t.start()
        cp_x.start()
        t_scr[pl.ds(rows, BM * SLAB), :] = jnp.zeros((BM * SLAB, LANES), F32)
        y_scr[pl.ds(rows, BM * SLAB), :] = jnp.zeros((BM * SLAB, LANES), F32)
        cp_t.wait()
        cp_x.wait()

    valid = bv_ref[idx] == 1
    changed = (j == 0) | (be_ref[idx] != be_ref[jnp.maximum(idx - 1, 0)])

    @pl.when(valid & changed)
    def _():
        wgb[...] = wg_ref[...].astype(BF16)
        wub[...] = wu_ref[...].astype(BF16)
        wdb[...] = wd_ref[...].astype(BF16)

    @pl.when(valid)
    def _():
        base = j * BM

        def slab_rows(r):
            return pl.ds(pl.multiple_of(tok_ref[0, base + r] * SLAB, SLAB), SLAB)

        for r in range(BM):
            xt[pl.ds(r, SLAB, stride=TILE_PITCH), :] = t_scr[slab_rows(r), :]
        x = jnp.concatenate([xt[c * TILE_PITCH:c * TILE_PITCH + BM, :] for c in range(SLAB)], axis=1).astype(BF16)
        hidden = _silu(_dot(x, wgb[...])) * _dot(x, wub[...]) * ws_ref[...]
        out = _dot(hidden.astype(BF16), wdb[...]) * g2_ref[...]
        for c in range(SLAB):
            ot[c * TILE_PITCH:c * TILE_PITCH + BM, :] = out[:, c * LANES:(c + 1) * LANES]
        for r0 in range(0, BM, SCATTER_GROUP):
            sl = [slab_rows(r0 + u) for u in range(SCATTER_GROUP)]
            vals = [y_scr[sl[u], :] + ot[pl.ds(r0 + u, SLAB, stride=TILE_PITCH), :] for u in range(SCATTER_GROUP)]
            for u in range(SCATTER_GROUP):
                y_scr[sl[u], :] = vals[u]

    @pl.when(j == nblk - 1)
    def _():
        def body(s, carry):
            base = pl.multiple_of(s * (TM_FIN * SLAB), TM_FIN * SLAB)
            chunks = [y_scr[pl.ds(base + c, TM_FIN, stride=SLAB), :] for c in range(SLAB)]
            ssq = chunks[0] * chunks[0]
            for c in range(1, SLAB):
                ssq = ssq + chunks[c] * chunks[c]
            inv = lax.rsqrt(jnp.sum(ssq, axis=1, keepdims=True) * (1.0 / D_MODEL) + EPS)
            for c in range(SLAB):
                stage[:, c * LANES:(c + 1) * LANES] = chunks[c] * inv * fw_ref[:, c * LANES:(c + 1) * LANES]
            cp = pltpu.make_async_copy(stage, o_hbm.at[b, pl.ds(s * TM_FIN, TM_FIN)], sems.at[2])
            cp.start()
            cp.wait()
            return carry

        lax.fori_loop(0, seq // TM_FIN, body, 0)


def _moe(t_slab, x1_slab, route, wg, wu, wd, mod3, final_w):
    b = t_slab.shape[0]
    seq = t_slab.shape[1] // SLAB
    d = D_MODEL
    blk_e, blk_valid, tok, wslot = _route_tables(route, seq)
    nblk = blk_e.shape[0] // b

    def w_spec(shape):
        return pl.BlockSpec((None,) + shape, lambda bi, j, be, bv: (be[bi * nblk + j], 0, 0))

    grid_spec = pltpu.PrefetchScalarGridSpec(
        num_scalar_prefetch=2,
        grid=(b, nblk),
        in_specs=[
            pl.BlockSpec((None, 1, nblk * BM), lambda bi, j, be, bv: (bi, 0, 0), memory_space=pltpu.SMEM),
            pl.BlockSpec((None, BM, 1), lambda bi, j, be, bv: (bi, j, 0)),
            w_spec((d, D_EXPERT)), w_spec((d, D_EXPERT)), w_spec((D_EXPERT, d)),
            pl.BlockSpec((None, 1, d), lambda bi, j, be, bv: (bi * 6 + 5, 0, 0)),
            pl.BlockSpec((1, d), lambda bi, j, be, bv: (0, 0)),
            pl.BlockSpec(memory_space=pl.ANY),
            pl.BlockSpec(memory_space=pl.ANY),
        ],
        out_specs=pl.BlockSpec(memory_space=pl.ANY),
        scratch_shapes=[
            pltpu.VMEM(((seq + BM) * SLAB, LANES), F32),
            pltpu.VMEM(((seq + BM) * SLAB, LANES), F32),
            pltpu.VMEM((d, D_EXPERT), BF16),
            pltpu.VMEM((d, D_EXPERT), BF16),
            pltpu.VMEM((D_EXPERT, d), BF16),
            pltpu.VMEM((SLAB * TILE_PITCH, LANES), F32),
            pltpu.VMEM((SLAB * TILE_PITCH, LANES), F32),
            pltpu.VMEM((TM_FIN, d), F32),
            pltpu.SemaphoreType.DMA((3,)),
        ],
    )
    return pl.pallas_call(
        functools.partial(_moe_kernel, seq, nblk),
        grid_spec=grid_spec,
        out_shape=jax.ShapeDtypeStruct((b, seq, d), F32),
        compiler_params=pltpu.CompilerParams(
            dimension_semantics=("arbitrary", "arbitrary"), vmem_limit_bytes=VMEM_LIMIT_MOE),
        name="experts_final_norm",
    )(blk_e, blk_valid, tok, wslot, wg, wu, wd, mod3, final_w.reshape(1, d), t_slab, x1_slab)


def _lane_pad(v, offset=0):
    v = v.reshape(-1).astype(F32)
    return jnp.zeros((1, LANES), F32).at[0, offset:offset + v.shape[0]].set(v)


def _to_col_major(t, rows):
    b, length, ch = t.shape
    return t.reshape(b, rows, GRID_W, ch).transpose(0, 2, 1, 3).reshape(b, length, ch)


def _from_col_major(t, rows):
    b, length, ch = t.shape
    return t.reshape(b, GRID_W, rows, ch).transpose(0, 2, 1, 3).reshape(b, length, ch)


def kernel(x, c, ctx, c_ctx, w_mod, b_mod, norm1_w, w_in, ssd_conv_w, ssd_conv_b, ssd_dt_bias, ssd_a_log, ssd_d, ssd_norm_w, ml_conv_w, ml_conv_b, ml_w_qk, ml_gate_b, ml_norm_w, ml_skip, w_out, norm2_w, moe_rg_w, moe_rg_b, moe_re_w, moe_re_b, moe_w_gate, moe_w_up, moe_w_down, final_norm_w):
    b, seq, d = x.shape
    ctx_len = ctx.shape[1]
    rows = seq // GRID_W
    ncc = ctx_len // CHUNK
    assert w_mod.shape[0] == 1 and d == D_MODEL and b + 1 <= 8
    assert seq % TM_IN == 0 and ctx_len % TM_IN == 0 and seq % TM_FIN == 0 and (2 * seq) % BM == 0

    c_all = jnp.zeros((8, d), F32).at[:b].set(c).at[b].set(c_ctx)
    mod = _modulation(c_all, w_mod[0], b_mod[0])
    mod3 = mod.reshape(8 * 6, 1, d)

    w = w_in[0]
    ssd_in = SSD_WIDTH + SSD_XBC + 2 * SSD_HEADS
    ml_main = 3 * ML_WIDTH
    n_gate = 2 * SSD_HEADS + 4 * ML_HEADS
    w_cat = jnp.concatenate([
        w[:, :SSD_WIDTH + SSD_XBC], w[:, ssd_in:ssd_in + ml_main],
        w[:, SSD_WIDTH + SSD_XBC:ssd_in], w[:, ssd_in + ml_main:],
        jnp.zeros((d, LANES - n_gate), F32)], axis=1).astype(BF16)
    z, xbc, ml, gates = _in_proj(x, ctx, mod3, norm1_w[0], w_cat)

    e_mats = []
    for direction in range(2):
        lane = jnp.arange(LANES)[:, None]
        head = (jnp.arange(SSD_WIDTH) // SSD_HEADDIM)[None, :]
        e_mats.append((lane == direction * SSD_HEADS + head).astype(F32))
    cw = jnp.zeros((8, SSD_XBC), F32).at[:CONV_W].set(ssd_conv_w[0])
    dtb = _lane_pad(ssd_dt_bias[0])
    alog = _lane_pad(ssd_a_log[0])
    ssd_params = [cw, ssd_conv_b[0].reshape(1, -1), dtb, dtb.reshape(LANES, 1), alog, alog.reshape(LANES, 1)]
    yb = _ssd_scan(True, ncc, xbc, gates, ssd_params + [e_mats[1]], None)
    dsk = jnp.repeat(ssd_d[0], SSD_HEADDIM).reshape(1, SSD_WIDTH)
    y_ssd = _ssd_scan(False, ncc, xbc, gates, ssd_params + [e_mats[0]],
                      (dsk, yb, z, ssd_norm_w[0].reshape(1, -1)))

    ml_cm = jnp.concatenate([ml[:, :ctx_len], _to_col_major(ml[:, ctx_len:], rows)], axis=1)
    g_cm = jnp.concatenate([gates[:, :ctx_len], _to_col_major(gates[:, ctx_len:], rows)], axis=1)
    eye = jnp.eye(ML_WIDTH // ML_QK_BLOCK, dtype=F32)
    w_bd = jnp.einsum("snjk,nm->snjmk", ml_w_qk[0], eye).reshape(2, ML_WIDTH, ML_WIDTH)
    wq = w_bd[0].astype(BF16)
    wk = (w_bd[1] * (ML_HEADDIM ** -0.5)).astype(BF16)
    mcw = jnp.zeros((8, ML_WIDTH), F32).at[:CONV_W].set(ml_conv_w[0])
    gb = _lane_pad(ml_gate_b[0], offset=2 * SSD_HEADS)
    ml_params = [mcw, ml_conv_b[0].reshape(1, -1), wq, wk, gb, gb.reshape(LANES, 1)]
    hb = _ml_scan(True, ncc, ml_cm, g_cm, ml_params, None)
    y_ml_cm = _ml_scan(False, ncc, ml_cm, g_cm, ml_params,
                       (hb, ml_norm_w[0].reshape(1, -1), ml_skip[0].reshape(1, -1)))
    y_ml = _from_col_major(y_ml_cm, rows)

    rw = jnp.concatenate([moe_rg_w[0], moe_re_w[0],
                          jnp.zeros((d, LANES - MOE_GROUPS - MOE_EXPERTS), F32)], axis=1)
    rw_hi = rw.astype(BF16)
    rw_lo = (rw - rw_hi.astype(F32)).astype(BF16)
    rb = _lane_pad(jnp.concatenate([moe_rg_b[0], moe_re_b[0]]))
    x1_slab, t_slab, route = _out_proj(x, y_ssd, y_ml, w_out[0].astype(BF16), mod3, norm2_w[0], rw_hi, rw_lo, rb)

    return _moe(t_slab, x1_slab, route, moe_w_gate[0], moe_w_up[0], moe_w_down[0], mod3, final_norm_w)
```

```python
import functools

import jax
import jax.numpy as jnp
from jax import lax
from jax.experimental import pallas as pl
from jax.experimental.pallas import tpu as pltpu

F32 = jnp.float32
BF16 = jnp.bfloat16
HIGHEST = lax.Precision.HIGHEST

D_MODEL = 1024
GRID_W = 64
EPS = 1e-6
CONV_W = 5
NEG_BIG = -1e30
CHUNK = 128
LANES = 128
HALO = 16

SSD_WIDTH = 512
SSD_HEADS = 8
SSD_HEADDIM = 64
SSD_GROUPS = 2
SSD_STATE = 128
SSD_XBC = SSD_WIDTH + 2 * SSD_GROUPS * SSD_STATE

ML_WIDTH = 512
ML_HEADS = 4
ML_HEADDIM = 128
ML_QK_BLOCK = 4

MOE_GROUPS = 4
MOE_EPG = 8
MOE_EXPERTS = 32
D_EXPERT = 256
ROUTE_LANE0 = MOE_GROUPS

TM_IN = 256
TM_FIN = 256
SLAB = D_MODEL // LANES
BM = 128
TILE_PITCH = BM + 8
SCATTER_GROUP = 4
VMEM_LIMIT = 48 * 1024 * 1024
VMEM_LIMIT_MOE = 56 * 1024 * 1024


def _silu(v):
    return v * jax.nn.sigmoid(v)


def _softplus(v):
    return jnp.maximum(v, 0.0) + jnp.log1p(jnp.exp(-jnp.abs(v)))


def _dot(a, b):
    return jnp.dot(a, b, preferred_element_type=F32)


def _dot_nt(a, b):
    return lax.dot_general(a, b, (((1,), (1,)), ((), ())), preferred_element_type=F32)


def _dot_hi(a, b):
    return jnp.dot(a, b, preferred_element_type=F32, precision=HIGHEST)


def _mod_kernel(c_ref, w_ref, b_ref, o_ref):
    c = c_ref[...]
    o_ref[...] = _dot_hi(_silu(c), w_ref[...]) + b_ref[...]


def _modulation(c_all, w_mod, b_mod):
    n = w_mod.shape[1]
    bn = 1536
    return pl.pallas_call(
        _mod_kernel,
        grid=(n // bn,),
        in_specs=[
            pl.BlockSpec((8, D_MODEL), lambda j: (0, 0)),
            pl.BlockSpec((D_MODEL, bn), lambda j: (0, j)),
            pl.BlockSpec((1, bn), lambda j: (0, j)),
        ],
        out_specs=pl.BlockSpec((8, bn), lambda j: (0, j)),
        out_shape=jax.ShapeDtypeStruct((8, n), F32),
        compiler_params=pltpu.CompilerParams(vmem_limit_bytes=VMEM_LIMIT),
        name="modulation",
    )(c_all, w_mod, b_mod.reshape(1, n))


def _in_kernel(nct, x_ref, ctx_ref, sh_ref, sc_ref, nw_ref, w_ref, z_ref, xbc_ref, ml_ref, g_ref):
    t = pl.program_id(1)
    xin = jnp.where(t < nct, ctx_ref[...], x_ref[...])
    ms = jnp.mean(xin * xin, axis=-1, keepdims=True)
    y = xin * lax.rsqrt(ms + EPS) * nw_ref[...]
    h = (y * (1.0 + sc_ref[...]) + sh_ref[...]).astype(BF16)
    z_ref[...] = _dot(h, w_ref[:, 0:512]).astype(BF16)
    for j in range(2):
        xbc_ref[:, j * 512:(j + 1) * 512] = _dot(h, w_ref[:, 512 + j * 512:1024 + j * 512]).astype(BF16)
    for j in range(3):
        ml_ref[:, j * 512:(j + 1) * 512] = _dot(h, w_ref[:, 1536 + j * 512:2048 + j * 512]).astype(BF16)
    g_ref[...] = _dot(h, w_ref[:, 3072:3200])


def _in_proj(x, ctx, mod3, norm_w, w_cat):
    b, seq, d = x.shape
    ctx_len = ctx.shape[1]
    nct, nlt = ctx_len // TM_IN, seq // TM_IN
    tot = ctx_len + seq
    n_ctx_row = b

    def mod_map(j):
        return lambda bi, t: (jnp.where(t < nct, n_ctx_row, bi) * 6 + j, 0, 0)

    out_shapes = (
        jax.ShapeDtypeStruct((b, tot, SSD_WIDTH), BF16),
        jax.ShapeDtypeStruct((b, tot, SSD_XBC), BF16),
        jax.ShapeDtypeStruct((b, tot, 3 * ML_WIDTH), BF16),
        jax.ShapeDtypeStruct((b, tot, LANES), F32),
    )

    def out_spec(width):
        return pl.BlockSpec((None, TM_IN, width), lambda bi, t: (bi, t, 0))

    return pl.pallas_call(
        functools.partial(_in_kernel, nct),
        grid=(b, nct + nlt),
        in_specs=[
            pl.BlockSpec((None, TM_IN, d), lambda bi, t: (bi, jnp.maximum(t - nct, 0), 0)),
            pl.BlockSpec((None, TM_IN, d), lambda bi, t: (bi, jnp.minimum(t, nct - 1), 0)),
            pl.BlockSpec((None, 1, d), mod_map(0)),
            pl.BlockSpec((None, 1, d), mod_map(1)),
            pl.BlockSpec((1, d), lambda bi, t: (0, 0)),
            pl.BlockSpec(w_cat.shape, lambda bi, t: (0, 0)),
        ],
        out_specs=[out_spec(SSD_WIDTH), out_spec(SSD_XBC), out_spec(3 * ML_WIDTH), out_spec(LANES)],
        out_shape=out_shapes,
        compiler_params=pltpu.CompilerParams(
            dimension_semantics=("arbitrary", "arbitrary"), vmem_limit_bytes=VMEM_LIMIT),
        name="in_proj",
    )(x, ctx, mod3, mod3, norm_w.reshape(1, d), w_cat)


def _chunk_eff(rev, ncc, nc, c):
    if not rev:
        return c
    return jnp.where(c < ncc, ncc - 1 - c, nc - 1 - (c - ncc))


def _lat_block(rev, ncc, nc, c):
    nl = nc - ncc
    if not rev:
        return jnp.maximum(c - ncc, 0)
    return jnp.where(c < ncc, nl - 1, nl - 1 - (c - ncc))


def _conv_silu(xm, xp_ref, xn_ref, cw_ref, cb_ref, first, last):
    xp = xp_ref[...].astype(F32)[HALO - 8:HALO] * jnp.where(first, 0.0, 1.0)
    xn = xn_ref[...].astype(F32)[0:8] * jnp.where(last, 0.0, 1.0)
    ext = jnp.concatenate([xp, xm, xn], axis=0)
    acc = cb_ref[...]
    for tap in range(CONV_W):
        off = 8 - CONV_W // 2 + tap
        acc = acc + cw_ref[tap:tap + 1, :] * ext[off:off + CHUNK]
    return _silu(acc)


def _tri(rev, transposed=False):
    row = lax.broadcasted_iota(jnp.int32, (CHUNK, CHUNK), 0)
    col = lax.broadcasted_iota(jnp.int32, (CHUNK, CHUNK), 1)
    if transposed:
        row, col = col, row
    keep = (col >= row) if rev else (col <= row)
    return keep


def _scan_specs(rev, ncc, nc, width_main, width_halo):
    ceff = functools.partial(_chunk_eff, rev, ncc, nc)
    blocks_per_chunk = CHUNK // HALO
    last_halo = nc * blocks_per_chunk - 1
    main = pl.BlockSpec((None, CHUNK, width_main), lambda bi, c: (bi, ceff(c), 0))
    prev = pl.BlockSpec((None, HALO, width_halo),
                        lambda bi, c: (bi, jnp.maximum(ceff(c) * blocks_per_chunk - 1, 0), 0))
    nxt = pl.BlockSpec((None, HALO, width_halo),
                       lambda bi, c: (bi, jnp.minimum((ceff(c) + 1) * blocks_per_chunk, last_halo), 0))
    gates = pl.BlockSpec((None, CHUNK, LANES), lambda bi, c: (bi, ceff(c), 0))
    return main, prev, nxt, gates


def _full(shape):
    return pl.BlockSpec(shape, lambda bi, c: (0,) * len(shape))


def _ssd_kernel(rev, ncc, nc, *refs):
    if rev:
        (xm_ref, xp_ref, xn_ref, g_ref, cw_ref, cb_ref, dtbr_ref, dtbc_ref, alr_ref, alc_ref, e_ref,
         o_ref, s_ref) = refs
    else:
        (xm_ref, xp_ref, xn_ref, g_ref, cw_ref, cb_ref, dtbr_ref, dtbc_ref, alr_ref, alc_ref, e_ref,
         dsk_ref, yb_ref, z_ref, nw_ref, o_ref, s_ref) = refs
    c = pl.program_id(1)
    ceff = _chunk_eff(rev, ncc, nc, c)
    first = (ceff == 0) | (ceff == ncc)
    last = (ceff == ncc - 1) | (ceff == nc - 1)

    @pl.when(c == 0)
    def _():
        s_ref[...] = jnp.zeros_like(s_ref)

    xbc = _conv_silu(xm_ref[...].astype(F32), xp_ref, xn_ref, cw_ref, cb_ref, first, last)
    xs = xbc[:, :SSD_WIDTH]

    lane0 = SSD_HEADS * int(rev)
    g = g_ref[...]
    lane = lax.broadcasted_iota(jnp.int32, (CHUNK, LANES), 1)
    lmask = (lane >= lane0) & (lane < lane0 + SSD_HEADS)
    dt = jnp.where(lmask, _softplus(g + dtbr_ref[...]), 0.0)
    a = dt * (-jnp.exp(alr_ref[...]))
    gt = g.T
    dt_t = _softplus(gt + dtbc_ref[...])[lane0:lane0 + SSD_HEADS]
    a_t = dt_t * (-jnp.exp(alc_ref[...][lane0:lane0 + SSD_HEADS]))

    cs = _dot_hi(_tri(rev).astype(F32), a)
    cs_t = _dot_hi(a_t, _tri(rev, transposed=True).astype(F32))
    e = e_ref[...]
    dtx = _dot_hi(dt, e)
    csx = _dot_hi(cs, e)
    end = 0 if rev else CHUNK - 1
    totx = csx[end:end + 1, :]
    ecsx = jnp.exp(csx)
    decx = jnp.exp(totx - csx)
    etotx = jnp.exp(totx)

    xdt = xs * dtx
    xdt_b = xdt.astype(BF16)
    xd_b = (xdt * decx).astype(BF16)
    keep = _tri(rev)
    half = lax.broadcasted_iota(jnp.int32, (CHUNK, LANES), 1) // SSD_HEADDIM

    y_blocks = []
    gs = SSD_GROUPS * SSD_STATE
    hpg = SSD_HEADS // SSD_GROUPS
    gw = hpg * SSD_HEADDIM
    for grp in range(SSD_GROUPS):
        bm = xbc[:, SSD_WIDTH + grp * SSD_STATE:SSD_WIDTH + (grp + 1) * SSD_STATE]
        cm = xbc[:, SSD_WIDTH + gs + grp * SSD_STATE:SSD_WIDTH + gs + (grp + 1) * SSD_STATE]
        bm_b, cm_b = bm.astype(BF16), cm.astype(BF16)
        cb = _dot_nt(cm_b, bm_b)
        s_old = s_ref[:, grp * gw:(grp + 1) * gw]
        y_off = _dot(cm_b, s_old.astype(BF16)) * ecsx[:, grp * gw:(grp + 1) * gw]
        for pair in range(hpg // 2):
            blk = grp * (hpg // 2) + pair
            xj = xdt_b[:, blk * LANES:(blk + 1) * LANES]
            acc = y_off[:, pair * LANES:(pair + 1) * LANES]
            for q in range(2):
                h = blk * 2 + q
                dl = cs[:, lane0 + h:lane0 + h + 1] - cs_t[h:h + 1, :]
                m = (cb * jnp.exp(jnp.where(keep, dl, NEG_BIG))).astype(BF16)
                acc = acc + _dot(m, jnp.where(half == q, xj, jnp.zeros_like(xj)))
            y_blocks.append(acc)
        s_new = s_old * etotx[:, grp * gw:(grp + 1) * gw] + _dot(bm.T.astype(BF16), xd_b[:, grp * gw:(grp + 1) * gw])
        s_ref[:, grp * gw:(grp + 1) * gw] = s_new

    y = jnp.concatenate(y_blocks, axis=1)
    if rev:
        o_ref[...] = y.astype(o_ref.dtype)
    else:
        y = y + yb_ref[...].astype(F32) + dsk_ref[...] * xs
        y = y * _silu(z_ref[...].astype(F32))
        y = y * lax.rsqrt(jnp.mean(y * y, axis=-1, keepdims=True) + EPS) * nw_ref[...]
        o_ref[...] = y.astype(o_ref.dtype)


def _ssd_scan(rev, ncc, xbc, gates, params, extra):
    b, tot, _ = xbc.shape
    nc = tot // CHUNK
    nl = nc - ncc
    main, prev, nxt, gspec = _scan_specs(rev, ncc, nc, SSD_XBC, SSD_XBC)
    in_specs = [main, prev, nxt, gspec] + [_full(p.shape) for p in params]
    args = [xbc, xbc, xbc, gates] + list(params)
    lat = functools.partial(_lat_block, rev, ncc, nc)
    out_spec = pl.BlockSpec((None, CHUNK, SSD_WIDTH), lambda bi, c: (bi, lat(c), 0))
    if not rev:
        dsk, yb, z, nw = extra
        ceff = functools.partial(_chunk_eff, rev, ncc, nc)
        in_specs += [_full(dsk.shape), out_spec,
                     pl.BlockSpec((None, CHUNK, SSD_WIDTH), lambda bi, c: (bi, ceff(c), 0)), _full(nw.shape)]
        args += [dsk, yb, z, nw]
    return pl.pallas_call(
        functools.partial(_ssd_kernel, rev, ncc, nc),
        grid=(b, nc),
        in_specs=in_specs,
        out_specs=out_spec,
        out_shape=jax.ShapeDtypeStruct((b, nl * CHUNK, SSD_WIDTH), BF16),
        scratch_shapes=[pltpu.VMEM((SSD_STATE, SSD_WIDTH), F32)],
        compiler_params=pltpu.CompilerParams(
            dimension_semantics=("arbitrary", "arbitrary"), vmem_limit_bytes=VMEM_LIMIT),
        name="ssd_rev" if rev else "ssd_fwd",
    )(*args)


def _ml_kernel(rev, ncc, nc, *refs):
    if rev:
        (mm_ref, mp_ref, mn_ref, g_ref, cw_ref, cb_ref, wq_ref, wk_ref, gbr_ref, gbc_ref,
         o_ref, c_ref, n_ref, mx_ref) = refs
    else:
        (mm_ref, mp_ref, mn_ref, g_ref, cw_ref, cb_ref, wq_ref, wk_ref, gbr_ref, gbc_ref,
         hb_ref, nw_ref, sk_ref, o_ref, c_ref, n_ref, mx_ref) = refs
    c = pl.program_id(1)
    ceff = _chunk_eff(rev, ncc, nc, c)
    first = (ceff == 0) | (ceff == ncc)
    last = (ceff == ncc - 1) | (ceff == nc - 1)

    @pl.when(c == 0)
    def _():
        c_ref[...] = jnp.zeros_like(c_ref)
        n_ref[...] = jnp.zeros_like(n_ref)
        mx_ref[...] = jnp.full(mx_ref.shape, NEG_BIG, F32)

    xconv = _conv_silu(mm_ref[:, 0:ML_WIDTH].astype(F32), mp_ref, mn_ref, cw_ref, cb_ref, first, last)
    xc_b = xconv.astype(BF16)
    q = _dot(xc_b, wq_ref[...])
    k = _dot(xc_b, wk_ref[...])

    i_lane0 = 16 + ML_HEADS * int(rev)
    f_lane0 = 24 + ML_HEADS * int(rev)
    ga = g_ref[...] + gbr_ref[...]
    lane = lax.broadcasted_iota(jnp.int32, (CHUNK, LANES), 1)
    logf = jnp.where((lane >= f_lane0) & (lane < f_lane0 + ML_HEADS), -_softplus(-ga), 0.0)
    cs = _dot_hi(_tri(rev).astype(F32), logf)
    gt = g_ref[...].T + gbc_ref[...]
    i_t = gt[16:24]
    logf_t = -_softplus(-gt[24:32])
    cs_t = _dot_hi(logf_t, _tri(rev, transposed=True).astype(F32))
    keep = _tri(rev)
    end = 0 if rev else CHUNK - 1

    outs = []
    for h in range(ML_HEADS):
        r = ML_HEADS * int(rev) + h
        sl = slice(h * ML_HEADDIM, (h + 1) * ML_HEADDIM)
        qh, kh = q[:, sl], k[:, sl]
        vh = mm_ref[:, ML_WIDTH + h * ML_HEADDIM:ML_WIDTH + (h + 1) * ML_HEADDIM]
        qh_b, kh_b = qh.astype(BF16), kh.astype(BF16)
        csc = cs[:, f_lane0 + h:f_lane0 + h + 1]
        csr = cs_t[r:r + 1, :]
        ic = ga[:, i_lane0 + h:i_lane0 + h + 1]
        ir = i_t[r:r + 1, :]
        tot = csr[:, end:end + 1]
        m_prev = mx_ref[h:h + 1, 0:1]
        c_prev = c_ref[h]
        n_prev = n_ref[h:h + 1, :]

        dlog = jnp.where(keep, csc - csr + ir, NEG_BIG)
        g_inter = csc + m_prev
        m_out = jnp.maximum(g_inter, jnp.max(dlog, axis=1, keepdims=True))
        scores = _dot_nt(qh_b, kh_b) * jnp.exp(dlog - m_out)
        w_inter = jnp.exp(g_inter - m_out)
        num = _dot(scores.astype(BF16), vh) + w_inter * _dot_nt(qh_b, c_prev.astype(BF16))
        den = jnp.sum(scores, axis=1, keepdims=True) + w_inter * jnp.sum(qh * n_prev, axis=1, keepdims=True)
        outs.append(num / jnp.maximum(jnp.abs(den), jnp.exp(-m_out)))

        a_r = tot - csr + ir
        m_loc = jnp.max(a_r, axis=1, keepdims=True)
        kw = kh * jnp.exp(tot - csc + ic - m_loc)
        c_loc = _dot(vh.astype(F32).T.astype(BF16), kw.astype(BF16))
        n_loc = jnp.sum(kw, axis=0, keepdims=True)
        m_new = jnp.maximum(tot + m_prev, m_loc)
        s_prev = jnp.exp(tot + m_prev - m_new)
        s_loc = jnp.exp(m_loc - m_new)
        c_ref[h] = s_prev * c_prev + s_loc * c_loc
        n_ref[h:h + 1, :] = s_prev * n_prev + s_loc * n_loc
        mx_ref[h:h + 1, :] = jnp.broadcast_to(m_new, (1, LANES))

    if rev:
        o_ref[...] = jnp.concatenate(outs, axis=1).astype(o_ref.dtype)
    else:
        normed = []
        for h in range(ML_HEADS):
            sl = slice(h * ML_HEADDIM, (h + 1) * ML_HEADDIM)
            o_gate = mm_ref[:, 2 * ML_WIDTH + h * ML_HEADDIM:2 * ML_WIDTH + (h + 1) * ML_HEADDIM].astype(F32)
            hh = jax.nn.sigmoid(o_gate) * (outs[h] + hb_ref[:, sl].astype(F32))
            normed.append(hh * lax.rsqrt(jnp.mean(hh * hh, axis=-1, keepdims=True) + EPS))
        y = jnp.concatenate(normed, axis=1) * nw_ref[...] + sk_ref[...] * xconv
        o_ref[...] = y.astype(o_ref.dtype)


def _ml_scan(rev, ncc, ml, gates, params, extra):
    b, tot, _ = ml.shape
    nc = tot // CHUNK
    nl = nc - ncc
    main, prev, nxt, gspec = _scan_specs(rev, ncc, nc, 3 * ML_WIDTH, ML_WIDTH)
    in_specs = [main, prev, nxt, gspec] + [_full(p.shape) for p in params]
    args = [ml, ml, ml, gates] + list(params)
    lat = functools.partial(_lat_block, rev, ncc, nc)
    out_spec = pl.BlockSpec((None, CHUNK, ML_WIDTH), lambda bi, c: (bi, lat(c), 0))
    if not rev:
        hb, nw, sk = extra
        in_specs += [out_spec, _full(nw.shape), _full(sk.shape)]
        args += [hb, nw, sk]
    return pl.pallas_call(
        functools.partial(_ml_kernel, rev, ncc, nc),
        grid=(b, nc),
        in_specs=in_specs,
        out_specs=out_spec,
        out_shape=jax.ShapeDtypeStruct((b, nl * CHUNK, ML_WIDTH), BF16),
        scratch_shapes=[pltpu.VMEM((ML_HEADS, ML_HEADDIM, ML_HEADDIM), F32),
                        pltpu.VMEM((8, ML_HEADDIM), F32),
                        pltpu.VMEM((8, LANES), F32)],
        compiler_params=pltpu.CompilerParams(
            dimension_semantics=("arbitrary", "arbitrary"), vmem_limit_bytes=VMEM_LIMIT),
        name="mlstm_rev" if rev else "mlstm_fwd",
    )(*args)


def _out_kernel(x_ref, ys_ref, ym_ref, wo_ref, g1_ref, sh_ref, sc_ref, nw_ref, rwh_ref, rwl_ref, rb_ref,
                x1s_ref, ts_ref, route_ref):
    mix = _dot(ys_ref[...], wo_ref[0:SSD_WIDTH, :]) + _dot(ym_ref[...], wo_ref[SSD_WIDTH:, :])
    x1 = x_ref[...] + g1_ref[...] * mix
    y = x1 * lax.rsqrt(jnp.mean(x1 * x1, axis=-1, keepdims=True) + EPS) * nw_ref[...]
    t = y * (1.0 + sc_ref[...]) + sh_ref[...]
    for j in range(SLAB):
        x1s_ref[pl.ds(j, TM_IN, stride=SLAB), :] = x1[:, j * LANES:(j + 1) * LANES]
        ts_ref[pl.ds(j, TM_IN, stride=SLAB), :] = t[:, j * LANES:(j + 1) * LANES]
    t_hi = t.astype(BF16)
    t_lo = (t - t_hi.astype(F32)).astype(BF16)
    lg = _dot(t_hi, rwh_ref[...]) + _dot(t_lo, rwh_ref[...]) + _dot(t_hi, rwl_ref[...]) + rb_ref[...]

    lane = lax.broadcasted_iota(jnp.int32, lg.shape, 1).astype(F32)
    gmask = lane < MOE_GROUPS
    gl = jnp.where(gmask, lg, NEG_BIG)
    gmax = jnp.max(gl, axis=1, keepdims=True)
    g_sel = jnp.min(jnp.where(gmask & (gl == gmax), lane, 1e9), axis=1, keepdims=True)
    p_group = 1.0 / jnp.sum(jnp.where(gmask, jnp.exp(gl - gmax), 0.0), axis=1, keepdims=True)
    lo = ROUTE_LANE0 + MOE_EPG * g_sel
    emask = (lane >= lo) & (lane < lo + MOE_EPG)
    l1 = jnp.max(jnp.where(emask, lg, NEG_BIG), axis=1, keepdims=True)
    i1 = jnp.min(jnp.where(emask & (lg == l1), lane, 1e9), axis=1, keepdims=True)
    emask2 = emask & (lane != i1)
    l2 = jnp.max(jnp.where(emask2, lg, NEG_BIG), axis=1, keepdims=True)
    i2 = jnp.min(jnp.where(emask2 & (lg == l2), lane, 1e9), axis=1, keepdims=True)
    r = jnp.exp(l2 - l1)
    w1 = p_group / (1.0 + r)
    w2 = p_group * r / (1.0 + r)
    route = (jnp.where(lane == 0, i1, 0.0) + jnp.where(lane == 1, i2, 0.0)
             + jnp.where(lane == 2, w1, 0.0) + jnp.where(lane == 3, w2, 0.0))
    route_ref[...] = route.T[0:8, :]


def _out_proj(x, y_ssd, y_ml, w_out_b, mod3, norm_w, rw_hi, rw_lo, rb):
    b, seq, d = x.shape
    nt = seq // TM_IN

    def row(j):
        return pl.BlockSpec((None, 1, d), lambda bi, t: (bi * 6 + j, 0, 0))

    def tile(width):
        return pl.BlockSpec((None, TM_IN, width), lambda bi, t: (bi, t, 0))

    return pl.pallas_call(
        _out_kernel,
        grid=(b, nt),
        in_specs=[tile(d), tile(SSD_WIDTH), tile(ML_WIDTH), _full(w_out_b.shape),
                  row(2), row(3), row(4), _full((1, d)), _full(rw_hi.shape), _full(rw_lo.shape), _full(rb.shape)],
        out_specs=[pl.BlockSpec((None, TM_IN * SLAB, LANES), lambda bi, t: (bi, t, 0)),
                   pl.BlockSpec((None, TM_IN * SLAB, LANES), lambda bi, t: (bi, t, 0)),
                   pl.BlockSpec((None, 8, TM_IN), lambda bi, t: (bi, 0, t))],
        out_shape=(jax.ShapeDtypeStruct((b, seq * SLAB, LANES), F32),
                   jax.ShapeDtypeStruct((b, seq * SLAB, LANES), F32),
                   jax.ShapeDtypeStruct((b, 8, seq), F32)),
        compiler_params=pltpu.CompilerParams(
            dimension_semantics=("arbitrary", "arbitrary"), vmem_limit_bytes=VMEM_LIMIT),
        name="out_proj_router",
    )(x, y_ssd, y_ml, w_out_b, mod3, mod3, mod3, norm_w.reshape(1, d), rw_hi, rw_lo, rb)


def _route_tables(route, seq):
    b = route.shape[0]
    n_inst = 2 * seq
    nblk = n_inst // BM + MOE_EXPERTS
    e_flat = (route[:, 0:2, :].astype(jnp.int32) - ROUTE_LANE0).reshape(b, n_inst)
    w_flat = route[:, 2:4, :].reshape(b, n_inst)
    tok_flat = jnp.tile(jnp.arange(seq, dtype=jnp.int32), 2 * b).reshape(b, n_inst)
    _, tok_sorted, w_sorted = lax.sort((e_flat, tok_flat, w_flat), dimension=1, num_keys=1)
    experts = jnp.arange(MOE_EXPERTS, dtype=jnp.int32)
    counts = jnp.sum((e_flat[:, None, :] == experts[None, :, None]).astype(jnp.int32), axis=2)
    nblk_e = (counts + BM - 1) // BM
    blk_end = jnp.cumsum(nblk_e, axis=1)
    blk_start = blk_end - nblk_e
    cnt_start = jnp.cumsum(counts, axis=1) - counts
    nb = blk_end[:, -1:]
    j = jnp.arange(nblk, dtype=jnp.int32)[None, :]
    valid_blk = j < nb
    jj = jnp.minimum(j, nb - 1)
    e_j = jnp.sum((jj[:, :, None] >= blk_end[:, None, :]).astype(jnp.int32), axis=2)
    onehot = (e_j[:, :, None] == experts[None, None, :]).astype(jnp.int32)
    take = lambda tbl: jnp.sum(onehot * tbl[:, None, :], axis=2)
    r = jnp.arange(BM, dtype=jnp.int32)[None, None, :]
    rank = ((jj - take(blk_start)) * BM)[:, :, None] + r
    valid = valid_blk[:, :, None] & (rank < take(counts)[:, :, None])
    sidx = jnp.clip(take(cnt_start)[:, :, None] + rank, 0, n_inst - 1).reshape(b, nblk * BM)
    tok = jnp.take_along_axis(tok_sorted, sidx, axis=1).reshape(b, nblk, BM)
    wslot = jnp.take_along_axis(w_sorted, sidx, axis=1).reshape(b, nblk, BM)
    tok = jnp.where(valid, tok, seq + r)
    wslot = jnp.where(valid, wslot, 0.0)
    return (e_j.reshape(-1), valid_blk.astype(jnp.int32).reshape(-1),
            tok.reshape(b * nblk, 1, BM), wslot.reshape(b * nblk, 1, BM))


def _moe_kernel(seq, nblk, be_ref, bv_ref, tok_ref, ws_ref, wg_ref, wu_ref, wd_ref, g2_ref, fw_ref,
                t_hbm, x1_hbm, o_hbm, t_scr, y_scr, wgb, wub, wdb, xt, ot, stage, sems):
    b = pl.program_id(0)
    j = pl.program_id(1)
    idx = b * nblk + j
    rows = seq * SLAB

    @pl.when(j == 0)
    def _():
        cp_t = pltpu.make_async_copy(t_hbm.at[b], t_scr.at[pl.ds(0, rows)], sems.at[0])
        cp_x = pltpu.make_async_copy(x1_hbm.at[b], y_scr.at[pl.ds(0, rows)], sems.at[1])
        cp_t.start()
        cp_x.start()
        t_scr[pl.ds(rows, BM * SLAB), :] = jnp.zeros((BM * SLAB, LANES), F32)
        y_scr[pl.ds(rows, BM * SLAB), :] = jnp.zeros((BM * SLAB, LANES), F32)
        cp_t.wait()
        cp_x.wait()

    valid = bv_ref[idx] == 1
    changed = (j == 0) | (be_ref[idx] != be_ref[jnp.maximum(idx - 1, 0)])

    @pl.when(valid & changed)
    def _():
        wgb[...] = wg_ref[...].astype(BF16)
        wub[...] = wu_ref[...].astype(BF16)
        wdb[...] = wd_ref[...].astype(BF16)

    @pl.when(valid)
    def _():
        def slab_rows(r):
            return pl.ds(pl.multiple_of(tok_ref[0, r] * SLAB, SLAB), SLAB)

        for r in range(BM):
            xt[pl.ds(r, SLAB, stride=TILE_PITCH), :] = t_scr[slab_rows(r), :]
        x = jnp.concatenate([xt[c * TILE_PITCH:c * TILE_PITCH + BM, :] for c in range(SLAB)], axis=1).astype(BF16)
        hidden = _silu(_dot(x, wgb[...])) * _dot(x, wub[...])
        out = _dot(hidden.astype(BF16), wdb[...]) * g2_ref[...]
        for c in range(SLAB):
            ot[c * TILE_PITCH:c * TILE_PITCH + BM, :] = out[:, c * LANES:(c + 1) * LANES]
        for r0 in range(0, BM, SCATTER_GROUP):
            sl = [slab_rows(r0 + u) for u in range(SCATTER_GROUP)]
            vals = [y_scr[sl[u], :] + ws_ref[0, r0 + u] * ot[pl.ds(r0 + u, SLAB, stride=TILE_PITCH), :]
                    for u in range(SCATTER_GROUP)]
            for u in range(SCATTER_GROUP):
                y_scr[sl[u], :] = vals[u]

    @pl.when(j == nblk - 1)
    def _():
        def body(s, carry):
            base = pl.multiple_of(s * (TM_FIN * SLAB), TM_FIN * SLAB)
            chunks = [y_scr[pl.ds(base + c, TM_FIN, stride=SLAB), :] for c in range(SLAB)]
            ssq = chunks[0] * chunks[0]
            for c in range(1, SLAB):
                ssq = ssq + chunks[c] * chunks[c]
            inv = lax.rsqrt(jnp.sum(ssq, axis=1, keepdims=True) * (1.0 / D_MODEL) + EPS)
            for c in range(SLAB):
                stage[:, c * LANES:(c + 1) * LANES] = chunks[c] * inv * fw_ref[:, c * LANES:(c + 1) * LANES]
            cp = pltpu.make_async_copy(stage, o_hbm.at[b, pl.ds(s * TM_FIN, TM_FIN)], sems.at[2])
            cp.start()
            cp.wait()
            return carry

        lax.fori_loop(0, seq // TM_FIN, body, 0)


def _moe(t_slab, x1_slab, route, wg, wu, wd, mod3, final_w):
    b = t_slab.shape[0]
    seq = t_slab.shape[1] // SLAB
    d = D_MODEL
    blk_e, blk_valid, tok, wslot = _route_tables(route, seq)
    nblk = blk_e.shape[0] // b

    def w_spec(shape):
        return pl.BlockSpec((None,) + shape, lambda bi, j, be, bv: (be[bi * nblk + j], 0, 0))

    grid_spec = pltpu.PrefetchScalarGridSpec(
        num_scalar_prefetch=2,
        grid=(b, nblk),
        in_specs=[
            pl.BlockSpec((None, 1, BM), lambda bi, j, be, bv: (bi * nblk + j, 0, 0), memory_space=pltpu.SMEM),
            pl.BlockSpec((None, 1, BM), lambda bi, j, be, bv: (bi * nblk + j, 0, 0), memory_space=pltpu.SMEM),
            w_spec((d, D_EXPERT)), w_spec((d, D_EXPERT)), w_spec((D_EXPERT, d)),
            pl.BlockSpec((None, 1, d), lambda bi, j, be, bv: (bi * 6 + 5, 0, 0)),
            pl.BlockSpec((1, d), lambda bi, j, be, bv: (0, 0)),
            pl.BlockSpec(memory_space=pl.ANY),
            pl.BlockSpec(memory_space=pl.ANY),
        ],
        out_specs=pl.BlockSpec(memory_space=pl.ANY),
        scratch_shapes=[
            pltpu.VMEM(((seq + BM) * SLAB, LANES), F32),
            pltpu.VMEM(((seq + BM) * SLAB, LANES), F32),
            pltpu.VMEM((d, D_EXPERT), BF16),
            pltpu.VMEM((d, D_EXPERT), BF16),
            pltpu.VMEM((D_EXPERT, d), BF16),
            pltpu.VMEM((SLAB * TILE_PITCH, LANES), F32),
            pltpu.VMEM((SLAB * TILE_PITCH, LANES), F32),
            pltpu.VMEM((TM_FIN, d), F32),
            pltpu.SemaphoreType.DMA((3,)),
        ],
    )
    return pl.pallas_call(
        functools.partial(_moe_kernel, seq, nblk),
        grid_spec=grid_spec,
        out_shape=jax.ShapeDtypeStruct((b, seq, d), F32),
        compiler_params=pltpu.CompilerParams(
            dimension_semantics=("arbitrary", "arbitrary"), vmem_limit_bytes=VMEM_LIMIT_MOE),
        name="experts_final_norm",
    )(blk_e, blk_valid, tok, wslot, wg, wu, wd, mod3, final_w.reshape(1, d), t_slab, x1_slab)


def _lane_pad(v, offset=0):
    v = v.reshape(-1).astype(F32)
    return jnp.zeros((1, LANES), F32).at[0, offset:offset + v.shape[0]].set(v)


def _to_col_major(t, rows):
    b, length, ch = t.shape
    return t.reshape(b, rows, GRID_W, ch).transpose(0, 2, 1, 3).reshape(b, length, ch)


def _from_col_major(t, rows):
    b, length, ch = t.shape
    return t.reshape(b, GRID_W, rows, ch).transpose(0, 2, 1, 3).reshape(b, length, ch)


def kernel(x, c, ctx, c_ctx, w_mod, b_mod, norm1_w, w_in, ssd_conv_w, ssd_conv_b, ssd_dt_bias, ssd_a_log, ssd_d, ssd_norm_w, ml_conv_w, ml_conv_b, ml_w_qk, ml_gate_b, ml_norm_w, ml_skip, w_out, norm2_w, moe_rg_w, moe_rg_b, moe_re_w, moe_re_b, moe_w_gate, moe_w_up, moe_w_down, final_norm_w):
    b, seq, d = x.shape
    ctx_len = ctx.shape[1]
    rows = seq // GRID_W
    ncc = ctx_len // CHUNK
    assert w_mod.shape[0] == 1 and d == D_MODEL and b + 1 <= 8
    assert seq % TM_IN == 0 and ctx_len % TM_IN == 0 and seq % TM_FIN == 0 and (2 * seq) % BM == 0

    c_all = jnp.zeros((8, d), F32).at[:b].set(c).at[b].set(c_ctx)
    mod = _modulation(c_all, w_mod[0], b_mod[0])
    mod3 = mod.reshape(8 * 6, 1, d)

    w = w_in[0]
    ssd_in = SSD_WIDTH + SSD_XBC + 2 * SSD_HEADS
    ml_main = 3 * ML_WIDTH
    n_gate = 2 * SSD_HEADS + 4 * ML_HEADS
    w_cat = jnp.concatenate([
        w[:, :SSD_WIDTH + SSD_XBC], w[:, ssd_in:ssd_in + ml_main],
        w[:, SSD_WIDTH + SSD_XBC:ssd_in], w[:, ssd_in + ml_main:],
        jnp.zeros((d, LANES - n_gate), F32)], axis=1).astype(BF16)
    z, xbc, ml, gates = _in_proj(x, ctx, mod3, norm1_w[0], w_cat)

    e_mats = []
    for direction in range(2):
        lane = jnp.arange(LANES)[:, None]
        head = (jnp.arange(SSD_WIDTH) // SSD_HEADDIM)[None, :]
        e_mats.append((lane == direction * SSD_HEADS + head).astype(F32))
    cw = jnp.zeros((8, SSD_XBC), F32).at[:CONV_W].set(ssd_conv_w[0])
    dtb = _lane_pad(ssd_dt_bias[0])
    alog = _lane_pad(ssd_a_log[0])
    ssd_params = [cw, ssd_conv_b[0].reshape(1, -1), dtb, dtb.reshape(LANES, 1), alog, alog.reshape(LANES, 1)]
    yb = _ssd_scan(True, ncc, xbc, gates, ssd_params + [e_mats[1]], None)
    dsk = jnp.repeat(ssd_d[0], SSD_HEADDIM).reshape(1, SSD_WIDTH)
    y_ssd = _ssd_scan(False, ncc, xbc, gates, ssd_params + [e_mats[0]],
                      (dsk, yb, z, ssd_norm_w[0].reshape(1, -1)))

    ml_cm = jnp.concatenate([ml[:, :ctx_len], _to_col_major(ml[:, ctx_len:], rows)], axis=1)
    g_cm = jnp.concatenate([gates[:, :ctx_len], _to_col_major(gates[:, ctx_len:], rows)], axis=1)
    w_rows = jnp.tile(ml_w_qk[0].reshape(2, ML_WIDTH, ML_QK_BLOCK), (1, 1, ML_WIDTH // ML_QK_BLOCK))
    blk_id = jnp.arange(ML_WIDTH) // ML_QK_BLOCK
    w_bd = jnp.where((blk_id[:, None] == blk_id[None, :])[None], w_rows, 0.0)
    wq = w_bd[0].astype(BF16)
    wk = (w_bd[1] * (ML_HEADDIM ** -0.5)).astype(BF16)
    mcw = jnp.zeros((8, ML_WIDTH), F32).at[:CONV_W].set(ml_conv_w[0])
    gb = _lane_pad(ml_gate_b[0], offset=2 * SSD_HEADS)
    ml_params = [mcw, ml_conv_b[0].reshape(1, -1), wq, wk, gb, gb.reshape(LANES, 1)]
    hb = _ml_scan(True, ncc, ml_cm, g_cm, ml_params, None)
    y_ml_cm = _ml_scan(False, ncc, ml_cm, g_cm, ml_params,
                       (hb, ml_norm_w[0].reshape(1, -1), ml_skip[0].reshape(1, -1)))
    y_ml = _from_col_major(y_ml_cm, rows)

    rw = jnp.concatenate([moe_rg_w[0], moe_re_w[0],
                          jnp.zeros((d, LANES - MOE_GROUPS - MOE_EXPERTS), F32)], axis=1)
    rw_hi = rw.astype(BF16)
    rw_lo = (rw - rw_hi.astype(F32)).astype(BF16)
    rb = _lane_pad(jnp.concatenate([moe_rg_b[0], moe_re_b[0]]))
    x1_slab, t_slab, route = _out_proj(x, y_ssd, y_ml, w_out[0].astype(BF16), mod3, norm2_w[0], rw_hi, rw_lo, rb)

    return _moe(t_slab, x1_slab, route, moe_w_gate[0], moe_w_up[0], moe_w_down[0], mod3, final_norm_w)
```

```python
import functools

import jax
import jax.numpy as jnp
from jax import lax
from jax.experimental import pallas as pl
from jax.experimental.pallas import tpu as pltpu

F32 = jnp.float32
BF16 = jnp.bfloat16
HIGHEST = lax.Precision.HIGHEST

D_MODEL = 1024
GRID_W = 64
EPS = 1e-6
CONV_W = 5
NEG_BIG = -1e30
CHUNK = 128
LANES = 128
HALO = 16
NB = 2

SSD_WIDTH = 512
SSD_HEADS = 8
SSD_HEADDIM = 64
SSD_GROUPS = 2
SSD_STATE = 128
SSD_XBC = SSD_WIDTH + 2 * SSD_GROUPS * SSD_STATE

ML_WIDTH = 512
ML_HEADS = 4
ML_HEADDIM = 128
ML_QK_BLOCK = 4

MOE_GROUPS = 4
MOE_EPG = 8
MOE_EXPERTS = 32
D_EXPERT = 256
ROUTE_LANE0 = MOE_GROUPS

TM_IN = 256
TM_FIN = 256
SLAB = D_MODEL // LANES
BM = 128
TILE_PITCH = BM + 8
SCATTER_GROUP = 4
VMEM_LIMIT = 48 * 1024 * 1024
VMEM_LIMIT_MOE = 56 * 1024 * 1024


def _silu(v):
    return v * jax.nn.sigmoid(v)


def _softplus(v):
    return jnp.maximum(v, 0.0) + jnp.log1p(jnp.exp(-jnp.abs(v)))


def _dot(a, b):
    return jnp.dot(a, b, preferred_element_type=F32)


def _dot_nt(a, b):
    return lax.dot_general(a, b, (((1,), (1,)), ((), ())), preferred_element_type=F32)


def _dot_hi(a, b):
    return jnp.dot(a, b, preferred_element_type=F32, precision=HIGHEST)


def _mod_kernel(c_ref, w_ref, b_ref, o_ref):
    c = c_ref[...]
    o_ref[...] = _dot_hi(_silu(c), w_ref[...]) + b_ref[...]


def _modulation(c_all, w_mod, b_mod):
    n = w_mod.shape[1]
    bn = 1536
    return pl.pallas_call(
        _mod_kernel,
        grid=(n // bn,),
        in_specs=[
            pl.BlockSpec((8, D_MODEL), lambda j: (0, 0)),
            pl.BlockSpec((D_MODEL, bn), lambda j: (0, j)),
            pl.BlockSpec((1, bn), lambda j: (0, j)),
        ],
        out_specs=pl.BlockSpec((8, bn), lambda j: (0, j)),
        out_shape=jax.ShapeDtypeStruct((8, n), F32),
        compiler_params=pltpu.CompilerParams(vmem_limit_bytes=VMEM_LIMIT),
        name="modulation",
    )(c_all, w_mod, b_mod.reshape(1, n))


def _in_kernel(nct, x_ref, ctx_ref, sh_ref, sc_ref, nw_ref, w_ref, z_ref, xbc_ref, ml_ref, g_ref):
    t = pl.program_id(1)
    xin = jnp.where(t < nct, ctx_ref[...], x_ref[...])
    ms = jnp.mean(xin * xin, axis=-1, keepdims=True)
    y = xin * lax.rsqrt(ms + EPS) * nw_ref[...]
    h = (y * (1.0 + sc_ref[...]) + sh_ref[...]).astype(BF16)
    z_ref[...] = _dot(h, w_ref[:, 0:512]).astype(BF16)
    for j in range(2):
        xbc_ref[:, j * 512:(j + 1) * 512] = _dot(h, w_ref[:, 512 + j * 512:1024 + j * 512]).astype(BF16)
    for j in range(3):
        ml_ref[:, j * 512:(j + 1) * 512] = _dot(h, w_ref[:, 1536 + j * 512:2048 + j * 512]).astype(BF16)
    g_ref[...] = _dot(h, w_ref[:, 3072:3200])


def _in_proj(x, ctx, mod3, norm_w, w_cat):
    b, seq, d = x.shape
    ctx_len = ctx.shape[1]
    nct, nlt = ctx_len // TM_IN, seq // TM_IN
    tot = ctx_len + seq
    n_ctx_row = b

    def mod_map(j):
        return lambda bi, t: (jnp.where(t < nct, n_ctx_row, bi) * 6 + j, 0, 0)

    out_shapes = (
        jax.ShapeDtypeStruct((b, tot, SSD_WIDTH), BF16),
        jax.ShapeDtypeStruct((b, tot, SSD_XBC), BF16),
        jax.ShapeDtypeStruct((b, tot, 3 * ML_WIDTH), BF16),
        jax.ShapeDtypeStruct((b, tot, LANES), F32),
    )

    def out_spec(width):
        return pl.BlockSpec((None, TM_IN, width), lambda bi, t: (bi, t, 0))

    return pl.pallas_call(
        functools.partial(_in_kernel, nct),
        grid=(b, nct + nlt),
        in_specs=[
            pl.BlockSpec((None, TM_IN, d), lambda bi, t: (bi, jnp.maximum(t - nct, 0), 0)),
            pl.BlockSpec((None, TM_IN, d), lambda bi, t: (bi, jnp.minimum(t, nct - 1), 0)),
            pl.BlockSpec((None, 1, d), mod_map(0)),
            pl.BlockSpec((None, 1, d), mod_map(1)),
            pl.BlockSpec((1, d), lambda bi, t: (0, 0)),
            pl.BlockSpec(w_cat.shape, lambda bi, t: (0, 0)),
        ],
        out_specs=[out_spec(SSD_WIDTH), out_spec(SSD_XBC), out_spec(3 * ML_WIDTH), out_spec(LANES)],
        out_shape=out_shapes,
        compiler_params=pltpu.CompilerParams(
            dimension_semantics=("arbitrary", "arbitrary"), vmem_limit_bytes=VMEM_LIMIT),
        name="in_proj",
    )(x, ctx, mod3, mod3, norm_w.reshape(1, d), w_cat)


def _chunk_eff(rev, ncc, nc, c):
    if not rev:
        return c
    return jnp.where(c < ncc, ncc - 1 - c, nc - 1 - (c - ncc))


def _lat_block(rev, ncc, nc, c):
    nl = nc - ncc
    if not rev:
        return jnp.maximum(c - ncc, 0)
    return jnp.where(c < ncc, nl - 1, nl - 1 - (c - ncc))


def _conv_silu(xm, xp, xn, cw_ref, cb_ref, first, last):
    xp = xp.astype(F32)[HALO - 8:HALO] * jnp.where(first, 0.0, 1.0)
    xn = xn.astype(F32)[0:8] * jnp.where(last, 0.0, 1.0)
    ext = jnp.concatenate([xp, xm, xn], axis=0)
    acc = cb_ref[...]
    for tap in range(CONV_W):
        off = 8 - CONV_W // 2 + tap
        acc = acc + cw_ref[tap:tap + 1, :] * ext[off:off + CHUNK]
    return _silu(acc)


def _tri(rev, transposed=False):
    row = lax.broadcasted_iota(jnp.int32, (CHUNK, CHUNK), 0)
    col = lax.broadcasted_iota(jnp.int32, (CHUNK, CHUNK), 1)
    if transposed:
        row, col = col, row
    keep = (col >= row) if rev else (col <= row)
    return keep


def _split3(a):
    a1 = a.astype(BF16)
    r1 = a - a1.astype(F32)
    a2 = r1.astype(BF16)
    a3 = (r1 - a2.astype(F32)).astype(BF16)
    return a1, a2, a3


def _dot_sel_l(m_b, a):
    p = _split3(a)
    return _dot(m_b, p[0]) + _dot(m_b, p[1]) + _dot(m_b, p[2])


def _dot_sel_r(a, m_b):
    p = _split3(a)
    return _dot(p[0], m_b) + _dot(p[1], m_b) + _dot(p[2], m_b)


def _row_spec(rows, width, row_block, col_block=0):
    return pl.BlockSpec((NB, rows, width), lambda bi, c: (bi, row_block(c), col_block))


def _halo_specs(rev, ncc, nc, width):
    ceff = functools.partial(_chunk_eff, rev, ncc, nc)
    per_chunk = CHUNK // HALO
    last = nc * per_chunk - 1
    prev = _row_spec(HALO, width, lambda c: jnp.maximum(ceff(c) * per_chunk - 1, 0))
    nxt = _row_spec(HALO, width, lambda c: jnp.minimum((ceff(c) + 1) * per_chunk, last))
    return prev, nxt


def _full(shape):
    return pl.BlockSpec(shape, lambda bi, c: (0,) * len(shape))


def _ssd_chunk(rev, xbc, g, dtbr_ref, dtbc_ref, alr_ref, alc_ref, e_ref, s_ref, s):
    xs = xbc[:, :SSD_WIDTH]
    lane0 = SSD_HEADS * int(rev)
    lane = lax.broadcasted_iota(jnp.int32, (CHUNK, LANES), 1)
    lmask = (lane >= lane0) & (lane < lane0 + SSD_HEADS)
    dt = jnp.where(lmask, _softplus(g + dtbr_ref[...]), 0.0)
    a = dt * (-jnp.exp(alr_ref[...]))
    gt = g.T
    dt_t = _softplus(gt + dtbc_ref[...])[lane0:lane0 + SSD_HEADS]
    a_t = dt_t * (-jnp.exp(alc_ref[...][lane0:lane0 + SSD_HEADS]))

    cs = _dot_sel_l(_tri(rev).astype(BF16), a)
    cs_t = _dot_sel_r(a_t, _tri(rev, transposed=True).astype(BF16))
    e = e_ref[...]
    dtx = _dot_sel_r(dt, e)
    csx = _dot_sel_r(cs, e)
    end = 0 if rev else CHUNK - 1
    totx = csx[end:end + 1, :]
    ecsx = jnp.exp(csx)
    decx = jnp.exp(totx - csx)
    etotx = jnp.exp(totx)

    xdt = xs * dtx
    xdt_b = xdt.astype(BF16)
    xd_b = (xdt * decx).astype(BF16)
    keep = _tri(rev)
    half = lax.broadcasted_iota(jnp.int32, (CHUNK, LANES), 1) // SSD_HEADDIM

    y_blocks = []
    gs = SSD_GROUPS * SSD_STATE
    hpg = SSD_HEADS // SSD_GROUPS
    gw = hpg * SSD_HEADDIM
    for grp in range(SSD_GROUPS):
        bm = xbc[:, SSD_WIDTH + grp * SSD_STATE:SSD_WIDTH + (grp + 1) * SSD_STATE]
        cm = xbc[:, SSD_WIDTH + gs + grp * SSD_STATE:SSD_WIDTH + gs + (grp + 1) * SSD_STATE]
        bm_b, cm_b = bm.astype(BF16), cm.astype(BF16)
        cb = _dot_nt(cm_b, bm_b)
        s_old = s_ref[s, :, grp * gw:(grp + 1) * gw]
        y_off = _dot(cm_b, s_old.astype(BF16)) * ecsx[:, grp * gw:(grp + 1) * gw]
        for pair in range(hpg // 2):
            blk = grp * (hpg // 2) + pair
            xj = xdt_b[:, blk * LANES:(blk + 1) * LANES]
            acc = y_off[:, pair * LANES:(pair + 1) * LANES]
            for q in range(2):
                h = blk * 2 + q
                dl = cs[:, lane0 + h:lane0 + h + 1] - cs_t[h:h + 1, :]
                m = (cb * jnp.exp(jnp.where(keep, dl, NEG_BIG))).astype(BF16)
                acc = acc + _dot(m, jnp.where(half == q, xj, jnp.zeros_like(xj)))
            y_blocks.append(acc)
        s_new = s_old * etotx[:, grp * gw:(grp + 1) * gw] + _dot(bm.T.astype(BF16), xd_b[:, grp * gw:(grp + 1) * gw])
        s_ref[s, :, grp * gw:(grp + 1) * gw] = s_new

    return jnp.concatenate(y_blocks, axis=1), xs


def _ssd_rev_kernel(ncc, nc, xm_ref, xp_ref, xn_ref, g_ref, cw_ref, cb_ref, dtbr_ref, dtbc_ref, alr_ref, alc_ref,
                    e_ref, o_ref, xc_ref, s_ref):
    c = pl.program_id(1)
    ceff = _chunk_eff(True, ncc, nc, c)
    first = (ceff == 0) | (ceff == ncc)
    last = (ceff == ncc - 1) | (ceff == nc - 1)

    @pl.when(c == 0)
    def _():
        s_ref[...] = jnp.zeros_like(s_ref)

    for s in range(NB):
        xbc = _conv_silu(xm_ref[s].astype(F32), xp_ref[s], xn_ref[s], cw_ref, cb_ref, first, last)
        xc_ref[s] = xbc.astype(BF16)
        y, _ = _ssd_chunk(True, xbc, g_ref[s], dtbr_ref, dtbc_ref, alr_ref, alc_ref, e_ref, s_ref, s)
        o_ref[s] = y.astype(o_ref.dtype)


def _ssd_fwd_kernel(xc_ref, g_ref, dtbr_ref, dtbc_ref, alr_ref, alc_ref, e_ref, dsk_ref, yb_ref, z_ref, nw_ref,
                    o_ref, s_ref):
    @pl.when(pl.program_id(1) == 0)
    def _():
        s_ref[...] = jnp.zeros_like(s_ref)

    for s in range(NB):
        y, xs = _ssd_chunk(False, xc_ref[s].astype(F32), g_ref[s], dtbr_ref, dtbc_ref, alr_ref, alc_ref, e_ref,
                           s_ref, s)
        y = y + yb_ref[s].astype(F32) + dsk_ref[...] * xs
        y = y * _silu(z_ref[s].astype(F32))
        y = y * lax.rsqrt(jnp.mean(y * y, axis=-1, keepdims=True) + EPS) * nw_ref[...]
        o_ref[s] = y.astype(o_ref.dtype)


def _ssd_scans(ncc, xbc, gates, z, conv_params, gate_params, e_mats, dsk, nw):
    b, tot, _ = xbc.shape
    nc = tot // CHUNK
    nl = nc - ncc
    state = pltpu.VMEM((NB, SSD_STATE, SSD_WIDTH), F32)
    params = pltpu.CompilerParams(dimension_semantics=("arbitrary", "arbitrary"), vmem_limit_bytes=VMEM_LIMIT)
    lat_shape = jax.ShapeDtypeStruct((b, nl * CHUNK, SSD_WIDTH), BF16)

    ceff = functools.partial(_chunk_eff, True, ncc, nc)
    lat = functools.partial(_lat_block, True, ncc, nc)
    prev, nxt = _halo_specs(True, ncc, nc, SSD_XBC)
    rev_params = list(conv_params) + list(gate_params) + [e_mats[1]]
    yb, xbc_act = pl.pallas_call(
        functools.partial(_ssd_rev_kernel, ncc, nc),
        grid=(b // NB, nc),
        in_specs=[_row_spec(CHUNK, SSD_XBC, ceff), prev, nxt, _row_spec(CHUNK, LANES, ceff)]
        + [_full(p.shape) for p in rev_params],
        out_specs=[_row_spec(CHUNK, SSD_WIDTH, lat), _row_spec(CHUNK, SSD_XBC, ceff)],
        out_shape=(lat_shape, jax.ShapeDtypeStruct(xbc.shape, BF16)),
        scratch_shapes=[state],
        compiler_params=params,
        name="ssd_rev",
    )(xbc, xbc, xbc, gates, *rev_params)

    ceff = functools.partial(_chunk_eff, False, ncc, nc)
    lat = functools.partial(_lat_block, False, ncc, nc)
    fwd_params = list(gate_params) + [e_mats[0], dsk]
    return pl.pallas_call(
        _ssd_fwd_kernel,
        grid=(b // NB, nc),
        in_specs=[_row_spec(CHUNK, SSD_XBC, ceff), _row_spec(CHUNK, LANES, ceff)]
        + [_full(p.shape) for p in fwd_params]
        + [_row_spec(CHUNK, SSD_WIDTH, lat), _row_spec(CHUNK, SSD_WIDTH, ceff), _full(nw.shape)],
        out_specs=_row_spec(CHUNK, SSD_WIDTH, lat),
        out_shape=lat_shape,
        scratch_shapes=[state],
        compiler_params=params,
        name="ssd_fwd",
    )(xbc_act, gates, *fwd_params, yb, z, nw)


ML_I_LANE0 = 2 * SSD_HEADS
ML_F_LANE0 = ML_I_LANE0 + 2 * ML_HEADS


def _ml_chunk(rev, q, k, q_b, k_b, v_b, g, gbr_ref, gbc_ref, c_ref, n_ref, mx_ref, s):
    i_lane0 = ML_I_LANE0 + ML_HEADS * int(rev)
    f_lane0 = ML_F_LANE0 + ML_HEADS * int(rev)
    ga = g + gbr_ref[...]
    lane = lax.broadcasted_iota(jnp.int32, (CHUNK, LANES), 1)
    logf = jnp.where((lane >= f_lane0) & (lane < f_lane0 + ML_HEADS), -_softplus(-ga), 0.0)
    cs = _dot_sel_l(_tri(rev).astype(BF16), logf)
    gt = g.T + gbc_ref[...]
    i_t = gt[ML_I_LANE0:ML_F_LANE0]
    logf_t = -_softplus(-gt[ML_F_LANE0:ML_F_LANE0 + 2 * ML_HEADS])
    cs_t = _dot_sel_r(logf_t, _tri(rev, transposed=True).astype(BF16))
    keep = _tri(rev)
    end = 0 if rev else CHUNK - 1

    outs = []
    for h in range(ML_HEADS):
        r = ML_HEADS * int(rev) + h
        sl = slice(h * ML_HEADDIM, (h + 1) * ML_HEADDIM)
        qh, kh = q[:, sl], k[:, sl]
        vh = v_b[:, sl]
        qh_b, kh_b = q_b[:, sl], k_b[:, sl]
        csc = cs[:, f_lane0 + h:f_lane0 + h + 1]
        csr = cs_t[r:r + 1, :]
        ic = ga[:, i_lane0 + h:i_lane0 + h + 1]
        ir = i_t[r:r + 1, :]
        tot = csr[:, end:end + 1]
        m_prev = mx_ref[s, h:h + 1, 0:1]
        c_prev = c_ref[s * ML_HEADS + h]
        n_prev = n_ref[s, h:h + 1, :]

        dlog = jnp.where(keep, csc - csr + ir, NEG_BIG)
        g_inter = csc + m_prev
        m_out = jnp.maximum(g_inter, jnp.max(dlog, axis=1, keepdims=True))
        scores = _dot_nt(qh_b, kh_b) * jnp.exp(dlog - m_out)
        w_inter = jnp.exp(g_inter - m_out)
        num = _dot(scores.astype(BF16), vh) + w_inter * _dot_nt(qh_b, c_prev.astype(BF16))
        den = jnp.sum(scores, axis=1, keepdims=True) + w_inter * jnp.sum(qh * n_prev, axis=1, keepdims=True)
        outs.append(num / jnp.maximum(jnp.abs(den), jnp.exp(-m_out)))

        a_r = tot - csr + ir
        m_loc = jnp.max(a_r, axis=1, keepdims=True)
        kw = kh * jnp.exp(tot - csc + ic - m_loc)
        c_loc = _dot(vh.astype(F32).T.astype(BF16), kw.astype(BF16))
        n_loc = jnp.sum(kw, axis=0, keepdims=True)
        m_new = jnp.maximum(tot + m_prev, m_loc)
        s_prev = jnp.exp(tot + m_prev - m_new)
        s_loc = jnp.exp(m_loc - m_new)
        c_ref[s * ML_HEADS + h] = s_prev * c_prev + s_loc * c_loc
        n_ref[s, h:h + 1, :] = s_prev * n_prev + s_loc * n_loc
        mx_ref[s, h:h + 1, :] = jnp.broadcast_to(m_new, (1, LANES))
    return outs


def _ml_init_state(c_ref, n_ref, mx_ref):
    c_ref[...] = jnp.zeros_like(c_ref)
    n_ref[...] = jnp.zeros_like(n_ref)
    mx_ref[...] = jnp.full(mx_ref.shape, NEG_BIG, F32)


def _ml_rev_kernel(ncc, nc, xm_ref, xp_ref, xn_ref, v_ref, g_ref, cw_ref, cb_ref, wq_ref, wk_ref, gbr_ref, gbc_ref,
                   o_ref, xc_ref, q_ref, k_ref, c_ref, n_ref, mx_ref):
    c = pl.program_id(1)
    ceff = _chunk_eff(True, ncc, nc, c)
    first = (ceff == 0) | (ceff == ncc)
    last = (ceff == ncc - 1) | (ceff == nc - 1)

    @pl.when(c == 0)
    def _():
        _ml_init_state(c_ref, n_ref, mx_ref)

    for s in range(NB):
        xconv = _conv_silu(xm_ref[s].astype(F32), xp_ref[s], xn_ref[s], cw_ref, cb_ref, first, last)
        xc_b = xconv.astype(BF16)
        q = _dot(xc_b, wq_ref[...])
        k = _dot(xc_b, wk_ref[...])
        q_b, k_b = q.astype(BF16), k.astype(BF16)
        xc_ref[s] = xc_b
        q_ref[s] = q_b
        k_ref[s] = k_b
        outs = _ml_chunk(True, q, k, q_b, k_b, v_ref[s], g_ref[s], gbr_ref, gbc_ref, c_ref, n_ref, mx_ref, s)
        o_ref[s] = jnp.concatenate(outs, axis=1).astype(o_ref.dtype)


def _ml_fwd_kernel(xc_ref, q_ref, k_ref, v_ref, og_ref, g_ref, gbr_ref, gbc_ref, hb_ref, nw_ref, sk_ref,
                   o_ref, c_ref, n_ref, mx_ref):
    @pl.when(pl.program_id(1) == 0)
    def _():
        _ml_init_state(c_ref, n_ref, mx_ref)

    for s in range(NB):
        q_b, k_b = q_ref[s], k_ref[s]
        outs = _ml_chunk(False, q_b.astype(F32), k_b.astype(F32), q_b, k_b, v_ref[s], g_ref[s], gbr_ref, gbc_ref,
                         c_ref, n_ref, mx_ref, s)
        normed = []
        for h in range(ML_HEADS):
            sl = slice(h * ML_HEADDIM, (h + 1) * ML_HEADDIM)
            hh = jax.nn.sigmoid(og_ref[s, :, sl].astype(F32)) * (outs[h] + hb_ref[s, :, sl].astype(F32))
            normed.append(hh * lax.rsqrt(jnp.mean(hh * hh, axis=-1, keepdims=True) + EPS))
        y = jnp.concatenate(normed, axis=1) * nw_ref[...] + sk_ref[...] * xc_ref[s].astype(F32)
        o_ref[s] = y.astype(o_ref.dtype)


def _ml_scans(ncc, ml, gates, conv_params, proj_params, gate_params, nw, sk):
    b, tot, _ = ml.shape
    nc = tot // CHUNK
    nl = nc - ncc
    scratch = [pltpu.VMEM((NB * ML_HEADS, ML_HEADDIM, ML_HEADDIM), F32),
               pltpu.VMEM((NB, 8, ML_HEADDIM), F32),
               pltpu.VMEM((NB, 8, LANES), F32)]
    params = pltpu.CompilerParams(dimension_semantics=("arbitrary", "arbitrary"), vmem_limit_bytes=VMEM_LIMIT)
    lat_shape = jax.ShapeDtypeStruct((b, nl * CHUNK, ML_WIDTH), BF16)
    act_shape = jax.ShapeDtypeStruct((b, tot, ML_WIDTH), BF16)

    ceff = functools.partial(_chunk_eff, True, ncc, nc)
    lat = functools.partial(_lat_block, True, ncc, nc)
    prev, nxt = _halo_specs(True, ncc, nc, ML_WIDTH)
    rev_params = list(conv_params) + list(proj_params) + list(gate_params)
    act_spec = _row_spec(CHUNK, ML_WIDTH, ceff)
    hb, xc, q, k = pl.pallas_call(
        functools.partial(_ml_rev_kernel, ncc, nc),
        grid=(b // NB, nc),
        in_specs=[act_spec, prev, nxt, _row_spec(CHUNK, ML_WIDTH, ceff, 1), _row_spec(CHUNK, LANES, ceff)]
        + [_full(p.shape) for p in rev_params],
        out_specs=[_row_spec(CHUNK, ML_WIDTH, lat), act_spec, act_spec, act_spec],
        out_shape=(lat_shape, act_shape, act_shape, act_shape),
        scratch_shapes=scratch,
        compiler_params=params,
        name="mlstm_rev",
    )(ml, ml, ml, ml, gates, *rev_params)

    ceff = functools.partial(_chunk_eff, False, ncc, nc)
    lat = functools.partial(_lat_block, False, ncc, nc)
    act_spec = _row_spec(CHUNK, ML_WIDTH, ceff)
    return pl.pallas_call(
        _ml_fwd_kernel,
        grid=(b // NB, nc),
        in_specs=[act_spec, act_spec, act_spec, _row_spec(CHUNK, ML_WIDTH, ceff, 1),
                  _row_spec(CHUNK, ML_WIDTH, ceff, 2), _row_spec(CHUNK, LANES, ceff)]
        + [_full(p.shape) for p in gate_params]
        + [_row_spec(CHUNK, ML_WIDTH, lat), _full(nw.shape), _full(sk.shape)],
        out_specs=_row_spec(CHUNK, ML_WIDTH, lat),
        out_shape=lat_shape,
        scratch_shapes=scratch,
        compiler_params=params,
        name="mlstm_fwd",
    )(xc, q, k, ml, ml, gates, *gate_params, hb, nw, sk)


def _out_kernel(x_ref, ys_ref, ym_ref, wo_ref, g1_ref, sh_ref, sc_ref, nw_ref, rwh_ref, rwl_ref, rb_ref,
                x1s_ref, ts_ref, route_ref):
    mix = _dot(ys_ref[...], wo_ref[0:SSD_WIDTH, :]) + _dot(ym_ref[...], wo_ref[SSD_WIDTH:, :])
    x1 = x_ref[...] + g1_ref[...] * mix
    y = x1 * lax.rsqrt(jnp.mean(x1 * x1, axis=-1, keepdims=True) + EPS) * nw_ref[...]
    t = y * (1.0 + sc_ref[...]) + sh_ref[...]
    for j in range(SLAB):
        x1s_ref[pl.ds(j, TM_IN, stride=SLAB), :] = x1[:, j * LANES:(j + 1) * LANES]
        ts_ref[pl.ds(j, TM_IN, stride=SLAB), :] = t[:, j * LANES:(j + 1) * LANES]
    t_hi = t.astype(BF16)
    t_lo = (t - t_hi.astype(F32)).astype(BF16)
    lg = _dot(t_hi, rwh_ref[...]) + _dot(t_lo, rwh_ref[...]) + _dot(t_hi, rwl_ref[...]) + rb_ref[...]

    lane = lax.broadcasted_iota(jnp.int32, lg.shape, 1).astype(F32)
    gmask = lane < MOE_GROUPS
    gl = jnp.where(gmask, lg, NEG_BIG)
    gmax = jnp.max(gl, axis=1, keepdims=True)
    g_sel = jnp.min(jnp.where(gmask & (gl == gmax), lane, 1e9), axis=1, keepdims=True)
    p_group = 1.0 / jnp.sum(jnp.where(gmask, jnp.exp(gl - gmax), 0.0), axis=1, keepdims=True)
    lo = ROUTE_LANE0 + MOE_EPG * g_sel
    emask = (lane >= lo) & (lane < lo + MOE_EPG)
    l1 = jnp.max(jnp.where(emask, lg, NEG_BIG), axis=1, keepdims=True)
    i1 = jnp.min(jnp.where(emask & (lg == l1), lane, 1e9), axis=1, keepdims=True)
    emask2 = emask & (lane != i1)
    l2 = jnp.max(jnp.where(emask2, lg, NEG_BIG), axis=1, keepdims=True)
    i2 = jnp.min(jnp.where(emask2 & (lg == l2), lane, 1e9), axis=1, keepdims=True)
    r = jnp.exp(l2 - l1)
    w1 = p_group / (1.0 + r)
    w2 = p_group * r / (1.0 + r)
    route = (jnp.where(lane == 0, i1, 0.0) + jnp.where(lane == 1, i2, 0.0)
             + jnp.where(lane == 2, w1, 0.0) + jnp.where(lane == 3, w2, 0.0))
    route_ref[...] = route.T[0:8, :]


def _out_proj(x, y_ssd, y_ml, w_out_b, mod3, norm_w, rw_hi, rw_lo, rb):
    b, seq, d = x.shape
    nt = seq // TM_IN

    def row(j):
        return pl.BlockSpec((None, 1, d), lambda bi, t: (bi * 6 + j, 0, 0))

    def tile(width):
        return pl.BlockSpec((None, TM_IN, width), lambda bi, t: (bi, t, 0))

    return pl.pallas_call(
        _out_kernel,
        grid=(b, nt),
        in_specs=[tile(d), tile(SSD_WIDTH), tile(ML_WIDTH), _full(w_out_b.shape),
                  row(2), row(3), row(4), _full((1, d)), _full(rw_hi.shape), _full(rw_lo.shape), _full(rb.shape)],
        out_specs=[pl.BlockSpec((None, TM_IN * SLAB, LANES), lambda bi, t: (bi, t, 0)),
                   pl.BlockSpec((None, TM_IN * SLAB, LANES), lambda bi, t: (bi, t, 0)),
                   pl.BlockSpec((None, 8, TM_IN), lambda bi, t: (bi, 0, t))],
        out_shape=(jax.ShapeDtypeStruct((b, seq * SLAB, LANES), F32),
                   jax.ShapeDtypeStruct((b, seq * SLAB, LANES), F32),
                   jax.ShapeDtypeStruct((b, 8, seq), F32)),
        compiler_params=pltpu.CompilerParams(
            dimension_semantics=("arbitrary", "arbitrary"), vmem_limit_bytes=VMEM_LIMIT),
        name="out_proj_router",
    )(x, y_ssd, y_ml, w_out_b, mod3, mod3, mod3, norm_w.reshape(1, d), rw_hi, rw_lo, rb)


def _route_tables(route, seq):
    b = route.shape[0]
    n_inst = 2 * seq
    nblk = n_inst // BM + MOE_EXPERTS
    e_flat = (route[:, 0:2, :].astype(jnp.int32) - ROUTE_LANE0).reshape(b, n_inst)
    w_flat = route[:, 2:4, :].reshape(b, n_inst)
    tok_flat = jnp.tile(jnp.arange(seq, dtype=jnp.int32), 2 * b).reshape(b, n_inst)
    _, tok_sorted, w_sorted = lax.sort((e_flat, tok_flat, w_flat), dimension=1, num_keys=1)
    experts = jnp.arange(MOE_EXPERTS, dtype=jnp.int32)
    counts = jnp.sum((e_flat[:, None, :] == experts[None, :, None]).astype(jnp.int32), axis=2)
    nblk_e = (counts + BM - 1) // BM
    blk_end = jnp.cumsum(nblk_e, axis=1)
    blk_start = blk_end - nblk_e
    cnt_start = jnp.cumsum(counts, axis=1) - counts
    nb = blk_end[:, -1:]
    j = jnp.arange(nblk, dtype=jnp.int32)[None, :]
    valid_blk = j < nb
    jj = jnp.minimum(j, nb - 1)
    e_j = jnp.sum((jj[:, :, None] >= blk_end[:, None, :]).astype(jnp.int32), axis=2)
    onehot = (e_j[:, :, None] == experts[None, None, :]).astype(jnp.int32)
    take = lambda tbl: jnp.sum(onehot * tbl[:, None, :], axis=2)
    r = jnp.arange(BM, dtype=jnp.int32)[None, None, :]
    rank = ((jj - take(blk_start)) * BM)[:, :, None] + r
    valid = valid_blk[:, :, None] & (rank < take(counts)[:, :, None])
    sidx = jnp.clip(take(cnt_start)[:, :, None] + rank, 0, n_inst - 1).reshape(b, nblk * BM)
    tok = jnp.take_along_axis(tok_sorted, sidx, axis=1).reshape(b, nblk, BM)
    wslot = jnp.take_along_axis(w_sorted, sidx, axis=1).reshape(b, nblk, BM)
    tok = jnp.where(valid, tok, seq + r)
    wslot = jnp.where(valid, wslot, 0.0)
    return (e_j.reshape(-1), valid_blk.astype(jnp.int32).reshape(-1),
            tok.reshape(b * nblk, 1, BM), wslot.reshape(b * nblk, 1, BM))


def _moe_kernel(seq, nblk, be_ref, bv_ref, tok_ref, ws_ref, wg_ref, wu_ref, wd_ref, g2_ref, fw_ref,
                t_hbm, x1_hbm, o_hbm, t_scr, y_scr, wgb, wub, wdb, xt, ot, stage, sems):
    b = pl.program_id(0)
    j = pl.program_id(1)
    idx = b * nblk + j
    rows = seq * SLAB

    @pl.when(j == 0)
    def _():
        cp_t = pltpu.make_async_copy(t_hbm.at[b], t_scr.at[pl.ds(0, rows)], sems.at[0])
        cp_x = pltpu.make_async_copy(x1_hbm.at[b], y_scr.at[pl.ds(0, rows)], sems.at[1])
        cp_t.start()
        cp_x.start()
        t_scr[pl.ds(rows, BM * SLAB), :] = jnp.zeros((BM * SLAB, LANES), F32)
        y_scr[pl.ds(rows, BM * SLAB), :] = jnp.zeros((BM * SLAB, LANES), F32)
        cp_t.wait()
        cp_x.wait()

    valid = bv_ref[idx] == 1
    changed = (j == 0) | (be_ref[idx] != be_ref[jnp.maximum(idx - 1, 0)])

    @pl.when(valid & changed)
    def _():
        wgb[...] = wg_ref[...].astype(BF16)
        wub[...] = wu_ref[...].astype(BF16)
        wdb[...] = wd_ref[...].astype(BF16)

    @pl.when(valid)
    def _():
        def slab_rows(r):
            return pl.ds(pl.multiple_of(tok_ref[0, r] * SLAB, SLAB), SLAB)

        for r in range(BM):
            xt[pl.ds(r, SLAB, stride=TILE_PITCH), :] = t_scr[slab_rows(r), :]
        x = jnp.concatenate([xt[c * TILE_PITCH:c * TILE_PITCH + BM, :] for c in range(SLAB)], axis=1).astype(BF16)
        hidden = _silu(_dot(x, wgb[...])) * _dot(x, wub[...])
        out = _dot(hidden.astype(BF16), wdb[...]) * g2_ref[...]
        for c in range(SLAB):
            ot[c * TILE_PITCH:c * TILE_PITCH + BM, :] = out[:, c * LANES:(c + 1) * LANES]
        for r0 in range(0, BM, SCATTER_GROUP):
            sl = [slab_rows(r0 + u) for u in range(SCATTER_GROUP)]
            vals = [y_scr[sl[u], :] + ws_ref[0, r0 + u] * ot[pl.ds(r0 + u, SLAB, stride=TILE_PITCH), :]
                    for u in range(SCATTER_GROUP)]
            for u in range(SCATTER_GROUP):
                y_scr[sl[u], :] = vals[u]

    @pl.when(j == nblk - 1)
    def _():
        def body(s, carry):
            base = pl.multiple_of(s * (TM_FIN * SLAB), TM_FIN * SLAB)
            chunks = [y_scr[pl.ds(base + c, TM_FIN, stride=SLAB), :] for c in range(SLAB)]
            ssq = chunks[0] * chunks[0]
            for c in range(1, SLAB):
                ssq = ssq + chunks[c] * chunks[c]
            inv = lax.rsqrt(jnp.sum(ssq, axis=1, keepdims=True) * (1.0 / D_MODEL) + EPS)
            for c in range(SLAB):
                stage[:, c * LANES:(c + 1) * LANES] = chunks[c] * inv * fw_ref[:, c * LANES:(c + 1) * LANES]
            cp = pltpu.make_async_copy(stage, o_hbm.at[b, pl.ds(s * TM_FIN, TM_FIN)], sems.at[2])
            cp.start()
            cp.wait()
            return carry

        lax.fori_loop(0, seq // TM_FIN, body, 0)


def _moe(t_slab, x1_slab, route, wg, wu, wd, mod3, final_w):
    b = t_slab.shape[0]
    seq = t_slab.shape[1] // SLAB
    d = D_MODEL
    blk_e, blk_valid, tok, wslot = _route_tables(route, seq)
    nblk = blk_e.shape[0] // b

    def w_spec(shape):
        return pl.BlockSpec((None,) + shape, lambda bi, j, be, bv: (be[bi * nblk + j], 0, 0))

    grid_spec = pltpu.PrefetchScalarGridSpec(
        num_scalar_prefetch=2,
        grid=(b, nblk),
        in_specs=[
            pl.BlockSpec((None, 1, BM), lambda bi, j, be, bv: (bi * nblk + j, 0, 0), memory_space=pltpu.SMEM),
            pl.BlockSpec((None, 1, BM), lambda bi, j, be, bv: (bi * nblk + j, 0, 0), memory_space=pltpu.SMEM),
            w_spec((d, D_EXPERT)), w_spec((d, D_EXPERT)), w_spec((D_EXPERT, d)),
            pl.BlockSpec((None, 1, d), lambda bi, j, be, bv: (bi * 6 + 5, 0, 0)),
            pl.BlockSpec((1, d), lambda bi, j, be, bv: (0, 0)),
            pl.BlockSpec(memory_space=pl.ANY),
            pl.BlockSpec(memory_space=pl.ANY),
        ],
        out_specs=pl.BlockSpec(memory_space=pl.ANY),
        scratch_shapes=[
            pltpu.VMEM(((seq + BM) * SLAB, LANES), F32),
            pltpu.VMEM(((seq + BM) * SLAB, LANES), F32),
            pltpu.VMEM((d, D_EXPERT), BF16),
            pltpu.VMEM((d, D_EXPERT), BF16),
            pltpu.VMEM((D_EXPERT, d), BF16),
            pltpu.VMEM((SLAB * TILE_PITCH, LANES), F32),
            pltpu.VMEM((SLAB * TILE_PITCH, LANES), F32),
            pltpu.VMEM((TM_FIN, d), F32),
            pltpu.SemaphoreType.DMA((3,)),
        ],
    )
    return pl.pallas_call(
        functools.partial(_moe_kernel, seq, nblk),
        grid_spec=grid_spec,
        out_shape=jax.ShapeDtypeStruct((b, seq, d), F32),
        compiler_params=pltpu.CompilerParams(
            dimension_semantics=("arbitrary", "arbitrary"), vmem_limit_bytes=VMEM_LIMIT_MOE),
        name="experts_final_norm",
    )(blk_e, blk_valid, tok, wslot, wg, wu, wd, mod3, final_w.reshape(1, d), t_slab, x1_slab)


def _lane_pad(v, offset=0):
    v = v.reshape(-1).astype(F32)
    return jnp.zeros((1, LANES), F32).at[0, offset:offset + v.shape[0]].set(v)


def _to_col_major(t, rows):
    b, length, ch = t.shape
    return t.reshape(b, rows, GRID_W, ch).transpose(0, 2, 1, 3).reshape(b, length, ch)


def _from_col_major(t, rows):
    b, length, ch = t.shape
    return t.reshape(b, GRID_W, rows, ch).transpose(0, 2, 1, 3).reshape(b, length, ch)


def kernel(x, c, ctx, c_ctx, w_mod, b_mod, norm1_w, w_in, ssd_conv_w, ssd_conv_b, ssd_dt_bias, ssd_a_log, ssd_d, ssd_norm_w, ml_conv_w, ml_conv_b, ml_w_qk, ml_gate_b, ml_norm_w, ml_skip, w_out, norm2_w, moe_rg_w, moe_rg_b, moe_re_w, moe_re_b, moe_w_gate, moe_w_up, moe_w_down, final_norm_w):
    b, seq, d = x.shape
    ctx_len = ctx.shape[1]
    rows = seq // GRID_W
    ncc = ctx_len // CHUNK
    assert w_mod.shape[0] == 1 and d == D_MODEL and b + 1 <= 8 and b % NB == 0
    assert seq % TM_IN == 0 and ctx_len % TM_IN == 0 and seq % TM_FIN == 0 and (2 * seq) % BM == 0

    c_all = jnp.zeros((8, d), F32).at[:b].set(c).at[b].set(c_ctx)
    mod = _modulation(c_all, w_mod[0], b_mod[0])
    mod3 = mod.reshape(8 * 6, 1, d)

    w = w_in[0]
    ssd_in = SSD_WIDTH + SSD_XBC + 2 * SSD_HEADS
    ml_main = 3 * ML_WIDTH
    n_gate = 2 * SSD_HEADS + 4 * ML_HEADS
    w_cat = jnp.concatenate([
        w[:, :SSD_WIDTH + SSD_XBC], w[:, ssd_in:ssd_in + ml_main],
        w[:, SSD_WIDTH + SSD_XBC:ssd_in], w[:, ssd_in + ml_main:],
        jnp.zeros((d, LANES - n_gate), F32)], axis=1).astype(BF16)
    z, xbc, ml, gates = _in_proj(x, ctx, mod3, norm1_w[0], w_cat)

    e_mats = []
    for direction in range(2):
        lane = jnp.arange(LANES)[:, None]
        head = (jnp.arange(SSD_WIDTH) // SSD_HEADDIM)[None, :]
        e_mats.append((lane == direction * SSD_HEADS + head).astype(BF16))
    cw = jnp.zeros((8, SSD_XBC), F32).at[:CONV_W].set(ssd_conv_w[0])
    dtb = _lane_pad(ssd_dt_bias[0])
    alog = _lane_pad(ssd_a_log[0])
    dsk = jnp.repeat(ssd_d[0], SSD_HEADDIM).reshape(1, SSD_WIDTH)
    y_ssd = _ssd_scans(ncc, xbc, gates, z, [cw, ssd_conv_b[0].reshape(1, -1)],
                       [dtb, dtb.reshape(LANES, 1), alog, alog.reshape(LANES, 1)], e_mats, dsk,
                       ssd_norm_w[0].reshape(1, -1))

    ml_cm = jnp.concatenate([ml[:, :ctx_len], _to_col_major(ml[:, ctx_len:], rows)], axis=1)
    g_cm = jnp.concatenate([gates[:, :ctx_len], _to_col_major(gates[:, ctx_len:], rows)], axis=1)
    w_rows = jnp.tile(ml_w_qk[0].reshape(2, ML_WIDTH, ML_QK_BLOCK), (1, 1, ML_WIDTH // ML_QK_BLOCK))
    blk_id = jnp.arange(ML_WIDTH) // ML_QK_BLOCK
    w_bd = jnp.where((blk_id[:, None] == blk_id[None, :])[None], w_rows, 0.0)
    wq = w_bd[0].astype(BF16)
    wk = (w_bd[1] * (ML_HEADDIM ** -0.5)).astype(BF16)
    mcw = jnp.zeros((8, ML_WIDTH), F32).at[:CONV_W].set(ml_conv_w[0])
    gb = _lane_pad(ml_gate_b[0], offset=2 * SSD_HEADS)
    y_ml_cm = _ml_scans(ncc, ml_cm, g_cm, [mcw, ml_conv_b[0].reshape(1, -1)], [wq, wk], [gb, gb.reshape(LANES, 1)],
                        ml_norm_w[0].reshape(1, -1), ml_skip[0].reshape(1, -1))
    y_ml = _from_col_major(y_ml_cm, rows)

    rw = jnp.concatenate([moe_rg_w[0], moe_re_w[0],
                          jnp.zeros((d, LANES - MOE_GROUPS - MOE_EXPERTS), F32)], axis=1)
    rw_hi = rw.astype(BF16)
    rw_lo = (rw - rw_hi.astype(F32)).astype(BF16)
    rb = _lane_pad(jnp.concatenate([moe_rg_b[0], moe_re_b[0]]))
    x1_slab, t_slab, route = _out_proj(x, y_ssd, y_ml, w_out[0].astype(BF16), mod3, norm2_w[0], rw_hi, rw_lo, rb)

    return _moe(t_slab, x1_slab, route, moe_w_gate[0], moe_w_up[0], moe_w_down[0], mod3, final_norm_w)
```

```python
import functools

import jax
import jax.numpy as jnp
from jax import lax
from jax.experimental import pallas as pl
from jax.experimental.pallas import tpu as pltpu

F32 = jnp.float32
BF16 = jnp.bfloat16
HIGHEST = lax.Precision.HIGHEST

D_MODEL = 1024
GRID_W = 64
EPS = 1e-6
CONV_W = 5
NEG_BIG = -1e30
CHUNK = 128
LANES = 128
HALO = 16
NB = 2

SSD_WIDTH = 512
SSD_HEADS = 8
SSD_HEADDIM = 64
SSD_GROUPS = 2
SSD_STATE = 128
SSD_XBC = SSD_WIDTH + 2 * SSD_GROUPS * SSD_STATE

ML_WIDTH = 512
ML_HEADS = 4
ML_HEADDIM = 128
ML_QK_BLOCK = 4

MOE_GROUPS = 4
MOE_EPG = 8
MOE_EXPERTS = 32
D_EXPERT = 256
ROUTE_LANE0 = MOE_GROUPS

TM_IN = 256
TM_FIN = 256
SLAB = D_MODEL // LANES
BM = 128
TILE_PITCH = BM + 8
SCATTER_GROUP = 4
VMEM_LIMIT = 48 * 1024 * 1024
VMEM_LIMIT_MOE = 56 * 1024 * 1024


def _silu(v):
    return v * jax.nn.sigmoid(v)


def _softplus(v):
    return jnp.maximum(v, 0.0) + jnp.log1p(jnp.exp(-jnp.abs(v)))


def _dot(a, b):
    return jnp.dot(a, b, preferred_element_type=F32)


def _dot_nt(a, b):
    return lax.dot_general(a, b, (((1,), (1,)), ((), ())), preferred_element_type=F32)


def _dot_hi(a, b):
    return jnp.dot(a, b, preferred_element_type=F32, precision=HIGHEST)


def _mod_kernel(c_ref, w_ref, b_ref, o_ref):
    c = c_ref[...]
    o_ref[...] = _dot_hi(_silu(c), w_ref[...]) + b_ref[...]


def _modulation(c_all, w_mod, b_mod):
    n = w_mod.shape[1]
    bn = 1536
    return pl.pallas_call(
        _mod_kernel,
        grid=(n // bn,),
        in_specs=[
            pl.BlockSpec((8, D_MODEL), lambda j: (0, 0)),
            pl.BlockSpec((D_MODEL, bn), lambda j: (0, j)),
            pl.BlockSpec((1, bn), lambda j: (0, j)),
        ],
        out_specs=pl.BlockSpec((8, bn), lambda j: (0, j)),
        out_shape=jax.ShapeDtypeStruct((8, n), F32),
        compiler_params=pltpu.CompilerParams(vmem_limit_bytes=VMEM_LIMIT),
        name="modulation",
    )(c_all, w_mod, b_mod.reshape(1, n))


def _in_kernel(nct, x_ref, ctx_ref, sh_ref, sc_ref, nw_ref, w_ref, z_ref, xbc_ref, ml_ref, g_ref):
    t = pl.program_id(1)
    xin = jnp.where(t < nct, ctx_ref[...], x_ref[...])
    ms = jnp.mean(xin * xin, axis=-1, keepdims=True)
    y = xin * lax.rsqrt(ms + EPS) * nw_ref[...]
    h = (y * (1.0 + sc_ref[...]) + sh_ref[...]).astype(BF16)
    z_ref[...] = _dot(h, w_ref[:, 0:512]).astype(BF16)
    for j in range(2):
        xbc_ref[:, j * 512:(j + 1) * 512] = _dot(h, w_ref[:, 512 + j * 512:1024 + j * 512]).astype(BF16)
    for j in range(3):
        ml_ref[:, j * 512:(j + 1) * 512] = _dot(h, w_ref[:, 1536 + j * 512:2048 + j * 512]).astype(BF16)
    g_ref[...] = _dot(h, w_ref[:, 3072:3200])


def _in_proj(x, ctx, mod3, norm_w, w_cat):
    b, seq, d = x.shape
    ctx_len = ctx.shape[1]
    nct, nlt = ctx_len // TM_IN, seq // TM_IN
    tot = ctx_len + seq
    n_ctx_row = b

    def mod_map(j):
        return lambda bi, t: (jnp.where(t < nct, n_ctx_row, bi) * 6 + j, 0, 0)

    out_shapes = (
        jax.ShapeDtypeStruct((b, tot, SSD_WIDTH), BF16),
        jax.ShapeDtypeStruct((b, tot, SSD_XBC), BF16),
        jax.ShapeDtypeStruct((b, tot, 3 * ML_WIDTH), BF16),
        jax.ShapeDtypeStruct((b, tot, LANES), F32),
    )

    def out_spec(width):
        return pl.BlockSpec((None, TM_IN, width), lambda bi, t: (bi, t, 0))

    return pl.pallas_call(
        functools.partial(_in_kernel, nct),
        grid=(b, nct + nlt),
        in_specs=[
            pl.BlockSpec((None, TM_IN, d), lambda bi, t: (bi, jnp.maximum(t - nct, 0), 0)),
            pl.BlockSpec((None, TM_IN, d), lambda bi, t: (bi, jnp.minimum(t, nct - 1), 0)),
            pl.BlockSpec((None, 1, d), mod_map(0)),
            pl.BlockSpec((None, 1, d), mod_map(1)),
            pl.BlockSpec((1, d), lambda bi, t: (0, 0)),
            pl.BlockSpec(w_cat.shape, lambda bi, t: (0, 0)),
        ],
        out_specs=[out_spec(SSD_WIDTH), out_spec(SSD_XBC), out_spec(3 * ML_WIDTH), out_spec(LANES)],
        out_shape=out_shapes,
        compiler_params=pltpu.CompilerParams(
            dimension_semantics=("arbitrary", "arbitrary"), vmem_limit_bytes=VMEM_LIMIT),
        name="in_proj",
    )(x, ctx, mod3, mod3, norm_w.reshape(1, d), w_cat)


def _chunk_eff(rev, ncc, nc, c):
    if not rev:
        return c
    return jnp.where(c < ncc, ncc - 1 - c, nc - 1 - (c - ncc))


def _lat_block(rev, ncc, nc, c):
    nl = nc - ncc
    if not rev:
        return jnp.maximum(c - ncc, 0)
    return jnp.where(c < ncc, nl - 1, nl - 1 - (c - ncc))


def _conv_silu(xm, xp, xn, cw_ref, cb_ref, first, last):
    xp = xp.astype(F32)[HALO - 8:HALO] * jnp.where(first, 0.0, 1.0)
    xn = xn.astype(F32)[0:8] * jnp.where(last, 0.0, 1.0)
    ext = jnp.concatenate([xp, xm, xn], axis=0)
    acc = cb_ref[...]
    for tap in range(CONV_W):
        off = 8 - CONV_W // 2 + tap
        acc = acc + cw_ref[tap:tap + 1, :] * ext[off:off + CHUNK]
    return _silu(acc)


def _tri(rev, transposed=False):
    row = lax.broadcasted_iota(jnp.int32, (CHUNK, CHUNK), 0)
    col = lax.broadcasted_iota(jnp.int32, (CHUNK, CHUNK), 1)
    if transposed:
        row, col = col, row
    keep = (col >= row) if rev else (col <= row)
    return keep


def _split3(a):
    a1 = a.astype(BF16)
    r1 = a - a1.astype(F32)
    a2 = r1.astype(BF16)
    a3 = (r1 - a2.astype(F32)).astype(BF16)
    return a1, a2, a3


def _dot_sel_l(m_b, a):
    p = _split3(a)
    return _dot(m_b, p[0]) + _dot(m_b, p[1]) + _dot(m_b, p[2])


def _dot_sel_r(a, m_b):
    p = _split3(a)
    return _dot(p[0], m_b) + _dot(p[1], m_b) + _dot(p[2], m_b)


def _row_spec(rows, width, row_block, col_block=0):
    return pl.BlockSpec((NB, rows, width), lambda bi, c: (bi, row_block(c), col_block))


def _halo_specs(rev, ncc, nc, width):
    ceff = functools.partial(_chunk_eff, rev, ncc, nc)
    per_chunk = CHUNK // HALO
    last = nc * per_chunk - 1
    prev = _row_spec(HALO, width, lambda c: jnp.maximum(ceff(c) * per_chunk - 1, 0))
    nxt = _row_spec(HALO, width, lambda c: jnp.minimum((ceff(c) + 1) * per_chunk, last))
    return prev, nxt


def _full(shape):
    return pl.BlockSpec(shape, lambda bi, c: (0,) * len(shape))


def _ssd_chunk(rev, xbc, g, dtbr_ref, dtbc_ref, alr_ref, alc_ref, e_ref, s_ref, s):
    xs = xbc[:, :SSD_WIDTH]
    lane0 = SSD_HEADS * int(rev)
    lane = lax.broadcasted_iota(jnp.int32, (CHUNK, LANES), 1)
    lmask = (lane >= lane0) & (lane < lane0 + SSD_HEADS)
    dt = jnp.where(lmask, _softplus(g + dtbr_ref[...]), 0.0)
    a = dt * (-jnp.exp(alr_ref[...]))
    gt = g.T
    dt_t = _softplus(gt + dtbc_ref[...])[lane0:lane0 + SSD_HEADS]
    a_t = dt_t * (-jnp.exp(alc_ref[...][lane0:lane0 + SSD_HEADS]))

    cs = _dot_sel_l(_tri(rev).astype(BF16), a)
    cs_t = _dot_sel_r(a_t, _tri(rev, transposed=True).astype(BF16))
    e = e_ref[...]
    dtx = _dot_sel_r(dt, e)
    csx = _dot_sel_r(cs, e)
    end = 0 if rev else CHUNK - 1
    totx = csx[end:end + 1, :]
    ecsx = jnp.exp(csx)
    decx = jnp.exp(totx - csx)
    etotx = jnp.exp(totx)

    xdt = xs * dtx
    xdt_b = xdt.astype(BF16)
    xd_b = (xdt * decx).astype(BF16)
    keep = _tri(rev)
    half = lax.broadcasted_iota(jnp.int32, (CHUNK, LANES), 1) // SSD_HEADDIM

    y_blocks = []
    gs = SSD_GROUPS * SSD_STATE
    hpg = SSD_HEADS // SSD_GROUPS
    gw = hpg * SSD_HEADDIM
    for grp in range(SSD_GROUPS):
        bm = xbc[:, SSD_WIDTH + grp * SSD_STATE:SSD_WIDTH + (grp + 1) * SSD_STATE]
        cm = xbc[:, SSD_WIDTH + gs + grp * SSD_STATE:SSD_WIDTH + gs + (grp + 1) * SSD_STATE]
        bm_b, cm_b = bm.astype(BF16), cm.astype(BF16)
        cb = _dot_nt(cm_b, bm_b)
        s_old = s_ref[s, :, grp * gw:(grp + 1) * gw]
        y_off = _dot(cm_b, s_old.astype(BF16)) * ecsx[:, grp * gw:(grp + 1) * gw]
        for pair in range(hpg // 2):
            blk = grp * (hpg // 2) + pair
            xj = xdt_b[:, blk * LANES:(blk + 1) * LANES]
            acc = y_off[:, pair * LANES:(pair + 1) * LANES]
            for q in range(2):
                h = blk * 2 + q
                dl = cs[:, lane0 + h:lane0 + h + 1] - cs_t[h:h + 1, :]
                m = (cb * jnp.exp(jnp.where(keep, dl, NEG_BIG))).astype(BF16)
                acc = acc + _dot(m, jnp.where(half == q, xj, jnp.zeros_like(xj)))
            y_blocks.append(acc)
        s_new = s_old * etotx[:, grp * gw:(grp + 1) * gw] + _dot(bm.T.astype(BF16), xd_b[:, grp * gw:(grp + 1) * gw])
        s_ref[s, :, grp * gw:(grp + 1) * gw] = s_new

    return jnp.concatenate(y_blocks, axis=1), xs


def _ssd_rev_kernel(ncc, nc, xm_ref, xp_ref, xn_ref, g_ref, cw_ref, cb_ref, dtbr_ref, dtbc_ref, alr_ref, alc_ref,
                    e_ref, o_ref, xc_ref, s_ref):
    c = pl.program_id(1)
    ceff = _chunk_eff(True, ncc, nc, c)
    first = (ceff == 0) | (ceff == ncc)
    last = (ceff == ncc - 1) | (ceff == nc - 1)

    @pl.when(c == 0)
    def _():
        s_ref[...] = jnp.zeros_like(s_ref)

    for s in range(NB):
        xbc = _conv_silu(xm_ref[s].astype(F32), xp_ref[s], xn_ref[s], cw_ref, cb_ref, first, last)
        xc_ref[s] = xbc.astype(BF16)
        y, _ = _ssd_chunk(True, xbc, g_ref[s], dtbr_ref, dtbc_ref, alr_ref, alc_ref, e_ref, s_ref, s)
        o_ref[s] = y.astype(o_ref.dtype)


def _ssd_fwd_kernel(xc_ref, g_ref, dtbr_ref, dtbc_ref, alr_ref, alc_ref, e_ref, dsk_ref, yb_ref, z_ref, nw_ref,
                    o_ref, s_ref):
    @pl.when(pl.program_id(1) == 0)
    def _():
        s_ref[...] = jnp.zeros_like(s_ref)

    for s in range(NB):
        y, xs = _ssd_chunk(False, xc_ref[s].astype(F32), g_ref[s], dtbr_ref, dtbc_ref, alr_ref, alc_ref, e_ref,
                           s_ref, s)
        y = y + yb_ref[s].astype(F32) + dsk_ref[...] * xs
        y = y * _silu(z_ref[s].astype(F32))
        y = y * lax.rsqrt(jnp.mean(y * y, axis=-1, keepdims=True) + EPS) * nw_ref[...]
        o_ref[s] = y.astype(o_ref.dtype)


def _ssd_scans(ncc, xbc, gates, z, conv_params, gate_params, e_mats, dsk, nw):
    b, tot, _ = xbc.shape
    nc = tot // CHUNK
    nl = nc - ncc
    state = pltpu.VMEM((NB, SSD_STATE, SSD_WIDTH), F32)
    params = pltpu.CompilerParams(dimension_semantics=("arbitrary", "arbitrary"), vmem_limit_bytes=VMEM_LIMIT)
    lat_shape = jax.ShapeDtypeStruct((b, nl * CHUNK, SSD_WIDTH), BF16)

    ceff = functools.partial(_chunk_eff, True, ncc, nc)
    lat = functools.partial(_lat_block, True, ncc, nc)
    prev, nxt = _halo_specs(True, ncc, nc, SSD_XBC)
    rev_params = list(conv_params) + list(gate_params) + [e_mats[1]]
    yb, xbc_act = pl.pallas_call(
        functools.partial(_ssd_rev_kernel, ncc, nc),
        grid=(b // NB, nc),
        in_specs=[_row_spec(CHUNK, SSD_XBC, ceff), prev, nxt, _row_spec(CHUNK, LANES, ceff)]
        + [_full(p.shape) for p in rev_params],
        out_specs=[_row_spec(CHUNK, SSD_WIDTH, lat), _row_spec(CHUNK, SSD_XBC, ceff)],
        out_shape=(lat_shape, jax.ShapeDtypeStruct(xbc.shape, BF16)),
        scratch_shapes=[state],
        compiler_params=params,
        name="ssd_rev",
    )(xbc, xbc, xbc, gates, *rev_params)

    ceff = functools.partial(_chunk_eff, False, ncc, nc)
    lat = functools.partial(_lat_block, False, ncc, nc)
    fwd_params = list(gate_params) + [e_mats[0], dsk]
    return pl.pallas_call(
        _ssd_fwd_kernel,
        grid=(b // NB, nc),
        in_specs=[_row_spec(CHUNK, SSD_XBC, ceff), _row_spec(CHUNK, LANES, ceff)]
        + [_full(p.shape) for p in fwd_params]
        + [_row_spec(CHUNK, SSD_WIDTH, lat), _row_spec(CHUNK, SSD_WIDTH, ceff), _full(nw.shape)],
        out_specs=_row_spec(CHUNK, SSD_WIDTH, lat),
        out_shape=lat_shape,
        scratch_shapes=[state],
        compiler_params=params,
        name="ssd_fwd",
    )(xbc_act, gates, *fwd_params, yb, z, nw)


ML_I_LANE0 = 2 * SSD_HEADS
ML_F_LANE0 = ML_I_LANE0 + 2 * ML_HEADS


def _ml_chunk(rev, q, k, q_b, k_b, v_b, g, gbr_ref, gbc_ref, c_ref, n_ref, mx_ref, s):
    i_lane0 = ML_I_LANE0 + ML_HEADS * int(rev)
    f_lane0 = ML_F_LANE0 + ML_HEADS * int(rev)
    ga = g + gbr_ref[...]
    lane = lax.broadcasted_iota(jnp.int32, (CHUNK, LANES), 1)
    logf = jnp.where((lane >= f_lane0) & (lane < f_lane0 + ML_HEADS), -_softplus(-ga), 0.0)
    cs = _dot_sel_l(_tri(rev).astype(BF16), logf)
    gt = g.T + gbc_ref[...]
    i_t = gt[ML_I_LANE0:ML_F_LANE0]
    logf_t = -_softplus(-gt[ML_F_LANE0:ML_F_LANE0 + 2 * ML_HEADS])
    cs_t = _dot_sel_r(logf_t, _tri(rev, transposed=True).astype(BF16))
    keep = _tri(rev)
    end = 0 if rev else CHUNK - 1

    outs = []
    for h in range(ML_HEADS):
        r = ML_HEADS * int(rev) + h
        sl = slice(h * ML_HEADDIM, (h + 1) * ML_HEADDIM)
        qh, kh = q[:, sl], k[:, sl]
        vh = v_b[:, sl]
        qh_b, kh_b = q_b[:, sl], k_b[:, sl]
        csc = cs[:, f_lane0 + h:f_lane0 + h + 1]
        csr = cs_t[r:r + 1, :]
        ic = ga[:, i_lane0 + h:i_lane0 + h + 1]
        ir = i_t[r:r + 1, :]
        tot = csr[:, end:end + 1]
        m_prev = mx_ref[s, h:h + 1, 0:1]
        c_prev = c_ref[s * ML_HEADS + h]
        n_prev = n_ref[s, h:h + 1, :]

        dlog = jnp.where(keep, csc - csr + ir, NEG_BIG)
        g_inter = csc + m_prev
        m_out = jnp.maximum(g_inter, jnp.max(dlog, axis=1, keepdims=True))
        scores = _dot_nt(qh_b, kh_b) * jnp.exp(dlog - m_out)
        w_inter = jnp.exp(g_inter - m_out)
        num = _dot(scores.astype(BF16), vh) + w_inter * _dot_nt(qh_b, c_prev.astype(BF16))
        den = jnp.sum(scores, axis=1, keepdims=True) + w_inter * jnp.sum(qh * n_prev, axis=1, keepdims=True)
        outs.append(num / jnp.maximum(jnp.abs(den), jnp.exp(-m_out)))

        a_r = tot - csr + ir
        m_loc = jnp.max(a_r, axis=1, keepdims=True)
        kw = kh * jnp.exp(tot - csc + ic - m_loc)
        c_loc = _dot(vh.astype(F32).T.astype(BF16), kw.astype(BF16))
        n_loc = jnp.sum(kw, axis=0, keepdims=True)
        m_new = jnp.maximum(tot + m_prev, m_loc)
        s_prev = jnp.exp(tot + m_prev - m_new)
        s_loc = jnp.exp(m_loc - m_new)
        c_ref[s * ML_HEADS + h] = s_prev * c_prev + s_loc * c_loc
        n_ref[s, h:h + 1, :] = s_prev * n_prev + s_loc * n_loc
        mx_ref[s, h:h + 1, :] = jnp.broadcast_to(m_new, (1, LANES))
    return outs


def _ml_init_state(c_ref, n_ref, mx_ref):
    c_ref[...] = jnp.zeros_like(c_ref)
    n_ref[...] = jnp.zeros_like(n_ref)
    mx_ref[...] = jnp.full(mx_ref.shape, NEG_BIG, F32)


def _ml_rev_kernel(ncc, nc, xm_ref, xp_ref, xn_ref, v_ref, g_ref, cw_ref, cb_ref, wq_ref, wk_ref, gbr_ref, gbc_ref,
                   o_ref, xc_ref, q_ref, k_ref, c_ref, n_ref, mx_ref):
    c = pl.program_id(1)
    ceff = _chunk_eff(True, ncc, nc, c)
    first = (ceff == 0) | (ceff == ncc)
    last = (ceff == ncc - 1) | (ceff == nc - 1)

    @pl.when(c == 0)
    def _():
        _ml_init_state(c_ref, n_ref, mx_ref)

    for s in range(NB):
        xconv = _conv_silu(xm_ref[s].astype(F32), xp_ref[s], xn_ref[s], cw_ref, cb_ref, first, last)
        xc_b = xconv.astype(BF16)
        q = _dot(xc_b, wq_ref[...])
        k = _dot(xc_b, wk_ref[...])
        q_b, k_b = q.astype(BF16), k.astype(BF16)
        xc_ref[s] = xc_b
        q_ref[s] = q_b
        k_ref[s] = k_b
        outs = _ml_chunk(True, q, k, q_b, k_b, v_ref[s], g_ref[s], gbr_ref, gbc_ref, c_ref, n_ref, mx_ref, s)
        o_ref[s] = jnp.concatenate(outs, axis=1).astype(o_ref.dtype)


def _ml_fwd_kernel(xc_ref, q_ref, k_ref, v_ref, og_ref, g_ref, gbr_ref, gbc_ref, hb_ref, nw_ref, sk_ref,
                   o_ref, c_ref, n_ref, mx_ref):
    @pl.when(pl.program_id(1) == 0)
    def _():
        _ml_init_state(c_ref, n_ref, mx_ref)

    for s in range(NB):
        q_b, k_b = q_ref[s], k_ref[s]
        outs = _ml_chunk(False, q_b.astype(F32), k_b.astype(F32), q_b, k_b, v_ref[s], g_ref[s], gbr_ref, gbc_ref,
                         c_ref, n_ref, mx_ref, s)
        normed = []
        for h in range(ML_HEADS):
            sl = slice(h * ML_HEADDIM, (h + 1) * ML_HEADDIM)
            hh = jax.nn.sigmoid(og_ref[s, :, sl].astype(F32)) * (outs[h] + hb_ref[s, :, sl].astype(F32))
            normed.append(hh * lax.rsqrt(jnp.mean(hh * hh, axis=-1, keepdims=True) + EPS))
        y = jnp.concatenate(normed, axis=1) * nw_ref[...] + sk_ref[...] * xc_ref[s].astype(F32)
        o_ref[s] = y.astype(o_ref.dtype)


def _ml_scans(ncc, ml, gates, conv_params, proj_params, gate_params, nw, sk):
    b, tot, _ = ml.shape
    nc = tot // CHUNK
    nl = nc - ncc
    scratch = [pltpu.VMEM((NB * ML_HEADS, ML_HEADDIM, ML_HEADDIM), F32),
               pltpu.VMEM((NB, 8, ML_HEADDIM), F32),
               pltpu.VMEM((NB, 8, LANES), F32)]
    params = pltpu.CompilerParams(dimension_semantics=("arbitrary", "arbitrary"), vmem_limit_bytes=VMEM_LIMIT)
    lat_shape = jax.ShapeDtypeStruct((b, nl * CHUNK, ML_WIDTH), BF16)
    act_shape = jax.ShapeDtypeStruct((b, tot, ML_WIDTH), BF16)

    ceff = functools.partial(_chunk_eff, True, ncc, nc)
    lat = functools.partial(_lat_block, True, ncc, nc)
    prev, nxt = _halo_specs(True, ncc, nc, ML_WIDTH)
    rev_params = list(conv_params) + list(proj_params) + list(gate_params)
    act_spec = _row_spec(CHUNK, ML_WIDTH, ceff)
    hb, xc, q, k = pl.pallas_call(
        functools.partial(_ml_rev_kernel, ncc, nc),
        grid=(b // NB, nc),
        in_specs=[act_spec, prev, nxt, _row_spec(CHUNK, ML_WIDTH, ceff, 1), _row_spec(CHUNK, LANES, ceff)]
        + [_full(p.shape) for p in rev_params],
        out_specs=[_row_spec(CHUNK, ML_WIDTH, lat), act_spec, act_spec, act_spec],
        out_shape=(lat_shape, act_shape, act_shape, act_shape),
        scratch_shapes=scratch,
        compiler_params=params,
        name="mlstm_rev",
    )(ml, ml, ml, ml, gates, *rev_params)

    ceff = functools.partial(_chunk_eff, False, ncc, nc)
    lat = functools.partial(_lat_block, False, ncc, nc)
    act_spec = _row_spec(CHUNK, ML_WIDTH, ceff)
    return pl.pallas_call(
        _ml_fwd_kernel,
        grid=(b // NB, nc),
        in_specs=[act_spec, act_spec, act_spec, _row_spec(CHUNK, ML_WIDTH, ceff, 1),
                  _row_spec(CHUNK, ML_WIDTH, ceff, 2), _row_spec(CHUNK, LANES, ceff)]
        + [_full(p.shape) for p in gate_params]
        + [_row_spec(CHUNK, ML_WIDTH, lat), _full(nw.shape), _full(sk.shape)],
        out_specs=_row_spec(CHUNK, ML_WIDTH, lat),
        out_shape=lat_shape,
        scratch_shapes=scratch,
        compiler_params=params,
        name="mlstm_fwd",
    )(xc, q, k, ml, ml, gates, *gate_params, hb, nw, sk)


def _out_kernel(x_ref, ys_ref, ym_ref, wo_ref, g1_ref, sh_ref, sc_ref, nw_ref, rwh_ref, rwl_ref, rb_ref,
                x1_ref, ts_ref, route_ref):
    mix = _dot(ys_ref[...], wo_ref[0:SSD_WIDTH, :]) + _dot(ym_ref[...], wo_ref[SSD_WIDTH:, :])
    x1 = x_ref[...] + g1_ref[...] * mix
    y = x1 * lax.rsqrt(jnp.mean(x1 * x1, axis=-1, keepdims=True) + EPS) * nw_ref[...]
    t = y * (1.0 + sc_ref[...]) + sh_ref[...]
    x1_ref[...] = x1
    for j in range(SLAB):
        ts_ref[pl.ds(j, TM_IN, stride=SLAB), :] = t[:, j * LANES:(j + 1) * LANES]
    t_hi = t.astype(BF16)
    t_lo = (t - t_hi.astype(F32)).astype(BF16)
    lg = _dot(t_hi, rwh_ref[...]) + _dot(t_lo, rwh_ref[...]) + _dot(t_hi, rwl_ref[...]) + rb_ref[...]

    lane = lax.broadcasted_iota(jnp.int32, lg.shape, 1).astype(F32)
    gmask = lane < MOE_GROUPS
    gl = jnp.where(gmask, lg, NEG_BIG)
    gmax = jnp.max(gl, axis=1, keepdims=True)
    g_sel = jnp.min(jnp.where(gmask & (gl == gmax), lane, 1e9), axis=1, keepdims=True)
    p_group = 1.0 / jnp.sum(jnp.where(gmask, jnp.exp(gl - gmax), 0.0), axis=1, keepdims=True)
    lo = ROUTE_LANE0 + MOE_EPG * g_sel
    emask = (lane >= lo) & (lane < lo + MOE_EPG)
    l1 = jnp.max(jnp.where(emask, lg, NEG_BIG), axis=1, keepdims=True)
    i1 = jnp.min(jnp.where(emask & (lg == l1), lane, 1e9), axis=1, keepdims=True)
    emask2 = emask & (lane != i1)
    l2 = jnp.max(jnp.where(emask2, lg, NEG_BIG), axis=1, keepdims=True)
    i2 = jnp.min(jnp.where(emask2 & (lg == l2), lane, 1e9), axis=1, keepdims=True)
    r = jnp.exp(l2 - l1)
    w1 = p_group / (1.0 + r)
    w2 = p_group * r / (1.0 + r)
    route = (jnp.where(lane == 0, i1, 0.0) + jnp.where(lane == 1, i2, 0.0)
             + jnp.where(lane == 2, w1, 0.0) + jnp.where(lane == 3, w2, 0.0))
    route_ref[...] = route.T[0:8, :]


def _out_proj(x, y_ssd, y_ml, w_out_b, mod3, norm_w, rw_hi, rw_lo, rb):
    b, seq, d = x.shape
    nt = seq // TM_IN

    def row(j):
        return pl.BlockSpec((None, 1, d), lambda bi, t: (bi * 6 + j, 0, 0))

    def tile(width):
        return pl.BlockSpec((None, TM_IN, width), lambda bi, t: (bi, t, 0))

    return pl.pallas_call(
        _out_kernel,
        grid=(b, nt),
        in_specs=[tile(d), tile(SSD_WIDTH), tile(ML_WIDTH), _full(w_out_b.shape),
                  row(2), row(3), row(4), _full((1, d)), _full(rw_hi.shape), _full(rw_lo.shape), _full(rb.shape)],
        out_specs=[tile(d),
                   pl.BlockSpec((None, TM_IN * SLAB, LANES), lambda bi, t: (bi, t, 0)),
                   pl.BlockSpec((None, 8, TM_IN), lambda bi, t: (bi, 0, t))],
        out_shape=(jax.ShapeDtypeStruct((b, seq, d), F32),
                   jax.ShapeDtypeStruct((b, seq * SLAB, LANES), F32),
                   jax.ShapeDtypeStruct((b, 8, seq), F32)),
        compiler_params=pltpu.CompilerParams(
            dimension_semantics=("arbitrary", "arbitrary"), vmem_limit_bytes=VMEM_LIMIT),
        name="out_proj_router",
    )(x, y_ssd, y_ml, w_out_b, mod3, mod3, mod3, norm_w.reshape(1, d), rw_hi, rw_lo, rb)


def _route_tables(route, seq):
    b = route.shape[0]
    n_inst = 2 * seq
    nblk = n_inst // BM + MOE_EXPERTS
    e_flat = (route[:, 0:2, :].astype(jnp.int32) - ROUTE_LANE0).reshape(b, n_inst)
    w_flat = route[:, 2:4, :].reshape(b, n_inst)
    tok_flat = jnp.tile(jnp.arange(seq, dtype=jnp.int32), 2 * b).reshape(b, n_inst)
    _, tok_sorted, w_sorted = lax.sort((e_flat, tok_flat, w_flat), dimension=1, num_keys=1)
    experts = jnp.arange(MOE_EXPERTS, dtype=jnp.int32)
    counts = jnp.sum((e_flat[:, None, :] == experts[None, :, None]).astype(jnp.int32), axis=2)
    nblk_e = (counts + BM - 1) // BM
    blk_end = jnp.cumsum(nblk_e, axis=1)
    blk_start = blk_end - nblk_e
    cnt_start = jnp.cumsum(counts, axis=1) - counts
    nb = blk_end[:, -1:]
    j = jnp.arange(nblk, dtype=jnp.int32)[None, :]
    valid_blk = j < nb
    jj = jnp.minimum(j, nb - 1)
    e_j = jnp.sum((jj[:, :, None] >= blk_end[:, None, :]).astype(jnp.int32), axis=2)
    onehot = (e_j[:, :, None] == experts[None, None, :]).astype(jnp.int32)
    take = lambda tbl: jnp.sum(onehot * tbl[:, None, :], axis=2)
    r = jnp.arange(BM, dtype=jnp.int32)[None, None, :]
    rank = ((jj - take(blk_start)) * BM)[:, :, None] + r
    valid = valid_blk[:, :, None] & (rank < take(counts)[:, :, None])
    sidx = jnp.clip(take(cnt_start)[:, :, None] + rank, 0, n_inst - 1).reshape(b, nblk * BM)
    tok = jnp.take_along_axis(tok_sorted, sidx, axis=1).reshape(b, nblk, BM)
    wslot = jnp.take_along_axis(w_sorted, sidx, axis=1).reshape(b, nblk, BM)
    tok = jnp.where(valid, tok, seq + r)
    wslot = jnp.where(valid, wslot, 0.0)
    return (nblk_e.reshape(-1), blk_start.reshape(-1), tok.reshape(b, 1, nblk * BM), wslot)


SEM_T, SEM_X, SEM_O = 0, 1, 3


def _moe_kernel(seq, nbe_ref, bs_ref, tok_ref, ws_ref, wg_ref, wu_ref, wd_ref, g2_ref, fw_ref,
                t_hbm, x1_hbm, o_hbm, t_scr, y_scr, wgb, wub, wdb, xt, ot, xin, stage, sems):
    b = pl.program_id(0)
    e = pl.program_id(1)
    nb = pl.num_programs(0)
    rows = seq * SLAB
    n_fin = seq // TM_FIN

    def t_copy(sample):
        return pltpu.make_async_copy(t_hbm.at[sample], t_scr.at[pl.ds(0, rows)], sems.at[SEM_T])

    def x1_copy(s, slot):
        return pltpu.make_async_copy(x1_hbm.at[b, pl.ds(s * TM_FIN, TM_FIN)], xin.at[slot], sems.at[SEM_X + slot])

    def out_copy(s, slot):
        return pltpu.make_async_copy(stage.at[slot], o_hbm.at[b, pl.ds(s * TM_FIN, TM_FIN)], sems.at[SEM_O + slot])

    @pl.when((e == 0) & (b == 0))
    def _():
        t_copy(b).start()

    @pl.when(e == 0)
    def _():
        y_scr[...] = jnp.zeros_like(y_scr)
        t_scr[pl.ds(rows, BM * SLAB), :] = jnp.zeros((BM * SLAB, LANES), F32)
        t_copy(b).wait()

    n_blocks = nbe_ref[b * MOE_EXPERTS + e]
    blk0 = bs_ref[b * MOE_EXPERTS + e]

    @pl.when(n_blocks > 0)
    def _():
        wgb[...] = wg_ref[...].astype(BF16)
        wub[...] = wu_ref[...].astype(BF16)
        wdb[...] = wd_ref[...].astype(BF16)
        g2 = g2_ref[...]
        diag = (lax.broadcasted_iota(jnp.int32, (BM, BM), 0) == lax.broadcasted_iota(jnp.int32, (BM, BM), 1))

        def block(i, carry):
            blk = blk0 + i
            base = blk * BM

            def slab_rows(r):
                return pl.ds(pl.multiple_of(tok_ref[0, base + r] * SLAB, SLAB), SLAB)

            for r in range(BM):
                xt[pl.ds(r, SLAB, stride=TILE_PITCH), :] = t_scr[slab_rows(r), :]
            x = jnp.concatenate([xt[c * TILE_PITCH:c * TILE_PITCH + BM, :] for c in range(SLAB)],
                                axis=1).astype(BF16)
            w_col = jnp.sum(jnp.where(diag, ws_ref[pl.ds(blk, 1), :], 0.0), axis=1, keepdims=True)
            hidden = _silu(_dot(x, wgb[...])) * _dot(x, wub[...]) * w_col
            out = _dot(hidden.astype(BF16), wdb[...]) * g2
            for c in range(SLAB):
                ot[c * TILE_PITCH:c * TILE_PITCH + BM, :] = out[:, c * LANES:(c + 1) * LANES]
            for r0 in range(0, BM, SCATTER_GROUP):
                sl = [slab_rows(r0 + u) for u in range(SCATTER_GROUP)]
                vals = [y_scr[sl[u], :] + ot[pl.ds(r0 + u, SLAB, stride=TILE_PITCH), :]
                        for u in range(SCATTER_GROUP)]
                for u in range(SCATTER_GROUP):
                    y_scr[sl[u], :] = vals[u]
            return carry

        lax.fori_loop(0, n_blocks, block, 0)

    @pl.when(e == MOE_EXPERTS - 1)
    def _():
        @pl.when(b + 1 < nb)
        def _():
            t_copy(b + 1).start()

        x1_copy(0, 0).start()

        def tile(s, carry):
            slot = s % 2
            x1_copy(s, slot).wait()

            @pl.when(s + 1 < n_fin)
            def _():
                x1_copy(s + 1, 1 - slot).start()

            @pl.when(s >= 2)
            def _():
                out_copy(s - 2, slot).wait()

            base = pl.multiple_of(s * (TM_FIN * SLAB), TM_FIN * SLAB)
            x1 = xin[slot]
            chunks = [x1[:, c * LANES:(c + 1) * LANES] + y_scr[pl.ds(base + c, TM_FIN, stride=SLAB), :]
                      for c in range(SLAB)]
            ssq = chunks[0] * chunks[0]
            for c in range(1, SLAB):
                ssq = ssq + chunks[c] * chunks[c]
            inv = lax.rsqrt(jnp.sum(ssq, axis=1, keepdims=True) * (1.0 / D_MODEL) + EPS)
            for c in range(SLAB):
                stage[slot, :, c * LANES:(c + 1) * LANES] = chunks[c] * inv * fw_ref[:, c * LANES:(c + 1) * LANES]
            out_copy(s, slot).start()
            return carry

        lax.fori_loop(0, n_fin, tile, 0)
        out_copy(n_fin - 2, n_fin % 2).wait()
        out_copy(n_fin - 1, (n_fin - 1) % 2).wait()


def _moe(t_slab, x1, route, wg, wu, wd, mod3, final_w):
    b, seq, d = x1.shape
    nblk_e, blk_start, tok, wslot = _route_tables(route, seq)
    nblk = wslot.shape[1]

    def w_spec(shape):
        return pl.BlockSpec((None,) + shape, lambda bi, e, nbe, bs: (e, 0, 0))

    grid_spec = pltpu.PrefetchScalarGridSpec(
        num_scalar_prefetch=2,
        grid=(b, MOE_EXPERTS),
        in_specs=[
            pl.BlockSpec((None, 1, nblk * BM), lambda bi, e, nbe, bs: (bi, 0, 0), memory_space=pltpu.SMEM),
            pl.BlockSpec((None, nblk, BM), lambda bi, e, nbe, bs: (bi, 0, 0)),
            w_spec((d, D_EXPERT)), w_spec((d, D_EXPERT)), w_spec((D_EXPERT, d)),
            pl.BlockSpec((None, 1, d), lambda bi, e, nbe, bs: (bi * 6 + 5, 0, 0)),
            pl.BlockSpec((1, d), lambda bi, e, nbe, bs: (0, 0)),
            pl.BlockSpec(memory_space=pl.ANY),
            pl.BlockSpec(memory_space=pl.ANY),
        ],
        out_specs=pl.BlockSpec(memory_space=pl.ANY),
        scratch_shapes=[
            pltpu.VMEM(((seq + BM) * SLAB, LANES), F32),
            pltpu.VMEM(((seq + BM) * SLAB, LANES), F32),
            pltpu.VMEM((d, D_EXPERT), BF16),
            pltpu.VMEM((d, D_EXPERT), BF16),
            pltpu.VMEM((D_EXPERT, d), BF16),
            pltpu.VMEM((SLAB * TILE_PITCH, LANES), F32),
            pltpu.VMEM((SLAB * TILE_PITCH, LANES), F32),
            pltpu.VMEM((2, TM_FIN, d), F32),
            pltpu.VMEM((2, TM_FIN, d), F32),
            pltpu.SemaphoreType.DMA((5,)),
        ],
    )
    return pl.pallas_call(
        functools.partial(_moe_kernel, seq),
        grid_spec=grid_spec,
        out_shape=jax.ShapeDtypeStruct((b, seq, d), F32),
        compiler_params=pltpu.CompilerParams(
            dimension_semantics=("arbitrary", "arbitrary"), vmem_limit_bytes=VMEM_LIMIT_MOE),
        name="experts_final_norm",
    )(nblk_e, blk_start, tok, wslot, wg, wu, wd, mod3, final_w.reshape(1, d), t_slab, x1)


def _lane_pad(v, offset=0):
    v = v.reshape(-1).astype(F32)
    return jnp.zeros((1, LANES), F32).at[0, offset:offset + v.shape[0]].set(v)


def _to_col_major(t, rows):
    b, length, ch = t.shape
    return t.reshape(b, rows, GRID_W, ch).transpose(0, 2, 1, 3).reshape(b, length, ch)


def _from_col_major(t, rows):
    b, length, ch = t.shape
    return t.reshape(b, GRID_W, rows, ch).transpose(0, 2, 1, 3).reshape(b, length, ch)


def kernel(x, c, ctx, c_ctx, w_mod, b_mod, norm1_w, w_in, ssd_conv_w, ssd_conv_b, ssd_dt_bias, ssd_a_log, ssd_d, ssd_norm_w, ml_conv_w, ml_conv_b, ml_w_qk, ml_gate_b, ml_norm_w, ml_skip, w_out, norm2_w, moe_rg_w, moe_rg_b, moe_re_w, moe_re_b, moe_w_gate, moe_w_up, moe_w_down, final_norm_w):
    b, seq, d = x.shape
    ctx_len = ctx.shape[1]
    rows = seq // GRID_W
    ncc = ctx_len // CHUNK
    assert w_mod.shape[0] == 1 and d == D_MODEL and b + 1 <= 8 and b % NB == 0
    assert seq % TM_IN == 0 and ctx_len % TM_IN == 0 and seq % TM_FIN == 0 and (2 * seq) % BM == 0

    c_all = jnp.zeros((8, d), F32).at[:b].set(c).at[b].set(c_ctx)
    mod = _modulation(c_all, w_mod[0], b_mod[0])
    mod3 = mod.reshape(8 * 6, 1, d)

    w = w_in[0]
    ssd_in = SSD_WIDTH + SSD_XBC + 2 * SSD_HEADS
    ml_main = 3 * ML_WIDTH
    n_gate = 2 * SSD_HEADS + 4 * ML_HEADS
    w_cat = jnp.concatenate([
        w[:, :SSD_WIDTH + SSD_XBC], w[:, ssd_in:ssd_in + ml_main],
        w[:, SSD_WIDTH + SSD_XBC:ssd_in], w[:, ssd_in + ml_main:],
        jnp.zeros((d, LANES - n_gate), F32)], axis=1).astype(BF16)
    z, xbc, ml, gates = _in_proj(x, ctx, mod3, norm1_w[0], w_cat)

    e_mats = []
    for direction in range(2):
        lane = jnp.arange(LANES)[:, None]
        head = (jnp.arange(SSD_WIDTH) // SSD_HEADDIM)[None, :]
        e_mats.append((lane == direction * SSD_HEADS + head).astype(BF16))
    cw = jnp.zeros((8, SSD_XBC), F32).at[:CONV_W].set(ssd_conv_w[0])
    dtb = _lane_pad(ssd_dt_bias[0])
    alog = _lane_pad(ssd_a_log[0])
    dsk = jnp.repeat(ssd_d[0], SSD_HEADDIM).reshape(1, SSD_WIDTH)
    y_ssd = _ssd_scans(ncc, xbc, gates, z, [cw, ssd_conv_b[0].reshape(1, -1)],
                       [dtb, dtb.reshape(LANES, 1), alog, alog.reshape(LANES, 1)], e_mats, dsk,
                       ssd_norm_w[0].reshape(1, -1))

    ml_cm = jnp.concatenate([ml[:, :ctx_len], _to_col_major(ml[:, ctx_len:], rows)], axis=1)
    g_cm = jnp.concatenate([gates[:, :ctx_len], _to_col_major(gates[:, ctx_len:], rows)], axis=1)
    w_rows = jnp.tile(ml_w_qk[0].reshape(2, ML_WIDTH, ML_QK_BLOCK), (1, 1, ML_WIDTH // ML_QK_BLOCK))
    blk_id = jnp.arange(ML_WIDTH) // ML_QK_BLOCK
    w_bd = jnp.where((blk_id[:, None] == blk_id[None, :])[None], w_rows, 0.0)
    wq = w_bd[0].astype(BF16)
    wk = (w_bd[1] * (ML_HEADDIM ** -0.5)).astype(BF16)
    mcw = jnp.zeros((8, ML_WIDTH), F32).at[:CONV_W].set(ml_conv_w[0])
    gb = _lane_pad(ml_gate_b[0], offset=2 * SSD_HEADS)
    y_ml_cm = _ml_scans(ncc, ml_cm, g_cm, [mcw, ml_conv_b[0].reshape(1, -1)], [wq, wk], [gb, gb.reshape(LANES, 1)],
                        ml_norm_w[0].reshape(1, -1), ml_skip[0].reshape(1, -1))
    y_ml = _from_col_major(y_ml_cm, rows)

    rw = jnp.concatenate([moe_rg_w[0], moe_re_w[0],
                          jnp.zeros((d, LANES - MOE_GROUPS - MOE_EXPERTS), F32)], axis=1)
    rw_hi = rw.astype(BF16)
    rw_lo = (rw - rw_hi.astype(F32)).astype(BF16)
    rb = _lane_pad(jnp.concatenate([moe_rg_b[0], moe_re_b[0]]))
    x1, t_slab, route = _out_proj(x, y_ssd, y_ml, w_out[0].astype(BF16), mod3, norm2_w[0], rw_hi, rw_lo, rb)

    return _moe(t_slab, x1, route, moe_w_gate[0], moe_w_up[0], moe_w_down[0], mod3, final_norm_w)
```

```python
import functools

import jax
import jax.numpy as jnp
from jax import lax
from jax.experimental import pallas as pl
from jax.experimental.pallas import tpu as pltpu

F32 = jnp.float32
BF16 = jnp.bfloat16
HIGHEST = lax.Precision.HIGHEST

D_MODEL = 1024
GRID_W = 64
EPS = 1e-6
CONV_W = 5
NEG_BIG = -1e30
CHUNK = 128
LANES = 128
HALO = 16
NB = 2

SSD_WIDTH = 512
SSD_HEADS = 8
SSD_HEADDIM = 64
SSD_GROUPS = 2
SSD_STATE = 128
SSD_XBC = SSD_WIDTH + 2 * SSD_GROUPS * SSD_STATE

ML_WIDTH = 512
ML_HEADS = 4
ML_HEADDIM = 128
ML_QK_BLOCK = 4

MOE_GROUPS = 4
MOE_EPG = 8
MOE_EXPERTS = 32
D_EXPERT = 256
ROUTE_LANE0 = MOE_GROUPS

TM_IN = 256
TM_FIN = 256
SLAB = D_MODEL // LANES
BM = 128
TILE_PITCH = BM + 8
SCATTER_GROUP = 4
VMEM_LIMIT = 48 * 1024 * 1024
VMEM_LIMIT_MOE = 56 * 1024 * 1024


def _silu(v):
    return v * jax.nn.sigmoid(v)


def _softplus(v):
    return jnp.maximum(v, 0.0) + jnp.log1p(jnp.exp(-jnp.abs(v)))


def _dot(a, b):
    return jnp.dot(a, b, preferred_element_type=F32)


def _dot_nt(a, b):
    return lax.dot_general(a, b, (((1,), (1,)), ((), ())), preferred_element_type=F32)


def _dot_hi(a, b):
    return jnp.dot(a, b, preferred_element_type=F32, precision=HIGHEST)


def _mod_kernel(c_ref, w_ref, b_ref, o_ref):
    c = c_ref[...]
    o_ref[...] = _dot_hi(_silu(c), w_ref[...]) + b_ref[...]


def _modulation(c_all, w_mod, b_mod):
    n = w_mod.shape[1]
    bn = 1536
    return pl.pallas_call(
        _mod_kernel,
        grid=(n // bn,),
        in_specs=[
            pl.BlockSpec((8, D_MODEL), lambda j: (0, 0)),
            pl.BlockSpec((D_MODEL, bn), lambda j: (0, j)),
            pl.BlockSpec((1, bn), lambda j: (0, j)),
        ],
        out_specs=pl.BlockSpec((8, bn), lambda j: (0, j)),
        out_shape=jax.ShapeDtypeStruct((8, n), F32),
        compiler_params=pltpu.CompilerParams(vmem_limit_bytes=VMEM_LIMIT),
        name="modulation",
    )(c_all, w_mod, b_mod.reshape(1, n))


W_TILE = 16
COL_Z, COL_XBC, COL_ML, COL_G = 0, SSD_WIDTH, SSD_WIDTH + SSD_XBC, SSD_WIDTH + SSD_XBC + 3 * ML_WIDTH
PROJ_CHUNK = 512


def _norm_mod(xin, sh_ref, sc_ref, nw_ref):
    ms = jnp.mean(xin * xin, axis=-1, keepdims=True)
    y = xin * lax.rsqrt(ms + EPS) * nw_ref[...]
    return (y * (1.0 + sc_ref[...]) + sh_ref[...]).astype(BF16)


def _in_lat_kernel(rows, x_ref, sh_ref, sc_ref, nw_ref, w_ref, z_ref, xbc_ref, ml_ref, g_ref, gml_ref, scr):
    tm = rows * W_TILE
    h = _norm_mod(x_ref[...].reshape(tm, D_MODEL), sh_ref, sc_ref, nw_ref)

    def proj(col, width=PROJ_CHUNK):
        return _dot(h, w_ref[:, col:col + width])

    def to_col_major(val, dst_ref, lo, dtype):
        n_slab = val.shape[1] // LANES
        for k in range(n_slab):
            scr[k] = val[:, k * LANES:(k + 1) * LANES]
        for j in range(W_TILE):
            for k in range(n_slab):
                dst_ref[j, :, lo + k * LANES:lo + (k + 1) * LANES] = (
                    scr[k, pl.ds(j, rows, stride=W_TILE), :].astype(dtype))

    z_ref[...] = proj(COL_Z).astype(BF16).reshape(rows, W_TILE, PROJ_CHUNK)
    for j in range(SSD_XBC // PROJ_CHUNK):
        lo = j * PROJ_CHUNK
        xbc_ref[:, :, lo:lo + PROJ_CHUNK] = proj(COL_XBC + lo).astype(BF16).reshape(rows, W_TILE, PROJ_CHUNK)
    for j in range(3 * ML_WIDTH // PROJ_CHUNK):
        to_col_major(proj(COL_ML + j * PROJ_CHUNK), ml_ref, j * PROJ_CHUNK, BF16)
    g = proj(COL_G, LANES)
    g_ref[...] = g.reshape(rows, W_TILE, LANES)
    to_col_major(g, gml_ref, 0, F32)


def _in_ctx_kernel(x_ref, sh_ref, sc_ref, nw_ref, w_ref, xbc_ref, ml_ref, g_ref):
    h = _norm_mod(x_ref[...], sh_ref, sc_ref, nw_ref)
    for j in range(SSD_XBC // PROJ_CHUNK):
        lo = j * PROJ_CHUNK
        xbc_ref[:, lo:lo + PROJ_CHUNK] = _dot(h, w_ref[:, COL_XBC + lo:COL_XBC + lo + PROJ_CHUNK]).astype(BF16)
    for j in range(3 * ML_WIDTH // PROJ_CHUNK):
        lo = j * PROJ_CHUNK
        ml_ref[:, lo:lo + PROJ_CHUNK] = _dot(h, w_ref[:, COL_ML + lo:COL_ML + lo + PROJ_CHUNK]).astype(BF16)
    g_ref[...] = _dot(h, w_ref[:, COL_G:COL_G + LANES])


def _in_proj(x, ctx, mod3, norm_w, w_cat):
    b, seq, d = x.shape
    ctx_len = ctx.shape[1]
    rows = seq // GRID_W
    widths = (SSD_WIDTH, SSD_XBC, 3 * ML_WIDTH, LANES, LANES)
    dtypes = (BF16, BF16, BF16, F32, F32)
    params = pltpu.CompilerParams(dimension_semantics=("arbitrary", "arbitrary"), vmem_limit_bytes=VMEM_LIMIT)
    nw = norm_w.reshape(1, d)

    def raster(width):
        return pl.BlockSpec((None, rows, W_TILE, width), lambda bi, wi: (bi, 0, wi, 0))

    def col_major(width):
        return pl.BlockSpec((None, W_TILE, rows, width), lambda bi, wi: (bi, wi, 0, 0))

    outs = pl.pallas_call(
        functools.partial(_in_lat_kernel, rows),
        grid=(b, GRID_W // W_TILE),
        in_specs=[
            raster(d),
            pl.BlockSpec((None, 1, d), lambda bi, wi: (bi * 6, 0, 0)),
            pl.BlockSpec((None, 1, d), lambda bi, wi: (bi * 6 + 1, 0, 0)),
            _full((1, d)), _full(w_cat.shape),
        ],
        out_specs=[raster(SSD_WIDTH), raster(SSD_XBC), col_major(3 * ML_WIDTH), raster(LANES), col_major(LANES)],
        out_shape=[jax.ShapeDtypeStruct((b, GRID_W, rows, w) if cm else (b, rows, GRID_W, w), t)
                   for w, t, cm in zip(widths, dtypes, (False, False, True, False, True))],
        scratch_shapes=[pltpu.VMEM((PROJ_CHUNK // LANES, rows * W_TILE, LANES), F32)],
        compiler_params=params,
        name="in_proj",
    )(x.reshape(b, rows, GRID_W, d), mod3, mod3, nw, w_cat)
    z, xbc, ml, gates, gates_cm = [o.reshape(b, seq, w) for o, w in zip(outs, widths)]

    ctx_row = b
    ctx_widths = (SSD_XBC, 3 * ML_WIDTH, LANES)
    xbc_c, ml_c, gates_c = pl.pallas_call(
        _in_ctx_kernel,
        grid=(b, 1),
        in_specs=[
            pl.BlockSpec((None, ctx_len, d), lambda bi, t: (bi, 0, 0)),
            pl.BlockSpec((None, 1, d), lambda bi, t: (ctx_row * 6, 0, 0)),
            pl.BlockSpec((None, 1, d), lambda bi, t: (ctx_row * 6 + 1, 0, 0)),
            _full((1, d)), _full(w_cat.shape),
        ],
        out_specs=[pl.BlockSpec((None, ctx_len, w), lambda bi, t: (bi, 0, 0)) for w in ctx_widths],
        out_shape=[jax.ShapeDtypeStruct((b, ctx_len, w), t) for w, t in zip(ctx_widths, (BF16, BF16, F32))],
        compiler_params=params,
        name="in_proj_ctx",
    )(ctx, mod3, mod3, nw, w_cat)
    return z, (xbc, xbc_c), (ml, ml_c), (gates, gates_c), (gates_cm, gates_c)


def _chunk_eff(rev, ncc, nc, c):
    nl = nc - ncc
    if not rev:
        return jnp.where(c < ncc, nl + c, c - ncc)
    return jnp.where(c < ncc, nc - 1 - c, nl - 1 - (c - ncc))


def _seq_ends(ncc, nc, ceff):
    nl = nc - ncc
    return (ceff == 0) | (ceff == nl), (ceff == nl - 1) | (ceff == nc - 1)


def _lat_block(rev, ncc, nc, c):
    nl = nc - ncc
    if not rev:
        return jnp.maximum(c - ncc, 0)
    return jnp.where(c < ncc, nl - 1, nl - 1 - (c - ncc))


def _conv_silu(xm, xp, xn, cw_ref, cb_ref, first, last):
    xp = xp.astype(F32)[HALO - 8:HALO] * jnp.where(first, 0.0, 1.0)
    xn = xn.astype(F32)[0:8] * jnp.where(last, 0.0, 1.0)
    ext = jnp.concatenate([xp, xm, xn], axis=0)
    acc = cb_ref[...]
    for tap in range(CONV_W):
        off = 8 - CONV_W // 2 + tap
        acc = acc + cw_ref[tap:tap + 1, :] * ext[off:off + CHUNK]
    return _silu(acc)


def _tri(rev, transposed=False):
    row = lax.broadcasted_iota(jnp.int32, (CHUNK, CHUNK), 0)
    col = lax.broadcasted_iota(jnp.int32, (CHUNK, CHUNK), 1)
    if transposed:
        row, col = col, row
    keep = (col >= row) if rev else (col <= row)
    return keep


def _split3(a):
    a1 = a.astype(BF16)
    r1 = a - a1.astype(F32)
    a2 = r1.astype(BF16)
    a3 = (r1 - a2.astype(F32)).astype(BF16)
    return a1, a2, a3


def _dot_sel_l(m_b, a):
    p = _split3(a)
    return _dot(m_b, p[0]) + _dot(m_b, p[1]) + _dot(m_b, p[2])


def _dot_sel_r(a, m_b):
    p = _split3(a)
    return _dot(p[0], m_b) + _dot(p[1], m_b) + _dot(p[2], m_b)


def _row_spec(rows, width, row_block, col_block=0):
    return pl.BlockSpec((NB, rows, width), lambda bi, c: (bi, row_block(c), col_block))


def _local_chunks(rev, ncc, nc, c):
    nl = nc - ncc
    cc = jnp.clip(ncc - 1 - c if rev else c, 0, ncc - 1)
    cl = jnp.clip(nl - 1 - (c - ncc) if rev else c - ncc, 0, nl - 1)
    return c < ncc, cc, cl


def _pair_specs(rev, ncc, nc, rows, width, col=0, halo=0):
    per = CHUNK // HALO

    def index(which, count):
        def fn(c):
            ch = _local_chunks(rev, ncc, nc, c)[which]
            if halo == 0:
                return ch
            if halo < 0:
                return jnp.maximum(ch * per - 1, 0)
            return jnp.minimum((ch + 1) * per, count * per - 1)
        return fn

    return [_row_spec(rows, width, index(2, nc - ncc), col), _row_spec(rows, width, index(1, ncc), col)]


def _pick(is_ctx, lat_ref, ctx_ref, s):
    return jnp.where(is_ctx, ctx_ref[s], lat_ref[s])


def _step_ends(rev, ncc, nc, c):
    is_ctx, cc, cl = _local_chunks(rev, ncc, nc, c)
    first = jnp.where(is_ctx, cc == 0, cl == 0)
    last = jnp.where(is_ctx, cc == ncc - 1, cl == nc - ncc - 1)
    return is_ctx, first, last


def _full(shape):
    return pl.BlockSpec(shape, lambda bi, c: (0,) * len(shape))


def _ssd_chunk(rev, xbc, g, dtbr_ref, dtbc_ref, alr_ref, alc_ref, e_ref, s_ref, s):
    xs = xbc[:, :SSD_WIDTH]
    lane0 = SSD_HEADS * int(rev)
    lane = lax.broadcasted_iota(jnp.int32, (CHUNK, LANES), 1)
    lmask = (lane >= lane0) & (lane < lane0 + SSD_HEADS)
    dt = jnp.where(lmask, _softplus(g + dtbr_ref[...]), 0.0)
    a = dt * (-jnp.exp(alr_ref[...]))
    gt = g.T
    dt_t = _softplus(gt + dtbc_ref[...])[lane0:lane0 + SSD_HEADS]
    a_t = dt_t * (-jnp.exp(alc_ref[...][lane0:lane0 + SSD_HEADS]))

    cs = _dot_sel_l(_tri(rev).astype(BF16), a)
    cs_t = _dot_sel_r(a_t, _tri(rev, transposed=True).astype(BF16))
    e = e_ref[...]
    dtx = _dot_sel_r(dt, e)
    csx = _dot_sel_r(cs, e)
    end = 0 if rev else CHUNK - 1
    totx = csx[end:end + 1, :]
    ecsx = jnp.exp(csx)
    decx = jnp.exp(totx - csx)
    etotx = jnp.exp(totx)

    xdt = xs * dtx
    xdt_b = xdt.astype(BF16)
    xd_b = (xdt * decx).astype(BF16)
    keep = _tri(rev)
    half = lax.broadcasted_iota(jnp.int32, (CHUNK, LANES), 1) // SSD_HEADDIM

    y_blocks = []
    gs = SSD_GROUPS * SSD_STATE
    hpg = SSD_HEADS // SSD_GROUPS
    gw = hpg * SSD_HEADDIM
    for grp in range(SSD_GROUPS):
        bm = xbc[:, SSD_WIDTH + grp * SSD_STATE:SSD_WIDTH + (grp + 1) * SSD_STATE]
        cm = xbc[:, SSD_WIDTH + gs + grp * SSD_STATE:SSD_WIDTH + gs + (grp + 1) * SSD_STATE]
        bm_b, cm_b = bm.astype(BF16), cm.astype(BF16)
        cb = _dot_nt(cm_b, bm_b)
        s_old = s_ref[s, :, grp * gw:(grp + 1) * gw]
        y_off = _dot(cm_b, s_old.astype(BF16)) * ecsx[:, grp * gw:(grp + 1) * gw]
        for pair in range(hpg // 2):
            blk = grp * (hpg // 2) + pair
            xj = xdt_b[:, blk * LANES:(blk + 1) * LANES]
            acc = y_off[:, pair * LANES:(pair + 1) * LANES]
            for q in range(2):
                h = blk * 2 + q
                dl = cs[:, lane0 + h:lane0 + h + 1] - cs_t[h:h + 1, :]
                m = (cb * jnp.exp(jnp.where(keep, dl, NEG_BIG))).astype(BF16)
                acc = acc + _dot(m, jnp.where(half == q, xj, jnp.zeros_like(xj)))
            y_blocks.append(acc)
        s_new = s_old * etotx[:, grp * gw:(grp + 1) * gw] + _dot(bm.T.astype(BF16), xd_b[:, grp * gw:(grp + 1) * gw])
        s_ref[s, :, grp * gw:(grp + 1) * gw] = s_new

    return jnp.concatenate(y_blocks, axis=1), xs


def _ssd_rev_kernel(ncc, nc, xml_ref, xmc_ref, xpl_ref, xpc_ref, xnl_ref, xnc_ref, gl_ref, gc_ref, cw_ref, cb_ref,
                    dtbr_ref, dtbc_ref, alr_ref, alc_ref, e_ref, o_ref, xc_ref, s_ref):
    c = pl.program_id(1)
    is_ctx, first, last = _step_ends(True, ncc, nc, c)

    @pl.when(c == 0)
    def _():
        s_ref[...] = jnp.zeros_like(s_ref)

    for s in range(NB):
        xbc = _conv_silu(_pick(is_ctx, xml_ref, xmc_ref, s).astype(F32), _pick(is_ctx, xpl_ref, xpc_ref, s),
                         _pick(is_ctx, xnl_ref, xnc_ref, s), cw_ref, cb_ref, first, last)
        xc_ref[s] = xbc.astype(BF16)
        y, _ = _ssd_chunk(True, xbc, _pick(is_ctx, gl_ref, gc_ref, s), dtbr_ref, dtbc_ref, alr_ref, alc_ref, e_ref,
                          s_ref, s)
        o_ref[s] = y.astype(o_ref.dtype)


def _ssd_fwd_kernel(ncc, xc_ref, gl_ref, gc_ref, dtbr_ref, dtbc_ref, alr_ref, alc_ref, e_ref, dsk_ref, yb_ref,
                    z_ref, nw_ref, o_ref, s_ref):
    is_ctx = pl.program_id(1) < ncc

    @pl.when(pl.program_id(1) == 0)
    def _():
        s_ref[...] = jnp.zeros_like(s_ref)

    for s in range(NB):
        y, xs = _ssd_chunk(False, xc_ref[s].astype(F32), _pick(is_ctx, gl_ref, gc_ref, s), dtbr_ref, dtbc_ref,
                           alr_ref, alc_ref, e_ref, s_ref, s)
        y = y + yb_ref[s].astype(F32) + dsk_ref[...] * xs
        y = y * _silu(z_ref[s].astype(F32))
        y = y * lax.rsqrt(jnp.mean(y * y, axis=-1, keepdims=True) + EPS) * nw_ref[...]
        o_ref[s] = y.astype(o_ref.dtype)


def _ssd_scans(xbc, gates, z, conv_params, gate_params, e_mats, dsk, nw):
    b, seq, _ = xbc[0].shape
    ncc = xbc[1].shape[1] // CHUNK
    nc = seq // CHUNK + ncc
    state = pltpu.VMEM((NB, SSD_STATE, SSD_WIDTH), F32)
    params = pltpu.CompilerParams(dimension_semantics=("arbitrary", "arbitrary"), vmem_limit_bytes=VMEM_LIMIT)
    lat_shape = jax.ShapeDtypeStruct((b, seq, SSD_WIDTH), BF16)

    ceff = functools.partial(_chunk_eff, True, ncc, nc)
    lat = functools.partial(_lat_block, True, ncc, nc)
    pair = functools.partial(_pair_specs, True, ncc, nc)
    rev_params = list(conv_params) + list(gate_params) + [e_mats[1]]
    yb, xbc_act = pl.pallas_call(
        functools.partial(_ssd_rev_kernel, ncc, nc),
        grid=(b // NB, nc),
        in_specs=pair(CHUNK, SSD_XBC) + pair(HALO, SSD_XBC, halo=-1) + pair(HALO, SSD_XBC, halo=1)
        + pair(CHUNK, LANES) + [_full(p.shape) for p in rev_params],
        out_specs=[_row_spec(CHUNK, SSD_WIDTH, lat), _row_spec(CHUNK, SSD_XBC, ceff)],
        out_shape=(lat_shape, jax.ShapeDtypeStruct((b, nc * CHUNK, SSD_XBC), BF16)),
        scratch_shapes=[state],
        compiler_params=params,
        name="ssd_rev",
    )(*xbc, *xbc, *xbc, *gates, *rev_params)

    ceff = functools.partial(_chunk_eff, False, ncc, nc)
    lat = functools.partial(_lat_block, False, ncc, nc)
    fwd_params = list(gate_params) + [e_mats[0], dsk]
    return pl.pallas_call(
        functools.partial(_ssd_fwd_kernel, ncc),
        grid=(b // NB, nc),
        in_specs=[_row_spec(CHUNK, SSD_XBC, ceff)] + _pair_specs(False, ncc, nc, CHUNK, LANES)
        + [_full(p.shape) for p in fwd_params]
        + [_row_spec(CHUNK, SSD_WIDTH, lat), _row_spec(CHUNK, SSD_WIDTH, lat), _full(nw.shape)],
        out_specs=_row_spec(CHUNK, SSD_WIDTH, lat),
        out_shape=lat_shape,
        scratch_shapes=[state],
        compiler_params=params,
        name="ssd_fwd",
    )(xbc_act, *gates, *fwd_params, yb, z, nw)


ML_I_LANE0 = 2 * SSD_HEADS
ML_F_LANE0 = ML_I_LANE0 + 2 * ML_HEADS


def _ml_chunk(rev, q, k, q_b, k_b, v_b, g, gbr_ref, gbc_ref, c_ref, n_ref, mx_ref, s):
    i_lane0 = ML_I_LANE0 + ML_HEADS * int(rev)
    f_lane0 = ML_F_LANE0 + ML_HEADS * int(rev)
    ga = g + gbr_ref[...]
    lane = lax.broadcasted_iota(jnp.int32, (CHUNK, LANES), 1)
    logf = jnp.where((lane >= f_lane0) & (lane < f_lane0 + ML_HEADS), -_softplus(-ga), 0.0)
    cs = _dot_sel_l(_tri(rev).astype(BF16), logf)
    gt = g.T + gbc_ref[...]
    i_t = gt[ML_I_LANE0:ML_F_LANE0]
    logf_t = -_softplus(-gt[ML_F_LANE0:ML_F_LANE0 + 2 * ML_HEADS])
    cs_t = _dot_sel_r(logf_t, _tri(rev, transposed=True).astype(BF16))
    keep = _tri(rev)
    end = 0 if rev else CHUNK - 1

    outs = []
    for h in range(ML_HEADS):
        r = ML_HEADS * int(rev) + h
        sl = slice(h * ML_HEADDIM, (h + 1) * ML_HEADDIM)
        qh, kh = q[:, sl], k[:, sl]
        vh = v_b[:, sl]
        qh_b, kh_b = q_b[:, sl], k_b[:, sl]
        csc = cs[:, f_lane0 + h:f_lane0 + h + 1]
        csr = cs_t[r:r + 1, :]
        ic = ga[:, i_lane0 + h:i_lane0 + h + 1]
        ir = i_t[r:r + 1, :]
        tot = csr[:, end:end + 1]
        m_prev = mx_ref[s, h:h + 1, 0:1]
        c_prev = c_ref[s * ML_HEADS + h]
        n_prev = n_ref[s, h:h + 1, :]

        dlog = jnp.where(keep, csc - csr + ir, NEG_BIG)
        g_inter = csc + m_prev
        m_out = jnp.maximum(g_inter, jnp.max(dlog, axis=1, keepdims=True))
        scores = _dot_nt(qh_b, kh_b) * jnp.exp(dlog - m_out)
        w_inter = jnp.exp(g_inter - m_out)
        num = _dot(scores.astype(BF16), vh) + w_inter * _dot_nt(qh_b, c_prev.astype(BF16))
        den = jnp.sum(scores, axis=1, keepdims=True) + w_inter * jnp.sum(qh * n_prev, axis=1, keepdims=True)
        outs.append(num / jnp.maximum(jnp.abs(den), jnp.exp(-m_out)))

        a_r = tot - csr + ir
        m_loc = jnp.max(a_r, axis=1, keepdims=True)
        kw = kh * jnp.exp(tot - csc + ic - m_loc)
        c_loc = _dot(vh.astype(F32).T.astype(BF16), kw.astype(BF16))
        n_loc = jnp.sum(kw, axis=0, keepdims=True)
        m_new = jnp.maximum(tot + m_prev, m_loc)
        s_prev = jnp.exp(tot + m_prev - m_new)
        s_loc = jnp.exp(m_loc - m_new)
        c_ref[s * ML_HEADS + h] = s_prev * c_prev + s_loc * c_loc
        n_ref[s, h:h + 1, :] = s_prev * n_prev + s_loc * n_loc
        mx_ref[s, h:h + 1, :] = jnp.broadcast_to(m_new, (1, LANES))
    return outs


def _ml_init_state(c_ref, n_ref, mx_ref):
    c_ref[...] = jnp.zeros_like(c_ref)
    n_ref[...] = jnp.zeros_like(n_ref)
    mx_ref[...] = jnp.full(mx_ref.shape, NEG_BIG, F32)


def _ml_rev_kernel(ncc, nc, xml_ref, xmc_ref, xpl_ref, xpc_ref, xnl_ref, xnc_ref, vl_ref, vc_ref, gl_ref, gc_ref,
                   cw_ref, cb_ref, wq_ref, wk_ref, gbr_ref, gbc_ref, o_ref, xc_ref, q_ref, k_ref, c_ref, n_ref, mx_ref):
    c = pl.program_id(1)
    is_ctx, first, last = _step_ends(True, ncc, nc, c)

    @pl.when(c == 0)
    def _():
        _ml_init_state(c_ref, n_ref, mx_ref)

    for s in range(NB):
        xconv = _conv_silu(_pick(is_ctx, xml_ref, xmc_ref, s).astype(F32), _pick(is_ctx, xpl_ref, xpc_ref, s),
                           _pick(is_ctx, xnl_ref, xnc_ref, s), cw_ref, cb_ref, first, last)
        v_b = _pick(is_ctx, vl_ref, vc_ref, s)
        g = _pick(is_ctx, gl_ref, gc_ref, s)
        xc_b = xconv.astype(BF16)
        q = _dot(xc_b, wq_ref[...])
        k = _dot(xc_b, wk_ref[...])
        q_b, k_b = q.astype(BF16), k.astype(BF16)
        xc_ref[s] = xc_b
        q_ref[s] = q_b
        k_ref[s] = k_b
        outs = _ml_chunk(True, q, k, q_b, k_b, v_b, g, gbr_ref, gbc_ref, c_ref, n_ref, mx_ref, s)
        o_ref[s] = jnp.concatenate(outs, axis=1).astype(o_ref.dtype)


def _ml_fwd_kernel(ncc, xc_ref, q_ref, k_ref, vl_ref, vc_ref, og_ref, gl_ref, gc_ref, gbr_ref, gbc_ref, hb_ref,
                   nw_ref, sk_ref, o_ref, c_ref, n_ref, mx_ref):
    is_ctx = pl.program_id(1) < ncc

    @pl.when(pl.program_id(1) == 0)
    def _():
        _ml_init_state(c_ref, n_ref, mx_ref)

    for s in range(NB):
        q_b, k_b = q_ref[s], k_ref[s]
        outs = _ml_chunk(False, q_b.astype(F32), k_b.astype(F32), q_b, k_b, _pick(is_ctx, vl_ref, vc_ref, s),
                         _pick(is_ctx, gl_ref, gc_ref, s), gbr_ref, gbc_ref, c_ref, n_ref, mx_ref, s)
        normed = []
        for h in range(ML_HEADS):
            sl = slice(h * ML_HEADDIM, (h + 1) * ML_HEADDIM)
            hh = jax.nn.sigmoid(og_ref[s, :, sl].astype(F32)) * (outs[h] + hb_ref[s, :, sl].astype(F32))
            normed.append(hh * lax.rsqrt(jnp.mean(hh * hh, axis=-1, keepdims=True) + EPS))
        y = jnp.concatenate(normed, axis=1) * nw_ref[...] + sk_ref[...] * xc_ref[s].astype(F32)
        o_ref[s] = y.astype(o_ref.dtype)


def _ml_scans(ml, gates, conv_params, proj_params, gate_params, nw, sk):
    b, seq, _ = ml[0].shape
    ncc = ml[1].shape[1] // CHUNK
    nc = seq // CHUNK + ncc
    scratch = [pltpu.VMEM((NB * ML_HEADS, ML_HEADDIM, ML_HEADDIM), F32),
               pltpu.VMEM((NB, 8, ML_HEADDIM), F32),
               pltpu.VMEM((NB, 8, LANES), F32)]
    params = pltpu.CompilerParams(dimension_semantics=("arbitrary", "arbitrary"), vmem_limit_bytes=VMEM_LIMIT)
    lat_shape = jax.ShapeDtypeStruct((b, seq, ML_WIDTH), BF16)
    act_shape = jax.ShapeDtypeStruct((b, nc * CHUNK, ML_WIDTH), BF16)

    ceff = functools.partial(_chunk_eff, True, ncc, nc)
    lat = functools.partial(_lat_block, True, ncc, nc)
    pair = functools.partial(_pair_specs, True, ncc, nc)
    rev_params = list(conv_params) + list(proj_params) + list(gate_params)
    act_spec = _row_spec(CHUNK, ML_WIDTH, ceff)
    hb, xc, q, k = pl.pallas_call(
        functools.partial(_ml_rev_kernel, ncc, nc),
        grid=(b // NB, nc),
        in_specs=pair(CHUNK, ML_WIDTH) + pair(HALO, ML_WIDTH, halo=-1) + pair(HALO, ML_WIDTH, halo=1)
        + pair(CHUNK, ML_WIDTH, col=1) + pair(CHUNK, LANES) + [_full(p.shape) for p in rev_params],
        out_specs=[_row_spec(CHUNK, ML_WIDTH, lat), act_spec, act_spec, act_spec],
        out_shape=(lat_shape, act_shape, act_shape, act_shape),
        scratch_shapes=scratch,
        compiler_params=params,
        name="mlstm_rev",
    )(*ml, *ml, *ml, *ml, *gates, *rev_params)

    ceff = functools.partial(_chunk_eff, False, ncc, nc)
    lat = functools.partial(_lat_block, False, ncc, nc)
    pair = functools.partial(_pair_specs, False, ncc, nc)
    act_spec = _row_spec(CHUNK, ML_WIDTH, ceff)
    return pl.pallas_call(
        functools.partial(_ml_fwd_kernel, ncc),
        grid=(b // NB, nc),
        in_specs=[act_spec, act_spec, act_spec] + pair(CHUNK, ML_WIDTH, col=1)
        + [_row_spec(CHUNK, ML_WIDTH, lat, 2)] + pair(CHUNK, LANES)
        + [_full(p.shape) for p in gate_params]
        + [_row_spec(CHUNK, ML_WIDTH, lat), _full(nw.shape), _full(sk.shape)],
        out_specs=_row_spec(CHUNK, ML_WIDTH, lat),
        out_shape=lat_shape,
        scratch_shapes=scratch,
        compiler_params=params,
        name="mlstm_fwd",
    )(xc, q, k, *ml, ml[0], *gates, *gate_params, hb, nw, sk)


def _out_kernel(x_ref, ys_ref, ym_ref, wo_ref, g1_ref, sh_ref, sc_ref, nw_ref, rwh_ref, rwl_ref, rb_ref,
                x1_ref, ts_ref, route_ref):
    mix = _dot(ys_ref[...], wo_ref[0:SSD_WIDTH, :]) + _dot(ym_ref[...], wo_ref[SSD_WIDTH:, :])
    x1 = x_ref[...] + g1_ref[...] * mix
    y = x1 * lax.rsqrt(jnp.mean(x1 * x1, axis=-1, keepdims=True) + EPS) * nw_ref[...]
    t = y * (1.0 + sc_ref[...]) + sh_ref[...]
    x1_ref[...] = x1
    for j in range(SLAB):
        ts_ref[pl.ds(j, TM_IN, stride=SLAB), :] = t[:, j * LANES:(j + 1) * LANES]
    t_hi = t.astype(BF16)
    t_lo = (t - t_hi.astype(F32)).astype(BF16)
    lg = _dot(t_hi, rwh_ref[...]) + _dot(t_lo, rwh_ref[...]) + _dot(t_hi, rwl_ref[...]) + rb_ref[...]

    lane = lax.broadcasted_iota(jnp.int32, lg.shape, 1).astype(F32)
    gmask = lane < MOE_GROUPS
    gl = jnp.where(gmask, lg, NEG_BIG)
    gmax = jnp.max(gl, axis=1, keepdims=True)
    g_sel = jnp.min(jnp.where(gmask & (gl == gmax), lane, 1e9), axis=1, keepdims=True)
    p_group = 1.0 / jnp.sum(jnp.where(gmask, jnp.exp(gl - gmax), 0.0), axis=1, keepdims=True)
    lo = ROUTE_LANE0 + MOE_EPG * g_sel
    emask = (lane >= lo) & (lane < lo + MOE_EPG)
    l1 = jnp.max(jnp.where(emask, lg, NEG_BIG), axis=1, keepdims=True)
    i1 = jnp.min(jnp.where(emask & (lg == l1), lane, 1e9), axis=1, keepdims=True)
    emask2 = emask & (lane != i1)
    l2 = jnp.max(jnp.where(emask2, lg, NEG_BIG), axis=1, keepdims=True)
    i2 = jnp.min(jnp.where(emask2 & (lg == l2), lane, 1e9), axis=1, keepdims=True)
    r = jnp.exp(l2 - l1)
    w1 = p_group / (1.0 + r)
    w2 = p_group * r / (1.0 + r)
    route = (jnp.where(lane == 0, i1, 0.0) + jnp.where(lane == 1, i2, 0.0)
             + jnp.where(lane == 2, w1, 0.0) + jnp.where(lane == 3, w2, 0.0))
    route_ref[...] = route.T[0:8, :]


def _out_proj(x, y_ssd, y_ml, w_out_b, mod3, norm_w, rw_hi, rw_lo, rb):
    b, seq, d = x.shape
    nt = seq // TM_IN

    def row(j):
        return pl.BlockSpec((None, 1, d), lambda bi, t: (bi * 6 + j, 0, 0))

    def tile(width):
        return pl.BlockSpec((None, TM_IN, width), lambda bi, t: (bi, t, 0))

    return pl.pallas_call(
        _out_kernel,
        grid=(b, nt),
        in_specs=[tile(d), tile(SSD_WIDTH), tile(ML_WIDTH), _full(w_out_b.shape),
                  row(2), row(3), row(4), _full((1, d)), _full(rw_hi.shape), _full(rw_lo.shape), _full(rb.shape)],
        out_specs=[tile(d),
                   pl.BlockSpec((None, TM_IN * SLAB, LANES), lambda bi, t: (bi, t, 0)),
                   pl.BlockSpec((None, 8, TM_IN), lambda bi, t: (bi, 0, t))],
        out_shape=(jax.ShapeDtypeStruct((b, seq, d), F32),
                   jax.ShapeDtypeStruct((b, seq * SLAB, LANES), F32),
                   jax.ShapeDtypeStruct((b, 8, seq), F32)),
        compiler_params=pltpu.CompilerParams(
            dimension_semantics=("arbitrary", "arbitrary"), vmem_limit_bytes=VMEM_LIMIT),
        name="out_proj_router",
    )(x, y_ssd, y_ml, w_out_b, mod3, mod3, mod3, norm_w.reshape(1, d), rw_hi, rw_lo, rb)


def _route_tables(route, seq):
    b = route.shape[0]
    n_inst = 2 * seq
    nblk = n_inst // BM + MOE_EXPERTS
    e_flat = (route[:, 0:2, :].astype(jnp.int32) - ROUTE_LANE0).reshape(b, n_inst)
    w_flat = route[:, 2:4, :].reshape(b, n_inst)
    tok_flat = jnp.tile(jnp.arange(seq, dtype=jnp.int32), 2 * b).reshape(b, n_inst)
    _, tok_sorted, w_sorted = lax.sort((e_flat, tok_flat, w_flat), dimension=1, num_keys=1)
    experts = jnp.arange(MOE_EXPERTS, dtype=jnp.int32)
    counts = jnp.sum((e_flat[:, None, :] == experts[None, :, None]).astype(jnp.int32), axis=2)
    nblk_e = (counts + BM - 1) // BM
    blk_end = jnp.cumsum(nblk_e, axis=1)
    blk_start = blk_end - nblk_e
    cnt_start = jnp.cumsum(counts, axis=1) - counts
    nb = blk_end[:, -1:]
    j = jnp.arange(nblk, dtype=jnp.int32)[None, :]
    valid_blk = j < nb
    jj = jnp.minimum(j, nb - 1)
    e_j = jnp.sum((jj[:, :, None] >= blk_end[:, None, :]).astype(jnp.int32), axis=2)
    onehot = (e_j[:, :, None] == experts[None, None, :]).astype(jnp.int32)
    take = lambda tbl: jnp.sum(onehot * tbl[:, None, :], axis=2)
    r = jnp.arange(BM, dtype=jnp.int32)[None, None, :]
    rank = ((jj - take(blk_start)) * BM)[:, :, None] + r
    valid = valid_blk[:, :, None] & (rank < take(counts)[:, :, None])
    sidx = jnp.clip(take(cnt_start)[:, :, None] + rank, 0, n_inst - 1).reshape(b, nblk * BM)
    tok = jnp.take_along_axis(tok_sorted, sidx, axis=1).reshape(b, nblk, BM)
    wslot = jnp.take_along_axis(w_sorted, sidx, axis=1).reshape(b, nblk, BM)
    tok = jnp.where(valid, tok, seq + r)
    wslot = jnp.where(valid, wslot, 0.0)
    return (nblk_e.reshape(-1), blk_start.reshape(-1), tok.reshape(b, 1, nblk * BM), wslot)


SEM_T, SEM_X, SEM_O = 0, 1, 3


def _moe_kernel(seq, nbe_ref, bs_ref, tok_ref, ws_ref, wg_ref, wu_ref, wd_ref, g2_ref, fw_ref,
                t_hbm, x1_hbm, o_hbm, t_scr, y_scr, wgb, wub, wdb, xt, ot, xin, stage, sems):
    b = pl.program_id(0)
    e = pl.program_id(1)
    nb = pl.num_programs(0)
    rows = seq * SLAB
    n_fin = seq // TM_FIN

    def t_copy(sample):
        return pltpu.make_async_copy(t_hbm.at[sample], t_scr.at[pl.ds(0, rows)], sems.at[SEM_T])

    def x1_copy(s, slot):
        return pltpu.make_async_copy(x1_hbm.at[b, pl.ds(s * TM_FIN, TM_FIN)], xin.at[slot], sems.at[SEM_X + slot])

    def out_copy(s, slot):
        return pltpu.make_async_copy(stage.at[slot], o_hbm.at[b, pl.ds(s * TM_FIN, TM_FIN)], sems.at[SEM_O + slot])

    @pl.when((e == 0) & (b == 0))
    def _():
        t_copy(b).start()

    @pl.when(e == 0)
    def _():
        y_scr[...] = jnp.zeros_like(y_scr)
        t_scr[pl.ds(rows, BM * SLAB), :] = jnp.zeros((BM * SLAB, LANES), F32)
        t_copy(b).wait()

    n_blocks = nbe_ref[b * MOE_EXPERTS + e]
    blk0 = bs_ref[b * MOE_EXPERTS + e]

    @pl.when(n_blocks > 0)
    def _():
        wgb[...] = wg_ref[...].astype(BF16)
        wub[...] = wu_ref[...].astype(BF16)
        wdb[...] = wd_ref[...].astype(BF16)
        g2 = g2_ref[...]
        diag = (lax.broadcasted_iota(jnp.int32, (BM, BM), 0) == lax.broadcasted_iota(jnp.int32, (BM, BM), 1))

        def block(i, carry):
            blk = blk0 + i
            base = blk * BM

            def slab_rows(r):
                return pl.ds(pl.multiple_of(tok_ref[0, base + r] * SLAB, SLAB), SLAB)

            for r in range(BM):
                xt[pl.ds(r, SLAB, stride=TILE_PITCH), :] = t_scr[slab_rows(r), :]
            x = jnp.concatenate([xt[c * TILE_PITCH:c * TILE_PITCH + BM, :] for c in range(SLAB)],
                                axis=1).astype(BF16)
            w_col = jnp.sum(jnp.where(diag, ws_ref[pl.ds(blk, 1), :], 0.0), axis=1, keepdims=True)
            hidden = _silu(_dot(x, wgb[...])) * _dot(x, wub[...]) * w_col
            out = _dot(hidden.astype(BF16), wdb[...]) * g2
            for c in range(SLAB):
                ot[c * TILE_PITCH:c * TILE_PITCH + BM, :] = out[:, c * LANES:(c + 1) * LANES]
            for r0 in range(0, BM, SCATTER_GROUP):
                sl = [slab_rows(r0 + u) for u in range(SCATTER_GROUP)]
                vals = [y_scr[sl[u], :] + ot[pl.ds(r0 + u, SLAB, stride=TILE_PITCH), :]
                        for u in range(SCATTER_GROUP)]
                for u in range(SCATTER_GROUP):
                    y_scr[sl[u], :] = vals[u]
            return carry

        lax.fori_loop(0, n_blocks, block, 0)

    @pl.when(e == MOE_EXPERTS - 1)
    def _():
        @pl.when(b + 1 < nb)
        def _():
            t_copy(b + 1).start()

        x1_copy(0, 0).start()

        def tile(s, carry):
            slot = s % 2
            x1_copy(s, slot).wait()

            @pl.when(s + 1 < n_fin)
            def _():
                x1_copy(s + 1, 1 - slot).start()

            @pl.when(s >= 2)
            def _():
                out_copy(s - 2, slot).wait()

            base = pl.multiple_of(s * (TM_FIN * SLAB), TM_FIN * SLAB)
            x1 = xin[slot]
            chunks = [x1[:, c * LANES:(c + 1) * LANES] + y_scr[pl.ds(base + c, TM_FIN, stride=SLAB), :]
                      for c in range(SLAB)]
            ssq = chunks[0] * chunks[0]
            for c in range(1, SLAB):
                ssq = ssq + chunks[c] * chunks[c]
            inv = lax.rsqrt(jnp.sum(ssq, axis=1, keepdims=True) * (1.0 / D_MODEL) + EPS)
            for c in range(SLAB):
                stage[slot, :, c * LANES:(c + 1) * LANES] = chunks[c] * inv * fw_ref[:, c * LANES:(c + 1) * LANES]
            out_copy(s, slot).start()
            return carry

        lax.fori_loop(0, n_fin, tile, 0)
        out_copy(n_fin - 2, n_fin % 2).wait()
        out_copy(n_fin - 1, (n_fin - 1) % 2).wait()


def _moe(t_slab, x1, route, wg, wu, wd, mod3, final_w):
    b, seq, d = x1.shape
    nblk_e, blk_start, tok, wslot = _route_tables(route, seq)
    nblk = wslot.shape[1]

    def w_spec(shape):
        return pl.BlockSpec((None,) + shape, lambda bi, e, nbe, bs: (e, 0, 0))

    grid_spec = pltpu.PrefetchScalarGridSpec(
        num_scalar_prefetch=2,
        grid=(b, MOE_EXPERTS),
        in_specs=[
            pl.BlockSpec((None, 1, nblk * BM), lambda bi, e, nbe, bs: (bi, 0, 0), memory_space=pltpu.SMEM),
            pl.BlockSpec((None, nblk, BM), lambda bi, e, nbe, bs: (bi, 0, 0)),
            w_spec((d, D_EXPERT)), w_spec((d, D_EXPERT)), w_spec((D_EXPERT, d)),
            pl.BlockSpec((None, 1, d), lambda bi, e, nbe, bs: (bi * 6 + 5, 0, 0)),
            pl.BlockSpec((1, d), lambda bi, e, nbe, bs: (0, 0)),
            pl.BlockSpec(memory_space=pl.ANY),
            pl.BlockSpec(memory_space=pl.ANY),
        ],
        out_specs=pl.BlockSpec(memory_space=pl.ANY),
        scratch_shapes=[
            pltpu.VMEM(((seq + BM) * SLAB, LANES), F32),
            pltpu.VMEM(((seq + BM) * SLAB, LANES), F32),
            pltpu.VMEM((d, D_EXPERT), BF16),
            pltpu.VMEM((d, D_EXPERT), BF16),
            pltpu.VMEM((D_EXPERT, d), BF16),
            pltpu.VMEM((SLAB * TILE_PITCH, LANES), F32),
            pltpu.VMEM((SLAB * TILE_PITCH, LANES), F32),
            pltpu.VMEM((2, TM_FIN, d), F32),
            pltpu.VMEM((2, TM_FIN, d), F32),
            pltpu.SemaphoreType.DMA((5,)),
        ],
    )
    return pl.pallas_call(
        functools.partial(_moe_kernel, seq),
        grid_spec=grid_spec,
        out_shape=jax.ShapeDtypeStruct((b, seq, d), F32),
        compiler_params=pltpu.CompilerParams(
            dimension_semantics=("arbitrary", "arbitrary"), vmem_limit_bytes=VMEM_LIMIT_MOE),
        name="experts_final_norm",
    )(nblk_e, blk_start, tok, wslot, wg, wu, wd, mod3, final_w.reshape(1, d), t_slab, x1)


def _lane_pad(v, offset=0):
    v = v.reshape(-1).astype(F32)
    return jnp.zeros((1, LANES), F32).at[0, offset:offset + v.shape[0]].set(v)


def _to_col_major(t, rows):
    b, length, ch = t.shape
    return t.reshape(b, rows, GRID_W, ch).transpose(0, 2, 1, 3).reshape(b, length, ch)


def _from_col_major(t, rows):
    b, length, ch = t.shape
    return t.reshape(b, GRID_W, rows, ch).transpose(0, 2, 1, 3).reshape(b, length, ch)


def kernel(x, c, ctx, c_ctx, w_mod, b_mod, norm1_w, w_in, ssd_conv_w, ssd_conv_b, ssd_dt_bias, ssd_a_log, ssd_d, ssd_norm_w, ml_conv_w, ml_conv_b, ml_w_qk, ml_gate_b, ml_norm_w, ml_skip, w_out, norm2_w, moe_rg_w, moe_rg_b, moe_re_w, moe_re_b, moe_w_gate, moe_w_up, moe_w_down, final_norm_w):
    b, seq, d = x.shape
    ctx_len = ctx.shape[1]
    rows = seq // GRID_W
    ncc = ctx_len // CHUNK
    assert w_mod.shape[0] == 1 and d == D_MODEL and b + 1 <= 8 and b % NB == 0
    assert seq % TM_IN == 0 and ctx_len % TM_IN == 0 and seq % TM_FIN == 0 and (2 * seq) % BM == 0

    c_all = jnp.zeros((8, d), F32).at[:b].set(c).at[b].set(c_ctx)
    mod = _modulation(c_all, w_mod[0], b_mod[0])
    mod3 = mod.reshape(8 * 6, 1, d)

    w = w_in[0]
    ssd_in = SSD_WIDTH + SSD_XBC + 2 * SSD_HEADS
    ml_main = 3 * ML_WIDTH
    n_gate = 2 * SSD_HEADS + 4 * ML_HEADS
    w_cat = jnp.concatenate([
        w[:, :SSD_WIDTH + SSD_XBC], w[:, ssd_in:ssd_in + ml_main],
        w[:, SSD_WIDTH + SSD_XBC:ssd_in], w[:, ssd_in + ml_main:],
        jnp.zeros((d, LANES - n_gate), F32)], axis=1).astype(BF16)
    z, xbc, ml, gates, gates_cm = _in_proj(x, ctx, mod3, norm1_w[0], w_cat)

    e_mats = []
    for direction in range(2):
        lane = jnp.arange(LANES)[:, None]
        head = (jnp.arange(SSD_WIDTH) // SSD_HEADDIM)[None, :]
        e_mats.append((lane == direction * SSD_HEADS + head).astype(BF16))
    cw = jnp.zeros((8, SSD_XBC), F32).at[:CONV_W].set(ssd_conv_w[0])
    dtb = _lane_pad(ssd_dt_bias[0])
    alog = _lane_pad(ssd_a_log[0])
    dsk = jnp.repeat(ssd_d[0], SSD_HEADDIM).reshape(1, SSD_WIDTH)
    y_ssd = _ssd_scans(xbc, gates, z, [cw, ssd_conv_b[0].reshape(1, -1)],
                       [dtb, dtb.reshape(LANES, 1), alog, alog.reshape(LANES, 1)], e_mats, dsk,
                       ssd_norm_w[0].reshape(1, -1))

    w_rows = jnp.tile(ml_w_qk[0].reshape(2, ML_WIDTH, ML_QK_BLOCK), (1, 1, ML_WIDTH // ML_QK_BLOCK))
    blk_id = jnp.arange(ML_WIDTH) // ML_QK_BLOCK
    w_bd = jnp.where((blk_id[:, None] == blk_id[None, :])[None], w_rows, 0.0)
    wq = w_bd[0].astype(BF16)
    wk = (w_bd[1] * (ML_HEADDIM ** -0.5)).astype(BF16)
    mcw = jnp.zeros((8, ML_WIDTH), F32).at[:CONV_W].set(ml_conv_w[0])
    gb = _lane_pad(ml_gate_b[0], offset=2 * SSD_HEADS)
    y_ml_cm = _ml_scans(ml, gates_cm, [mcw, ml_conv_b[0].reshape(1, -1)], [wq, wk], [gb, gb.reshape(LANES, 1)],
                        ml_norm_w[0].reshape(1, -1), ml_skip[0].reshape(1, -1))
    y_ml = _from_col_major(y_ml_cm, rows)

    rw = jnp.concatenate([moe_rg_w[0], moe_re_w[0],
                          jnp.zeros((d, LANES - MOE_GROUPS - MOE_EXPERTS), F32)], axis=1)
    rw_hi = rw.astype(BF16)
    rw_lo = (rw - rw_hi.astype(F32)).astype(BF16)
    rb = _lane_pad(jnp.concatenate([moe_rg_b[0], moe_re_b[0]]))
    x1, t_slab, route = _out_proj(x, y_ssd, y_ml, w_out[0].astype(BF16), mod3, norm2_w[0], rw_hi, rw_lo, rb)

    return _moe(t_slab, x1, route, moe_w_gate[0], moe_w_up[0], moe_w_down[0], mod3, final_norm_w)
```

```python
import functools

import jax
import jax.numpy as jnp
from jax import lax
from jax.experimental import pallas as pl
from jax.experimental.pallas import tpu as pltpu

F32 = jnp.float32
BF16 = jnp.bfloat16
HIGHEST = lax.Precision.HIGHEST

D_MODEL = 1024
GRID_W = 64
EPS = 1e-6
CONV_W = 5
NEG_BIG = -1e30
CHUNK = 128
LANES = 128
HALO = 16
NB = 2

SSD_WIDTH = 512
SSD_HEADS = 8
SSD_HEADDIM = 64
SSD_GROUPS = 2
SSD_STATE = 128
SSD_XBC = SSD_WIDTH + 2 * SSD_GROUPS * SSD_STATE

ML_WIDTH = 512
ML_HEADS = 4
ML_HEADDIM = 128
ML_QK_BLOCK = 4

MOE_GROUPS = 4
MOE_EPG = 8
MOE_EXPERTS = 32
D_EXPERT = 256
ROUTE_LANE0 = MOE_GROUPS

TM_IN = 512
TM_FIN = 256
SLAB = D_MODEL // LANES
BM = 128
TILE_PITCH = BM + 8
SCATTER_GROUP = 4
VMEM_LIMIT = 48 * 1024 * 1024
VMEM_LIMIT_MOE = 56 * 1024 * 1024


def _silu(v):
    return v * jax.nn.sigmoid(v)


def _softplus(v):
    return jnp.maximum(v, 0.0) + jnp.log1p(jnp.exp(-jnp.abs(v)))


def _dot(a, b):
    return jnp.dot(a, b, preferred_element_type=F32)


def _dot_nt(a, b):
    return lax.dot_general(a, b, (((1,), (1,)), ((), ())), preferred_element_type=F32)


def _dot_hi(a, b):
    return jnp.dot(a, b, preferred_element_type=F32, precision=HIGHEST)


def _mod_kernel(c_ref, w_ref, b_ref, o_ref):
    c = c_ref[...]
    o_ref[...] = _dot_hi(_silu(c), w_ref[...]) + b_ref[...]


def _modulation(c_all, w_mod, b_mod):
    n = w_mod.shape[1]
    bn = 1536
    return pl.pallas_call(
        _mod_kernel,
        grid=(n // bn,),
        in_specs=[
            pl.BlockSpec((8, D_MODEL), lambda j: (0, 0)),
            pl.BlockSpec((D_MODEL, bn), lambda j: (0, j)),
            pl.BlockSpec((1, bn), lambda j: (0, j)),
        ],
        out_specs=pl.BlockSpec((8, bn), lambda j: (0, j)),
        out_shape=jax.ShapeDtypeStruct((8, n), F32),
        compiler_params=pltpu.CompilerParams(vmem_limit_bytes=VMEM_LIMIT),
        name="modulation",
    )(c_all, w_mod, b_mod.reshape(1, n))


W_TILE = 16
COL_Z, COL_XBC, COL_ML, COL_G = 0, SSD_WIDTH, SSD_WIDTH + SSD_XBC, SSD_WIDTH + SSD_XBC + 3 * ML_WIDTH
PROJ_CHUNK = 512


def _norm_mod(xin, sh_ref, sc_ref, nw_ref):
    ms = jnp.mean(xin * xin, axis=-1, keepdims=True)
    y = xin * lax.rsqrt(ms + EPS) * nw_ref[...]
    return (y * (1.0 + sc_ref[...]) + sh_ref[...]).astype(BF16)


def _in_lat_kernel(rows, x_ref, sh_ref, sc_ref, nw_ref, w_ref, z_ref, xbc_ref, ml_ref, g_ref, gml_ref, scr):
    tm = rows * W_TILE
    h = _norm_mod(x_ref[...].reshape(tm, D_MODEL), sh_ref, sc_ref, nw_ref)

    def proj(col, width=PROJ_CHUNK):
        return _dot(h, w_ref[:, col:col + width])

    def to_col_major(val, dst_ref, lo, dtype):
        n_slab = val.shape[1] // LANES
        for k in range(n_slab):
            scr[k] = val[:, k * LANES:(k + 1) * LANES]
        for j in range(W_TILE):
            for k in range(n_slab):
                dst_ref[j, :, lo + k * LANES:lo + (k + 1) * LANES] = (
                    scr[k, pl.ds(j, rows, stride=W_TILE), :].astype(dtype))

    z_ref[...] = proj(COL_Z).astype(BF16).reshape(rows, W_TILE, PROJ_CHUNK)
    for j in range(SSD_XBC // PROJ_CHUNK):
        lo = j * PROJ_CHUNK
        xbc_ref[:, :, lo:lo + PROJ_CHUNK] = proj(COL_XBC + lo).astype(BF16).reshape(rows, W_TILE, PROJ_CHUNK)
    for j in range(3 * ML_WIDTH // PROJ_CHUNK):
        to_col_major(proj(COL_ML + j * PROJ_CHUNK), ml_ref, j * PROJ_CHUNK, BF16)
    g = proj(COL_G, LANES)
    g_ref[...] = g.reshape(rows, W_TILE, LANES)
    to_col_major(g, gml_ref, 0, F32)


def _in_ctx_kernel(x_ref, sh_ref, sc_ref, nw_ref, w_ref, xbc_ref, ml_ref, g_ref):
    h = _norm_mod(x_ref[...], sh_ref, sc_ref, nw_ref)
    for j in range(SSD_XBC // PROJ_CHUNK):
        lo = j * PROJ_CHUNK
        xbc_ref[:, lo:lo + PROJ_CHUNK] = _dot(h, w_ref[:, COL_XBC + lo:COL_XBC + lo + PROJ_CHUNK]).astype(BF16)
    for j in range(3 * ML_WIDTH // PROJ_CHUNK):
        lo = j * PROJ_CHUNK
        ml_ref[:, lo:lo + PROJ_CHUNK] = _dot(h, w_ref[:, COL_ML + lo:COL_ML + lo + PROJ_CHUNK]).astype(BF16)
    g_ref[...] = _dot(h, w_ref[:, COL_G:COL_G + LANES])


def _in_proj(x, ctx, mod3, norm_w, w_cat):
    b, seq, d = x.shape
    ctx_len = ctx.shape[1]
    rows = seq // GRID_W
    widths = (SSD_WIDTH, SSD_XBC, 3 * ML_WIDTH, LANES, LANES)
    dtypes = (BF16, BF16, BF16, F32, F32)
    params = pltpu.CompilerParams(dimension_semantics=("arbitrary", "arbitrary"), vmem_limit_bytes=VMEM_LIMIT)
    nw = norm_w.reshape(1, d)

    def raster(width):
        return pl.BlockSpec((None, rows, W_TILE, width), lambda bi, wi: (bi, 0, wi, 0))

    def col_major(width):
        return pl.BlockSpec((None, W_TILE, rows, width), lambda bi, wi: (bi, wi, 0, 0))

    outs = pl.pallas_call(
        functools.partial(_in_lat_kernel, rows),
        grid=(b, GRID_W // W_TILE),
        in_specs=[
            raster(d),
            pl.BlockSpec((None, 1, d), lambda bi, wi: (bi * 6, 0, 0)),
            pl.BlockSpec((None, 1, d), lambda bi, wi: (bi * 6 + 1, 0, 0)),
            _full((1, d)), _full(w_cat.shape),
        ],
        out_specs=[raster(SSD_WIDTH), raster(SSD_XBC), col_major(3 * ML_WIDTH), raster(LANES), col_major(LANES)],
        out_shape=[jax.ShapeDtypeStruct((b, GRID_W, rows, w) if cm else (b, rows, GRID_W, w), t)
                   for w, t, cm in zip(widths, dtypes, (False, False, True, False, True))],
        scratch_shapes=[pltpu.VMEM((PROJ_CHUNK // LANES, rows * W_TILE, LANES), F32)],
        compiler_params=params,
        name="in_proj",
    )(x.reshape(b, rows, GRID_W, d), mod3, mod3, nw, w_cat)
    z, xbc, ml, gates, gates_cm = [o.reshape(b, seq, w) for o, w in zip(outs, widths)]

    ctx_row = b
    ctx_widths = (SSD_XBC, 3 * ML_WIDTH, LANES)
    xbc_c, ml_c, gates_c = pl.pallas_call(
        _in_ctx_kernel,
        grid=(b, 1),
        in_specs=[
            pl.BlockSpec((None, ctx_len, d), lambda bi, t: (bi, 0, 0)),
            pl.BlockSpec((None, 1, d), lambda bi, t: (ctx_row * 6, 0, 0)),
            pl.BlockSpec((None, 1, d), lambda bi, t: (ctx_row * 6 + 1, 0, 0)),
            _full((1, d)), _full(w_cat.shape),
        ],
        out_specs=[pl.BlockSpec((None, ctx_len, w), lambda bi, t: (bi, 0, 0)) for w in ctx_widths],
        out_shape=[jax.ShapeDtypeStruct((b, ctx_len, w), t) for w, t in zip(ctx_widths, (BF16, BF16, F32))],
        compiler_params=params,
        name="in_proj_ctx",
    )(ctx, mod3, mod3, nw, w_cat)
    return z, (xbc, xbc_c), (ml, ml_c), (gates, gates_c), (gates_cm, gates_c)


def _chunk_eff(rev, ncc, nc, c):
    nl = nc - ncc
    if not rev:
        return jnp.where(c < ncc, nl + c, c - ncc)
    return jnp.where(c < ncc, nc - 1 - c, nl - 1 - (c - ncc))


def _seq_ends(ncc, nc, ceff):
    nl = nc - ncc
    return (ceff == 0) | (ceff == nl), (ceff == nl - 1) | (ceff == nc - 1)


def _lat_block(rev, ncc, nc, c):
    nl = nc - ncc
    if not rev:
        return jnp.maximum(c - ncc, 0)
    return jnp.where(c < ncc, nl - 1, nl - 1 - (c - ncc))


def _conv_silu(xm, xp, xn, cw_ref, cb_ref, first, last):
    xp = xp.astype(F32)[HALO - 8:HALO] * jnp.where(first, 0.0, 1.0)
    xn = xn.astype(F32)[0:8] * jnp.where(last, 0.0, 1.0)
    ext = jnp.concatenate([xp, xm, xn], axis=0)
    acc = cb_ref[...]
    for tap in range(CONV_W):
        off = 8 - CONV_W // 2 + tap
        acc = acc + cw_ref[tap:tap + 1, :] * ext[off:off + CHUNK]
    return _silu(acc)


def _tri(rev, transposed=False):
    row = lax.broadcasted_iota(jnp.int32, (CHUNK, CHUNK), 0)
    col = lax.broadcasted_iota(jnp.int32, (CHUNK, CHUNK), 1)
    if transposed:
        row, col = col, row
    keep = (col >= row) if rev else (col <= row)
    return keep


def _split3(a):
    a1 = a.astype(BF16)
    r1 = a - a1.astype(F32)
    a2 = r1.astype(BF16)
    a3 = (r1 - a2.astype(F32)).astype(BF16)
    return a1, a2, a3


def _dot_sel_l(m_b, a):
    p = _split3(a)
    return _dot(m_b, p[0]) + _dot(m_b, p[1]) + _dot(m_b, p[2])


def _dot_sel_r(a, m_b):
    p = _split3(a)
    return _dot(p[0], m_b) + _dot(p[1], m_b) + _dot(p[2], m_b)


def _row_spec(rows, width, row_block, col_block=0):
    return pl.BlockSpec((NB, rows, width), lambda bi, c: (bi, row_block(c), col_block))


def _local_chunks(rev, ncc, nc, c):
    nl = nc - ncc
    cc = jnp.clip(ncc - 1 - c if rev else c, 0, ncc - 1)
    cl = jnp.clip(nl - 1 - (c - ncc) if rev else c - ncc, 0, nl - 1)
    return c < ncc, cc, cl


def _pair_specs(rev, ncc, nc, rows, width, col=0, halo=0):
    per = CHUNK // HALO

    def index(which, count):
        def fn(c):
            ch = _local_chunks(rev, ncc, nc, c)[which]
            if halo == 0:
                return ch
            if halo < 0:
                return jnp.maximum(ch * per - 1, 0)
            return jnp.minimum((ch + 1) * per, count * per - 1)
        return fn

    return [_row_spec(rows, width, index(2, nc - ncc), col), _row_spec(rows, width, index(1, ncc), col)]


def _pick(is_ctx, lat_ref, ctx_ref, s):
    return jnp.where(is_ctx, ctx_ref[s], lat_ref[s])


def _step_ends(rev, ncc, nc, c):
    is_ctx, cc, cl = _local_chunks(rev, ncc, nc, c)
    first = jnp.where(is_ctx, cc == 0, cl == 0)
    last = jnp.where(is_ctx, cc == ncc - 1, cl == nc - ncc - 1)
    return is_ctx, first, last


def _full(shape):
    return pl.BlockSpec(shape, lambda bi, c: (0,) * len(shape))


def _ssd_chunk(rev, xbc, g, dtbr_ref, dtbc_ref, alr_ref, alc_ref, e_ref, s_ref, s):
    xs = xbc[:, :SSD_WIDTH]
    lane0 = SSD_HEADS * int(rev)
    lane = lax.broadcasted_iota(jnp.int32, (CHUNK, LANES), 1)
    lmask = (lane >= lane0) & (lane < lane0 + SSD_HEADS)
    dt = jnp.where(lmask, _softplus(g + dtbr_ref[...]), 0.0)
    a = dt * (-jnp.exp(alr_ref[...]))
    gt = g.T
    dt_t = _softplus(gt + dtbc_ref[...])[lane0:lane0 + SSD_HEADS]
    a_t = dt_t * (-jnp.exp(alc_ref[...][lane0:lane0 + SSD_HEADS]))

    cs = _dot_sel_l(_tri(rev).astype(BF16), a)
    cs_t = _dot_sel_r(a_t, _tri(rev, transposed=True).astype(BF16))
    e = e_ref[...]
    dtx = _dot_sel_r(dt, e)
    csx = _dot_sel_r(cs, e)
    end = 0 if rev else CHUNK - 1
    totx = csx[end:end + 1, :]
    ecsx = jnp.exp(csx)
    decx = jnp.exp(totx - csx)
    etotx = jnp.exp(totx)

    xdt = xs * dtx
    xdt_b = xdt.astype(BF16)
    xd_b = (xdt * decx).astype(BF16)
    keep = _tri(rev)
    half = lax.broadcasted_iota(jnp.int32, (CHUNK, LANES), 1) // SSD_HEADDIM

    y_blocks = []
    gs = SSD_GROUPS * SSD_STATE
    hpg = SSD_HEADS // SSD_GROUPS
    gw = hpg * SSD_HEADDIM
    for grp in range(SSD_GROUPS):
        bm = xbc[:, SSD_WIDTH + grp * SSD_STATE:SSD_WIDTH + (grp + 1) * SSD_STATE]
        cm = xbc[:, SSD_WIDTH + gs + grp * SSD_STATE:SSD_WIDTH + gs + (grp + 1) * SSD_STATE]
        bm_b, cm_b = bm.astype(BF16), cm.astype(BF16)
        cb = _dot_nt(cm_b, bm_b)
        s_old = s_ref[s, :, grp * gw:(grp + 1) * gw]
        y_off = _dot(cm_b, s_old.astype(BF16)) * ecsx[:, grp * gw:(grp + 1) * gw]
        for pair in range(hpg // 2):
            blk = grp * (hpg // 2) + pair
            xj = xdt_b[:, blk * LANES:(blk + 1) * LANES]
            acc = y_off[:, pair * LANES:(pair + 1) * LANES]
            for q in range(2):
                h = blk * 2 + q
                dl = cs[:, lane0 + h:lane0 + h + 1] - cs_t[h:h + 1, :]
                m = (cb * jnp.exp(jnp.where(keep, dl, NEG_BIG))).astype(BF16)
                acc = acc + _dot(m, jnp.where(half == q, xj, jnp.zeros_like(xj)))
            y_blocks.append(acc)
        s_new = s_old * etotx[:, grp * gw:(grp + 1) * gw] + _dot(bm.T.astype(BF16), xd_b[:, grp * gw:(grp + 1) * gw])
        s_ref[s, :, grp * gw:(grp + 1) * gw] = s_new

    return jnp.concatenate(y_blocks, axis=1), xs


def _ssd_rev_kernel(ncc, nc, xml_ref, xmc_ref, xpl_ref, xpc_ref, xnl_ref, xnc_ref, gl_ref, gc_ref, cw_ref, cb_ref,
                    dtbr_ref, dtbc_ref, alr_ref, alc_ref, e_ref, o_ref, xc_ref, s_ref):
    c = pl.program_id(1)
    is_ctx, first, last = _step_ends(True, ncc, nc, c)

    @pl.when(c == 0)
    def _():
        s_ref[...] = jnp.zeros_like(s_ref)

    for s in range(NB):
        xbc = _conv_silu(_pick(is_ctx, xml_ref, xmc_ref, s).astype(F32), _pick(is_ctx, xpl_ref, xpc_ref, s),
                         _pick(is_ctx, xnl_ref, xnc_ref, s), cw_ref, cb_ref, first, last)
        xc_ref[s] = xbc.astype(BF16)
        y, _ = _ssd_chunk(True, xbc, _pick(is_ctx, gl_ref, gc_ref, s), dtbr_ref, dtbc_ref, alr_ref, alc_ref, e_ref,
                          s_ref, s)
        o_ref[s] = y.astype(o_ref.dtype)


def _ssd_fwd_kernel(ncc, xc_ref, gl_ref, gc_ref, dtbr_ref, dtbc_ref, alr_ref, alc_ref, e_ref, dsk_ref, yb_ref,
                    z_ref, nw_ref, o_ref, s_ref):
    is_ctx = pl.program_id(1) < ncc

    @pl.when(pl.program_id(1) == 0)
    def _():
        s_ref[...] = jnp.zeros_like(s_ref)

    for s in range(NB):
        y, xs = _ssd_chunk(False, xc_ref[s].astype(F32), _pick(is_ctx, gl_ref, gc_ref, s), dtbr_ref, dtbc_ref,
                           alr_ref, alc_ref, e_ref, s_ref, s)
        y = y + yb_ref[s].astype(F32) + dsk_ref[...] * xs
        y = y * _silu(z_ref[s].astype(F32))
        y = y * lax.rsqrt(jnp.mean(y * y, axis=-1, keepdims=True) + EPS) * nw_ref[...]
        o_ref[s] = y.astype(o_ref.dtype)


def _ssd_scans(xbc, gates, z, conv_params, gate_params, e_mats, dsk, nw):
    b, seq, _ = xbc[0].shape
    ncc = xbc[1].shape[1] // CHUNK
    nc = seq // CHUNK + ncc
    state = pltpu.VMEM((NB, SSD_STATE, SSD_WIDTH), F32)
    params = pltpu.CompilerParams(dimension_semantics=("arbitrary", "arbitrary"), vmem_limit_bytes=VMEM_LIMIT)
    lat_shape = jax.ShapeDtypeStruct((b, seq, SSD_WIDTH), BF16)

    ceff = functools.partial(_chunk_eff, True, ncc, nc)
    lat = functools.partial(_lat_block, True, ncc, nc)
    pair = functools.partial(_pair_specs, True, ncc, nc)
    rev_params = list(conv_params) + list(gate_params) + [e_mats[1]]
    yb, xbc_act = pl.pallas_call(
        functools.partial(_ssd_rev_kernel, ncc, nc),
        grid=(b // NB, nc),
        in_specs=pair(CHUNK, SSD_XBC) + pair(HALO, SSD_XBC, halo=-1) + pair(HALO, SSD_XBC, halo=1)
        + pair(CHUNK, LANES) + [_full(p.shape) for p in rev_params],
        out_specs=[_row_spec(CHUNK, SSD_WIDTH, lat), _row_spec(CHUNK, SSD_XBC, ceff)],
        out_shape=(lat_shape, jax.ShapeDtypeStruct((b, nc * CHUNK, SSD_XBC), BF16)),
        scratch_shapes=[state],
        compiler_params=params,
        name="ssd_rev",
    )(*xbc, *xbc, *xbc, *gates, *rev_params)

    ceff = functools.partial(_chunk_eff, False, ncc, nc)
    lat = functools.partial(_lat_block, False, ncc, nc)
    fwd_params = list(gate_params) + [e_mats[0], dsk]
    return pl.pallas_call(
        functools.partial(_ssd_fwd_kernel, ncc),
        grid=(b // NB, nc),
        in_specs=[_row_spec(CHUNK, SSD_XBC, ceff)] + _pair_specs(False, ncc, nc, CHUNK, LANES)
        + [_full(p.shape) for p in fwd_params]
        + [_row_spec(CHUNK, SSD_WIDTH, lat), _row_spec(CHUNK, SSD_WIDTH, lat), _full(nw.shape)],
        out_specs=_row_spec(CHUNK, SSD_WIDTH, lat),
        out_shape=lat_shape,
        scratch_shapes=[state],
        compiler_params=params,
        name="ssd_fwd",
    )(xbc_act, *gates, *fwd_params, yb, z, nw)


ML_I_LANE0 = 2 * SSD_HEADS
ML_F_LANE0 = ML_I_LANE0 + 2 * ML_HEADS


def _ml_chunk(rev, k, q_b, k_b, v_b, g, gbr_ref, gbc_ref, c_ref, n_ref, mx_ref, s):
    i_lane0 = ML_I_LANE0 + ML_HEADS * int(rev)
    f_lane0 = ML_F_LANE0 + ML_HEADS * int(rev)
    ga = g + gbr_ref[...]
    lane = lax.broadcasted_iota(jnp.int32, (CHUNK, LANES), 1)
    logf = jnp.where((lane >= f_lane0) & (lane < f_lane0 + ML_HEADS), -_softplus(-ga), 0.0)
    cs = _dot_sel_l(_tri(rev).astype(BF16), logf)
    gt = g.T + gbc_ref[...]
    i_t = gt[ML_I_LANE0:ML_F_LANE0]
    logf_t = -_softplus(-gt[ML_F_LANE0:ML_F_LANE0 + 2 * ML_HEADS])
    cs_t = _dot_sel_r(logf_t, _tri(rev, transposed=True).astype(BF16))
    keep = _tri(rev)
    end = 0 if rev else CHUNK - 1

    u_t = i_t - cs_t
    lane8 = lax.broadcasted_iota(jnp.int32, u_t.shape, 1)
    pm = u_t
    step = 1
    while step < CHUNK:
        if rev:
            pm = jnp.maximum(pm, jnp.where(lane8 < CHUNK - step, pltpu.roll(pm, CHUNK - step, axis=1), NEG_BIG))
        else:
            pm = jnp.maximum(pm, jnp.where(lane8 >= step, pltpu.roll(pm, step, axis=1), NEG_BIG))
        step *= 2
    pm_c = jnp.concatenate([pm, jnp.zeros((CHUNK - pm.shape[0], CHUNK), F32)], axis=0).T
    ones_b = jnp.ones((CHUNK, ML_HEADDIM), BF16)

    outs = []
    for h in range(ML_HEADS):
        r = ML_HEADS * int(rev) + h
        sl = slice(h * ML_HEADDIM, (h + 1) * ML_HEADDIM)
        kh = k[:, sl]
        vh = v_b[:, sl]
        qh_b, kh_b = q_b[:, sl], k_b[:, sl]
        csc = cs[:, f_lane0 + h:f_lane0 + h + 1]
        u_col = ga[:, i_lane0 + h:i_lane0 + h + 1] - csc
        u_row = u_t[r:r + 1, :]
        tot = cs_t[r:r + 1, end:end + 1]
        u_max = pm[r:r + 1, end:end + 1]
        m_prev = mx_ref[s, h:h + 1, 0:1]
        c_prev = c_ref[s * ML_HEADS + h]
        n_prev = n_ref[s, h:h + 1, :]

        mm = jnp.maximum(m_prev, pm_c[:, r:r + 1])
        scores = _dot_nt(qh_b, kh_b) * jnp.exp(jnp.where(keep, u_row - mm, NEG_BIG))
        w_inter = jnp.exp(m_prev - mm)
        intra = _dot(scores.astype(BF16), jnp.concatenate([vh, ones_b], axis=1))
        c_aug = jnp.concatenate([c_prev, jnp.broadcast_to(n_prev, (CHUNK, ML_HEADDIM))], axis=0).astype(BF16)
        both = intra + w_inter * _dot_nt(qh_b, c_aug)
        den = both[:, ML_HEADDIM:ML_HEADDIM + 1]
        outs.append(both[:, :ML_HEADDIM] / jnp.maximum(jnp.abs(den), jnp.exp(-(csc + mm))))

        m_loc = tot + u_max
        kw = kh * jnp.exp(u_col - u_max)
        c_loc = _dot(vh.astype(F32).T.astype(BF16), kw.astype(BF16))
        n_loc = jnp.sum(kw, axis=0, keepdims=True)
        m_new = jnp.maximum(tot + m_prev, m_loc)
        s_prev = jnp.exp(tot + m_prev - m_new)
        s_loc = jnp.exp(m_loc - m_new)
        c_ref[s * ML_HEADS + h] = s_prev * c_prev + s_loc * c_loc
        n_ref[s, h:h + 1, :] = s_prev * n_prev + s_loc * n_loc
        mx_ref[s, h:h + 1, :] = jnp.broadcast_to(m_new, (1, LANES))
    return outs


def _ml_init_state(c_ref, n_ref, mx_ref):
    c_ref[...] = jnp.zeros_like(c_ref)
    n_ref[...] = jnp.zeros_like(n_ref)
    mx_ref[...] = jnp.full(mx_ref.shape, NEG_BIG, F32)


def _ml_rev_kernel(ncc, nc, xml_ref, xmc_ref, xpl_ref, xpc_ref, xnl_ref, xnc_ref, vl_ref, vc_ref, gl_ref, gc_ref,
                   cw_ref, cb_ref, wq_ref, wk_ref, gbr_ref, gbc_ref, o_ref, xc_ref, q_ref, k_ref, c_ref, n_ref, mx_ref):
    c = pl.program_id(1)
    is_ctx, first, last = _step_ends(True, ncc, nc, c)

    @pl.when(c == 0)
    def _():
        _ml_init_state(c_ref, n_ref, mx_ref)

    for s in range(NB):
        xconv = _conv_silu(_pick(is_ctx, xml_ref, xmc_ref, s).astype(F32), _pick(is_ctx, xpl_ref, xpc_ref, s),
                           _pick(is_ctx, xnl_ref, xnc_ref, s), cw_ref, cb_ref, first, last)
        v_b = _pick(is_ctx, vl_ref, vc_ref, s)
        g = _pick(is_ctx, gl_ref, gc_ref, s)
        xc_b = xconv.astype(BF16)
        q = _dot(xc_b, wq_ref[...])
        k = _dot(xc_b, wk_ref[...])
        q_b, k_b = q.astype(BF16), k.astype(BF16)
        xc_ref[s] = xc_b
        q_ref[s] = q_b
        k_ref[s] = k_b
        outs = _ml_chunk(True, k, q_b, k_b, v_b, g, gbr_ref, gbc_ref, c_ref, n_ref, mx_ref, s)
        o_ref[s] = jnp.concatenate(outs, axis=1).astype(o_ref.dtype)


def _ml_fwd_kernel(ncc, xc_ref, q_ref, k_ref, vl_ref, vc_ref, og_ref, gl_ref, gc_ref, gbr_ref, gbc_ref, hb_ref,
                   nw_ref, sk_ref, o_ref, c_ref, n_ref, mx_ref):
    is_ctx = pl.program_id(1) < ncc

    @pl.when(pl.program_id(1) == 0)
    def _():
        _ml_init_state(c_ref, n_ref, mx_ref)

    for s in range(NB):
        q_b, k_b = q_ref[s], k_ref[s]
        outs = _ml_chunk(False, k_b.astype(F32), q_b, k_b, _pick(is_ctx, vl_ref, vc_ref, s),
                         _pick(is_ctx, gl_ref, gc_ref, s), gbr_ref, gbc_ref, c_ref, n_ref, mx_ref, s)
        normed = []
        for h in range(ML_HEADS):
            sl = slice(h * ML_HEADDIM, (h + 1) * ML_HEADDIM)
            hh = jax.nn.sigmoid(og_ref[s, :, sl].astype(F32)) * (outs[h] + hb_ref[s, :, sl].astype(F32))
            normed.append(hh * lax.rsqrt(jnp.mean(hh * hh, axis=-1, keepdims=True) + EPS))
        y = jnp.concatenate(normed, axis=1) * nw_ref[...] + sk_ref[...] * xc_ref[s].astype(F32)
        o_ref[s] = y.astype(o_ref.dtype)


def _ml_scans(ml, gates, conv_params, proj_params, gate_params, nw, sk):
    b, seq, _ = ml[0].shape
    ncc = ml[1].shape[1] // CHUNK
    nc = seq // CHUNK + ncc
    scratch = [pltpu.VMEM((NB * ML_HEADS, ML_HEADDIM, ML_HEADDIM), F32),
               pltpu.VMEM((NB, 8, ML_HEADDIM), F32),
               pltpu.VMEM((NB, 8, LANES), F32)]
    params = pltpu.CompilerParams(dimension_semantics=("arbitrary", "arbitrary"), vmem_limit_bytes=VMEM_LIMIT)
    lat_shape = jax.ShapeDtypeStruct((b, seq, ML_WIDTH), BF16)
    act_shape = jax.ShapeDtypeStruct((b, nc * CHUNK, ML_WIDTH), BF16)

    ceff = functools.partial(_chunk_eff, True, ncc, nc)
    lat = functools.partial(_lat_block, True, ncc, nc)
    pair = functools.partial(_pair_specs, True, ncc, nc)
    rev_params = list(conv_params) + list(proj_params) + list(gate_params)
    act_spec = _row_spec(CHUNK, ML_WIDTH, ceff)
    hb, xc, q, k = pl.pallas_call(
        functools.partial(_ml_rev_kernel, ncc, nc),
        grid=(b // NB, nc),
        in_specs=pair(CHUNK, ML_WIDTH) + pair(HALO, ML_WIDTH, halo=-1) + pair(HALO, ML_WIDTH, halo=1)
        + pair(CHUNK, ML_WIDTH, col=1) + pair(CHUNK, LANES) + [_full(p.shape) for p in rev_params],
        out_specs=[_row_spec(CHUNK, ML_WIDTH, lat), act_spec, act_spec, act_spec],
        out_shape=(lat_shape, act_shape, act_shape, act_shape),
        scratch_shapes=scratch,
        compiler_params=params,
        name="mlstm_rev",
    )(*ml, *ml, *ml, *ml, *gates, *rev_params)

    ceff = functools.partial(_chunk_eff, False, ncc, nc)
    lat = functools.partial(_lat_block, False, ncc, nc)
    pair = functools.partial(_pair_specs, False, ncc, nc)
    act_spec = _row_spec(CHUNK, ML_WIDTH, ceff)
    return pl.pallas_call(
        functools.partial(_ml_fwd_kernel, ncc),
        grid=(b // NB, nc),
        in_specs=[act_spec, act_spec, act_spec] + pair(CHUNK, ML_WIDTH, col=1)
        + [_row_spec(CHUNK, ML_WIDTH, lat, 2)] + pair(CHUNK, LANES)
        + [_full(p.shape) for p in gate_params]
        + [_row_spec(CHUNK, ML_WIDTH, lat), _full(nw.shape), _full(sk.shape)],
        out_specs=_row_spec(CHUNK, ML_WIDTH, lat),
        out_shape=lat_shape,
        scratch_shapes=scratch,
        compiler_params=params,
        name="mlstm_fwd",
    )(xc, q, k, *ml, ml[0], *gates, *gate_params, hb, nw, sk)


def _out_kernel(x_ref, ys_ref, ym_ref, wo_ref, g1_ref, sh_ref, sc_ref, nw_ref, rw_ref, rb_ref,
                x1_ref, ts_ref, route_ref):
    mix = _dot(ys_ref[...], wo_ref[0:SSD_WIDTH, :]) + _dot(ym_ref[...], wo_ref[SSD_WIDTH:, :])
    x1 = x_ref[...] + g1_ref[...] * mix
    y = x1 * lax.rsqrt(jnp.mean(x1 * x1, axis=-1, keepdims=True) + EPS) * nw_ref[...]
    t = y * (1.0 + sc_ref[...]) + sh_ref[...]
    x1_ref[...] = x1
    for j in range(SLAB):
        ts_ref[pl.ds(j, TM_IN, stride=SLAB), :] = t[:, j * LANES:(j + 1) * LANES]
    lg2 = _dot(t.astype(BF16), rw_ref[...])
    lg = lg2[:, :LANES] + lg2[:, LANES:] + rb_ref[...]

    lane = lax.broadcasted_iota(jnp.int32, lg.shape, 1).astype(F32)
    gmask = lane < MOE_GROUPS
    gl = jnp.where(gmask, lg, NEG_BIG)
    gmax = jnp.max(gl, axis=1, keepdims=True)
    g_sel = jnp.min(jnp.where(gmask & (gl == gmax), lane, 1e9), axis=1, keepdims=True)
    p_group = 1.0 / jnp.sum(jnp.where(gmask, jnp.exp(gl - gmax), 0.0), axis=1, keepdims=True)
    lo = ROUTE_LANE0 + MOE_EPG * g_sel
    emask = (lane >= lo) & (lane < lo + MOE_EPG)
    l1 = jnp.max(jnp.where(emask, lg, NEG_BIG), axis=1, keepdims=True)
    i1 = jnp.min(jnp.where(emask & (lg == l1), lane, 1e9), axis=1, keepdims=True)
    emask2 = emask & (lane != i1)
    l2 = jnp.max(jnp.where(emask2, lg, NEG_BIG), axis=1, keepdims=True)
    i2 = jnp.min(jnp.where(emask2 & (lg == l2), lane, 1e9), axis=1, keepdims=True)
    r = jnp.exp(l2 - l1)
    w1 = p_group / (1.0 + r)
    w2 = p_group * r / (1.0 + r)
    route = (jnp.where(lane == 0, i1, 0.0) + jnp.where(lane == 1, i2, 0.0)
             + jnp.where(lane == 2, w1, 0.0) + jnp.where(lane == 3, w2, 0.0))
    route_ref[...] = route.T[0:8, :]


def _out_proj(x, y_ssd, y_ml, w_out_b, mod3, norm_w, rw2, rb):
    b, seq, d = x.shape
    nt = seq // TM_IN

    def row(j):
        return pl.BlockSpec((None, 1, d), lambda bi, t: (bi * 6 + j, 0, 0))

    def tile(width):
        return pl.BlockSpec((None, TM_IN, width), lambda bi, t: (bi, t, 0))

    return pl.pallas_call(
        _out_kernel,
        grid=(b, nt),
        in_specs=[tile(d), tile(SSD_WIDTH), tile(ML_WIDTH), _full(w_out_b.shape),
                  row(2), row(3), row(4), _full((1, d)), _full(rw2.shape), _full(rb.shape)],
        out_specs=[tile(d),
                   pl.BlockSpec((None, TM_IN * SLAB, LANES), lambda bi, t: (bi, t, 0)),
                   pl.BlockSpec((None, 8, TM_IN), lambda bi, t: (bi, 0, t))],
        out_shape=(jax.ShapeDtypeStruct((b, seq, d), F32),
                   jax.ShapeDtypeStruct((b, seq * SLAB, LANES), F32),
                   jax.ShapeDtypeStruct((b, 8, seq), F32)),
        compiler_params=pltpu.CompilerParams(
            dimension_semantics=("arbitrary", "arbitrary"), vmem_limit_bytes=VMEM_LIMIT),
        name="out_proj_router",
    )(x, y_ssd, y_ml, w_out_b, mod3, mod3, mod3, norm_w.reshape(1, d), rw2, rb)


def _route_tables(route, seq):
    b = route.shape[0]
    n_inst = 2 * seq
    nblk = n_inst // BM + MOE_EXPERTS
    e_flat = (route[:, 0:2, :].astype(jnp.int32) - ROUTE_LANE0).reshape(b, n_inst)
    w_flat = route[:, 2:4, :].reshape(b, n_inst)
    tok_flat = jnp.tile(jnp.arange(seq, dtype=jnp.int32), 2 * b).reshape(b, n_inst)
    _, tok_sorted, w_sorted = lax.sort((e_flat, tok_flat, w_flat), dimension=1, num_keys=1)
    experts = jnp.arange(MOE_EXPERTS, dtype=jnp.int32)
    counts = jnp.sum((e_flat[:, None, :] == experts[None, :, None]).astype(jnp.int32), axis=2)
    nblk_e = (counts + BM - 1) // BM
    blk_end = jnp.cumsum(nblk_e, axis=1)
    blk_start = blk_end - nblk_e
    cnt_start = jnp.cumsum(counts, axis=1) - counts
    nb = blk_end[:, -1:]
    j = jnp.arange(nblk, dtype=jnp.int32)[None, :]
    valid_blk = j < nb
    jj = jnp.minimum(j, nb - 1)
    e_j = jnp.sum((jj[:, :, None] >= blk_end[:, None, :]).astype(jnp.int32), axis=2)
    onehot = (e_j[:, :, None] == experts[None, None, :]).astype(jnp.int32)
    take = lambda tbl: jnp.sum(onehot * tbl[:, None, :], axis=2)
    r = jnp.arange(BM, dtype=jnp.int32)[None, None, :]
    rank = ((jj - take(blk_start)) * BM)[:, :, None] + r
    valid = valid_blk[:, :, None] & (rank < take(counts)[:, :, None])
    sidx = jnp.clip(take(cnt_start)[:, :, None] + rank, 0, n_inst - 1).reshape(b, nblk * BM)
    tok = jnp.take_along_axis(tok_sorted, sidx, axis=1).reshape(b, nblk, BM)
    wslot = jnp.take_along_axis(w_sorted, sidx, axis=1).reshape(b, nblk, BM)
    tok = jnp.where(valid, tok, seq + r)
    wslot = jnp.where(valid, wslot, 0.0)
    return (nblk_e.reshape(-1), blk_start.reshape(-1), tok.reshape(b, 1, nblk * BM), wslot)


SEM_T, SEM_X, SEM_O = 0, 1, 3


def _moe_kernel(seq, nbe_ref, bs_ref, tok_ref, ws_ref, wg_ref, wu_ref, wd_ref, g2_ref, fw_ref,
                t_hbm, x1_hbm, o_hbm, t_scr, y_scr, wgb, wub, wdb, xt, ot, xin, stage, sems):
    b = pl.program_id(0)
    e = pl.program_id(1)
    nb = pl.num_programs(0)
    rows = seq * SLAB
    n_fin = seq // TM_FIN

    def t_copy(sample):
        return pltpu.make_async_copy(t_hbm.at[sample], t_scr.at[pl.ds(0, rows)], sems.at[SEM_T])

    def x1_copy(s, slot):
        return pltpu.make_async_copy(x1_hbm.at[b, pl.ds(s * TM_FIN, TM_FIN)], xin.at[slot], sems.at[SEM_X + slot])

    def out_copy(s, slot):
        return pltpu.make_async_copy(stage.at[slot], o_hbm.at[b, pl.ds(s * TM_FIN, TM_FIN)], sems.at[SEM_O + slot])

    @pl.when((e == 0) & (b == 0))
    def _():
        t_copy(b).start()

    @pl.when(e == 0)
    def _():
        y_scr[...] = jnp.zeros_like(y_scr)
        t_scr[pl.ds(rows, BM * SLAB), :] = jnp.zeros((BM * SLAB, LANES), F32)
        t_copy(b).wait()

    n_blocks = nbe_ref[b * MOE_EXPERTS + e]
    blk0 = bs_ref[b * MOE_EXPERTS + e]

    @pl.when(n_blocks > 0)
    def _():
        wgb[...] = wg_ref[...].astype(BF16)
        wub[...] = wu_ref[...].astype(BF16)
        wdb[...] = wd_ref[...].astype(BF16)
        g2 = g2_ref[...]
        diag = (lax.broadcasted_iota(jnp.int32, (BM, BM), 0) == lax.broadcasted_iota(jnp.int32, (BM, BM), 1))

        def block(i, carry):
            blk = blk0 + i
            base = blk * BM

            def slab_rows(r):
                return pl.ds(pl.multiple_of(tok_ref[0, base + r] * SLAB, SLAB), SLAB)

            for r in range(BM):
                xt[pl.ds(r, SLAB, stride=TILE_PITCH), :] = t_scr[slab_rows(r), :]
            x = jnp.concatenate([xt[c * TILE_PITCH:c * TILE_PITCH + BM, :] for c in range(SLAB)],
                                axis=1).astype(BF16)
            w_col = jnp.sum(jnp.where(diag, ws_ref[pl.ds(blk, 1), :], 0.0), axis=1, keepdims=True)
            hidden = _silu(_dot(x, wgb[...])) * _dot(x, wub[...]) * w_col
            out = _dot(hidden.astype(BF16), wdb[...]) * g2
            for c in range(SLAB):
                ot[c * TILE_PITCH:c * TILE_PITCH + BM, :] = out[:, c * LANES:(c + 1) * LANES]
            for r0 in range(0, BM, SCATTER_GROUP):
                sl = [slab_rows(r0 + u) for u in range(SCATTER_GROUP)]
                vals = [y_scr[sl[u], :] + ot[pl.ds(r0 + u, SLAB, stride=TILE_PITCH), :]
                        for u in range(SCATTER_GROUP)]
                for u in range(SCATTER_GROUP):
                    y_scr[sl[u], :] = vals[u]
            return carry

        lax.fori_loop(0, n_blocks, block, 0)

    @pl.when(e == MOE_EXPERTS - 1)
    def _():
        @pl.when(b + 1 < nb)
        def _():
            t_copy(b + 1).start()

        x1_copy(0, 0).start()

        def tile(s, carry):
            slot = s % 2
            x1_copy(s, slot).wait()

            @pl.when(s + 1 < n_fin)
            def _():
                x1_copy(s + 1, 1 - slot).start()

            @pl.when(s >= 2)
            def _():
                out_copy(s - 2, slot).wait()

            base = pl.multiple_of(s * (TM_FIN * SLAB), TM_FIN * SLAB)
            x1 = xin[slot]
            chunks = [x1[:, c * LANES:(c + 1) * LANES] + y_scr[pl.ds(base + c, TM_FIN, stride=SLAB), :]
                      for c in range(SLAB)]
            ssq = chunks[0] * chunks[0]
            for c in range(1, SLAB):
                ssq = ssq + chunks[c] * chunks[c]
            inv = lax.rsqrt(jnp.sum(ssq, axis=1, keepdims=True) * (1.0 / D_MODEL) + EPS)
            for c in range(SLAB):
                stage[slot, :, c * LANES:(c + 1) * LANES] = chunks[c] * inv * fw_ref[:, c * LANES:(c + 1) * LANES]
            out_copy(s, slot).start()
            return carry

        lax.fori_loop(0, n_fin, tile, 0)
        out_copy(n_fin - 2, n_fin % 2).wait()
        out_copy(n_fin - 1, (n_fin - 1) % 2).wait()


def _moe(t_slab, x1, route, wg, wu, wd, mod3, final_w):
    b, seq, d = x1.shape
    nblk_e, blk_start, tok, wslot = _route_tables(route, seq)
    nblk = wslot.shape[1]

    def w_spec(shape):
        return pl.BlockSpec((None,) + shape, lambda bi, e, nbe, bs: (e, 0, 0))

    grid_spec = pltpu.PrefetchScalarGridSpec(
        num_scalar_prefetch=2,
        grid=(b, MOE_EXPERTS),
        in_specs=[
            pl.BlockSpec((None, 1, nblk * BM), lambda bi, e, nbe, bs: (bi, 0, 0), memory_space=pltpu.SMEM),
            pl.BlockSpec((None, nblk, BM), lambda bi, e, nbe, bs: (bi, 0, 0)),
            w_spec((d, D_EXPERT)), w_spec((d, D_EXPERT)), w_spec((D_EXPERT, d)),
            pl.BlockSpec((None, 1, d), lambda bi, e, nbe, bs: (bi * 6 + 5, 0, 0)),
            pl.BlockSpec((1, d), lambda bi, e, nbe, bs: (0, 0)),
            pl.BlockSpec(memory_space=pl.ANY),
            pl.BlockSpec(memory_space=pl.ANY),
        ],
        out_specs=pl.BlockSpec(memory_space=pl.ANY),
        scratch_shapes=[
            pltpu.VMEM(((seq + BM) * SLAB, LANES), F32),
            pltpu.VMEM(((seq + BM) * SLAB, LANES), F32),
            pltpu.VMEM((d, D_EXPERT), BF16),
            pltpu.VMEM((d, D_EXPERT), BF16),
            pltpu.VMEM((D_EXPERT, d), BF16),
            pltpu.VMEM((SLAB * TILE_PITCH, LANES), F32),
            pltpu.VMEM((SLAB * TILE_PITCH, LANES), F32),
            pltpu.VMEM((2, TM_FIN, d), F32),
            pltpu.VMEM((2, TM_FIN, d), F32),
            pltpu.SemaphoreType.DMA((5,)),
        ],
    )
    return pl.pallas_call(
        functools.partial(_moe_kernel, seq),
        grid_spec=grid_spec,
        out_shape=jax.ShapeDtypeStruct((b, seq, d), F32),
        compiler_params=pltpu.CompilerParams(
            dimension_semantics=("arbitrary", "arbitrary"), vmem_limit_bytes=VMEM_LIMIT_MOE),
        name="experts_final_norm",
    )(nblk_e, blk_start, tok, wslot, wg, wu, wd, mod3, final_w.reshape(1, d), t_slab, x1)


def _lane_pad(v, offset=0):
    v = v.reshape(-1).astype(F32)
    return jnp.zeros((1, LANES), F32).at[0, offset:offset + v.shape[0]].set(v)


def _to_col_major(t, rows):
    b, length, ch = t.shape
    return t.reshape(b, rows, GRID_W, ch).transpose(0, 2, 1, 3).reshape(b, length, ch)


def _from_col_major(t, rows):
    b, length, ch = t.shape
    return t.reshape(b, GRID_W, rows, ch).transpose(0, 2, 1, 3).reshape(b, length, ch)


def kernel(x, c, ctx, c_ctx, w_mod, b_mod, norm1_w, w_in, ssd_conv_w, ssd_conv_b, ssd_dt_bias, ssd_a_log, ssd_d, ssd_norm_w, ml_conv_w, ml_conv_b, ml_w_qk, ml_gate_b, ml_norm_w, ml_skip, w_out, norm2_w, moe_rg_w, moe_rg_b, moe_re_w, moe_re_b, moe_w_gate, moe_w_up, moe_w_down, final_norm_w):
    b, seq, d = x.shape
    ctx_len = ctx.shape[1]
    rows = seq // GRID_W
    ncc = ctx_len // CHUNK
    assert w_mod.shape[0] == 1 and d == D_MODEL and b + 1 <= 8 and b % NB == 0
    assert seq % TM_IN == 0 and ctx_len % CHUNK == 0 and seq % TM_FIN == 0 and (2 * seq) % BM == 0
    assert seq == rows * GRID_W and GRID_W % W_TILE == 0 and rows % 8 == 0

    c_all = jnp.zeros((8, d), F32).at[:b].set(c).at[b].set(c_ctx)
    mod = _modulation(c_all, w_mod[0], b_mod[0])
    mod3 = mod.reshape(8 * 6, 1, d)

    w = w_in[0]
    ssd_in = SSD_WIDTH + SSD_XBC + 2 * SSD_HEADS
    ml_main = 3 * ML_WIDTH
    n_gate = 2 * SSD_HEADS + 4 * ML_HEADS
    w_cat = jnp.concatenate([
        w[:, :SSD_WIDTH + SSD_XBC], w[:, ssd_in:ssd_in + ml_main],
        w[:, SSD_WIDTH + SSD_XBC:ssd_in], w[:, ssd_in + ml_main:],
        jnp.zeros((d, LANES - n_gate), F32)], axis=1).astype(BF16)
    z, xbc, ml, gates, gates_cm = _in_proj(x, ctx, mod3, norm1_w[0], w_cat)

    e_mats = []
    for direction in range(2):
        lane = jnp.arange(LANES)[:, None]
        head = (jnp.arange(SSD_WIDTH) // SSD_HEADDIM)[None, :]
        e_mats.append((lane == direction * SSD_HEADS + head).astype(BF16))
    cw = jnp.zeros((8, SSD_XBC), F32).at[:CONV_W].set(ssd_conv_w[0])
    dtb = _lane_pad(ssd_dt_bias[0])
    alog = _lane_pad(ssd_a_log[0])
    dsk = jnp.repeat(ssd_d[0], SSD_HEADDIM).reshape(1, SSD_WIDTH)
    y_ssd = _ssd_scans(xbc, gates, z, [cw, ssd_conv_b[0].reshape(1, -1)],
                       [dtb, dtb.reshape(LANES, 1), alog, alog.reshape(LANES, 1)], e_mats, dsk,
                       ssd_norm_w[0].reshape(1, -1))

    w_rows = jnp.tile(ml_w_qk[0].reshape(2, ML_WIDTH, ML_QK_BLOCK), (1, 1, ML_WIDTH // ML_QK_BLOCK))
    blk_id = jnp.arange(ML_WIDTH) // ML_QK_BLOCK
    w_bd = jnp.where((blk_id[:, None] == blk_id[None, :])[None], w_rows, 0.0)
    wq = w_bd[0].astype(BF16)
    wk = (w_bd[1] * (ML_HEADDIM ** -0.5)).astype(BF16)
    mcw = jnp.zeros((8, ML_WIDTH), F32).at[:CONV_W].set(ml_conv_w[0])
    gb = _lane_pad(ml_gate_b[0], offset=2 * SSD_HEADS)
    y_ml_cm = _ml_scans(ml, gates_cm, [mcw, ml_conv_b[0].reshape(1, -1)], [wq, wk], [gb, gb.reshape(LANES, 1)],
                        ml_norm_w[0].reshape(1, -1), ml_skip[0].reshape(1, -1))
    y_ml = _from_col_major(y_ml_cm, rows)

    rw = jnp.concatenate([moe_rg_w[0], moe_re_w[0],
                          jnp.zeros((d, LANES - MOE_GROUPS - MOE_EXPERTS), F32)], axis=1)
    rw_hi = rw.astype(BF16)
    rw_lo = (rw - rw_hi.astype(F32)).astype(BF16)
    rb = _lane_pad(jnp.concatenate([moe_rg_b[0], moe_re_b[0]]))
    rw2 = jnp.concatenate([rw_hi, rw_lo], axis=1)
    x1, t_slab, route = _out_proj(x, y_ssd, y_ml, w_out[0].astype(BF16), mod3, norm2_w[0], rw2, rb)

    return _moe(t_slab, x1, route, moe_w_gate[0], moe_w_up[0], moe_w_down[0], mod3, final_norm_w)
```

```python
import functools

import jax
import jax.numpy as jnp
from jax import lax
from jax.experimental import pallas as pl
from jax.experimental.pallas import tpu as pltpu

F32 = jnp.float32
BF16 = jnp.bfloat16
HIGHEST = lax.Precision.HIGHEST

D_MODEL = 1024
GRID_W = 64
EPS = 1e-6
CONV_W = 5
NEG_BIG = -1e30
CHUNK = 128
LANES = 128
HALO = 16
NB = 2

SSD_WIDTH = 512
SSD_HEADS = 8
SSD_HEADDIM = 64
SSD_GROUPS = 2
SSD_STATE = 128
SSD_XBC = SSD_WIDTH + 2 * SSD_GROUPS * SSD_STATE

ML_WIDTH = 512
ML_HEADS = 4
ML_HEADDIM = 128
ML_QK_BLOCK = 4

MOE_GROUPS = 4
MOE_EPG = 8
MOE_EXPERTS = 32
D_EXPERT = 256
ROUTE_LANE0 = MOE_GROUPS

TM_IN = 512
TM_FIN = 256
SLAB = D_MODEL // LANES
BM = 128
TILE_PITCH = BM + 8
SCATTER_GROUP = 4
VMEM_LIMIT = 48 * 1024 * 1024
VMEM_LIMIT_MOE = 56 * 1024 * 1024


def _silu(v):
    return v * jax.nn.sigmoid(v)


def _softplus(v):
    return jnp.maximum(v, 0.0) + jnp.log1p(jnp.exp(-jnp.abs(v)))


def _dot(a, b):
    return jnp.dot(a, b, preferred_element_type=F32)


def _dot_nt(a, b):
    return lax.dot_general(a, b, (((1,), (1,)), ((), ())), preferred_element_type=F32)


def _dot_hi(a, b):
    return jnp.dot(a, b, preferred_element_type=F32, precision=HIGHEST)


def _mod_kernel(c_ref, w_ref, b_ref, o_ref):
    c = c_ref[...]
    o_ref[...] = _dot_hi(_silu(c), w_ref[...]) + b_ref[...]


def _modulation(c_all, w_mod, b_mod):
    n = w_mod.shape[1]
    bn = 1536
    return pl.pallas_call(
        _mod_kernel,
        grid=(n // bn,),
        in_specs=[
            pl.BlockSpec((8, D_MODEL), lambda j: (0, 0)),
            pl.BlockSpec((D_MODEL, bn), lambda j: (0, j)),
            pl.BlockSpec((1, bn), lambda j: (0, j)),
        ],
        out_specs=pl.BlockSpec((8, bn), lambda j: (0, j)),
        out_shape=jax.ShapeDtypeStruct((8, n), F32),
        compiler_params=pltpu.CompilerParams(vmem_limit_bytes=VMEM_LIMIT),
        name="modulation",
    )(c_all, w_mod, b_mod.reshape(1, n))


W_TILE = 16
COL_Z, COL_XBC, COL_ML, COL_G = 0, SSD_WIDTH, SSD_WIDTH + SSD_XBC, SSD_WIDTH + SSD_XBC + 3 * ML_WIDTH
PROJ_CHUNK = 512


def _norm_mod(xin, sh_ref, sc_ref, nw_ref):
    ms = jnp.mean(xin * xin, axis=-1, keepdims=True)
    y = xin * lax.rsqrt(ms + EPS) * nw_ref[...]
    return (y * (1.0 + sc_ref[...]) + sh_ref[...]).astype(BF16)


def _in_lat_kernel(rows, x_ref, sh_ref, sc_ref, nw_ref, w_ref, z_ref, xbc_ref, ml_ref, g_ref, gml_ref, scr):
    tm = rows * W_TILE
    h = _norm_mod(x_ref[...].reshape(tm, D_MODEL), sh_ref, sc_ref, nw_ref)

    def proj(col, width=PROJ_CHUNK):
        return _dot(h, w_ref[:, col:col + width])

    def to_col_major(val, dst_ref, lo, dtype):
        n_slab = val.shape[1] // LANES
        for k in range(n_slab):
            scr[k] = val[:, k * LANES:(k + 1) * LANES]
        for j in range(W_TILE):
            for k in range(n_slab):
                dst_ref[j, :, lo + k * LANES:lo + (k + 1) * LANES] = (
                    scr[k, pl.ds(j, rows, stride=W_TILE), :].astype(dtype))

    z_ref[...] = proj(COL_Z).astype(BF16).reshape(rows, W_TILE, PROJ_CHUNK)
    for j in range(SSD_XBC // PROJ_CHUNK):
        lo = j * PROJ_CHUNK
        xbc_ref[:, :, lo:lo + PROJ_CHUNK] = proj(COL_XBC + lo).astype(BF16).reshape(rows, W_TILE, PROJ_CHUNK)
    for j in range(3 * ML_WIDTH // PROJ_CHUNK):
        to_col_major(proj(COL_ML + j * PROJ_CHUNK), ml_ref, j * PROJ_CHUNK, BF16)
    g = proj(COL_G, LANES)
    g_ref[...] = g.reshape(rows, W_TILE, LANES)
    to_col_major(g, gml_ref, 0, F32)


def _in_ctx_kernel(x_ref, sh_ref, sc_ref, nw_ref, w_ref, xbc_ref, ml_ref, g_ref):
    h = _norm_mod(x_ref[...], sh_ref, sc_ref, nw_ref)
    for j in range(SSD_XBC // PROJ_CHUNK):
        lo = j * PROJ_CHUNK
        xbc_ref[:, lo:lo + PROJ_CHUNK] = _dot(h, w_ref[:, COL_XBC + lo:COL_XBC + lo + PROJ_CHUNK]).astype(BF16)
    for j in range(3 * ML_WIDTH // PROJ_CHUNK):
        lo = j * PROJ_CHUNK
        ml_ref[:, lo:lo + PROJ_CHUNK] = _dot(h, w_ref[:, COL_ML + lo:COL_ML + lo + PROJ_CHUNK]).astype(BF16)
    g_ref[...] = _dot(h, w_ref[:, COL_G:COL_G + LANES])


def _in_proj(x, ctx, mod3, norm_w, w_cat):
    b, seq, d = x.shape
    ctx_len = ctx.shape[1]
    rows = seq // GRID_W
    widths = (SSD_WIDTH, SSD_XBC, 3 * ML_WIDTH, LANES, LANES)
    dtypes = (BF16, BF16, BF16, F32, F32)
    params = pltpu.CompilerParams(dimension_semantics=("arbitrary", "arbitrary"), vmem_limit_bytes=VMEM_LIMIT)
    nw = norm_w.reshape(1, d)

    def raster(width):
        return pl.BlockSpec((None, rows, W_TILE, width), lambda bi, wi: (bi, 0, wi, 0))

    def col_major(width):
        return pl.BlockSpec((None, W_TILE, rows, width), lambda bi, wi: (bi, wi, 0, 0))

    outs = pl.pallas_call(
        functools.partial(_in_lat_kernel, rows),
        grid=(b, GRID_W // W_TILE),
        in_specs=[
            raster(d),
            pl.BlockSpec((None, 1, d), lambda bi, wi: (bi * 6, 0, 0)),
            pl.BlockSpec((None, 1, d), lambda bi, wi: (bi * 6 + 1, 0, 0)),
            _full((1, d)), _full(w_cat.shape),
        ],
        out_specs=[raster(SSD_WIDTH), raster(SSD_XBC), col_major(3 * ML_WIDTH), raster(LANES), col_major(LANES)],
        out_shape=[jax.ShapeDtypeStruct((b, GRID_W, rows, w) if cm else (b, rows, GRID_W, w), t)
                   for w, t, cm in zip(widths, dtypes, (False, False, True, False, True))],
        scratch_shapes=[pltpu.VMEM((PROJ_CHUNK // LANES, rows * W_TILE, LANES), F32)],
        compiler_params=params,
        name="in_proj",
    )(x.reshape(b, rows, GRID_W, d), mod3, mod3, nw, w_cat)
    z, xbc, ml, gates, gates_cm = [o.reshape(b, seq, w) for o, w in zip(outs, widths)]

    ctx_row = b
    ctx_widths = (SSD_XBC, 3 * ML_WIDTH, LANES)
    xbc_c, ml_c, gates_c = pl.pallas_call(
        _in_ctx_kernel,
        grid=(b, 1),
        in_specs=[
            pl.BlockSpec((None, ctx_len, d), lambda bi, t: (bi, 0, 0)),
            pl.BlockSpec((None, 1, d), lambda bi, t: (ctx_row * 6, 0, 0)),
            pl.BlockSpec((None, 1, d), lambda bi, t: (ctx_row * 6 + 1, 0, 0)),
            _full((1, d)), _full(w_cat.shape),
        ],
        out_specs=[pl.BlockSpec((None, ctx_len, w), lambda bi, t: (bi, 0, 0)) for w in ctx_widths],
        out_shape=[jax.ShapeDtypeStruct((b, ctx_len, w), t) for w, t in zip(ctx_widths, (BF16, BF16, F32))],
        compiler_params=params,
        name="in_proj_ctx",
    )(ctx, mod3, mod3, nw, w_cat)
    return z, (xbc, xbc_c), (ml, ml_c), (gates, gates_c), (gates_cm, gates_c)


def _chunk_eff(rev, ncc, nc, c):
    nl = nc - ncc
    if not rev:
        return jnp.where(c < ncc, nl + c, c - ncc)
    return jnp.where(c < ncc, nc - 1 - c, nl - 1 - (c - ncc))


def _seq_ends(ncc, nc, ceff):
    nl = nc - ncc
    return (ceff == 0) | (ceff == nl), (ceff == nl - 1) | (ceff == nc - 1)


def _lat_block(rev, ncc, nc, c):
    nl = nc - ncc
    if not rev:
        return jnp.maximum(c - ncc, 0)
    return jnp.where(c < ncc, nl - 1, nl - 1 - (c - ncc))


CONV_SIDE_TAPS = tuple(t for t in range(CONV_W) if t != CONV_W // 2)


def _shift_matrices():
    row = jnp.arange(CHUNK)[:, None]
    col = jnp.arange(CHUNK + 2 * HALO)[None, :]
    return jnp.stack([col == row + HALO + tap - CONV_W // 2 for tap in CONV_SIDE_TAPS]).astype(BF16)


def _conv_silu(xm, xp, xn, cw_ref, cb_ref, sh_ref, first, last):
    xp = jnp.where(first, jnp.zeros_like(xp), xp)
    xn = jnp.where(last, jnp.zeros_like(xn), xn)
    ext = jnp.concatenate([xp, xm, xn], axis=0)
    mid = CONV_W // 2
    acc = cb_ref[...] + cw_ref[mid:mid + 1, :] * xm.astype(F32)
    for i, tap in enumerate(CONV_SIDE_TAPS):
        acc = acc + cw_ref[tap:tap + 1, :] * _dot(sh_ref[i], ext)
    return _silu(acc)


def _tri(rev, transposed=False):
    row = lax.broadcasted_iota(jnp.int32, (CHUNK, CHUNK), 0)
    col = lax.broadcasted_iota(jnp.int32, (CHUNK, CHUNK), 1)
    if transposed:
        row, col = col, row
    keep = (col >= row) if rev else (col <= row)
    return keep


def _split3(a):
    a1 = a.astype(BF16)
    r1 = a - a1.astype(F32)
    a2 = r1.astype(BF16)
    a3 = (r1 - a2.astype(F32)).astype(BF16)
    return a1, a2, a3


def _dot_sel_l(m_b, a):
    p = _split3(a)
    return _dot(m_b, p[0]) + _dot(m_b, p[1]) + _dot(m_b, p[2])


def _dot_sel_r(a, m_b):
    p = _split3(a)
    return _dot(p[0], m_b) + _dot(p[1], m_b) + _dot(p[2], m_b)


def _row_spec(rows, width, row_block, col_block=0):
    return pl.BlockSpec((NB, rows, width), lambda bi, c: (bi, row_block(c), col_block))


def _local_chunks(rev, ncc, nc, c):
    nl = nc - ncc
    cc = jnp.clip(ncc - 1 - c if rev else c, 0, ncc - 1)
    cl = jnp.clip(nl - 1 - (c - ncc) if rev else c - ncc, 0, nl - 1)
    return c < ncc, cc, cl


def _pair_specs(rev, ncc, nc, rows, width, col=0, halo=0):
    per = CHUNK // HALO

    def index(which, count):
        def fn(c):
            ch = _local_chunks(rev, ncc, nc, c)[which]
            if halo == 0:
                return ch
            if halo < 0:
                return jnp.maximum(ch * per - 1, 0)
            return jnp.minimum((ch + 1) * per, count * per - 1)
        return fn

    return [_row_spec(rows, width, index(2, nc - ncc), col), _row_spec(rows, width, index(1, ncc), col)]


def _pick(is_ctx, lat_ref, ctx_ref, s):
    return jnp.where(is_ctx, ctx_ref[s], lat_ref[s])


def _step_ends(rev, ncc, nc, c):
    is_ctx, cc, cl = _local_chunks(rev, ncc, nc, c)
    first = jnp.where(is_ctx, cc == 0, cl == 0)
    last = jnp.where(is_ctx, cc == ncc - 1, cl == nc - ncc - 1)
    return is_ctx, first, last


def _full(shape):
    return pl.BlockSpec(shape, lambda bi, c: (0,) * len(shape))


def _ssd_chunk(rev, xbc, g, dtbr_ref, dtbc_ref, alr_ref, alc_ref, e_ref, s_ref, s):
    xs = xbc[:, :SSD_WIDTH]
    lane0 = SSD_HEADS * int(rev)
    lane = lax.broadcasted_iota(jnp.int32, (CHUNK, LANES), 1)
    lmask = (lane >= lane0) & (lane < lane0 + SSD_HEADS)
    dt = jnp.where(lmask, _softplus(g + dtbr_ref[...]), 0.0)
    a = dt * (-jnp.exp(alr_ref[...]))
    gt = g.T
    dt_t = _softplus(gt + dtbc_ref[...])[lane0:lane0 + SSD_HEADS]
    a_t = dt_t * (-jnp.exp(alc_ref[...][lane0:lane0 + SSD_HEADS]))

    cs = _dot_sel_l(_tri(rev).astype(BF16), a)
    cs_t = _dot_sel_r(a_t, _tri(rev, transposed=True).astype(BF16))
    e = e_ref[...]
    dtx = _dot_sel_r(dt, e)
    csx = _dot_sel_r(cs, e)
    end = 0 if rev else CHUNK - 1
    totx = csx[end:end + 1, :]
    ecsx = jnp.exp(csx)
    decx = jnp.exp(totx - csx)
    etotx = jnp.exp(totx)

    xdt = xs * dtx
    xdt_b = xdt.astype(BF16)
    xd_b = (xdt * decx).astype(BF16)
    keep = _tri(rev)
    half = lax.broadcasted_iota(jnp.int32, (CHUNK, LANES), 1) // SSD_HEADDIM

    y_blocks = []
    gs = SSD_GROUPS * SSD_STATE
    hpg = SSD_HEADS // SSD_GROUPS
    gw = hpg * SSD_HEADDIM
    for grp in range(SSD_GROUPS):
        bm = xbc[:, SSD_WIDTH + grp * SSD_STATE:SSD_WIDTH + (grp + 1) * SSD_STATE]
        cm = xbc[:, SSD_WIDTH + gs + grp * SSD_STATE:SSD_WIDTH + gs + (grp + 1) * SSD_STATE]
        bm_b, cm_b = bm.astype(BF16), cm.astype(BF16)
        cb = _dot_nt(cm_b, bm_b)
        s_old = s_ref[s, :, grp * gw:(grp + 1) * gw]
        y_off = _dot(cm_b, s_old.astype(BF16)) * ecsx[:, grp * gw:(grp + 1) * gw]
        for pair in range(hpg // 2):
            blk = grp * (hpg // 2) + pair
            xj = xdt_b[:, blk * LANES:(blk + 1) * LANES]
            acc = y_off[:, pair * LANES:(pair + 1) * LANES]
            for q in range(2):
                h = blk * 2 + q
                dl = cs[:, lane0 + h:lane0 + h + 1] - cs_t[h:h + 1, :]
                m = (cb * jnp.exp(jnp.where(keep, dl, NEG_BIG))).astype(BF16)
                acc = acc + _dot(m, jnp.where(half == q, xj, jnp.zeros_like(xj)))
            y_blocks.append(acc)
        s_new = s_old * etotx[:, grp * gw:(grp + 1) * gw] + _dot(bm.T.astype(BF16), xd_b[:, grp * gw:(grp + 1) * gw])
        s_ref[s, :, grp * gw:(grp + 1) * gw] = s_new

    return jnp.concatenate(y_blocks, axis=1), xs


def _ssd_rev_kernel(ncc, nc, xml_ref, xmc_ref, xpl_ref, xpc_ref, xnl_ref, xnc_ref, gl_ref, gc_ref, cw_ref, cb_ref,
                    sh_ref, dtbr_ref, dtbc_ref, alr_ref, alc_ref, e_ref, o_ref, xc_ref, s_ref):
    c = pl.program_id(1)
    is_ctx, first, last = _step_ends(True, ncc, nc, c)

    @pl.when(c == 0)
    def _():
        s_ref[...] = jnp.zeros_like(s_ref)

    for s in range(NB):
        xbc = _conv_silu(_pick(is_ctx, xml_ref, xmc_ref, s), _pick(is_ctx, xpl_ref, xpc_ref, s),
                         _pick(is_ctx, xnl_ref, xnc_ref, s), cw_ref, cb_ref, sh_ref, first, last)
        xc_ref[s] = xbc.astype(BF16)
        y, _ = _ssd_chunk(True, xbc, _pick(is_ctx, gl_ref, gc_ref, s), dtbr_ref, dtbc_ref, alr_ref, alc_ref, e_ref,
                          s_ref, s)
        o_ref[s] = y.astype(o_ref.dtype)


def _ssd_fwd_kernel(ncc, xc_ref, gl_ref, gc_ref, dtbr_ref, dtbc_ref, alr_ref, alc_ref, e_ref, dsk_ref, yb_ref,
                    z_ref, nw_ref, o_ref, s_ref):
    is_ctx = pl.program_id(1) < ncc

    @pl.when(pl.program_id(1) == 0)
    def _():
        s_ref[...] = jnp.zeros_like(s_ref)

    for s in range(NB):
        y, xs = _ssd_chunk(False, xc_ref[s].astype(F32), _pick(is_ctx, gl_ref, gc_ref, s), dtbr_ref, dtbc_ref,
                           alr_ref, alc_ref, e_ref, s_ref, s)
        y = y + yb_ref[s].astype(F32) + dsk_ref[...] * xs
        y = y * _silu(z_ref[s].astype(F32))
        y = y * lax.rsqrt(jnp.mean(y * y, axis=-1, keepdims=True) + EPS) * nw_ref[...]
        o_ref[s] = y.astype(o_ref.dtype)


def _ssd_scans(xbc, gates, z, conv_params, gate_params, e_mats, dsk, nw):
    b, seq, _ = xbc[0].shape
    ncc = xbc[1].shape[1] // CHUNK
    nc = seq // CHUNK + ncc
    state = pltpu.VMEM((NB, SSD_STATE, SSD_WIDTH), F32)
    params = pltpu.CompilerParams(dimension_semantics=("arbitrary", "arbitrary"), vmem_limit_bytes=VMEM_LIMIT)
    lat_shape = jax.ShapeDtypeStruct((b, seq, SSD_WIDTH), BF16)

    ceff = functools.partial(_chunk_eff, True, ncc, nc)
    lat = functools.partial(_lat_block, True, ncc, nc)
    pair = functools.partial(_pair_specs, True, ncc, nc)
    rev_params = list(conv_params) + list(gate_params) + [e_mats[1]]
    yb, xbc_act = pl.pallas_call(
        functools.partial(_ssd_rev_kernel, ncc, nc),
        grid=(b // NB, nc),
        in_specs=pair(CHUNK, SSD_XBC) + pair(HALO, SSD_XBC, halo=-1) + pair(HALO, SSD_XBC, halo=1)
        + pair(CHUNK, LANES) + [_full(p.shape) for p in rev_params],
        out_specs=[_row_spec(CHUNK, SSD_WIDTH, lat), _row_spec(CHUNK, SSD_XBC, ceff)],
        out_shape=(lat_shape, jax.ShapeDtypeStruct((b, nc * CHUNK, SSD_XBC), BF16)),
        scratch_shapes=[state],
        compiler_params=params,
        name="ssd_rev",
    )(*xbc, *xbc, *xbc, *gates, *rev_params)

    ceff = functools.partial(_chunk_eff, False, ncc, nc)
    lat = functools.partial(_lat_block, False, ncc, nc)
    fwd_params = list(gate_params) + [e_mats[0], dsk]
    return pl.pallas_call(
        functools.partial(_ssd_fwd_kernel, ncc),
        grid=(b // NB, nc),
        in_specs=[_row_spec(CHUNK, SSD_XBC, ceff)] + _pair_specs(False, ncc, nc, CHUNK, LANES)
        + [_full(p.shape) for p in fwd_params]
        + [_row_spec(CHUNK, SSD_WIDTH, lat), _row_spec(CHUNK, SSD_WIDTH, lat), _full(nw.shape)],
        out_specs=_row_spec(CHUNK, SSD_WIDTH, lat),
        out_shape=lat_shape,
        scratch_shapes=[state],
        compiler_params=params,
        name="ssd_fwd",
    )(xbc_act, *gates, *fwd_params, yb, z, nw)


ML_I_LANE0 = 2 * SSD_HEADS
ML_F_LANE0 = ML_I_LANE0 + 2 * ML_HEADS


def _ml_chunk(rev, k, q_b, k_b, v_b, g, gbr_ref, gbc_ref, c_ref, n_ref, mx_ref, s):
    i_lane0 = ML_I_LANE0 + ML_HEADS * int(rev)
    f_lane0 = ML_F_LANE0 + ML_HEADS * int(rev)
    ga = g + gbr_ref[...]
    lane = lax.broadcasted_iota(jnp.int32, (CHUNK, LANES), 1)
    logf = jnp.where((lane >= f_lane0) & (lane < f_lane0 + ML_HEADS), -_softplus(-ga), 0.0)
    cs = _dot_sel_l(_tri(rev).astype(BF16), logf)
    gt = g.T + gbc_ref[...]
    i_t = gt[ML_I_LANE0:ML_F_LANE0]
    logf_t = -_softplus(-gt[ML_F_LANE0:ML_F_LANE0 + 2 * ML_HEADS])
    cs_t = _dot_sel_r(logf_t, _tri(rev, transposed=True).astype(BF16))
    keep = _tri(rev)
    end = 0 if rev else CHUNK - 1

    u_t = i_t - cs_t
    lane8 = lax.broadcasted_iota(jnp.int32, u_t.shape, 1)
    pm = u_t
    step = 1
    while step < CHUNK:
        if rev:
            pm = jnp.maximum(pm, jnp.where(lane8 < CHUNK - step, pltpu.roll(pm, CHUNK - step, axis=1), NEG_BIG))
        else:
            pm = jnp.maximum(pm, jnp.where(lane8 >= step, pltpu.roll(pm, step, axis=1), NEG_BIG))
        step *= 2
    pm_c = jnp.concatenate([pm, jnp.zeros((CHUNK - pm.shape[0], CHUNK), F32)], axis=0).T
    ones_b = jnp.ones((CHUNK, ML_HEADDIM), BF16)

    outs = []
    for h in range(ML_HEADS):
        r = ML_HEADS * int(rev) + h
        sl = slice(h * ML_HEADDIM, (h + 1) * ML_HEADDIM)
        kh = k[:, sl]
        vh = v_b[:, sl]
        qh_b, kh_b = q_b[:, sl], k_b[:, sl]
        csc = cs[:, f_lane0 + h:f_lane0 + h + 1]
        u_col = ga[:, i_lane0 + h:i_lane0 + h + 1] - csc
        u_row = u_t[r:r + 1, :]
        tot = cs_t[r:r + 1, end:end + 1]
        u_max = pm[r:r + 1, end:end + 1]
        m_prev = mx_ref[s, h:h + 1, 0:1]
        c_prev = c_ref[s * ML_HEADS + h]
        n_prev = n_ref[s, h:h + 1, :]

        mm = jnp.maximum(m_prev, pm_c[:, r:r + 1])
        scores = _dot_nt(qh_b, kh_b) * jnp.exp(jnp.where(keep, u_row - mm, NEG_BIG))
        w_inter = jnp.exp(m_prev - mm)
        intra = _dot(scores.astype(BF16), jnp.concatenate([vh, ones_b], axis=1))
        c_aug = jnp.concatenate([c_prev, jnp.broadcast_to(n_prev, (CHUNK, ML_HEADDIM))], axis=0).astype(BF16)
        both = intra + w_inter * _dot_nt(qh_b, c_aug)
        den = both[:, ML_HEADDIM:ML_HEADDIM + 1]
        outs.append(both[:, :ML_HEADDIM] / jnp.maximum(jnp.abs(den), jnp.exp(-(csc + mm))))

        m_loc = tot + u_max
        kw = kh * jnp.exp(u_col - u_max)
        c_loc = _dot(vh.astype(F32).T.astype(BF16), kw.astype(BF16))
        n_loc = jnp.sum(kw, axis=0, keepdims=True)
        m_new = jnp.maximum(tot + m_prev, m_loc)
        s_prev = jnp.exp(tot + m_prev - m_new)
        s_loc = jnp.exp(m_loc - m_new)
        c_ref[s * ML_HEADS + h] = s_prev * c_prev + s_loc * c_loc
        n_ref[s, h:h + 1, :] = s_prev * n_prev + s_loc * n_loc
        mx_ref[s, h:h + 1, :] = jnp.broadcast_to(m_new, (1, LANES))
    return outs


def _ml_init_state(c_ref, n_ref, mx_ref):
    c_ref[...] = jnp.zeros_like(c_ref)
    n_ref[...] = jnp.zeros_like(n_ref)
    mx_ref[...] = jnp.full(mx_ref.shape, NEG_BIG, F32)


def _ml_rev_kernel(ncc, nc, xml_ref, xmc_ref, xpl_ref, xpc_ref, xnl_ref, xnc_ref, vl_ref, vc_ref, gl_ref, gc_ref,
                   cw_ref, cb_ref, sh_ref, wq_ref, wk_ref, gbr_ref, gbc_ref, o_ref, xc_ref, q_ref, k_ref, c_ref, n_ref,
                   mx_ref):
    c = pl.program_id(1)
    is_ctx, first, last = _step_ends(True, ncc, nc, c)

    @pl.when(c == 0)
    def _():
        _ml_init_state(c_ref, n_ref, mx_ref)

    for s in range(NB):
        xconv = _conv_silu(_pick(is_ctx, xml_ref, xmc_ref, s), _pick(is_ctx, xpl_ref, xpc_ref, s),
                           _pick(is_ctx, xnl_ref, xnc_ref, s), cw_ref, cb_ref, sh_ref, first, last)
        v_b = _pick(is_ctx, vl_ref, vc_ref, s)
        g = _pick(is_ctx, gl_ref, gc_ref, s)
        xc_b = xconv.astype(BF16)
        q = _dot(xc_b, wq_ref[...])
        k = _dot(xc_b, wk_ref[...])
        q_b, k_b = q.astype(BF16), k.astype(BF16)
        xc_ref[s] = xc_b
        q_ref[s] = q_b
        k_ref[s] = k_b
        outs = _ml_chunk(True, k, q_b, k_b, v_b, g, gbr_ref, gbc_ref, c_ref, n_ref, mx_ref, s)
        o_ref[s] = jnp.concatenate(outs, axis=1).astype(o_ref.dtype)


def _ml_fwd_kernel(ncc, xc_ref, q_ref, k_ref, vl_ref, vc_ref, og_ref, gl_ref, gc_ref, gbr_ref, gbc_ref, hb_ref,
                   nw_ref, sk_ref, o_ref, c_ref, n_ref, mx_ref):
    is_ctx = pl.program_id(1) < ncc

    @pl.when(pl.program_id(1) == 0)
    def _():
        _ml_init_state(c_ref, n_ref, mx_ref)

    for s in range(NB):
        q_b, k_b = q_ref[s], k_ref[s]
        outs = _ml_chunk(False, k_b.astype(F32), q_b, k_b, _pick(is_ctx, vl_ref, vc_ref, s),
                         _pick(is_ctx, gl_ref, gc_ref, s), gbr_ref, gbc_ref, c_ref, n_ref, mx_ref, s)
        normed = []
        for h in range(ML_HEADS):
            sl = slice(h * ML_HEADDIM, (h + 1) * ML_HEADDIM)
            hh = jax.nn.sigmoid(og_ref[s, :, sl].astype(F32)) * (outs[h] + hb_ref[s, :, sl].astype(F32))
            normed.append(hh * lax.rsqrt(jnp.mean(hh * hh, axis=-1, keepdims=True) + EPS))
        y = jnp.concatenate(normed, axis=1) * nw_ref[...] + sk_ref[...] * xc_ref[s].astype(F32)
        o_ref[s] = y.astype(o_ref.dtype)


def _ml_scans(ml, gates, conv_params, proj_params, gate_params, nw, sk):
    b, seq, _ = ml[0].shape
    ncc = ml[1].shape[1] // CHUNK
    nc = seq // CHUNK + ncc
    scratch = [pltpu.VMEM((NB * ML_HEADS, ML_HEADDIM, ML_HEADDIM), F32),
               pltpu.VMEM((NB, 8, ML_HEADDIM), F32),
               pltpu.VMEM((NB, 8, LANES), F32)]
    params = pltpu.CompilerParams(dimension_semantics=("arbitrary", "arbitrary"), vmem_limit_bytes=VMEM_LIMIT)
    lat_shape = jax.ShapeDtypeStruct((b, seq, ML_WIDTH), BF16)
    act_shape = jax.ShapeDtypeStruct((b, nc * CHUNK, ML_WIDTH), BF16)

    ceff = functools.partial(_chunk_eff, True, ncc, nc)
    lat = functools.partial(_lat_block, True, ncc, nc)
    pair = functools.partial(_pair_specs, True, ncc, nc)
    rev_params = list(conv_params) + list(proj_params) + list(gate_params)
    act_spec = _row_spec(CHUNK, ML_WIDTH, ceff)
    hb, xc, q, k = pl.pallas_call(
        functools.partial(_ml_rev_kernel, ncc, nc),
        grid=(b // NB, nc),
        in_specs=pair(CHUNK, ML_WIDTH) + pair(HALO, ML_WIDTH, halo=-1) + pair(HALO, ML_WIDTH, halo=1)
        + pair(CHUNK, ML_WIDTH, col=1) + pair(CHUNK, LANES) + [_full(p.shape) for p in rev_params],
        out_specs=[_row_spec(CHUNK, ML_WIDTH, lat), act_spec, act_spec, act_spec],
        out_shape=(lat_shape, act_shape, act_shape, act_shape),
        scratch_shapes=scratch,
        compiler_params=params,
        name="mlstm_rev",
    )(*ml, *ml, *ml, *ml, *gates, *rev_params)

    ceff = functools.partial(_chunk_eff, False, ncc, nc)
    lat = functools.partial(_lat_block, False, ncc, nc)
    pair = functools.partial(_pair_specs, False, ncc, nc)
    act_spec = _row_spec(CHUNK, ML_WIDTH, ceff)
    return pl.pallas_call(
        functools.partial(_ml_fwd_kernel, ncc),
        grid=(b // NB, nc),
        in_specs=[act_spec, act_spec, act_spec] + pair(CHUNK, ML_WIDTH, col=1)
        + [_row_spec(CHUNK, ML_WIDTH, lat, 2)] + pair(CHUNK, LANES)
        + [_full(p.shape) for p in gate_params]
        + [_row_spec(CHUNK, ML_WIDTH, lat), _full(nw.shape), _full(sk.shape)],
        out_specs=_row_spec(CHUNK, ML_WIDTH, lat),
        out_shape=lat_shape,
        scratch_shapes=scratch,
        compiler_params=params,
        name="mlstm_fwd",
    )(xc, q, k, *ml, ml[0], *gates, *gate_params, hb, nw, sk)


def _out_kernel(x_ref, ys_ref, ym_ref, wo_ref, g1_ref, sh_ref, sc_ref, nw_ref, rw_ref, rb_ref,
                x1_ref, ts_ref, route_ref):
    mix = _dot(ys_ref[...], wo_ref[0:SSD_WIDTH, :]) + _dot(ym_ref[...], wo_ref[SSD_WIDTH:, :])
    x1 = x_ref[...] + g1_ref[...] * mix
    y = x1 * lax.rsqrt(jnp.mean(x1 * x1, axis=-1, keepdims=True) + EPS) * nw_ref[...]
    t = y * (1.0 + sc_ref[...]) + sh_ref[...]
    x1_ref[...] = x1
    for j in range(SLAB):
        ts_ref[pl.ds(j, TM_IN, stride=SLAB), :] = t[:, j * LANES:(j + 1) * LANES]
    lg2 = _dot(t.astype(BF16), rw_ref[...])
    lg = lg2[:, :LANES] + lg2[:, LANES:] + rb_ref[...]

    lane = lax.broadcasted_iota(jnp.int32, lg.shape, 1).astype(F32)
    gmask = lane < MOE_GROUPS
    gl = jnp.where(gmask, lg, NEG_BIG)
    gmax = jnp.max(gl, axis=1, keepdims=True)
    g_sel = jnp.min(jnp.where(gmask & (gl == gmax), lane, 1e9), axis=1, keepdims=True)
    p_group = 1.0 / jnp.sum(jnp.where(gmask, jnp.exp(gl - gmax), 0.0), axis=1, keepdims=True)
    lo = ROUTE_LANE0 + MOE_EPG * g_sel
    emask = (lane >= lo) & (lane < lo + MOE_EPG)
    l1 = jnp.max(jnp.where(emask, lg, NEG_BIG), axis=1, keepdims=True)
    i1 = jnp.min(jnp.where(emask & (lg == l1), lane, 1e9), axis=1, keepdims=True)
    emask2 = emask & (lane != i1)
    l2 = jnp.max(jnp.where(emask2, lg, NEG_BIG), axis=1, keepdims=True)
    i2 = jnp.min(jnp.where(emask2 & (lg == l2), lane, 1e9), axis=1, keepdims=True)
    r = jnp.exp(l2 - l1)
    w1 = p_group / (1.0 + r)
    w2 = p_group * r / (1.0 + r)
    route = (jnp.where(lane == 0, i1, 0.0) + jnp.where(lane == 1, i2, 0.0)
             + jnp.where(lane == 2, w1, 0.0) + jnp.where(lane == 3, w2, 0.0))
    route_ref[...] = route.T[0:8, :]


def _out_proj(x, y_ssd, y_ml, w_out_b, mod3, norm_w, rw2, rb):
    b, seq, d = x.shape
    nt = seq // TM_IN

    def row(j):
        return pl.BlockSpec((None, 1, d), lambda bi, t: (bi * 6 + j, 0, 0))

    def tile(width):
        return pl.BlockSpec((None, TM_IN, width), lambda bi, t: (bi, t, 0))

    return pl.pallas_call(
        _out_kernel,
        grid=(b, nt),
        in_specs=[tile(d), tile(SSD_WIDTH), tile(ML_WIDTH), _full(w_out_b.shape),
                  row(2), row(3), row(4), _full((1, d)), _full(rw2.shape), _full(rb.shape)],
        out_specs=[tile(d),
                   pl.BlockSpec((None, TM_IN * SLAB, LANES), lambda bi, t: (bi, t, 0)),
                   pl.BlockSpec((None, 8, TM_IN), lambda bi, t: (bi, 0, t))],
        out_shape=(jax.ShapeDtypeStruct((b, seq, d), F32),
                   jax.ShapeDtypeStruct((b, seq * SLAB, LANES), F32),
                   jax.ShapeDtypeStruct((b, 8, seq), F32)),
        compiler_params=pltpu.CompilerParams(
            dimension_semantics=("arbitrary", "arbitrary"), vmem_limit_bytes=VMEM_LIMIT),
        name="out_proj_router",
    )(x, y_ssd, y_ml, w_out_b, mod3, mod3, mod3, norm_w.reshape(1, d), rw2, rb)


def _route_tables(route, seq):
    b = route.shape[0]
    n_inst = 2 * seq
    nblk = n_inst // BM + MOE_EXPERTS
    e_flat = (route[:, 0:2, :].astype(jnp.int32) - ROUTE_LANE0).reshape(b, n_inst)
    w_flat = route[:, 2:4, :].reshape(b, n_inst)
    tok_flat = jnp.tile(jnp.arange(seq, dtype=jnp.int32), 2 * b).reshape(b, n_inst)
    _, tok_sorted, w_sorted = lax.sort((e_flat, tok_flat, w_flat), dimension=1, num_keys=1)
    experts = jnp.arange(MOE_EXPERTS, dtype=jnp.int32)
    counts = jnp.sum((e_flat[:, None, :] == experts[None, :, None]).astype(jnp.int32), axis=2)
    nblk_e = (counts + BM - 1) // BM
    blk_end = jnp.cumsum(nblk_e, axis=1)
    blk_start = blk_end - nblk_e
    cnt_start = jnp.cumsum(counts, axis=1) - counts
    nb = blk_end[:, -1:]
    j = jnp.arange(nblk, dtype=jnp.int32)[None, :]
    valid_blk = j < nb
    jj = jnp.minimum(j, nb - 1)
    e_j = jnp.sum((jj[:, :, None] >= blk_end[:, None, :]).astype(jnp.int32), axis=2)
    onehot = (e_j[:, :, None] == experts[None, None, :]).astype(jnp.int32)
    take = lambda tbl: jnp.sum(onehot * tbl[:, None, :], axis=2)
    r = jnp.arange(BM, dtype=jnp.int32)[None, None, :]
    rank = ((jj - take(blk_start)) * BM)[:, :, None] + r
    valid = valid_blk[:, :, None] & (rank < take(counts)[:, :, None])
    sidx = jnp.clip(take(cnt_start)[:, :, None] + rank, 0, n_inst - 1).reshape(b, nblk * BM)
    tok = jnp.take_along_axis(tok_sorted, sidx, axis=1).reshape(b, nblk, BM)
    wslot = jnp.take_along_axis(w_sorted, sidx, axis=1).reshape(b, nblk, BM)
    tok = jnp.where(valid, tok, seq + r) * SLAB
    wslot = jnp.where(valid, wslot, 0.0)
    return (nblk_e.reshape(-1), blk_start.reshape(-1), tok.reshape(b, 1, nblk * BM), wslot)


SEM_T, SEM_X, SEM_O = 0, 1, 3


def _moe_kernel(seq, nbe_ref, bs_ref, tok_ref, ws_ref, wg_ref, wu_ref, wd_ref, g2_ref, fw_ref,
                t_hbm, x1_hbm, o_hbm, t_scr, y_scr, wgb, wub, wdb, xt, ot, xin, stage, sems):
    b = pl.program_id(0)
    e = pl.program_id(1)
    nb = pl.num_programs(0)
    rows = seq * SLAB
    n_fin = seq // TM_FIN

    def t_copy(sample):
        return pltpu.make_async_copy(t_hbm.at[sample], t_scr.at[pl.ds(0, rows)], sems.at[SEM_T])

    def x1_copy(s, slot):
        return pltpu.make_async_copy(x1_hbm.at[b, pl.ds(s * TM_FIN, TM_FIN)], xin.at[slot], sems.at[SEM_X + slot])

    def out_copy(s, slot):
        return pltpu.make_async_copy(stage.at[slot], o_hbm.at[b, pl.ds(s * TM_FIN, TM_FIN)], sems.at[SEM_O + slot])

    @pl.when((e == 0) & (b == 0))
    def _():
        t_copy(b).start()

    @pl.when(e == 0)
    def _():
        y_scr[...] = jnp.zeros_like(y_scr)
        t_scr[pl.ds(rows, BM * SLAB), :] = jnp.zeros((BM * SLAB, LANES), F32)
        t_copy(b).wait()

    n_blocks = nbe_ref[b * MOE_EXPERTS + e]
    blk0 = bs_ref[b * MOE_EXPERTS + e]

    @pl.when(n_blocks > 0)
    def _():
        wgb[...] = wg_ref[...].astype(BF16)
        wub[...] = wu_ref[...].astype(BF16)
        wdb[...] = wd_ref[...].astype(BF16)
        g2 = g2_ref[...]
        diag = (lax.broadcasted_iota(jnp.int32, (BM, BM), 0) == lax.broadcasted_iota(jnp.int32, (BM, BM), 1))

        def block(i, carry):
            blk = blk0 + i
            base = blk * BM

            def slab_rows(r):
                return pl.ds(pl.multiple_of(tok_ref[0, base + r], SLAB), SLAB)

            for r in range(BM):
                xt[pl.ds(r, SLAB, stride=TILE_PITCH), :] = t_scr[slab_rows(r), :]
            x = jnp.concatenate([xt[c * TILE_PITCH:c * TILE_PITCH + BM, :] for c in range(SLAB)],
                                axis=1).astype(BF16)
            w_col = jnp.sum(jnp.where(diag, ws_ref[pl.ds(blk, 1), :], 0.0), axis=1, keepdims=True)
            hidden = _silu(_dot(x, wgb[...])) * _dot(x, wub[...]) * w_col
            out = _dot(hidden.astype(BF16), wdb[...]) * g2
            for c in range(SLAB):
                ot[c * TILE_PITCH:c * TILE_PITCH + BM, :] = out[:, c * LANES:(c + 1) * LANES]
            for r0 in range(0, BM, SCATTER_GROUP):
                sl = [slab_rows(r0 + u) for u in range(SCATTER_GROUP)]
                vals = [y_scr[sl[u], :] + ot[pl.ds(r0 + u, SLAB, stride=TILE_PITCH), :]
                        for u in range(SCATTER_GROUP)]
                for u in range(SCATTER_GROUP):
                    y_scr[sl[u], :] = vals[u]
            return carry

        lax.fori_loop(0, n_blocks, block, 0)

    @pl.when(e == MOE_EXPERTS - 1)
    def _():
        @pl.when(b + 1 < nb)
        def _():
            t_copy(b + 1).start()

        x1_copy(0, 0).start()

        def tile(s, carry):
            slot = s % 2
            x1_copy(s, slot).wait()

            @pl.when(s + 1 < n_fin)
            def _():
                x1_copy(s + 1, 1 - slot).start()

            @pl.when(s >= 2)
            def _():
                out_copy(s - 2, slot).wait()

            base = pl.multiple_of(s * (TM_FIN * SLAB), TM_FIN * SLAB)
            x1 = xin[slot]
            chunks = [x1[:, c * LANES:(c + 1) * LANES] + y_scr[pl.ds(base + c, TM_FIN, stride=SLAB), :]
                      for c in range(SLAB)]
            ssq = chunks[0] * chunks[0]
            for c in range(1, SLAB):
                ssq = ssq + chunks[c] * chunks[c]
            inv = lax.rsqrt(jnp.sum(ssq, axis=1, keepdims=True) * (1.0 / D_MODEL) + EPS)
            for c in range(SLAB):
                stage[slot, :, c * LANES:(c + 1) * LANES] = chunks[c] * inv * fw_ref[:, c * LANES:(c + 1) * LANES]
            out_copy(s, slot).start()
            return carry

        lax.fori_loop(0, n_fin, tile, 0)
        out_copy(n_fin - 2, n_fin % 2).wait()
        out_copy(n_fin - 1, (n_fin - 1) % 2).wait()


def _moe(t_slab, x1, route, wg, wu, wd, mod3, final_w):
    b, seq, d = x1.shape
    nblk_e, blk_start, tok, wslot = _route_tables(route, seq)
    nblk = wslot.shape[1]

    def w_spec(shape):
        return pl.BlockSpec((None,) + shape, lambda bi, e, nbe, bs: (e, 0, 0))

    grid_spec = pltpu.PrefetchScalarGridSpec(
        num_scalar_prefetch=2,
        grid=(b, MOE_EXPERTS),
        in_specs=[
            pl.BlockSpec((None, 1, nblk * BM), lambda bi, e, nbe, bs: (bi, 0, 0), memory_space=pltpu.SMEM),
            pl.BlockSpec((None, nblk, BM), lambda bi, e, nbe, bs: (bi, 0, 0)),
            w_spec((d, D_EXPERT)), w_spec((d, D_EXPERT)), w_spec((D_EXPERT, d)),
            pl.BlockSpec((None, 1, d), lambda bi, e, nbe, bs: (bi * 6 + 5, 0, 0)),
            pl.BlockSpec((1, d), lambda bi, e, nbe, bs: (0, 0)),
            pl.BlockSpec(memory_space=pl.ANY),
            pl.BlockSpec(memory_space=pl.ANY),
        ],
        out_specs=pl.BlockSpec(memory_space=pl.ANY),
        scratch_shapes=[
            pltpu.VMEM(((seq + BM) * SLAB, LANES), F32),
            pltpu.VMEM(((seq + BM) * SLAB, LANES), F32),
            pltpu.VMEM((d, D_EXPERT), BF16),
            pltpu.VMEM((d, D_EXPERT), BF16),
            pltpu.VMEM((D_EXPERT, d), BF16),
            pltpu.VMEM((SLAB * TILE_PITCH, LANES), F32),
            pltpu.VMEM((SLAB * TILE_PITCH, LANES), F32),
            pltpu.VMEM((2, TM_FIN, d), F32),
            pltpu.VMEM((2, TM_FIN, d), F32),
            pltpu.SemaphoreType.DMA((5,)),
        ],
    )
    return pl.pallas_call(
        functools.partial(_moe_kernel, seq),
        grid_spec=grid_spec,
        out_shape=jax.ShapeDtypeStruct((b, seq, d), F32),
        compiler_params=pltpu.CompilerParams(
            dimension_semantics=("arbitrary", "arbitrary"), vmem_limit_bytes=VMEM_LIMIT_MOE),
        name="experts_final_norm",
    )(nblk_e, blk_start, tok, wslot, wg, wu, wd, mod3, final_w.reshape(1, d), t_slab, x1)


def _lane_pad(v, offset=0):
    v = v.reshape(-1).astype(F32)
    return jnp.zeros((1, LANES), F32).at[0, offset:offset + v.shape[0]].set(v)


def _to_col_major(t, rows):
    b, length, ch = t.shape
    return t.reshape(b, rows, GRID_W, ch).transpose(0, 2, 1, 3).reshape(b, length, ch)


def _from_col_major(t, rows):
    b, length, ch = t.shape
    return t.reshape(b, GRID_W, rows, ch).transpose(0, 2, 1, 3).reshape(b, length, ch)


def kernel(x, c, ctx, c_ctx, w_mod, b_mod, norm1_w, w_in, ssd_conv_w, ssd_conv_b, ssd_dt_bias, ssd_a_log, ssd_d, ssd_norm_w, ml_conv_w, ml_conv_b, ml_w_qk, ml_gate_b, ml_norm_w, ml_skip, w_out, norm2_w, moe_rg_w, moe_rg_b, moe_re_w, moe_re_b, moe_w_gate, moe_w_up, moe_w_down, final_norm_w):
    b, seq, d = x.shape
    ctx_len = ctx.shape[1]
    rows = seq // GRID_W
    ncc = ctx_len // CHUNK
    assert w_mod.shape[0] == 1 and d == D_MODEL and b + 1 <= 8 and b % NB == 0
    assert seq % TM_IN == 0 and ctx_len % CHUNK == 0 and seq % TM_FIN == 0 and (2 * seq) % BM == 0
    assert seq == rows * GRID_W and GRID_W % W_TILE == 0 and rows % 8 == 0

    c_all = jnp.zeros((8, d), F32).at[:b].set(c).at[b].set(c_ctx)
    mod = _modulation(c_all, w_mod[0], b_mod[0])
    mod3 = mod.reshape(8 * 6, 1, d)

    w = w_in[0]
    ssd_in = SSD_WIDTH + SSD_XBC + 2 * SSD_HEADS
    ml_main = 3 * ML_WIDTH
    n_gate = 2 * SSD_HEADS + 4 * ML_HEADS
    w_cat = jnp.concatenate([
        w[:, :SSD_WIDTH + SSD_XBC], w[:, ssd_in:ssd_in + ml_main],
        w[:, SSD_WIDTH + SSD_XBC:ssd_in], w[:, ssd_in + ml_main:],
        jnp.zeros((d, LANES - n_gate), F32)], axis=1).astype(BF16)
    z, xbc, ml, gates, gates_cm = _in_proj(x, ctx, mod3, norm1_w[0], w_cat)

    e_mats = []
    for direction in range(2):
        lane = jnp.arange(LANES)[:, None]
        head = (jnp.arange(SSD_WIDTH) // SSD_HEADDIM)[None, :]
        e_mats.append((lane == direction * SSD_HEADS + head).astype(BF16))
    cw = jnp.zeros((8, SSD_XBC), F32).at[:CONV_W].set(ssd_conv_w[0])
    dtb = _lane_pad(ssd_dt_bias[0])
    alog = _lane_pad(ssd_a_log[0])
    dsk = jnp.repeat(ssd_d[0], SSD_HEADDIM).reshape(1, SSD_WIDTH)
    shifts = _shift_matrices()
    y_ssd = _ssd_scans(xbc, gates, z, [cw, ssd_conv_b[0].reshape(1, -1), shifts],
                       [dtb, dtb.reshape(LANES, 1), alog, alog.reshape(LANES, 1)], e_mats, dsk,
                       ssd_norm_w[0].reshape(1, -1))

    w_rows = jnp.tile(ml_w_qk[0].reshape(2, ML_WIDTH, ML_QK_BLOCK), (1, 1, ML_WIDTH // ML_QK_BLOCK))
    blk_id = jnp.arange(ML_WIDTH) // ML_QK_BLOCK
    w_bd = jnp.where((blk_id[:, None] == blk_id[None, :])[None], w_rows, 0.0)
    wq = w_bd[0].astype(BF16)
    wk = (w_bd[1] * (ML_HEADDIM ** -0.5)).astype(BF16)
    mcw = jnp.zeros((8, ML_WIDTH), F32).at[:CONV_W].set(ml_conv_w[0])
    gb = _lane_pad(ml_gate_b[0], offset=2 * SSD_HEADS)
    y_ml_cm = _ml_scans(ml, gates_cm, [mcw, ml_conv_b[0].reshape(1, -1), shifts], [wq, wk], [gb, gb.reshape(LANES, 1)],
                        ml_norm_w[0].reshape(1, -1), ml_skip[0].reshape(1, -1))
    y_ml = _from_col_major(y_ml_cm, rows)

    rw = jnp.concatenate([moe_rg_w[0], moe_re_w[0],
                          jnp.zeros((d, LANES - MOE_GROUPS - MOE_EXPERTS), F32)], axis=1)
    rw_hi = rw.astype(BF16)
    rw_lo = (rw - rw_hi.astype(F32)).astype(BF16)
    rb = _lane_pad(jnp.concatenate([moe_rg_b[0], moe_re_b[0]]))
    rw2 = jnp.concatenate([rw_hi, rw_lo], axis=1)
    x1, t_slab, route = _out_proj(x, y_ssd, y_ml, w_out[0].astype(BF16), mod3, norm2_w[0], rw2, rb)

    return _moe(t_slab, x1, route, moe_w_gate[0], moe_w_up[0], moe_w_down[0], mod3, final_norm_w)
```

```python
import functools

import jax
import jax.numpy as jnp
from jax import lax
from jax.experimental import pallas as pl
from jax.experimental.pallas import tpu as pltpu

F32 = jnp.float32
BF16 = jnp.bfloat16
HIGHEST = lax.Precision.HIGHEST

D_MODEL = 1024
GRID_W = 64
EPS = 1e-6
CONV_W = 5
NEG_BIG = -1e30
CHUNK = 128
LANES = 128
HALO = 16
NB = 2

SSD_WIDTH = 512
SSD_HEADS = 8
SSD_HEADDIM = 64
SSD_GROUPS = 2
SSD_STATE = 128
SSD_XBC = SSD_WIDTH + 2 * SSD_GROUPS * SSD_STATE

ML_WIDTH = 512
ML_HEADS = 4
ML_HEADDIM = 128
ML_QK_BLOCK = 4

MOE_GROUPS = 4
MOE_EPG = 8
MOE_EXPERTS = 32
D_EXPERT = 256
ROUTE_LANE0 = MOE_GROUPS

TM_IN = 512
TM_FIN = 256
SLAB = D_MODEL // LANES
BM = 128
TILE_PITCH = BM + 8
SCATTER_GROUP = 4
VMEM_LIMIT = 48 * 1024 * 1024
VMEM_LIMIT_MOE = 56 * 1024 * 1024


def _silu(v):
    return v * jax.nn.sigmoid(v)


def _softplus(v):
    return jnp.maximum(v, 0.0) + jnp.log1p(jnp.exp(-jnp.abs(v)))


def _dot(a, b):
    return jnp.dot(a, b, preferred_element_type=F32)


def _dot_nt(a, b):
    return lax.dot_general(a, b, (((1,), (1,)), ((), ())), preferred_element_type=F32)


def _dot_hi(a, b):
    return jnp.dot(a, b, preferred_element_type=F32, precision=HIGHEST)


def _mod_kernel(c_ref, w_ref, b_ref, o_ref):
    c = c_ref[...]
    o_ref[...] = _dot_hi(_silu(c), w_ref[...]) + b_ref[...]


def _modulation(c_all, w_mod, b_mod):
    n = w_mod.shape[1]
    bn = 1536
    return pl.pallas_call(
        _mod_kernel,
        grid=(n // bn,),
        in_specs=[
            pl.BlockSpec((8, D_MODEL), lambda j: (0, 0)),
            pl.BlockSpec((D_MODEL, bn), lambda j: (0, j)),
            pl.BlockSpec((1, bn), lambda j: (0, j)),
        ],
        out_specs=pl.BlockSpec((8, bn), lambda j: (0, j)),
        out_shape=jax.ShapeDtypeStruct((8, n), F32),
        compiler_params=pltpu.CompilerParams(vmem_limit_bytes=VMEM_LIMIT),
        name="modulation",
    )(c_all, w_mod, b_mod.reshape(1, n))


W_TILE = 16
COL_Z, COL_XBC, COL_ML, COL_G = 0, SSD_WIDTH, SSD_WIDTH + SSD_XBC, SSD_WIDTH + SSD_XBC + 3 * ML_WIDTH
PROJ_CHUNK = 512


def _norm_mod(xin, sh_ref, sc_ref, nw_ref):
    ms = jnp.mean(xin * xin, axis=-1, keepdims=True)
    y = xin * lax.rsqrt(ms + EPS) * nw_ref[...]
    return (y * (1.0 + sc_ref[...]) + sh_ref[...]).astype(BF16)


def _in_lat_kernel(rows, x_ref, sh_ref, sc_ref, nw_ref, w_ref, z_ref, xbc_ref, ml_ref, g_ref, gml_ref, scr):
    tm = rows * W_TILE
    h = _norm_mod(x_ref[...].reshape(tm, D_MODEL), sh_ref, sc_ref, nw_ref)

    def proj(col, width=PROJ_CHUNK):
        return _dot(h, w_ref[:, col:col + width])

    def to_col_major(val, dst_ref, lo, dtype):
        n_slab = val.shape[1] // LANES
        for k in range(n_slab):
            scr[k] = val[:, k * LANES:(k + 1) * LANES]
        for j in range(W_TILE):
            for k in range(n_slab):
                dst_ref[j, :, lo + k * LANES:lo + (k + 1) * LANES] = (
                    scr[k, pl.ds(j, rows, stride=W_TILE), :].astype(dtype))

    z_ref[...] = proj(COL_Z).astype(BF16).reshape(rows, W_TILE, PROJ_CHUNK)
    for j in range(SSD_XBC // PROJ_CHUNK):
        lo = j * PROJ_CHUNK
        xbc_ref[:, :, lo:lo + PROJ_CHUNK] = proj(COL_XBC + lo).astype(BF16).reshape(rows, W_TILE, PROJ_CHUNK)
    for j in range(3 * ML_WIDTH // PROJ_CHUNK):
        to_col_major(proj(COL_ML + j * PROJ_CHUNK), ml_ref, j * PROJ_CHUNK, BF16)
    g = proj(COL_G, LANES)
    g_ref[...] = g.reshape(rows, W_TILE, LANES)
    to_col_major(g, gml_ref, 0, F32)


def _in_ctx_kernel(x_ref, sh_ref, sc_ref, nw_ref, w_ref, xbc_ref, ml_ref, g_ref):
    h = _norm_mod(x_ref[...], sh_ref, sc_ref, nw_ref)
    for j in range(SSD_XBC // PROJ_CHUNK):
        lo = j * PROJ_CHUNK
        xbc_ref[:, lo:lo + PROJ_CHUNK] = _dot(h, w_ref[:, COL_XBC + lo:COL_XBC + lo + PROJ_CHUNK]).astype(BF16)
    for j in range(3 * ML_WIDTH // PROJ_CHUNK):
        lo = j * PROJ_CHUNK
        ml_ref[:, lo:lo + PROJ_CHUNK] = _dot(h, w_ref[:, COL_ML + lo:COL_ML + lo + PROJ_CHUNK]).astype(BF16)
    g_ref[...] = _dot(h, w_ref[:, COL_G:COL_G + LANES])


def _in_proj(x, ctx, mod3, norm_w, w_cat):
    b, seq, d = x.shape
    ctx_len = ctx.shape[1]
    rows = seq // GRID_W
    widths = (SSD_WIDTH, SSD_XBC, 3 * ML_WIDTH, LANES, LANES)
    dtypes = (BF16, BF16, BF16, F32, F32)
    params = pltpu.CompilerParams(dimension_semantics=("arbitrary", "arbitrary"), vmem_limit_bytes=VMEM_LIMIT)
    nw = norm_w.reshape(1, d)

    def raster(width):
        return pl.BlockSpec((None, rows, W_TILE, width), lambda bi, wi: (bi, 0, wi, 0))

    def col_major(width):
        return pl.BlockSpec((None, W_TILE, rows, width), lambda bi, wi: (bi, wi, 0, 0))

    outs = pl.pallas_call(
        functools.partial(_in_lat_kernel, rows),
        grid=(b, GRID_W // W_TILE),
        in_specs=[
            raster(d),
            pl.BlockSpec((None, 1, d), lambda bi, wi: (bi * 6, 0, 0)),
            pl.BlockSpec((None, 1, d), lambda bi, wi: (bi * 6 + 1, 0, 0)),
            _full((1, d)), _full(w_cat.shape),
        ],
        out_specs=[raster(SSD_WIDTH), raster(SSD_XBC), col_major(3 * ML_WIDTH), raster(LANES), col_major(LANES)],
        out_shape=[jax.ShapeDtypeStruct((b, GRID_W, rows, w) if cm else (b, rows, GRID_W, w), t)
                   for w, t, cm in zip(widths, dtypes, (False, False, True, False, True))],
        scratch_shapes=[pltpu.VMEM((PROJ_CHUNK // LANES, rows * W_TILE, LANES), F32)],
        compiler_params=params,
        name="in_proj",
    )(x.reshape(b, rows, GRID_W, d), mod3, mod3, nw, w_cat)
    z, xbc, ml, gates, gates_cm = [o.reshape(b, seq, w) for o, w in zip(outs, widths)]

    ctx_row = b
    ctx_widths = (SSD_XBC, 3 * ML_WIDTH, LANES)
    xbc_c, ml_c, gates_c = pl.pallas_call(
        _in_ctx_kernel,
        grid=(b, 1),
        in_specs=[
            pl.BlockSpec((None, ctx_len, d), lambda bi, t: (bi, 0, 0)),
            pl.BlockSpec((None, 1, d), lambda bi, t: (ctx_row * 6, 0, 0)),
            pl.BlockSpec((None, 1, d), lambda bi, t: (ctx_row * 6 + 1, 0, 0)),
            _full((1, d)), _full(w_cat.shape),
        ],
        out_specs=[pl.BlockSpec((None, ctx_len, w), lambda bi, t: (bi, 0, 0)) for w in ctx_widths],
        out_shape=[jax.ShapeDtypeStruct((b, ctx_len, w), t) for w, t in zip(ctx_widths, (BF16, BF16, F32))],
        compiler_params=params,
        name="in_proj_ctx",
    )(ctx, mod3, mod3, nw, w_cat)
    return z, (xbc, xbc_c), (ml, ml_c), (gates, gates_c), (gates_cm, gates_c)


def _chunk_eff(rev, ncc, nc, c):
    nl = nc - ncc
    if not rev:
        return jnp.where(c < ncc, nl + c, c - ncc)
    return jnp.where(c < ncc, nc - 1 - c, nl - 1 - (c - ncc))


def _seq_ends(ncc, nc, ceff):
    nl = nc - ncc
    return (ceff == 0) | (ceff == nl), (ceff == nl - 1) | (ceff == nc - 1)


def _lat_block(rev, ncc, nc, c):
    nl = nc - ncc
    if not rev:
        return jnp.maximum(c - ncc, 0)
    return jnp.where(c < ncc, nl - 1, nl - 1 - (c - ncc))


CONV_SIDE_TAPS = tuple(t for t in range(CONV_W) if t != CONV_W // 2)


def _shift_matrices():
    row = jnp.arange(CHUNK)[:, None]
    col = jnp.arange(CHUNK + 2 * HALO)[None, :]
    return jnp.stack([col == row + HALO + tap - CONV_W // 2 for tap in CONV_SIDE_TAPS]).astype(BF16)


def _conv_silu(xm, xp, xn, cw_ref, cb_ref, sh_ref, first, last):
    xp = jnp.where(first, jnp.zeros_like(xp), xp)
    xn = jnp.where(last, jnp.zeros_like(xn), xn)
    ext = jnp.concatenate([xp, xm, xn], axis=0)
    mid = CONV_W // 2
    acc = cb_ref[...] + cw_ref[mid:mid + 1, :] * xm.astype(F32)
    for i, tap in enumerate(CONV_SIDE_TAPS):
        acc = acc + cw_ref[tap:tap + 1, :] * _dot(sh_ref[i], ext)
    return _silu(acc)


def _tri(rev, transposed=False):
    row = lax.broadcasted_iota(jnp.int32, (CHUNK, CHUNK), 0)
    col = lax.broadcasted_iota(jnp.int32, (CHUNK, CHUNK), 1)
    if transposed:
        row, col = col, row
    keep = (col >= row) if rev else (col <= row)
    return keep


def _split3(a):
    a1 = a.astype(BF16)
    r1 = a - a1.astype(F32)
    a2 = r1.astype(BF16)
    a3 = (r1 - a2.astype(F32)).astype(BF16)
    return a1, a2, a3


def _dot_sel_l(m_b, a):
    p = _split3(a)
    return _dot(m_b, p[0]) + _dot(m_b, p[1]) + _dot(m_b, p[2])


def _dot_sel_r(a, m_b):
    p = _split3(a)
    return _dot(p[0], m_b) + _dot(p[1], m_b) + _dot(p[2], m_b)


def _row_spec(rows, width, row_block, col_block=0):
    return pl.BlockSpec((NB, rows, width), lambda bi, c: (bi, row_block(c), col_block))


def _local_chunks(rev, ncc, nc, c):
    nl = nc - ncc
    cc = jnp.clip(ncc - 1 - c if rev else c, 0, ncc - 1)
    cl = jnp.clip(nl - 1 - (c - ncc) if rev else c - ncc, 0, nl - 1)
    return c < ncc, cc, cl


def _pair_specs(rev, ncc, nc, rows, width, col=0, halo=0):
    per = CHUNK // HALO

    def index(which, count):
        def fn(c):
            ch = _local_chunks(rev, ncc, nc, c)[which]
            if halo == 0:
                return ch
            if halo < 0:
                return jnp.maximum(ch * per - 1, 0)
            return jnp.minimum((ch + 1) * per, count * per - 1)
        return fn

    return [_row_spec(rows, width, index(2, nc - ncc), col), _row_spec(rows, width, index(1, ncc), col)]


def _pick(is_ctx, lat_ref, ctx_ref, s):
    return jnp.where(is_ctx, ctx_ref[s], lat_ref[s])


def _step_ends(rev, ncc, nc, c):
    is_ctx, cc, cl = _local_chunks(rev, ncc, nc, c)
    first = jnp.where(is_ctx, cc == 0, cl == 0)
    last = jnp.where(is_ctx, cc == ncc - 1, cl == nc - ncc - 1)
    return is_ctx, first, last


def _full(shape):
    return pl.BlockSpec(shape, lambda bi, c: (0,) * len(shape))


def _ssd_chunk(rev, xbc, g, dtbr_ref, dtbc_ref, alr_ref, alc_ref, e_ref, s_ref, s):
    xs = xbc[:, :SSD_WIDTH]
    lane0 = SSD_HEADS * int(rev)
    lane = lax.broadcasted_iota(jnp.int32, (CHUNK, LANES), 1)
    lmask = (lane >= lane0) & (lane < lane0 + SSD_HEADS)
    dt = jnp.where(lmask, _softplus(g + dtbr_ref[...]), 0.0)
    a = dt * (-jnp.exp(alr_ref[...]))
    gt = g.T
    dt_t = _softplus(gt + dtbc_ref[...])[lane0:lane0 + SSD_HEADS]
    a_t = dt_t * (-jnp.exp(alc_ref[...][lane0:lane0 + SSD_HEADS]))

    cs = _dot_sel_l(_tri(rev).astype(BF16), a)
    cs_t = _dot_sel_r(a_t, _tri(rev, transposed=True).astype(BF16))
    e = e_ref[...]
    dtx = _dot_sel_r(dt, e)
    csx = _dot_sel_r(cs, e)
    end = 0 if rev else CHUNK - 1
    totx = csx[end:end + 1, :]
    ecsx = jnp.exp(csx)
    decx = jnp.exp(totx - csx)
    etotx = jnp.exp(totx)

    xdt = xs * dtx
    xdt_b = xdt.astype(BF16)
    xd_b = (xdt * decx).astype(BF16)
    keep = _tri(rev)
    half = lax.broadcasted_iota(jnp.int32, (CHUNK, LANES), 1) // SSD_HEADDIM

    y_blocks = []
    gs = SSD_GROUPS * SSD_STATE
    hpg = SSD_HEADS // SSD_GROUPS
    gw = hpg * SSD_HEADDIM
    for grp in range(SSD_GROUPS):
        bm = xbc[:, SSD_WIDTH + grp * SSD_STATE:SSD_WIDTH + (grp + 1) * SSD_STATE]
        cm = xbc[:, SSD_WIDTH + gs + grp * SSD_STATE:SSD_WIDTH + gs + (grp + 1) * SSD_STATE]
        bm_b, cm_b = bm.astype(BF16), cm.astype(BF16)
        cb = _dot_nt(cm_b, bm_b)
        s_old = s_ref[s, :, grp * gw:(grp + 1) * gw]
        y_off = _dot(cm_b, s_old.astype(BF16)) * ecsx[:, grp * gw:(grp + 1) * gw]
        for pair in range(hpg // 2):
            blk = grp * (hpg // 2) + pair
            xj = xdt_b[:, blk * LANES:(blk + 1) * LANES]
            acc = y_off[:, pair * LANES:(pair + 1) * LANES]
            for q in range(2):
                h = blk * 2 + q
                dl = cs[:, lane0 + h:lane0 + h + 1] - cs_t[h:h + 1, :]
                m = (cb * jnp.exp(jnp.where(keep, dl, NEG_BIG))).astype(BF16)
                acc = acc + _dot(m, jnp.where(half == q, xj, jnp.zeros_like(xj)))
            y_blocks.append(acc)
        s_new = s_old * etotx[:, grp * gw:(grp + 1) * gw] + _dot(bm.T.astype(BF16), xd_b[:, grp * gw:(grp + 1) * gw])
        s_ref[s, :, grp * gw:(grp + 1) * gw] = s_new

    return jnp.concatenate(y_blocks, axis=1), xs


def _ssd_rev_kernel(ncc, nc, xml_ref, xmc_ref, xpl_ref, xpc_ref, xnl_ref, xnc_ref, gl_ref, gc_ref, cw_ref, cb_ref,
                    sh_ref, dtbr_ref, dtbc_ref, alr_ref, alc_ref, e_ref, o_ref, xc_ref, s_ref):
    c = pl.program_id(1)
    is_ctx, first, last = _step_ends(True, ncc, nc, c)

    @pl.when(c == 0)
    def _():
        s_ref[...] = jnp.zeros_like(s_ref)

    for s in range(NB):
        xbc = _conv_silu(_pick(is_ctx, xml_ref, xmc_ref, s), _pick(is_ctx, xpl_ref, xpc_ref, s),
                         _pick(is_ctx, xnl_ref, xnc_ref, s), cw_ref, cb_ref, sh_ref, first, last)
        xc_ref[s] = xbc.astype(BF16)
        y, _ = _ssd_chunk(True, xbc, _pick(is_ctx, gl_ref, gc_ref, s), dtbr_ref, dtbc_ref, alr_ref, alc_ref, e_ref,
                          s_ref, s)
        o_ref[s] = y.astype(o_ref.dtype)


def _ssd_fwd_kernel(ncc, xc_ref, gl_ref, gc_ref, dtbr_ref, dtbc_ref, alr_ref, alc_ref, e_ref, dsk_ref, yb_ref,
                    z_ref, nw_ref, o_ref, s_ref):
    is_ctx = pl.program_id(1) < ncc

    @pl.when(pl.program_id(1) == 0)
    def _():
        s_ref[...] = jnp.zeros_like(s_ref)

    for s in range(NB):
        y, xs = _ssd_chunk(False, xc_ref[s].astype(F32), _pick(is_ctx, gl_ref, gc_ref, s), dtbr_ref, dtbc_ref,
                           alr_ref, alc_ref, e_ref, s_ref, s)
        y = y + yb_ref[s].astype(F32) + dsk_ref[...] * xs
        y = y * _silu(z_ref[s].astype(F32))
        y = y * lax.rsqrt(jnp.mean(y * y, axis=-1, keepdims=True) + EPS) * nw_ref[...]
        o_ref[s] = y.astype(o_ref.dtype)


def _ssd_scans(xbc, gates, z, conv_params, gate_params, e_mats, dsk, nw):
    b, seq, _ = xbc[0].shape
    ncc = xbc[1].shape[1] // CHUNK
    nc = seq // CHUNK + ncc
    state = pltpu.VMEM((NB, SSD_STATE, SSD_WIDTH), F32)
    params = pltpu.CompilerParams(dimension_semantics=("arbitrary", "arbitrary"), vmem_limit_bytes=VMEM_LIMIT)
    lat_shape = jax.ShapeDtypeStruct((b, seq, SSD_WIDTH), BF16)

    ceff = functools.partial(_chunk_eff, True, ncc, nc)
    lat = functools.partial(_lat_block, True, ncc, nc)
    pair = functools.partial(_pair_specs, True, ncc, nc)
    rev_params = list(conv_params) + list(gate_params) + [e_mats[1]]
    yb, xbc_act = pl.pallas_call(
        functools.partial(_ssd_rev_kernel, ncc, nc),
        grid=(b // NB, nc),
        in_specs=pair(CHUNK, SSD_XBC) + pair(HALO, SSD_XBC, halo=-1) + pair(HALO, SSD_XBC, halo=1)
        + pair(CHUNK, LANES) + [_full(p.shape) for p in rev_params],
        out_specs=[_row_spec(CHUNK, SSD_WIDTH, lat), _row_spec(CHUNK, SSD_XBC, ceff)],
        out_shape=(lat_shape, jax.ShapeDtypeStruct((b, nc * CHUNK, SSD_XBC), BF16)),
        scratch_shapes=[state],
        compiler_params=params,
        name="ssd_rev",
    )(*xbc, *xbc, *xbc, *gates, *rev_params)

    ceff = functools.partial(_chunk_eff, False, ncc, nc)
    lat = functools.partial(_lat_block, False, ncc, nc)
    fwd_params = list(gate_params) + [e_mats[0], dsk]
    return pl.pallas_call(
        functools.partial(_ssd_fwd_kernel, ncc),
        grid=(b // NB, nc),
        in_specs=[_row_spec(CHUNK, SSD_XBC, ceff)] + _pair_specs(False, ncc, nc, CHUNK, LANES)
        + [_full(p.shape) for p in fwd_params]
        + [_row_spec(CHUNK, SSD_WIDTH, lat), _row_spec(CHUNK, SSD_WIDTH, lat), _full(nw.shape)],
        out_specs=_row_spec(CHUNK, SSD_WIDTH, lat),
        out_shape=lat_shape,
        scratch_shapes=[state],
        compiler_params=params,
        name="ssd_fwd",
    )(xbc_act, *gates, *fwd_params, yb, z, nw)


ML_I_LANE0 = 2 * SSD_HEADS
ML_F_LANE0 = ML_I_LANE0 + 2 * ML_HEADS


def _ml_gates(rev, g, gbr_ref, gbc_ref):
    i_lane0 = ML_I_LANE0 + ML_HEADS * int(rev)
    f_lane0 = ML_F_LANE0 + ML_HEADS * int(rev)
    ga = g + gbr_ref[...]
    lane = lax.broadcasted_iota(jnp.int32, (CHUNK, LANES), 1)
    logf = jnp.where((lane >= f_lane0) & (lane < f_lane0 + ML_HEADS), -_softplus(-ga), 0.0)
    cs = _dot_sel_l(_tri(rev).astype(BF16), logf)
    gt = g.T + gbc_ref[...]
    i_t = gt[ML_I_LANE0:ML_F_LANE0]
    logf_t = -_softplus(-gt[ML_F_LANE0:ML_F_LANE0 + 2 * ML_HEADS])
    cs_t = _dot_sel_r(logf_t, _tri(rev, transposed=True).astype(BF16))
    end = 0 if rev else CHUNK - 1

    u_t = i_t - cs_t
    u_c = ga - pltpu.roll(cs, LANES - (ML_F_LANE0 - ML_I_LANE0), axis=1)
    row = lax.broadcasted_iota(jnp.int32, (CHUNK, LANES), 0)
    pm = u_c
    step = 1
    while step < CHUNK:
        if rev:
            pm = jnp.maximum(pm, jnp.where(row < CHUNK - step, pltpu.roll(pm, CHUNK - step, axis=0), NEG_BIG))
        else:
            pm = jnp.maximum(pm, jnp.where(row >= step, pltpu.roll(pm, step, axis=0), NEG_BIG))
        step *= 2

    heads = []
    for h in range(ML_HEADS):
        r = ML_HEADS * int(rev) + h
        li = i_lane0 + h
        heads.append(dict(
            csc=cs[:, f_lane0 + h:f_lane0 + h + 1], u_col=u_c[:, li:li + 1], u_row=u_t[r:r + 1, :],
            tot=cs_t[r:r + 1, end:end + 1], u_max=pm[end:end + 1, li:li + 1], pm_col=pm[:, li:li + 1]))
    return heads


def _ml_chunks(rev, samples, gbr_ref, gbc_ref, c_ref, n_ref, mx_ref):
    keep = _tri(rev)
    ones_b = jnp.ones((CHUNK, ML_HEADDIM), BF16)
    units = []
    for s, (k, q_b, k_b, v_b, g) in enumerate(samples):
        for h, gq in enumerate(_ml_gates(rev, g, gbr_ref, gbc_ref)):
            sl = slice(h * ML_HEADDIM, (h + 1) * ML_HEADDIM)
            units.append(dict(gq, s=s, h=h, kh=k[:, sl], vh=v_b[:, sl], qh_b=q_b[:, sl], kh_b=k_b[:, sl],
                              m_prev=mx_ref[s, h:h + 1, 0:1], c_prev=c_ref[s * ML_HEADS + h],
                              n_prev=n_ref[s, h:h + 1, :]))

    for u in units:
        u["qk"] = _dot_nt(u["qh_b"], u["kh_b"])
        u["vt_b"] = u["vh"].astype(F32).T.astype(BF16)
    for u in units:
        u["mm"] = jnp.maximum(u["m_prev"], u["pm_col"])
        u["scores_b"] = (u["qk"] * jnp.exp(jnp.where(keep, u["u_row"] - u["mm"], NEG_BIG))).astype(BF16)
    for u in units:
        c_aug = jnp.concatenate([u["c_prev"], jnp.broadcast_to(u["n_prev"], (CHUNK, ML_HEADDIM))], axis=0)
        u["intra"] = _dot(u["scores_b"], jnp.concatenate([u["vh"], ones_b], axis=1))
        u["inter"] = _dot_nt(u["qh_b"], c_aug.astype(BF16))
    outs = [[None] * ML_HEADS for _ in samples]
    for u in units:
        both = u["intra"] + jnp.exp(u["m_prev"] - u["mm"]) * u["inter"]
        den = both[:, ML_HEADDIM:ML_HEADDIM + 1]
        outs[u["s"]][u["h"]] = both[:, :ML_HEADDIM] / jnp.maximum(jnp.abs(den), jnp.exp(-(u["csc"] + u["mm"])))
    for u in units:
        kw = u["kh"] * jnp.exp(u["u_col"] - u["u_max"])
        u["c_loc"] = _dot(u["vt_b"], kw.astype(BF16))
        u["n_loc"] = jnp.sum(kw, axis=0, keepdims=True)
    for u in units:
        s, h, tot, m_prev = u["s"], u["h"], u["tot"], u["m_prev"]
        m_loc = tot + u["u_max"]
        m_new = jnp.maximum(tot + m_prev, m_loc)
        s_prev = jnp.exp(tot + m_prev - m_new)
        s_loc = jnp.exp(m_loc - m_new)
        c_ref[s * ML_HEADS + h] = s_prev * u["c_prev"] + s_loc * u["c_loc"]
        n_ref[s, h:h + 1, :] = s_prev * u["n_prev"] + s_loc * u["n_loc"]
        mx_ref[s, h:h + 1, :] = jnp.broadcast_to(m_new, (1, LANES))
    return outs


def _ml_init_state(c_ref, n_ref, mx_ref):
    c_ref[...] = jnp.zeros_like(c_ref)
    n_ref[...] = jnp.zeros_like(n_ref)
    mx_ref[...] = jnp.full(mx_ref.shape, NEG_BIG, F32)


def _ml_rev_kernel(ncc, nc, xml_ref, xmc_ref, xpl_ref, xpc_ref, xnl_ref, xnc_ref, vl_ref, vc_ref, gl_ref, gc_ref,
                   cw_ref, cb_ref, sh_ref, wq_ref, wk_ref, gbr_ref, gbc_ref, o_ref, xc_ref, q_ref, k_ref, c_ref, n_ref,
                   mx_ref):
    c = pl.program_id(1)
    is_ctx, first, last = _step_ends(True, ncc, nc, c)

    @pl.when(c == 0)
    def _():
        _ml_init_state(c_ref, n_ref, mx_ref)

    samples = []
    for s in range(NB):
        xconv = _conv_silu(_pick(is_ctx, xml_ref, xmc_ref, s), _pick(is_ctx, xpl_ref, xpc_ref, s),
                           _pick(is_ctx, xnl_ref, xnc_ref, s), cw_ref, cb_ref, sh_ref, first, last)
        xc_b = xconv.astype(BF16)
        q = _dot(xc_b, wq_ref[...])
        k = _dot(xc_b, wk_ref[...])
        q_b, k_b = q.astype(BF16), k.astype(BF16)
        xc_ref[s] = xc_b
        q_ref[s] = q_b
        k_ref[s] = k_b
        samples.append((k, q_b, k_b, _pick(is_ctx, vl_ref, vc_ref, s), _pick(is_ctx, gl_ref, gc_ref, s)))
    outs = _ml_chunks(True, samples, gbr_ref, gbc_ref, c_ref, n_ref, mx_ref)
    for s in range(NB):
        o_ref[s] = jnp.concatenate(outs[s], axis=1).astype(o_ref.dtype)


def _ml_fwd_kernel(ncc, xc_ref, q_ref, k_ref, vl_ref, vc_ref, og_ref, gl_ref, gc_ref, gbr_ref, gbc_ref, hb_ref,
                   nw_ref, sk_ref, o_ref, c_ref, n_ref, mx_ref):
    is_ctx = pl.program_id(1) < ncc

    @pl.when(pl.program_id(1) == 0)
    def _():
        _ml_init_state(c_ref, n_ref, mx_ref)

    samples = []
    for s in range(NB):
        q_b, k_b = q_ref[s], k_ref[s]
        samples.append((k_b.astype(F32), q_b, k_b, _pick(is_ctx, vl_ref, vc_ref, s),
                        _pick(is_ctx, gl_ref, gc_ref, s)))
    outs = _ml_chunks(False, samples, gbr_ref, gbc_ref, c_ref, n_ref, mx_ref)
    gated = []
    for s in range(NB):
        for h in range(ML_HEADS):
            sl = slice(h * ML_HEADDIM, (h + 1) * ML_HEADDIM)
            gated.append(jax.nn.sigmoid(og_ref[s, :, sl].astype(F32)) * (outs[s][h] + hb_ref[s, :, sl].astype(F32)))
    scale = [lax.rsqrt(jnp.mean(hh * hh, axis=-1, keepdims=True) + EPS) for hh in gated]
    normed = [hh * sc for hh, sc in zip(gated, scale)]
    for s in range(NB):
        y = jnp.concatenate(normed[s * ML_HEADS:(s + 1) * ML_HEADS], axis=1) * nw_ref[...]
        o_ref[s] = (y + sk_ref[...] * xc_ref[s].astype(F32)).astype(o_ref.dtype)


def _ml_scans(ml, gates, conv_params, proj_params, gate_params, nw, sk):
    b, seq, _ = ml[0].shape
    ncc = ml[1].shape[1] // CHUNK
    nc = seq // CHUNK + ncc
    scratch = [pltpu.VMEM((NB * ML_HEADS, ML_HEADDIM, ML_HEADDIM), F32),
               pltpu.VMEM((NB, 8, ML_HEADDIM), F32),
               pltpu.VMEM((NB, 8, LANES), F32)]
    params = pltpu.CompilerParams(dimension_semantics=("arbitrary", "arbitrary"), vmem_limit_bytes=VMEM_LIMIT)
    lat_shape = jax.ShapeDtypeStruct((b, seq, ML_WIDTH), BF16)
    act_shape = jax.ShapeDtypeStruct((b, nc * CHUNK, ML_WIDTH), BF16)

    ceff = functools.partial(_chunk_eff, True, ncc, nc)
    lat = functools.partial(_lat_block, True, ncc, nc)
    pair = functools.partial(_pair_specs, True, ncc, nc)
    rev_params = list(conv_params) + list(proj_params) + list(gate_params)
    act_spec = _row_spec(CHUNK, ML_WIDTH, ceff)
    hb, xc, q, k = pl.pallas_call(
        functools.partial(_ml_rev_kernel, ncc, nc),
        grid=(b // NB, nc),
        in_specs=pair(CHUNK, ML_WIDTH) + pair(HALO, ML_WIDTH, halo=-1) + pair(HALO, ML_WIDTH, halo=1)
        + pair(CHUNK, ML_WIDTH, col=1) + pair(CHUNK, LANES) + [_full(p.shape) for p in rev_params],
        out_specs=[_row_spec(CHUNK, ML_WIDTH, lat), act_spec, act_spec, act_spec],
        out_shape=(lat_shape, act_shape, act_shape, act_shape),
        scratch_shapes=scratch,
        compiler_params=params,
        name="mlstm_rev",
    )(*ml, *ml, *ml, *ml, *gates, *rev_params)

    ceff = functools.partial(_chunk_eff, False, ncc, nc)
    lat = functools.partial(_lat_block, False, ncc, nc)
    pair = functools.partial(_pair_specs, False, ncc, nc)
    act_spec = _row_spec(CHUNK, ML_WIDTH, ceff)
    return pl.pallas_call(
        functools.partial(_ml_fwd_kernel, ncc),
        grid=(b // NB, nc),
        in_specs=[act_spec, act_spec, act_spec] + pair(CHUNK, ML_WIDTH, col=1)
        + [_row_spec(CHUNK, ML_WIDTH, lat, 2)] + pair(CHUNK, LANES)
        + [_full(p.shape) for p in gate_params]
        + [_row_spec(CHUNK, ML_WIDTH, lat), _full(nw.shape), _full(sk.shape)],
        out_specs=_row_spec(CHUNK, ML_WIDTH, lat),
        out_shape=lat_shape,
        scratch_shapes=scratch,
        compiler_params=params,
        name="mlstm_fwd",
    )(xc, q, k, *ml, ml[0], *gates, *gate_params, hb, nw, sk)


def _out_kernel(x_ref, ys_ref, ym_ref, wo_ref, g1_ref, sh_ref, sc_ref, nw_ref, rw_ref, rb_ref,
                x1_ref, ts_ref, route_ref):
    mix = _dot(ys_ref[...], wo_ref[0:SSD_WIDTH, :]) + _dot(ym_ref[...], wo_ref[SSD_WIDTH:, :])
    x1 = x_ref[...] + g1_ref[...] * mix
    y = x1 * lax.rsqrt(jnp.mean(x1 * x1, axis=-1, keepdims=True) + EPS) * nw_ref[...]
    t = y * (1.0 + sc_ref[...]) + sh_ref[...]
    x1_ref[...] = x1
    for j in range(SLAB):
        ts_ref[pl.ds(j, TM_IN, stride=SLAB), :] = t[:, j * LANES:(j + 1) * LANES]
    lg2 = _dot(t.astype(BF16), rw_ref[...])
    lg = lg2[:, :LANES] + lg2[:, LANES:] + rb_ref[...]

    lane = lax.broadcasted_iota(jnp.int32, lg.shape, 1).astype(F32)
    gmask = lane < MOE_GROUPS
    gl = jnp.where(gmask, lg, NEG_BIG)
    gmax = jnp.max(gl, axis=1, keepdims=True)
    g_sel = jnp.min(jnp.where(gmask & (gl == gmax), lane, 1e9), axis=1, keepdims=True)
    p_group = 1.0 / jnp.sum(jnp.where(gmask, jnp.exp(gl - gmax), 0.0), axis=1, keepdims=True)
    lo = ROUTE_LANE0 + MOE_EPG * g_sel
    emask = (lane >= lo) & (lane < lo + MOE_EPG)
    l1 = jnp.max(jnp.where(emask, lg, NEG_BIG), axis=1, keepdims=True)
    i1 = jnp.min(jnp.where(emask & (lg == l1), lane, 1e9), axis=1, keepdims=True)
    emask2 = emask & (lane != i1)
    l2 = jnp.max(jnp.where(emask2, lg, NEG_BIG), axis=1, keepdims=True)
    i2 = jnp.min(jnp.where(emask2 & (lg == l2), lane, 1e9), axis=1, keepdims=True)
    r = jnp.exp(l2 - l1)
    w1 = p_group / (1.0 + r)
    w2 = p_group * r / (1.0 + r)
    route = (jnp.where(lane == 0, i1, 0.0) + jnp.where(lane == 1, i2, 0.0)
             + jnp.where(lane == 2, w1, 0.0) + jnp.where(lane == 3, w2, 0.0))
    route_ref[...] = route.T[0:8, :]


def _out_proj(x, y_ssd, y_ml, w_out_b, mod3, norm_w, rw2, rb):
    b, seq, d = x.shape
    nt = seq // TM_IN

    def row(j):
        return pl.BlockSpec((None, 1, d), lambda bi, t: (bi * 6 + j, 0, 0))

    def tile(width):
        return pl.BlockSpec((None, TM_IN, width), lambda bi, t: (bi, t, 0))

    return pl.pallas_call(
        _out_kernel,
        grid=(b, nt),
        in_specs=[tile(d), tile(SSD_WIDTH), tile(ML_WIDTH), _full(w_out_b.shape),
                  row(2), row(3), row(4), _full((1, d)), _full(rw2.shape), _full(rb.shape)],
        out_specs=[tile(d),
                   pl.BlockSpec((None, TM_IN * SLAB, LANES), lambda bi, t: (bi, t, 0)),
                   pl.BlockSpec((None, 8, TM_IN), lambda bi, t: (bi, 0, t))],
        out_shape=(jax.ShapeDtypeStruct((b, seq, d), F32),
                   jax.ShapeDtypeStruct((b, seq * SLAB, LANES), F32),
                   jax.ShapeDtypeStruct((b, 8, seq), F32)),
        compiler_params=pltpu.CompilerParams(
            dimension_semantics=("arbitrary", "arbitrary"), vmem_limit_bytes=VMEM_LIMIT),
        name="out_proj_router",
    )(x, y_ssd, y_ml, w_out_b, mod3, mod3, mod3, norm_w.reshape(1, d), rw2, rb)


def _route_tables(route, seq):
    b = route.shape[0]
    n_inst = 2 * seq
    nblk = n_inst // BM + MOE_EXPERTS
    e_flat = (route[:, 0:2, :].astype(jnp.int32) - ROUTE_LANE0).reshape(b, n_inst)
    w_flat = route[:, 2:4, :].reshape(b, n_inst)
    tok_flat = jnp.tile(jnp.arange(seq, dtype=jnp.int32), 2 * b).reshape(b, n_inst)
    _, tok_sorted, w_sorted = lax.sort((e_flat, tok_flat, w_flat), dimension=1, num_keys=1)
    experts = jnp.arange(MOE_EXPERTS, dtype=jnp.int32)
    counts = jnp.sum((e_flat[:, None, :] == experts[None, :, None]).astype(jnp.int32), axis=2)
    nblk_e = (counts + BM - 1) // BM
    blk_end = jnp.cumsum(nblk_e, axis=1)
    blk_start = blk_end - nblk_e
    cnt_start = jnp.cumsum(counts, axis=1) - counts
    nb = blk_end[:, -1:]
    j = jnp.arange(nblk, dtype=jnp.int32)[None, :]
    valid_blk = j < nb
    jj = jnp.minimum(j, nb - 1)
    e_j = jnp.sum((jj[:, :, None] >= blk_end[:, None, :]).astype(jnp.int32), axis=2)
    onehot = (e_j[:, :, None] == experts[None, None, :]).astype(jnp.int32)
    take = lambda tbl: jnp.sum(onehot * tbl[:, None, :], axis=2)
    r = jnp.arange(BM, dtype=jnp.int32)[None, None, :]
    rank = ((jj - take(blk_start)) * BM)[:, :, None] + r
    valid = valid_blk[:, :, None] & (rank < take(counts)[:, :, None])
    sidx = jnp.clip(take(cnt_start)[:, :, None] + rank, 0, n_inst - 1).reshape(b, nblk * BM)
    tok = jnp.take_along_axis(tok_sorted, sidx, axis=1).reshape(b, nblk, BM)
    wslot = jnp.take_along_axis(w_sorted, sidx, axis=1).reshape(b, nblk, BM)
    tok = jnp.where(valid, tok, seq + r) * SLAB
    wslot = jnp.where(valid, wslot, 0.0)
    return (nblk_e.reshape(-1), blk_start.reshape(-1), tok.reshape(b, 1, nblk * BM), wslot)


SEM_T, SEM_X, SEM_O = 0, 1, 3


def _moe_kernel(seq, nbe_ref, bs_ref, tok_ref, ws_ref, wg_ref, wu_ref, wd_ref, g2_ref, fw_ref,
                t_hbm, x1_hbm, o_hbm, t_scr, y_scr, wgb, wub, wdb, xt, ot, xin, stage, sems):
    b = pl.program_id(0)
    e = pl.program_id(1)
    nb = pl.num_programs(0)
    rows = seq * SLAB
    n_fin = seq // TM_FIN

    def t_copy(sample):
        return pltpu.make_async_copy(t_hbm.at[sample], t_scr.at[pl.ds(0, rows)], sems.at[SEM_T])

    def x1_copy(s, slot):
        return pltpu.make_async_copy(x1_hbm.at[b, pl.ds(s * TM_FIN, TM_FIN)], xin.at[slot], sems.at[SEM_X + slot])

    def out_copy(s, slot):
        return pltpu.make_async_copy(stage.at[slot], o_hbm.at[b, pl.ds(s * TM_FIN, TM_FIN)], sems.at[SEM_O + slot])

    @pl.when((e == 0) & (b == 0))
    def _():
        t_copy(b).start()

    @pl.when(e == 0)
    def _():
        y_scr[...] = jnp.zeros_like(y_scr)
        t_scr[pl.ds(rows, BM * SLAB), :] = jnp.zeros((BM * SLAB, LANES), F32)
        t_copy(b).wait()

    n_blocks = nbe_ref[b * MOE_EXPERTS + e]
    blk0 = bs_ref[b * MOE_EXPERTS + e]

    @pl.when(n_blocks > 0)
    def _():
        wgb[...] = wg_ref[...].astype(BF16)
        wub[...] = wu_ref[...].astype(BF16)
        wdb[...] = wd_ref[...].astype(BF16)
        g2 = g2_ref[...]
        diag = (lax.broadcasted_iota(jnp.int32, (BM, BM), 0) == lax.broadcasted_iota(jnp.int32, (BM, BM), 1))

        def block(i, carry):
            blk = blk0 + i
            base = blk * BM

            def slab_rows(r):
                return pl.ds(pl.multiple_of(tok_ref[0, base + r], SLAB), SLAB)

            for r in range(BM):
                xt[pl.ds(r, SLAB, stride=TILE_PITCH), :] = t_scr[slab_rows(r), :]
            x = jnp.concatenate([xt[c * TILE_PITCH:c * TILE_PITCH + BM, :] for c in range(SLAB)],
                                axis=1).astype(BF16)
            w_col = jnp.sum(jnp.where(diag, ws_ref[pl.ds(blk, 1), :], 0.0), axis=1, keepdims=True)
            hidden = _silu(_dot(x, wgb[...])) * _dot(x, wub[...]) * w_col
            out = _dot(hidden.astype(BF16), wdb[...]) * g2
            for c in range(SLAB):
                ot[c * TILE_PITCH:c * TILE_PITCH + BM, :] = out[:, c * LANES:(c + 1) * LANES]
            for r0 in range(0, BM, SCATTER_GROUP):
                sl = [slab_rows(r0 + u) for u in range(SCATTER_GROUP)]
                vals = [y_scr[sl[u], :] + ot[pl.ds(r0 + u, SLAB, stride=TILE_PITCH), :]
                        for u in range(SCATTER_GROUP)]
                for u in range(SCATTER_GROUP):
                    y_scr[sl[u], :] = vals[u]
            return carry

        lax.fori_loop(0, n_blocks, block, 0)

    @pl.when(e == MOE_EXPERTS - 1)
    def _():
        @pl.when(b + 1 < nb)
        def _():
            t_copy(b + 1).start()

        x1_copy(0, 0).start()

        def tile(s, carry):
            slot = s % 2
            x1_copy(s, slot).wait()

            @pl.when(s + 1 < n_fin)
            def _():
                x1_copy(s + 1, 1 - slot).start()

            @pl.when(s >= 2)
            def _():
                out_copy(s - 2, slot).wait()

            base = pl.multiple_of(s * (TM_FIN * SLAB), TM_FIN * SLAB)
            x1 = xin[slot]
            chunks = [x1[:, c * LANES:(c + 1) * LANES] + y_scr[pl.ds(base + c, TM_FIN, stride=SLAB), :]
                      for c in range(SLAB)]
            ssq = chunks[0] * chunks[0]
            for c in range(1, SLAB):
                ssq = ssq + chunks[c] * chunks[c]
            inv = lax.rsqrt(jnp.sum(ssq, axis=1, keepdims=True) * (1.0 / D_MODEL) + EPS)
            for c in range(SLAB):
                stage[slot, :, c * LANES:(c + 1) * LANES] = chunks[c] * inv * fw_ref[:, c * LANES:(c + 1) * LANES]
            out_copy(s, slot).start()
            return carry

        lax.fori_loop(0, n_fin, tile, 0)
        out_copy(n_fin - 2, n_fin % 2).wait()
        out_copy(n_fin - 1, (n_fin - 1) % 2).wait()


def _moe(t_slab, x1, route, wg, wu, wd, mod3, final_w):
    b, seq, d = x1.shape
    nblk_e, blk_start, tok, wslot = _route_tables(route, seq)
    nblk = wslot.shape[1]

    def w_spec(shape):
        return pl.BlockSpec((None,) + shape, lambda bi, e, nbe, bs: (e, 0, 0))

    grid_spec = pltpu.PrefetchScalarGridSpec(
        num_scalar_prefetch=2,
        grid=(b, MOE_EXPERTS),
        in_specs=[
            pl.BlockSpec((None, 1, nblk * BM), lambda bi, e, nbe, bs: (bi, 0, 0), memory_space=pltpu.SMEM),
            pl.BlockSpec((None, nblk, BM), lambda bi, e, nbe, bs: (bi, 0, 0)),
            w_spec((d, D_EXPERT)), w_spec((d, D_EXPERT)), w_spec((D_EXPERT, d)),
            pl.BlockSpec((None, 1, d), lambda bi, e, nbe, bs: (bi * 6 + 5, 0, 0)),
            pl.BlockSpec((1, d), lambda bi, e, nbe, bs: (0, 0)),
            pl.BlockSpec(memory_space=pl.ANY),
            pl.BlockSpec(memory_space=pl.ANY),
        ],
        out_specs=pl.BlockSpec(memory_space=pl.ANY),
        scratch_shapes=[
            pltpu.VMEM(((seq + BM) * SLAB, LANES), F32),
            pltpu.VMEM(((seq + BM) * SLAB, LANES), F32),
            pltpu.VMEM((d, D_EXPERT), BF16),
            pltpu.VMEM((d, D_EXPERT), BF16),
            pltpu.VMEM((D_EXPERT, d), BF16),
            pltpu.VMEM((SLAB * TILE_PITCH, LANES), F32),
            pltpu.VMEM((SLAB * TILE_PITCH, LANES), F32),
            pltpu.VMEM((2, TM_FIN, d), F32),
            pltpu.VMEM((2, TM_FIN, d), F32),
            pltpu.SemaphoreType.DMA((5,)),
        ],
    )
    return pl.pallas_call(
        functools.partial(_moe_kernel, seq),
        grid_spec=grid_spec,
        out_shape=jax.ShapeDtypeStruct((b, seq, d), F32),
        compiler_params=pltpu.CompilerParams(
            dimension_semantics=("arbitrary", "arbitrary"), vmem_limit_bytes=VMEM_LIMIT_MOE),
        name="experts_final_norm",
    )(nblk_e, blk_start, tok, wslot, wg, wu, wd, mod3, final_w.reshape(1, d), t_slab, x1)


def _lane_pad(v, offset=0):
    v = v.reshape(-1).astype(F32)
    return jnp.zeros((1, LANES), F32).at[0, offset:offset + v.shape[0]].set(v)


def _to_col_major(t, rows):
    b, length, ch = t.shape
    return t.reshape(b, rows, GRID_W, ch).transpose(0, 2, 1, 3).reshape(b, length, ch)


def _from_col_major(t, rows):
    b, length, ch = t.shape
    return t.reshape(b, GRID_W, rows, ch).transpose(0, 2, 1, 3).reshape(b, length, ch)


def kernel(x, c, ctx, c_ctx, w_mod, b_mod, norm1_w, w_in, ssd_conv_w, ssd_conv_b, ssd_dt_bias, ssd_a_log, ssd_d, ssd_norm_w, ml_conv_w, ml_conv_b, ml_w_qk, ml_gate_b, ml_norm_w, ml_skip, w_out, norm2_w, moe_rg_w, moe_rg_b, moe_re_w, moe_re_b, moe_w_gate, moe_w_up, moe_w_down, final_norm_w):
    b, seq, d = x.shape
    ctx_len = ctx.shape[1]
    rows = seq // GRID_W
    ncc = ctx_len // CHUNK
    assert w_mod.shape[0] == 1 and d == D_MODEL and b + 1 <= 8 and b % NB == 0
    assert seq % TM_IN == 0 and ctx_len % CHUNK == 0 and seq % TM_FIN == 0 and (2 * seq) % BM == 0
    assert seq == rows * GRID_W and GRID_W % W_TILE == 0 and rows % 8 == 0

    c_all = jnp.zeros((8, d), F32).at[:b].set(c).at[b].set(c_ctx)
    mod = _modulation(c_all, w_mod[0], b_mod[0])
    mod3 = mod.reshape(8 * 6, 1, d)

    w = w_in[0]
    ssd_in = SSD_WIDTH + SSD_XBC + 2 * SSD_HEADS
    ml_main = 3 * ML_WIDTH
    n_gate = 2 * SSD_HEADS + 4 * ML_HEADS
    w_cat = jnp.concatenate([
        w[:, :SSD_WIDTH + SSD_XBC], w[:, ssd_in:ssd_in + ml_main],
        w[:, SSD_WIDTH + SSD_XBC:ssd_in], w[:, ssd_in + ml_main:],
        jnp.zeros((d, LANES - n_gate), F32)], axis=1).astype(BF16)
    z, xbc, ml, gates, gates_cm = _in_proj(x, ctx, mod3, norm1_w[0], w_cat)

    e_mats = []
    for direction in range(2):
        lane = jnp.arange(LANES)[:, None]
        head = (jnp.arange(SSD_WIDTH) // SSD_HEADDIM)[None, :]
        e_mats.append((lane == direction * SSD_HEADS + head).astype(BF16))
    cw = jnp.zeros((8, SSD_XBC), F32).at[:CONV_W].set(ssd_conv_w[0])
    dtb = _lane_pad(ssd_dt_bias[0])
    alog = _lane_pad(ssd_a_log[0])
    dsk = jnp.repeat(ssd_d[0], SSD_HEADDIM).reshape(1, SSD_WIDTH)
    shifts = _shift_matrices()
    y_ssd = _ssd_scans(xbc, gates, z, [cw, ssd_conv_b[0].reshape(1, -1), shifts],
                       [dtb, dtb.reshape(LANES, 1), alog, alog.reshape(LANES, 1)], e_mats, dsk,
                       ssd_norm_w[0].reshape(1, -1))

    w_rows = jnp.tile(ml_w_qk[0].reshape(2, ML_WIDTH, ML_QK_BLOCK), (1, 1, ML_WIDTH // ML_QK_BLOCK))
    blk_id = jnp.arange(ML_WIDTH) // ML_QK_BLOCK
    w_bd = jnp.where((blk_id[:, None] == blk_id[None, :])[None], w_rows, 0.0)
    wq = w_bd[0].astype(BF16)
    wk = (w_bd[1] * (ML_HEADDIM ** -0.5)).astype(BF16)
    mcw = jnp.zeros((8, ML_WIDTH), F32).at[:CONV_W].set(ml_conv_w[0])
    gb = _lane_pad(ml_gate_b[0], offset=2 * SSD_HEADS)
    y_ml_cm = _ml_scans(ml, gates_cm, [mcw, ml_conv_b[0].reshape(1, -1), shifts], [wq, wk], [gb, gb.reshape(LANES, 1)],
                        ml_norm_w[0].reshape(1, -1), ml_skip[0].reshape(1, -1))
    y_ml = _from_col_major(y_ml_cm, rows)

    rw = jnp.concatenate([moe_rg_w[0], moe_re_w[0],
                          jnp.zeros((d, LANES - MOE_GROUPS - MOE_EXPERTS), F32)], axis=1)
    rw_hi = rw.astype(BF16)
    rw_lo = (rw - rw_hi.astype(F32)).astype(BF16)
    rb = _lane_pad(jnp.concatenate([moe_rg_b[0], moe_re_b[0]]))
    rw2 = jnp.concatenate([rw_hi, rw_lo], axis=1)
    x1, t_slab, route = _out_proj(x, y_ssd, y_ml, w_out[0].astype(BF16), mod3, norm2_w[0], rw2, rb)

    return _moe(t_slab, x1, route, moe_w_gate[0], moe_w_up[0], moe_w_down[0], mod3, final_norm_w)
```

```python
import functools

import jax
import jax.numpy as jnp
from jax import lax
from jax.experimental import pallas as pl
from jax.experimental.pallas import tpu as pltpu

F32 = jnp.float32
BF16 = jnp.bfloat16
HIGHEST = lax.Precision.HIGHEST

D_MODEL = 1024
GRID_W = 64
EPS = 1e-6
CONV_W = 5
NEG_BIG = -1e30
CHUNK = 128
LANES = 128
HALO = 16
NB = 4

SSD_WIDTH = 512
SSD_HEADS = 8
SSD_HEADDIM = 64
SSD_GROUPS = 2
SSD_STATE = 128
SSD_XBC = SSD_WIDTH + 2 * SSD_GROUPS * SSD_STATE

ML_WIDTH = 512
ML_HEADS = 4
ML_HEADDIM = 128
ML_QK_BLOCK = 4

MOE_GROUPS = 4
MOE_EPG = 8
MOE_EXPERTS = 32
D_EXPERT = 256
ROUTE_LANE0 = MOE_GROUPS

TM_IN = 512
TM_FIN = 256
SLAB = D_MODEL // LANES
BM = 128
TILE_PITCH = BM + 8
SCATTER_GROUP = 4
VMEM_LIMIT = 48 * 1024 * 1024
VMEM_LIMIT_MOE = 56 * 1024 * 1024


def _silu(v):
    return v * jax.nn.sigmoid(v)


def _softplus(v):
    return jnp.maximum(v, 0.0) + jnp.log1p(jnp.exp(-jnp.abs(v)))


def _dot(a, b):
    return jnp.dot(a, b, preferred_element_type=F32)


def _dot_nt(a, b):
    return lax.dot_general(a, b, (((1,), (1,)), ((), ())), preferred_element_type=F32)


def _dot_hi(a, b):
    return jnp.dot(a, b, preferred_element_type=F32, precision=HIGHEST)


def _mod_kernel(c_ref, w_ref, b_ref, o_ref):
    c = c_ref[...]
    o_ref[...] = _dot_hi(_silu(c), w_ref[...]) + b_ref[...]


def _modulation(c_all, w_mod, b_mod):
    n = w_mod.shape[1]
    bn = 1536
    return pl.pallas_call(
        _mod_kernel,
        grid=(n // bn,),
        in_specs=[
            pl.BlockSpec((8, D_MODEL), lambda j: (0, 0)),
            pl.BlockSpec((D_MODEL, bn), lambda j: (0, j)),
            pl.BlockSpec((1, bn), lambda j: (0, j)),
        ],
        out_specs=pl.BlockSpec((8, bn), lambda j: (0, j)),
        out_shape=jax.ShapeDtypeStruct((8, n), F32),
        compiler_params=pltpu.CompilerParams(vmem_limit_bytes=VMEM_LIMIT),
        name="modulation",
    )(c_all, w_mod, b_mod.reshape(1, n))


W_TILE = 16
COL_Z, COL_XBC, COL_ML, COL_G = 0, SSD_WIDTH, SSD_WIDTH + SSD_XBC, SSD_WIDTH + SSD_XBC + 3 * ML_WIDTH
PROJ_CHUNK = 512


def _norm_mod(xin, sh_ref, sc_ref, nw_ref):
    ms = jnp.mean(xin * xin, axis=-1, keepdims=True)
    y = xin * lax.rsqrt(ms + EPS) * nw_ref[...]
    return (y * (1.0 + sc_ref[...]) + sh_ref[...]).astype(BF16)


def _in_lat_kernel(rows, x_ref, sh_ref, sc_ref, nw_ref, w_ref, z_ref, xbc_ref, ml_ref, g_ref, gml_ref, scr):
    tm = rows * W_TILE
    h = _norm_mod(x_ref[...].reshape(tm, D_MODEL), sh_ref, sc_ref, nw_ref)

    def proj(col, width=PROJ_CHUNK):
        return _dot(h, w_ref[:, col:col + width])

    def to_col_major(val, dst_ref, lo, dtype):
        n_slab = val.shape[1] // LANES
        for k in range(n_slab):
            scr[k] = val[:, k * LANES:(k + 1) * LANES]
        for j in range(W_TILE):
            for k in range(n_slab):
                dst_ref[j, :, lo + k * LANES:lo + (k + 1) * LANES] = (
                    scr[k, pl.ds(j, rows, stride=W_TILE), :].astype(dtype))

    z_ref[...] = proj(COL_Z).astype(BF16).reshape(rows, W_TILE, PROJ_CHUNK)
    for j in range(SSD_XBC // PROJ_CHUNK):
        lo = j * PROJ_CHUNK
        xbc_ref[:, :, lo:lo + PROJ_CHUNK] = proj(COL_XBC + lo).astype(BF16).reshape(rows, W_TILE, PROJ_CHUNK)
    for j in range(3 * ML_WIDTH // PROJ_CHUNK):
        to_col_major(proj(COL_ML + j * PROJ_CHUNK), ml_ref, j * PROJ_CHUNK, BF16)
    g = proj(COL_G, LANES)
    g_ref[...] = g.reshape(rows, W_TILE, LANES)
    to_col_major(g, gml_ref, 0, F32)


def _in_ctx_kernel(x_ref, sh_ref, sc_ref, nw_ref, w_ref, xbc_ref, ml_ref, g_ref):
    h = _norm_mod(x_ref[...], sh_ref, sc_ref, nw_ref)
    for j in range(SSD_XBC // PROJ_CHUNK):
        lo = j * PROJ_CHUNK
        xbc_ref[:, lo:lo + PROJ_CHUNK] = _dot(h, w_ref[:, COL_XBC + lo:COL_XBC + lo + PROJ_CHUNK]).astype(BF16)
    for j in range(3 * ML_WIDTH // PROJ_CHUNK):
        lo = j * PROJ_CHUNK
        ml_ref[:, lo:lo + PROJ_CHUNK] = _dot(h, w_ref[:, COL_ML + lo:COL_ML + lo + PROJ_CHUNK]).astype(BF16)
    g_ref[...] = _dot(h, w_ref[:, COL_G:COL_G + LANES])


def _in_proj(x, ctx, mod3, norm_w, w_cat):
    b, seq, d = x.shape
    ctx_len = ctx.shape[1]
    rows = seq // GRID_W
    widths = (SSD_WIDTH, SSD_XBC, 3 * ML_WIDTH, LANES, LANES)
    dtypes = (BF16, BF16, BF16, F32, F32)
    params = pltpu.CompilerParams(dimension_semantics=("arbitrary", "arbitrary"), vmem_limit_bytes=VMEM_LIMIT)
    nw = norm_w.reshape(1, d)

    def raster(width):
        return pl.BlockSpec((None, rows, W_TILE, width), lambda bi, wi: (bi, 0, wi, 0))

    def col_major(width):
        return pl.BlockSpec((None, W_TILE, rows, width), lambda bi, wi: (bi, wi, 0, 0))

    outs = pl.pallas_call(
        functools.partial(_in_lat_kernel, rows),
        grid=(b, GRID_W // W_TILE),
        in_specs=[
            raster(d),
            pl.BlockSpec((None, 1, d), lambda bi, wi: (bi * 6, 0, 0)),
            pl.BlockSpec((None, 1, d), lambda bi, wi: (bi * 6 + 1, 0, 0)),
            _full((1, d)), _full(w_cat.shape),
        ],
        out_specs=[raster(SSD_WIDTH), raster(SSD_XBC), col_major(3 * ML_WIDTH), raster(LANES), col_major(LANES)],
        out_shape=[jax.ShapeDtypeStruct((b, GRID_W, rows, w) if cm else (b, rows, GRID_W, w), t)
                   for w, t, cm in zip(widths, dtypes, (False, False, True, False, True))],
        scratch_shapes=[pltpu.VMEM((PROJ_CHUNK // LANES, rows * W_TILE, LANES), F32)],
        compiler_params=params,
        name="in_proj",
    )(x.reshape(b, rows, GRID_W, d), mod3, mod3, nw, w_cat)
    z, xbc, ml, gates, gates_cm = [o.reshape(b, seq, w) for o, w in zip(outs, widths)]

    ctx_row = b
    ctx_widths = (SSD_XBC, 3 * ML_WIDTH, LANES)
    xbc_c, ml_c, gates_c = pl.pallas_call(
        _in_ctx_kernel,
        grid=(b, 1),
        in_specs=[
            pl.BlockSpec((None, ctx_len, d), lambda bi, t: (bi, 0, 0)),
            pl.BlockSpec((None, 1, d), lambda bi, t: (ctx_row * 6, 0, 0)),
            pl.BlockSpec((None, 1, d), lambda bi, t: (ctx_row * 6 + 1, 0, 0)),
            _full((1, d)), _full(w_cat.shape),
        ],
        out_specs=[pl.BlockSpec((None, ctx_len, w), lambda bi, t: (bi, 0, 0)) for w in ctx_widths],
        out_shape=[jax.ShapeDtypeStruct((b, ctx_len, w), t) for w, t in zip(ctx_widths, (BF16, BF16, F32))],
        compiler_params=params,
        name="in_proj_ctx",
    )(ctx, mod3, mod3, nw, w_cat)
    return z, (xbc, xbc_c), (ml, ml_c), (gates, gates_c), (gates_cm, gates_c)


def _chunk_eff(rev, ncc, nc, c):
    nl = nc - ncc
    if not rev:
        return jnp.where(c < ncc, nl + c, c - ncc)
    return jnp.where(c < ncc, nc - 1 - c, nl - 1 - (c - ncc))


def _seq_ends(ncc, nc, ceff):
    nl = nc - ncc
    return (ceff == 0) | (ceff == nl), (ceff == nl - 1) | (ceff == nc - 1)


def _lat_block(rev, ncc, nc, c):
    nl = nc - ncc
    if not rev:
        return jnp.maximum(c - ncc, 0)
    return jnp.where(c < ncc, nl - 1, nl - 1 - (c - ncc))


CONV_SIDE_TAPS = tuple(t for t in range(CONV_W) if t != CONV_W // 2)


def _shift_matrices():
    row = jnp.arange(CHUNK)[:, None]
    col = jnp.arange(CHUNK + 2 * HALO)[None, :]
    return jnp.stack([col == row + HALO + tap - CONV_W // 2 for tap in CONV_SIDE_TAPS]).astype(BF16)


def _conv_silu(xm, xp, xn, cw_ref, cb_ref, sh_ref, first, last):
    xp = jnp.where(first, jnp.zeros_like(xp), xp)
    xn = jnp.where(last, jnp.zeros_like(xn), xn)
    ext = jnp.concatenate([xp, xm, xn], axis=0)
    mid = CONV_W // 2
    acc = cb_ref[...] + cw_ref[mid:mid + 1, :] * xm.astype(F32)
    for i, tap in enumerate(CONV_SIDE_TAPS):
        acc = acc + cw_ref[tap:tap + 1, :] * _dot(sh_ref[i], ext)
    return _silu(acc)


def _tri(rev, transposed=False):
    row = lax.broadcasted_iota(jnp.int32, (CHUNK, CHUNK), 0)
    col = lax.broadcasted_iota(jnp.int32, (CHUNK, CHUNK), 1)
    if transposed:
        row, col = col, row
    keep = (col >= row) if rev else (col <= row)
    return keep


def _split3(a):
    a1 = a.astype(BF16)
    r1 = a - a1.astype(F32)
    a2 = r1.astype(BF16)
    a3 = (r1 - a2.astype(F32)).astype(BF16)
    return a1, a2, a3


def _dot_sel_l(m_b, a):
    p = _split3(a)
    return _dot(m_b, p[0]) + _dot(m_b, p[1]) + _dot(m_b, p[2])


def _dot_sel_r(a, m_b):
    p = _split3(a)
    return _dot(p[0], m_b) + _dot(p[1], m_b) + _dot(p[2], m_b)


def _row_spec(rows, width, row_block, col_block=0):
    return pl.BlockSpec((NB, rows, width), lambda bi, c: (bi, row_block(c), col_block))


def _local_chunks(rev, ncc, nc, c):
    nl = nc - ncc
    cc = jnp.clip(ncc - 1 - c if rev else c, 0, ncc - 1)
    cl = jnp.clip(nl - 1 - (c - ncc) if rev else c - ncc, 0, nl - 1)
    return c < ncc, cc, cl


def _pair_specs(rev, ncc, nc, rows, width, col=0, halo=0):
    per = CHUNK // HALO

    def index(which, count):
        def fn(c):
            ch = _local_chunks(rev, ncc, nc, c)[which]
            if halo == 0:
                return ch
            if halo < 0:
                return jnp.maximum(ch * per - 1, 0)
            return jnp.minimum((ch + 1) * per, count * per - 1)
        return fn

    return [_row_spec(rows, width, index(2, nc - ncc), col), _row_spec(rows, width, index(1, ncc), col)]


def _pick(is_ctx, lat_ref, ctx_ref, s):
    return jnp.where(is_ctx, ctx_ref[s], lat_ref[s])


def _step_ends(rev, ncc, nc, c):
    is_ctx, cc, cl = _local_chunks(rev, ncc, nc, c)
    first = jnp.where(is_ctx, cc == 0, cl == 0)
    last = jnp.where(is_ctx, cc == ncc - 1, cl == nc - ncc - 1)
    return is_ctx, first, last


def _full(shape):
    return pl.BlockSpec(shape, lambda bi, c: (0,) * len(shape))


def _ssd_gates(rev, xs, g, dtbr_ref, dtbc_ref, alr_ref, alc_ref, e_ref):
    lane0 = SSD_HEADS * int(rev)
    lane = lax.broadcasted_iota(jnp.int32, (CHUNK, LANES), 1)
    lmask = (lane >= lane0) & (lane < lane0 + SSD_HEADS)
    dt = jnp.where(lmask, _softplus(g + dtbr_ref[...]), 0.0)
    a = dt * (-jnp.exp(alr_ref[...]))
    gt = g.T
    dt_t = _softplus(gt + dtbc_ref[...])[lane0:lane0 + SSD_HEADS]
    a_t = dt_t * (-jnp.exp(alc_ref[...][lane0:lane0 + SSD_HEADS]))

    cs = _dot_sel_l(_tri(rev).astype(BF16), a)
    cs_t = _dot_sel_r(a_t, _tri(rev, transposed=True).astype(BF16))
    e = e_ref[...]
    dtx = _dot_sel_r(dt, e)
    csx = _dot_sel_r(cs, e)
    end = 0 if rev else CHUNK - 1
    totx = csx[end:end + 1, :]
    ecsx = jnp.exp(csx)
    decx = jnp.exp(totx - csx)
    etotx = jnp.exp(totx)

    xdt = xs * dtx
    return dict(cs=cs, cs_t=cs_t, ecsx=ecsx, etotx=etotx, xdt_b=xdt.astype(BF16), xd_b=(xdt * decx).astype(BF16))


def _ssd_chunks(rev, xbcs, gs_, dtbr_ref, dtbc_ref, alr_ref, alc_ref, e_ref, s_ref):
    lane0 = SSD_HEADS * int(rev)
    keep = _tri(rev)
    half = lax.broadcasted_iota(jnp.int32, (CHUNK, LANES), 1) // SSD_HEADDIM
    n_bc = SSD_GROUPS * SSD_STATE
    hpg = SSD_HEADS // SSD_GROUPS
    gw = hpg * SSD_HEADDIM

    gq = [_ssd_gates(rev, xbc[:, :SSD_WIDTH], g, dtbr_ref, dtbc_ref, alr_ref, alc_ref, e_ref)
          for xbc, g in zip(xbcs, gs_)]
    units = []
    for s, xbc in enumerate(xbcs):
        for grp in range(SSD_GROUPS):
            bm = xbc[:, SSD_WIDTH + grp * SSD_STATE:SSD_WIDTH + (grp + 1) * SSD_STATE]
            cm = xbc[:, SSD_WIDTH + n_bc + grp * SSD_STATE:SSD_WIDTH + n_bc + (grp + 1) * SSD_STATE]
            units.append(dict(s=s, grp=grp, cols=slice(grp * gw, (grp + 1) * gw), bm=bm, bm_b=bm.astype(BF16),
                              cm_b=cm.astype(BF16), s_old=s_ref[s, :, grp * gw:(grp + 1) * gw]))
    for u in units:
        u["cb"] = _dot_nt(u["cm_b"], u["bm_b"])
        u["y_off"] = _dot(u["cm_b"], u["s_old"].astype(BF16)) * gq[u["s"]]["ecsx"][:, u["cols"]]
        u["bt_b"] = u["bm"].T.astype(BF16)
    for u in units:
        q_ = gq[u["s"]]
        masks = []
        for hh in range(hpg):
            h = u["grp"] * hpg + hh
            dl = q_["cs"][:, lane0 + h:lane0 + h + 1] - q_["cs_t"][h:h + 1, :]
            masks.append((u["cb"] * jnp.exp(jnp.where(keep, dl, NEG_BIG))).astype(BF16))
        u["masks"] = masks
    for u in units:
        q_ = gq[u["s"]]
        blocks = []
        for pair in range(hpg // 2):
            blk = u["grp"] * (hpg // 2) + pair
            xj = q_["xdt_b"][:, blk * LANES:(blk + 1) * LANES]
            acc = u["y_off"][:, pair * LANES:(pair + 1) * LANES]
            for q in range(2):
                acc = acc + _dot(u["masks"][pair * 2 + q], jnp.where(half == q, xj, jnp.zeros_like(xj)))
            blocks.append(acc)
        u["y"] = blocks
    for u in units:
        q_ = gq[u["s"]]
        s_new = u["s_old"] * q_["etotx"][:, u["cols"]] + _dot(u["bt_b"], q_["xd_b"][:, u["cols"]])
        s_ref[u["s"], :, u["cols"]] = s_new
    return [jnp.concatenate([blk for u in units if u["s"] == s for blk in u["y"]], axis=1)
            for s in range(len(xbcs))]


def _ssd_rev_kernel(ncc, nc, xml_ref, xmc_ref, xpl_ref, xpc_ref, xnl_ref, xnc_ref, gl_ref, gc_ref, cw_ref, cb_ref,
                    sh_ref, dtbr_ref, dtbc_ref, alr_ref, alc_ref, e_ref, o_ref, xc_ref, s_ref):
    c = pl.program_id(1)
    is_ctx, first, last = _step_ends(True, ncc, nc, c)

    @pl.when(c == 0)
    def _():
        s_ref[...] = jnp.zeros_like(s_ref)

    xbcs = [_conv_silu(_pick(is_ctx, xml_ref, xmc_ref, s), _pick(is_ctx, xpl_ref, xpc_ref, s),
                       _pick(is_ctx, xnl_ref, xnc_ref, s), cw_ref, cb_ref, sh_ref, first, last) for s in range(NB)]
    for s in range(NB):
        xc_ref[s] = xbcs[s].astype(BF16)
    ys = _ssd_chunks(True, xbcs, [_pick(is_ctx, gl_ref, gc_ref, s) for s in range(NB)], dtbr_ref, dtbc_ref,
                     alr_ref, alc_ref, e_ref, s_ref)
    for s in range(NB):
        o_ref[s] = ys[s].astype(o_ref.dtype)


def _ssd_fwd_kernel(ncc, xc_ref, gl_ref, gc_ref, dtbr_ref, dtbc_ref, alr_ref, alc_ref, e_ref, dsk_ref, yb_ref,
                    z_ref, nw_ref, o_ref, s_ref):
    is_ctx = pl.program_id(1) < ncc

    @pl.when(pl.program_id(1) == 0)
    def _():
        s_ref[...] = jnp.zeros_like(s_ref)

    xbcs = [xc_ref[s].astype(F32) for s in range(NB)]
    ys = _ssd_chunks(False, xbcs, [_pick(is_ctx, gl_ref, gc_ref, s) for s in range(NB)], dtbr_ref, dtbc_ref,
                     alr_ref, alc_ref, e_ref, s_ref)
    ys = [(ys[s] + yb_ref[s].astype(F32) + dsk_ref[...] * xbcs[s][:, :SSD_WIDTH]) * _silu(z_ref[s].astype(F32))
          for s in range(NB)]
    scale = [lax.rsqrt(jnp.mean(y * y, axis=-1, keepdims=True) + EPS) for y in ys]
    for s in range(NB):
        o_ref[s] = (ys[s] * scale[s] * nw_ref[...]).astype(o_ref.dtype)


def _ssd_scans(xbc, gates, z, conv_params, gate_params, e_mats, dsk, nw):
    b, seq, _ = xbc[0].shape
    ncc = xbc[1].shape[1] // CHUNK
    nc = seq // CHUNK + ncc
    state = pltpu.VMEM((NB, SSD_STATE, SSD_WIDTH), F32)
    params = pltpu.CompilerParams(dimension_semantics=("arbitrary", "arbitrary"), vmem_limit_bytes=VMEM_LIMIT)
    lat_shape = jax.ShapeDtypeStruct((b, seq, SSD_WIDTH), BF16)

    ceff = functools.partial(_chunk_eff, True, ncc, nc)
    lat = functools.partial(_lat_block, True, ncc, nc)
    pair = functools.partial(_pair_specs, True, ncc, nc)
    rev_params = list(conv_params) + list(gate_params) + [e_mats[1]]
    yb, xbc_act = pl.pallas_call(
        functools.partial(_ssd_rev_kernel, ncc, nc),
        grid=(b // NB, nc),
        in_specs=pair(CHUNK, SSD_XBC) + pair(HALO, SSD_XBC, halo=-1) + pair(HALO, SSD_XBC, halo=1)
        + pair(CHUNK, LANES) + [_full(p.shape) for p in rev_params],
        out_specs=[_row_spec(CHUNK, SSD_WIDTH, lat), _row_spec(CHUNK, SSD_XBC, ceff)],
        out_shape=(lat_shape, jax.ShapeDtypeStruct((b, nc * CHUNK, SSD_XBC), BF16)),
        scratch_shapes=[state],
        compiler_params=params,
        name="ssd_rev",
    )(*xbc, *xbc, *xbc, *gates, *rev_params)

    ceff = functools.partial(_chunk_eff, False, ncc, nc)
    lat = functools.partial(_lat_block, False, ncc, nc)
    fwd_params = list(gate_params) + [e_mats[0], dsk]
    return pl.pallas_call(
        functools.partial(_ssd_fwd_kernel, ncc),
        grid=(b // NB, nc),
        in_specs=[_row_spec(CHUNK, SSD_XBC, ceff)] + _pair_specs(False, ncc, nc, CHUNK, LANES)
        + [_full(p.shape) for p in fwd_params]
        + [_row_spec(CHUNK, SSD_WIDTH, lat), _row_spec(CHUNK, SSD_WIDTH, lat), _full(nw.shape)],
        out_specs=_row_spec(CHUNK, SSD_WIDTH, lat),
        out_shape=lat_shape,
        scratch_shapes=[state],
        compiler_params=params,
        name="ssd_fwd",
    )(xbc_act, *gates, *fwd_params, yb, z, nw)


ML_I_LANE0 = 2 * SSD_HEADS
ML_F_LANE0 = ML_I_LANE0 + 2 * ML_HEADS


def _ml_gates(rev, g, gbr_ref, gbc_ref):
    i_lane0 = ML_I_LANE0 + ML_HEADS * int(rev)
    f_lane0 = ML_F_LANE0 + ML_HEADS * int(rev)
    ga = g + gbr_ref[...]
    lane = lax.broadcasted_iota(jnp.int32, (CHUNK, LANES), 1)
    logf = jnp.where((lane >= f_lane0) & (lane < f_lane0 + ML_HEADS), -_softplus(-ga), 0.0)
    cs = _dot_sel_l(_tri(rev).astype(BF16), logf)
    gt = g.T + gbc_ref[...]
    i_t = gt[ML_I_LANE0:ML_F_LANE0]
    logf_t = -_softplus(-gt[ML_F_LANE0:ML_F_LANE0 + 2 * ML_HEADS])
    cs_t = _dot_sel_r(logf_t, _tri(rev, transposed=True).astype(BF16))
    end = 0 if rev else CHUNK - 1

    u_t = i_t - cs_t
    u_c = ga - pltpu.roll(cs, LANES - (ML_F_LANE0 - ML_I_LANE0), axis=1)
    row = lax.broadcasted_iota(jnp.int32, (CHUNK, LANES), 0)
    pm = u_c
    step = 1
    while step < CHUNK:
        if rev:
            pm = jnp.maximum(pm, jnp.where(row < CHUNK - step, pltpu.roll(pm, CHUNK - step, axis=0), NEG_BIG))
        else:
            pm = jnp.maximum(pm, jnp.where(row >= step, pltpu.roll(pm, step, axis=0), NEG_BIG))
        step *= 2

    heads = []
    for h in range(ML_HEADS):
        r = ML_HEADS * int(rev) + h
        li = i_lane0 + h
        heads.append(dict(
            csc=cs[:, f_lane0 + h:f_lane0 + h + 1], u_col=u_c[:, li:li + 1], u_row=u_t[r:r + 1, :],
            tot=cs_t[r:r + 1, end:end + 1], u_max=pm[end:end + 1, li:li + 1], pm_col=pm[:, li:li + 1]))
    return heads


def _ml_chunks(rev, samples, gbr_ref, gbc_ref, c_ref, n_ref, mx_ref):
    keep = _tri(rev)
    ones_b = jnp.ones((CHUNK, ML_HEADDIM), BF16)
    units = []
    for s, (k, q_b, k_b, v_b, g) in enumerate(samples):
        for h, gq in enumerate(_ml_gates(rev, g, gbr_ref, gbc_ref)):
            sl = slice(h * ML_HEADDIM, (h + 1) * ML_HEADDIM)
            units.append(dict(gq, s=s, h=h, kh=k[:, sl], vh=v_b[:, sl], qh_b=q_b[:, sl], kh_b=k_b[:, sl],
                              m_prev=mx_ref[s, h:h + 1, 0:1], c_prev=c_ref[s * ML_HEADS + h],
                              n_prev=n_ref[s, h:h + 1, :]))

    for u in units:
        u["qk"] = _dot_nt(u["qh_b"], u["kh_b"])
        u["vt_b"] = u["vh"].astype(F32).T.astype(BF16)
    for u in units:
        u["mm"] = jnp.maximum(u["m_prev"], u["pm_col"])
        u["scores_b"] = (u["qk"] * jnp.exp(jnp.where(keep, u["u_row"] - u["mm"], NEG_BIG))).astype(BF16)
    for u in units:
        c_aug = jnp.concatenate([u["c_prev"], jnp.broadcast_to(u["n_prev"], (CHUNK, ML_HEADDIM))], axis=0)
        u["intra"] = _dot(u["scores_b"], jnp.concatenate([u["vh"], ones_b], axis=1))
        u["inter"] = _dot_nt(u["qh_b"], c_aug.astype(BF16))
    outs = [[None] * ML_HEADS for _ in samples]
    for u in units:
        both = u["intra"] + jnp.exp(u["m_prev"] - u["mm"]) * u["inter"]
        den = both[:, ML_HEADDIM:ML_HEADDIM + 1]
        outs[u["s"]][u["h"]] = both[:, :ML_HEADDIM] / jnp.maximum(jnp.abs(den), jnp.exp(-(u["csc"] + u["mm"])))
    for u in units:
        kw = u["kh"] * jnp.exp(u["u_col"] - u["u_max"])
        u["c_loc"] = _dot(u["vt_b"], kw.astype(BF16))
        u["n_loc"] = jnp.sum(kw, axis=0, keepdims=True)
    for u in units:
        s, h, tot, m_prev = u["s"], u["h"], u["tot"], u["m_prev"]
        m_loc = tot + u["u_max"]
        m_new = jnp.maximum(tot + m_prev, m_loc)
        s_prev = jnp.exp(tot + m_prev - m_new)
        s_loc = jnp.exp(m_loc - m_new)
        c_ref[s * ML_HEADS + h] = s_prev * u["c_prev"] + s_loc * u["c_loc"]
        n_ref[s, h:h + 1, :] = s_prev * u["n_prev"] + s_loc * u["n_loc"]
        mx_ref[s, h:h + 1, :] = jnp.broadcast_to(m_new, (1, LANES))
    return outs


def _ml_init_state(c_ref, n_ref, mx_ref):
    c_ref[...] = jnp.zeros_like(c_ref)
    n_ref[...] = jnp.zeros_like(n_ref)
    mx_ref[...] = jnp.full(mx_ref.shape, NEG_BIG, F32)


def _ml_rev_kernel(ncc, nc, xml_ref, xmc_ref, xpl_ref, xpc_ref, xnl_ref, xnc_ref, vl_ref, vc_ref, gl_ref, gc_ref,
                   cw_ref, cb_ref, sh_ref, wq_ref, wk_ref, gbr_ref, gbc_ref, o_ref, xc_ref, q_ref, k_ref, c_ref, n_ref,
                   mx_ref):
    c = pl.program_id(1)
    is_ctx, first, last = _step_ends(True, ncc, nc, c)

    @pl.when(c == 0)
    def _():
        _ml_init_state(c_ref, n_ref, mx_ref)

    samples = []
    for s in range(NB):
        xconv = _conv_silu(_pick(is_ctx, xml_ref, xmc_ref, s), _pick(is_ctx, xpl_ref, xpc_ref, s),
                           _pick(is_ctx, xnl_ref, xnc_ref, s), cw_ref, cb_ref, sh_ref, first, last)
        xc_b = xconv.astype(BF16)
        q = _dot(xc_b, wq_ref[...])
        k = _dot(xc_b, wk_ref[...])
        q_b, k_b = q.astype(BF16), k.astype(BF16)
        xc_ref[s] = xc_b
        q_ref[s] = q_b
        k_ref[s] = k_b
        samples.append((k, q_b, k_b, _pick(is_ctx, vl_ref, vc_ref, s), _pick(is_ctx, gl_ref, gc_ref, s)))
    outs = _ml_chunks(True, samples, gbr_ref, gbc_ref, c_ref, n_ref, mx_ref)
    for s in range(NB):
        o_ref[s] = jnp.concatenate(outs[s], axis=1).astype(o_ref.dtype)


def _ml_fwd_kernel(ncc, xc_ref, q_ref, k_ref, vl_ref, vc_ref, og_ref, gl_ref, gc_ref, gbr_ref, gbc_ref, hb_ref,
                   nw_ref, sk_ref, o_ref, c_ref, n_ref, mx_ref):
    is_ctx = pl.program_id(1) < ncc

    @pl.when(pl.program_id(1) == 0)
    def _():
        _ml_init_state(c_ref, n_ref, mx_ref)

    samples = []
    for s in range(NB):
        q_b, k_b = q_ref[s], k_ref[s]
        samples.append((k_b.astype(F32), q_b, k_b, _pick(is_ctx, vl_ref, vc_ref, s),
                        _pick(is_ctx, gl_ref, gc_ref, s)))
    outs = _ml_chunks(False, samples, gbr_ref, gbc_ref, c_ref, n_ref, mx_ref)
    gated = []
    for s in range(NB):
        for h in range(ML_HEADS):
            sl = slice(h * ML_HEADDIM, (h + 1) * ML_HEADDIM)
            gated.append(jax.nn.sigmoid(og_ref[s, :, sl].astype(F32)) * (outs[s][h] + hb_ref[s, :, sl].astype(F32)))
    scale = [lax.rsqrt(jnp.mean(hh * hh, axis=-1, keepdims=True) + EPS) for hh in gated]
    normed = [hh * sc for hh, sc in zip(gated, scale)]
    for s in range(NB):
        y = jnp.concatenate(normed[s * ML_HEADS:(s + 1) * ML_HEADS], axis=1) * nw_ref[...]
        o_ref[s] = (y + sk_ref[...] * xc_ref[s].astype(F32)).astype(o_ref.dtype)


def _ml_scans(ml, gates, conv_params, proj_params, gate_params, nw, sk):
    b, seq, _ = ml[0].shape
    ncc = ml[1].shape[1] // CHUNK
    nc = seq // CHUNK + ncc
    scratch = [pltpu.VMEM((NB * ML_HEADS, ML_HEADDIM, ML_HEADDIM), F32),
               pltpu.VMEM((NB, 8, ML_HEADDIM), F32),
               pltpu.VMEM((NB, 8, LANES), F32)]
    params = pltpu.CompilerParams(dimension_semantics=("arbitrary", "arbitrary"), vmem_limit_bytes=VMEM_LIMIT)
    lat_shape = jax.ShapeDtypeStruct((b, seq, ML_WIDTH), BF16)
    act_shape = jax.ShapeDtypeStruct((b, nc * CHUNK, ML_WIDTH), BF16)

    ceff = functools.partial(_chunk_eff, True, ncc, nc)
    lat = functools.partial(_lat_block, True, ncc, nc)
    pair = functools.partial(_pair_specs, True, ncc, nc)
    rev_params = list(conv_params) + list(proj_params) + list(gate_params)
    act_spec = _row_spec(CHUNK, ML_WIDTH, ceff)
    hb, xc, q, k = pl.pallas_call(
        functools.partial(_ml_rev_kernel, ncc, nc),
        grid=(b // NB, nc),
        in_specs=pair(CHUNK, ML_WIDTH) + pair(HALO, ML_WIDTH, halo=-1) + pair(HALO, ML_WIDTH, halo=1)
        + pair(CHUNK, ML_WIDTH, col=1) + pair(CHUNK, LANES) + [_full(p.shape) for p in rev_params],
        out_specs=[_row_spec(CHUNK, ML_WIDTH, lat), act_spec, act_spec, act_spec],
        out_shape=(lat_shape, act_shape, act_shape, act_shape),
        scratch_shapes=scratch,
        compiler_params=params,
        name="mlstm_rev",
    )(*ml, *ml, *ml, *ml, *gates, *rev_params)

    ceff = functools.partial(_chunk_eff, False, ncc, nc)
    lat = functools.partial(_lat_block, False, ncc, nc)
    pair = functools.partial(_pair_specs, False, ncc, nc)
    act_spec = _row_spec(CHUNK, ML_WIDTH, ceff)
    return pl.pallas_call(
        functools.partial(_ml_fwd_kernel, ncc),
        grid=(b // NB, nc),
        in_specs=[act_spec, act_spec, act_spec] + pair(CHUNK, ML_WIDTH, col=1)
        + [_row_spec(CHUNK, ML_WIDTH, lat, 2)] + pair(CHUNK, LANES)
        + [_full(p.shape) for p in gate_params]
        + [_row_spec(CHUNK, ML_WIDTH, lat), _full(nw.shape), _full(sk.shape)],
        out_specs=_row_spec(CHUNK, ML_WIDTH, lat),
        out_shape=lat_shape,
        scratch_shapes=scratch,
        compiler_params=params,
        name="mlstm_fwd",
    )(xc, q, k, *ml, ml[0], *gates, *gate_params, hb, nw, sk)


def _out_kernel(x_ref, ys_ref, ym_ref, wo_ref, g1_ref, sh_ref, sc_ref, nw_ref, rw_ref, rb_ref,
                x1_ref, ts_ref, route_ref):
    mix = _dot(ys_ref[...], wo_ref[0:SSD_WIDTH, :]) + _dot(ym_ref[...], wo_ref[SSD_WIDTH:, :])
    x1 = x_ref[...] + g1_ref[...] * mix
    y = x1 * lax.rsqrt(jnp.mean(x1 * x1, axis=-1, keepdims=True) + EPS) * nw_ref[...]
    t = y * (1.0 + sc_ref[...]) + sh_ref[...]
    x1_ref[...] = x1
    for j in range(SLAB):
        ts_ref[pl.ds(j, TM_IN, stride=SLAB), :] = t[:, j * LANES:(j + 1) * LANES]
    lg2 = _dot(t.astype(BF16), rw_ref[...])
    lg = lg2[:, :LANES] + lg2[:, LANES:] + rb_ref[...]

    lane = lax.broadcasted_iota(jnp.int32, lg.shape, 1).astype(F32)
    gmask = lane < MOE_GROUPS
    gl = jnp.where(gmask, lg, NEG_BIG)
    gmax = jnp.max(gl, axis=1, keepdims=True)
    g_sel = jnp.min(jnp.where(gmask & (gl == gmax), lane, 1e9), axis=1, keepdims=True)
    p_group = 1.0 / jnp.sum(jnp.where(gmask, jnp.exp(gl - gmax), 0.0), axis=1, keepdims=True)
    lo = ROUTE_LANE0 + MOE_EPG * g_sel
    emask = (lane >= lo) & (lane < lo + MOE_EPG)
    l1 = jnp.max(jnp.where(emask, lg, NEG_BIG), axis=1, keepdims=True)
    i1 = jnp.min(jnp.where(emask & (lg == l1), lane, 1e9), axis=1, keepdims=True)
    emask2 = emask & (lane != i1)
    l2 = jnp.max(jnp.where(emask2, lg, NEG_BIG), axis=1, keepdims=True)
    i2 = jnp.min(jnp.where(emask2 & (lg == l2), lane, 1e9), axis=1, keepdims=True)
    r = jnp.exp(l2 - l1)
    w1 = p_group / (1.0 + r)
    w2 = p_group * r / (1.0 + r)
    route = (jnp.where(lane == 0, i1, 0.0) + jnp.where(lane == 1, i2, 0.0)
             + jnp.where(lane == 2, w1, 0.0) + jnp.where(lane == 3, w2, 0.0))
    route_ref[...] = route.T[0:8, :]


def _out_proj(x, y_ssd, y_ml, w_out_b, mod3, norm_w, rw2, rb):
    b, seq, d = x.shape
    nt = seq // TM_IN

    def row(j):
        return pl.BlockSpec((None, 1, d), lambda bi, t: (bi * 6 + j, 0, 0))

    def tile(width):
        return pl.BlockSpec((None, TM_IN, width), lambda bi, t: (bi, t, 0))

    return pl.pallas_call(
        _out_kernel,
        grid=(b, nt),
        in_specs=[tile(d), tile(SSD_WIDTH), tile(ML_WIDTH), _full(w_out_b.shape),
                  row(2), row(3), row(4), _full((1, d)), _full(rw2.shape), _full(rb.shape)],
        out_specs=[tile(d),
                   pl.BlockSpec((None, TM_IN * SLAB, LANES), lambda bi, t: (bi, t, 0)),
                   pl.BlockSpec((None, 8, TM_IN), lambda bi, t: (bi, 0, t))],
        out_shape=(jax.ShapeDtypeStruct((b, seq, d), F32),
                   jax.ShapeDtypeStruct((b, seq * SLAB, LANES), F32),
                   jax.ShapeDtypeStruct((b, 8, seq), F32)),
        compiler_params=pltpu.CompilerParams(
            dimension_semantics=("arbitrary", "arbitrary"), vmem_limit_bytes=VMEM_LIMIT),
        name="out_proj_router",
    )(x, y_ssd, y_ml, w_out_b, mod3, mod3, mod3, norm_w.reshape(1, d), rw2, rb)


def _route_tables(route, seq):
    b = route.shape[0]
    n_inst = 2 * seq
    nblk = n_inst // BM + MOE_EXPERTS
    e_flat = (route[:, 0:2, :].astype(jnp.int32) - ROUTE_LANE0).reshape(b, n_inst)
    w_flat = route[:, 2:4, :].reshape(b, n_inst)
    tok_flat = jnp.tile(jnp.arange(seq, dtype=jnp.int32), 2 * b).reshape(b, n_inst)
    _, tok_sorted, w_sorted = lax.sort((e_flat, tok_flat, w_flat), dimension=1, num_keys=1)
    experts = jnp.arange(MOE_EXPERTS, dtype=jnp.int32)
    counts = jnp.sum((e_flat[:, None, :] == experts[None, :, None]).astype(jnp.int32), axis=2)
    nblk_e = (counts + BM - 1) // BM
    blk_end = jnp.cumsum(nblk_e, axis=1)
    blk_start = blk_end - nblk_e
    cnt_start = jnp.cumsum(counts, axis=1) - counts
    nb = blk_end[:, -1:]
    j = jnp.arange(nblk, dtype=jnp.int32)[None, :]
    valid_blk = j < nb
    jj = jnp.minimum(j, nb - 1)
    e_j = jnp.sum((jj[:, :, None] >= blk_end[:, None, :]).astype(jnp.int32), axis=2)
    onehot = (e_j[:, :, None] == experts[None, None, :]).astype(jnp.int32)
    take = lambda tbl: jnp.sum(onehot * tbl[:, None, :], axis=2)
    r = jnp.arange(BM, dtype=jnp.int32)[None, None, :]
    rank = ((jj - take(blk_start)) * BM)[:, :, None] + r
    valid = valid_blk[:, :, None] & (rank < take(counts)[:, :, None])
    sidx = jnp.clip(take(cnt_start)[:, :, None] + rank, 0, n_inst - 1).reshape(b, nblk * BM)
    tok = jnp.take_along_axis(tok_sorted, sidx, axis=1).reshape(b, nblk, BM)
    wslot = jnp.take_along_axis(w_sorted, sidx, axis=1).reshape(b, nblk, BM)
    tok = jnp.where(valid, tok, seq + r) * SLAB
    wslot = jnp.where(valid, wslot, 0.0)
    return (nblk_e.reshape(-1), blk_start.reshape(-1), tok.reshape(b, 1, nblk * BM), wslot)


SEM_T, SEM_X, SEM_O = 0, 1, 3


def _moe_kernel(seq, nbe_ref, bs_ref, tok_ref, ws_ref, wg_ref, wu_ref, wd_ref, g2_ref, fw_ref,
                t_hbm, x1_hbm, o_hbm, t_scr, y_scr, wgb, wub, wdb, xt, ot, xin, stage, sems):
    b = pl.program_id(0)
    e = pl.program_id(1)
    nb = pl.num_programs(0)
    rows = seq * SLAB
    n_fin = seq // TM_FIN

    def t_copy(sample):
        return pltpu.make_async_copy(t_hbm.at[sample], t_scr.at[pl.ds(0, rows)], sems.at[SEM_T])

    def x1_copy(s, slot):
        return pltpu.make_async_copy(x1_hbm.at[b, pl.ds(s * TM_FIN, TM_FIN)], xin.at[slot], sems.at[SEM_X + slot])

    def out_copy(s, slot):
        return pltpu.make_async_copy(stage.at[slot], o_hbm.at[b, pl.ds(s * TM_FIN, TM_FIN)], sems.at[SEM_O + slot])

    @pl.when((e == 0) & (b == 0))
    def _():
        t_copy(b).start()

    @pl.when(e == 0)
    def _():
        y_scr[...] = jnp.zeros_like(y_scr)
        t_scr[pl.ds(rows, BM * SLAB), :] = jnp.zeros((BM * SLAB, LANES), F32)
        t_copy(b).wait()

    n_blocks = nbe_ref[b * MOE_EXPERTS + e]
    blk0 = bs_ref[b * MOE_EXPERTS + e]

    @pl.when(n_blocks > 0)
    def _():
        wgb[...] = wg_ref[...].astype(BF16)
        wub[...] = wu_ref[...].astype(BF16)
        wdb[...] = wd_ref[...].astype(BF16)
        g2 = g2_ref[...]
        diag = (lax.broadcasted_iota(jnp.int32, (BM, BM), 0) == lax.broadcasted_iota(jnp.int32, (BM, BM), 1))

        def block(i, carry):
            blk = blk0 + i
            base = blk * BM

            def slab_rows(r):
                return pl.ds(pl.multiple_of(tok_ref[0, base + r], SLAB), SLAB)

            for r in range(BM):
                xt[pl.ds(r, SLAB, stride=TILE_PITCH), :] = t_scr[slab_rows(r), :]
            x = jnp.concatenate([xt[c * TILE_PITCH:c * TILE_PITCH + BM, :] for c in range(SLAB)],
                                axis=1).astype(BF16)
            w_col = jnp.sum(jnp.where(diag, ws_ref[pl.ds(blk, 1), :], 0.0), axis=1, keepdims=True)
            hidden = _silu(_dot(x, wgb[...])) * _dot(x, wub[...]) * w_col
            out = _dot(hidden.astype(BF16), wdb[...]) * g2
            for c in range(SLAB):
                ot[c * TILE_PITCH:c * TILE_PITCH + BM, :] = out[:, c * LANES:(c + 1) * LANES]
            for r0 in range(0, BM, SCATTER_GROUP):
                sl = [slab_rows(r0 + u) for u in range(SCATTER_GROUP)]
                vals = [y_scr[sl[u], :] + ot[pl.ds(r0 + u, SLAB, stride=TILE_PITCH), :]
                        for u in range(SCATTER_GROUP)]
                for u in range(SCATTER_GROUP):
                    y_scr[sl[u], :] = vals[u]
            return carry

        lax.fori_loop(0, n_blocks, block, 0)

    @pl.when(e == MOE_EXPERTS - 1)
    def _():
        @pl.when(b + 1 < nb)
        def _():
            t_copy(b + 1).start()

        x1_copy(0, 0).start()

        def tile(s, carry):
            slot = s % 2
            x1_copy(s, slot).wait()

            @pl.when(s + 1 < n_fin)
            def _():
                x1_copy(s + 1, 1 - slot).start()

            @pl.when(s >= 2)
            def _():
                out_copy(s - 2, slot).wait()

            base = pl.multiple_of(s * (TM_FIN * SLAB), TM_FIN * SLAB)
            x1 = xin[slot]
            chunks = [x1[:, c * LANES:(c + 1) * LANES] + y_scr[pl.ds(base + c, TM_FIN, stride=SLAB), :]
                      for c in range(SLAB)]
            ssq = chunks[0] * chunks[0]
            for c in range(1, SLAB):
                ssq = ssq + chunks[c] * chunks[c]
            inv = lax.rsqrt(jnp.sum(ssq, axis=1, keepdims=True) * (1.0 / D_MODEL) + EPS)
            for c in range(SLAB):
                stage[slot, :, c * LANES:(c + 1) * LANES] = chunks[c] * inv * fw_ref[:, c * LANES:(c + 1) * LANES]
            out_copy(s, slot).start()
            return carry

        lax.fori_loop(0, n_fin, tile, 0)
        out_copy(n_fin - 2, n_fin % 2).wait()
        out_copy(n_fin - 1, (n_fin - 1) % 2).wait()


def _moe(t_slab, x1, route, wg, wu, wd, mod3, final_w):
    b, seq, d = x1.shape
    nblk_e, blk_start, tok, wslot = _route_tables(route, seq)
    nblk = wslot.shape[1]

    def w_spec(shape):
        return pl.BlockSpec((None,) + shape, lambda bi, e, nbe, bs: (e, 0, 0))

    grid_spec = pltpu.PrefetchScalarGridSpec(
        num_scalar_prefetch=2,
        grid=(b, MOE_EXPERTS),
        in_specs=[
            pl.BlockSpec((None, 1, nblk * BM), lambda bi, e, nbe, bs: (bi, 0, 0), memory_space=pltpu.SMEM),
            pl.BlockSpec((None, nblk, BM), lambda bi, e, nbe, bs: (bi, 0, 0)),
            w_spec((d, D_EXPERT)), w_spec((d, D_EXPERT)), w_spec((D_EXPERT, d)),
            pl.BlockSpec((None, 1, d), lambda bi, e, nbe, bs: (bi * 6 + 5, 0, 0)),
            pl.BlockSpec((1, d), lambda bi, e, nbe, bs: (0, 0)),
            pl.BlockSpec(memory_space=pl.ANY),
            pl.BlockSpec(memory_space=pl.ANY),
        ],
        out_specs=pl.BlockSpec(memory_space=pl.ANY),
        scratch_shapes=[
            pltpu.VMEM(((seq + BM) * SLAB, LANES), F32),
            pltpu.VMEM(((seq + BM) * SLAB, LANES), F32),
            pltpu.VMEM((d, D_EXPERT), BF16),
            pltpu.VMEM((d, D_EXPERT), BF16),
            pltpu.VMEM((D_EXPERT, d), BF16),
            pltpu.VMEM((SLAB * TILE_PITCH, LANES), F32),
            pltpu.VMEM((SLAB * TILE_PITCH, LANES), F32),
            pltpu.VMEM((2, TM_FIN, d), F32),
            pltpu.VMEM((2, TM_FIN, d), F32),
            pltpu.SemaphoreType.DMA((5,)),
        ],
    )
    return pl.pallas_call(
        functools.partial(_moe_kernel, seq),
        grid_spec=grid_spec,
        out_shape=jax.ShapeDtypeStruct((b, seq, d), F32),
        compiler_params=pltpu.CompilerParams(
            dimension_semantics=("arbitrary", "arbitrary"), vmem_limit_bytes=VMEM_LIMIT_MOE),
        name="experts_final_norm",
    )(nblk_e, blk_start, tok, wslot, wg, wu, wd, mod3, final_w.reshape(1, d), t_slab, x1)


def _lane_pad(v, offset=0):
    v = v.reshape(-1).astype(F32)
    return jnp.zeros((1, LANES), F32).at[0, offset:offset + v.shape[0]].set(v)


def _to_col_major(t, rows):
    b, length, ch = t.shape
    return t.reshape(b, rows, GRID_W, ch).transpose(0, 2, 1, 3).reshape(b, length, ch)


def _from_col_major(t, rows):
    b, length, ch = t.shape
    return t.reshape(b, GRID_W, rows, ch).transpose(0, 2, 1, 3).reshape(b, length, ch)


def kernel(x, c, ctx, c_ctx, w_mod, b_mod, norm1_w, w_in, ssd_conv_w, ssd_conv_b, ssd_dt_bias, ssd_a_log, ssd_d, ssd_norm_w, ml_conv_w, ml_conv_b, ml_w_qk, ml_gate_b, ml_norm_w, ml_skip, w_out, norm2_w, moe_rg_w, moe_rg_b, moe_re_w, moe_re_b, moe_w_gate, moe_w_up, moe_w_down, final_norm_w):
    b, seq, d = x.shape
    ctx_len = ctx.shape[1]
    rows = seq // GRID_W
    ncc = ctx_len // CHUNK
    assert w_mod.shape[0] == 1 and d == D_MODEL and b + 1 <= 8 and b % NB == 0
    assert seq % TM_IN == 0 and ctx_len % CHUNK == 0 and seq % TM_FIN == 0 and (2 * seq) % BM == 0
    assert seq == rows * GRID_W and GRID_W % W_TILE == 0 and rows % 8 == 0

    c_all = jnp.zeros((8, d), F32).at[:b].set(c).at[b].set(c_ctx)
    mod = _modulation(c_all, w_mod[0], b_mod[0])
    mod3 = mod.reshape(8 * 6, 1, d)

    w = w_in[0]
    ssd_in = SSD_WIDTH + SSD_XBC + 2 * SSD_HEADS
    ml_main = 3 * ML_WIDTH
    n_gate = 2 * SSD_HEADS + 4 * ML_HEADS
    w_cat = jnp.concatenate([
        w[:, :SSD_WIDTH + SSD_XBC], w[:, ssd_in:ssd_in + ml_main],
        w[:, SSD_WIDTH + SSD_XBC:ssd_in], w[:, ssd_in + ml_main:],
        jnp.zeros((d, LANES - n_gate), F32)], axis=1).astype(BF16)
    z, xbc, ml, gates, gates_cm = _in_proj(x, ctx, mod3, norm1_w[0], w_cat)

    e_mats = []
    for direction in range(2):
        lane = jnp.arange(LANES)[:, None]
        head = (jnp.arange(SSD_WIDTH) // SSD_HEADDIM)[None, :]
        e_mats.append((lane == direction * SSD_HEADS + head).astype(BF16))
    cw = jnp.zeros((8, SSD_XBC), F32).at[:CONV_W].set(ssd_conv_w[0])
    dtb = _lane_pad(ssd_dt_bias[0])
    alog = _lane_pad(ssd_a_log[0])
    dsk = jnp.repeat(ssd_d[0], SSD_HEADDIM).reshape(1, SSD_WIDTH)
    shifts = _shift_matrices()
    y_ssd = _ssd_scans(xbc, gates, z, [cw, ssd_conv_b[0].reshape(1, -1), shifts],
                       [dtb, dtb.reshape(LANES, 1), alog, alog.reshape(LANES, 1)], e_mats, dsk,
                       ssd_norm_w[0].reshape(1, -1))

    w_rows = jnp.tile(ml_w_qk[0].reshape(2, ML_WIDTH, ML_QK_BLOCK), (1, 1, ML_WIDTH // ML_QK_BLOCK))
    blk_id = jnp.arange(ML_WIDTH) // ML_QK_BLOCK
    w_bd = jnp.where((blk_id[:, None] == blk_id[None, :])[None], w_rows, 0.0)
    wq = w_bd[0].astype(BF16)
    wk = (w_bd[1] * (ML_HEADDIM ** -0.5)).astype(BF16)
    mcw = jnp.zeros((8, ML_WIDTH), F32).at[:CONV_W].set(ml_conv_w[0])
    gb = _lane_pad(ml_gate_b[0], offset=2 * SSD_HEADS)
    y_ml_cm = _ml_scans(ml, gates_cm, [mcw, ml_conv_b[0].reshape(1, -1), shifts], [wq, wk], [gb, gb.reshape(LANES, 1)],
                        ml_norm_w[0].reshape(1, -1), ml_skip[0].reshape(1, -1))
    y_ml = _from_col_major(y_ml_cm, rows)

    rw = jnp.concatenate([moe_rg_w[0], moe_re_w[0],
                          jnp.zeros((d, LANES - MOE_GROUPS - MOE_EXPERTS), F32)], axis=1)
    rw_hi = rw.astype(BF16)
    rw_lo = (rw - rw_hi.astype(F32)).astype(BF16)
    rb = _lane_pad(jnp.concatenate([moe_rg_b[0], moe_re_b[0]]))
    rw2 = jnp.concatenate([rw_hi, rw_lo], axis=1)
    x1, t_slab, route = _out_proj(x, y_ssd, y_ml, w_out[0].astype(BF16), mod3, norm2_w[0], rw2, rb)

    return _moe(t_slab, x1, route, moe_w_gate[0], moe_w_up[0], moe_w_down[0], mod3, final_norm_w)
```

```python
import functools

import jax
import jax.numpy as jnp
from jax import lax
from jax.experimental import pallas as pl
from jax.experimental.pallas import tpu as pltpu

F32 = jnp.float32
BF16 = jnp.bfloat16
HIGHEST = lax.Precision.HIGHEST

D_MODEL = 1024
GRID_W = 64
EPS = 1e-6
CONV_W = 5
NEG_BIG = -1e30
CHUNK = 128
LANES = 128
HALO = 16
NB = 4

SSD_WIDTH = 512
SSD_HEADS = 8
SSD_HEADDIM = 64
SSD_GROUPS = 2
SSD_STATE = 128
SSD_XBC = SSD_WIDTH + 2 * SSD_GROUPS * SSD_STATE

ML_WIDTH = 512
ML_HEADS = 4
ML_HEADDIM = 128
ML_QK_BLOCK = 4

MOE_GROUPS = 4
MOE_EPG = 8
MOE_EXPERTS = 32
D_EXPERT = 256
ROUTE_LANE0 = MOE_GROUPS

TM_IN = 512
TM_FIN = 256
SLAB = D_MODEL // LANES
BM = 128
BLOCK_PARTS = 1
PART = BM // BLOCK_PARTS
PART_PITCH = PART + 8
SCATTER_GROUP = 4
VMEM_LIMIT = 48 * 1024 * 1024
VMEM_LIMIT_MOE = 56 * 1024 * 1024


def _silu(v):
    return v * jax.nn.sigmoid(v)


def _softplus(v):
    return jnp.maximum(v, 0.0) + jnp.log1p(jnp.exp(-jnp.abs(v)))


def _dot(a, b):
    return jnp.dot(a, b, preferred_element_type=F32)


def _dot_nt(a, b):
    return lax.dot_general(a, b, (((1,), (1,)), ((), ())), preferred_element_type=F32)


def _dot_hi(a, b):
    return jnp.dot(a, b, preferred_element_type=F32, precision=HIGHEST)


def _mod_kernel(c_ref, w_ref, b_ref, o_ref):
    c = c_ref[...]
    o_ref[...] = _dot_hi(_silu(c), w_ref[...]) + b_ref[...]


def _modulation(c_all, w_mod, b_mod):
    n = w_mod.shape[1]
    bn = 1536
    return pl.pallas_call(
        _mod_kernel,
        grid=(n // bn,),
        in_specs=[
            pl.BlockSpec((8, D_MODEL), lambda j: (0, 0)),
            pl.BlockSpec((D_MODEL, bn), lambda j: (0, j)),
            pl.BlockSpec((1, bn), lambda j: (0, j)),
        ],
        out_specs=pl.BlockSpec((8, bn), lambda j: (0, j)),
        out_shape=jax.ShapeDtypeStruct((8, n), F32),
        compiler_params=pltpu.CompilerParams(vmem_limit_bytes=VMEM_LIMIT),
        name="modulation",
    )(c_all, w_mod, b_mod.reshape(1, n))


W_TILE = 16
COL_Z, COL_XBC, COL_ML, COL_G = 0, SSD_WIDTH, SSD_WIDTH + SSD_XBC, SSD_WIDTH + SSD_XBC + 3 * ML_WIDTH
PROJ_CHUNK = 512


def _norm_mod(xin, sh_ref, sc_ref, nw_ref):
    ms = jnp.mean(xin * xin, axis=-1, keepdims=True)
    y = xin * lax.rsqrt(ms + EPS) * nw_ref[...]
    return (y * (1.0 + sc_ref[...]) + sh_ref[...]).astype(BF16)


def _in_lat_kernel(rows, x_ref, sh_ref, sc_ref, nw_ref, w_ref, z_ref, xbc_ref, ml_ref, g_ref, gml_ref, scr):
    tm = rows * W_TILE
    h = _norm_mod(x_ref[...].reshape(tm, D_MODEL), sh_ref, sc_ref, nw_ref)

    def proj(col, width=PROJ_CHUNK):
        return _dot(h, w_ref[:, col:col + width])

    def to_col_major(val, slab0, dst_ref, lo, dtype):
        n_slab = val.shape[1] // LANES
        for k in range(n_slab):
            scr[slab0 + k] = val[:, k * LANES:(k + 1) * LANES]
        for j in range(W_TILE):
            for k in range(n_slab):
                dst_ref[j, :, lo + k * LANES:lo + (k + 1) * LANES] = (
                    scr[slab0 + k, pl.ds(j, rows, stride=W_TILE), :].astype(dtype))

    slabs = PROJ_CHUNK // LANES
    g = proj(COL_G, LANES)
    g_ref[...] = g.reshape(rows, W_TILE, LANES)
    to_col_major(g, 0, gml_ref, 0, F32)
    for j in range(3 * ML_WIDTH // PROJ_CHUNK):
        to_col_major(proj(COL_ML + j * PROJ_CHUNK), 1 + j * slabs, ml_ref, j * PROJ_CHUNK, BF16)
    z_ref[...] = proj(COL_Z).astype(BF16).reshape(rows, W_TILE, PROJ_CHUNK)
    for j in range(SSD_XBC // PROJ_CHUNK):
        lo = j * PROJ_CHUNK
        xbc_ref[:, :, lo:lo + PROJ_CHUNK] = proj(COL_XBC + lo).astype(BF16).reshape(rows, W_TILE, PROJ_CHUNK)


def _in_ctx_kernel(x_ref, sh_ref, sc_ref, nw_ref, w_ref, xbc_ref, ml_ref, g_ref):
    h = _norm_mod(x_ref[...], sh_ref, sc_ref, nw_ref)
    for j in range(SSD_XBC // PROJ_CHUNK):
        lo = j * PROJ_CHUNK
        xbc_ref[:, lo:lo + PROJ_CHUNK] = _dot(h, w_ref[:, COL_XBC + lo:COL_XBC + lo + PROJ_CHUNK]).astype(BF16)
    for j in range(3 * ML_WIDTH // PROJ_CHUNK):
        lo = j * PROJ_CHUNK
        ml_ref[:, lo:lo + PROJ_CHUNK] = _dot(h, w_ref[:, COL_ML + lo:COL_ML + lo + PROJ_CHUNK]).astype(BF16)
    g_ref[...] = _dot(h, w_ref[:, COL_G:COL_G + LANES])


def _in_proj(x, ctx, mod3, norm_w, w_cat):
    b, seq, d = x.shape
    ctx_len = ctx.shape[1]
    rows = seq // GRID_W
    widths = (SSD_WIDTH, SSD_XBC, 3 * ML_WIDTH, LANES, LANES)
    dtypes = (BF16, BF16, BF16, F32, F32)
    params = pltpu.CompilerParams(dimension_semantics=("arbitrary", "arbitrary"), vmem_limit_bytes=VMEM_LIMIT)
    nw = norm_w.reshape(1, d)

    def raster(width):
        return pl.BlockSpec((None, rows, W_TILE, width), lambda bi, wi: (bi, 0, wi, 0))

    def col_major(width):
        return pl.BlockSpec((None, W_TILE, rows, width), lambda bi, wi: (bi, wi, 0, 0))

    outs = pl.pallas_call(
        functools.partial(_in_lat_kernel, rows),
        grid=(b, GRID_W // W_TILE),
        in_specs=[
            raster(d),
            pl.BlockSpec((None, 1, d), lambda bi, wi: (bi * 6, 0, 0)),
            pl.BlockSpec((None, 1, d), lambda bi, wi: (bi * 6 + 1, 0, 0)),
            _full((1, d)), _full(w_cat.shape),
        ],
        out_specs=[raster(SSD_WIDTH), raster(SSD_XBC), col_major(3 * ML_WIDTH), raster(LANES), col_major(LANES)],
        out_shape=[jax.ShapeDtypeStruct((b, GRID_W, rows, w) if cm else (b, rows, GRID_W, w), t)
                   for w, t, cm in zip(widths, dtypes, (False, False, True, False, True))],
        scratch_shapes=[pltpu.VMEM((1 + 3 * ML_WIDTH // LANES, rows * W_TILE, LANES), F32)],
        compiler_params=params,
        name="in_proj",
    )(x.reshape(b, rows, GRID_W, d), mod3, mod3, nw, w_cat)
    z, xbc, ml, gates, gates_cm = [o.reshape(b, seq, w) for o, w in zip(outs, widths)]

    ctx_row = b
    ctx_widths = (SSD_XBC, 3 * ML_WIDTH, LANES)
    xbc_c, ml_c, gates_c = pl.pallas_call(
        _in_ctx_kernel,
        grid=(b, 1),
        in_specs=[
            pl.BlockSpec((None, ctx_len, d), lambda bi, t: (bi, 0, 0)),
            pl.BlockSpec((None, 1, d), lambda bi, t: (ctx_row * 6, 0, 0)),
            pl.BlockSpec((None, 1, d), lambda bi, t: (ctx_row * 6 + 1, 0, 0)),
            _full((1, d)), _full(w_cat.shape),
        ],
        out_specs=[pl.BlockSpec((None, ctx_len, w), lambda bi, t: (bi, 0, 0)) for w in ctx_widths],
        out_shape=[jax.ShapeDtypeStruct((b, ctx_len, w), t) for w, t in zip(ctx_widths, (BF16, BF16, F32))],
        compiler_params=params,
        name="in_proj_ctx",
    )(ctx, mod3, mod3, nw, w_cat)
    return z, (xbc, xbc_c), (ml, ml_c), (gates, gates_c), (gates_cm, gates_c)


def _chunk_eff(rev, ncc, nc, c):
    nl = nc - ncc
    if not rev:
        return jnp.where(c < ncc, nl + c, c - ncc)
    return jnp.where(c < ncc, nc - 1 - c, nl - 1 - (c - ncc))


def _seq_ends(ncc, nc, ceff):
    nl = nc - ncc
    return (ceff == 0) | (ceff == nl), (ceff == nl - 1) | (ceff == nc - 1)


def _lat_block(rev, ncc, nc, c):
    nl = nc - ncc
    if not rev:
        return jnp.maximum(c - ncc, 0)
    return jnp.where(c < ncc, nl - 1, nl - 1 - (c - ncc))


CONV_SIDE_TAPS = tuple(t for t in range(CONV_W) if t != CONV_W // 2)


def _shift_matrices():
    row = jnp.arange(CHUNK)[:, None]
    col = jnp.arange(CHUNK + 2 * HALO)[None, :]
    return jnp.stack([col == row + HALO + tap - CONV_W // 2 for tap in CONV_SIDE_TAPS]).astype(BF16)


def _conv_silu(xm, xp, xn, cw_ref, cb_ref, sh_ref, first, last):
    xp = jnp.where(first, jnp.zeros_like(xp), xp)
    xn = jnp.where(last, jnp.zeros_like(xn), xn)
    ext = jnp.concatenate([xp, xm, xn], axis=0)
    mid = CONV_W // 2
    acc = cb_ref[...] + cw_ref[mid:mid + 1, :] * xm.astype(F32)
    for i, tap in enumerate(CONV_SIDE_TAPS):
        acc = acc + cw_ref[tap:tap + 1, :] * _dot(sh_ref[i], ext)
    return _silu(acc)


def _tri(rev, transposed=False):
    row = lax.broadcasted_iota(jnp.int32, (CHUNK, CHUNK), 0)
    col = lax.broadcasted_iota(jnp.int32, (CHUNK, CHUNK), 1)
    if transposed:
        row, col = col, row
    keep = (col >= row) if rev else (col <= row)
    return keep


def _split3(a):
    a1 = a.astype(BF16)
    r1 = a - a1.astype(F32)
    a2 = r1.astype(BF16)
    a3 = (r1 - a2.astype(F32)).astype(BF16)
    return a1, a2, a3


def _dot_sel_l(m_b, a):
    p = _split3(a)
    return _dot(m_b, p[0]) + _dot(m_b, p[1]) + _dot(m_b, p[2])


def _dot_sel_r(a, m_b):
    p = _split3(a)
    return _dot(p[0], m_b) + _dot(p[1], m_b) + _dot(p[2], m_b)


def _row_spec(rows, width, row_block, col_block=0):
    return pl.BlockSpec((NB, rows, width), lambda bi, c: (bi, row_block(c), col_block))


def _local_chunks(rev, ncc, nc, c):
    nl = nc - ncc
    cc = jnp.clip(ncc - 1 - c if rev else c, 0, ncc - 1)
    cl = jnp.clip(nl - 1 - (c - ncc) if rev else c - ncc, 0, nl - 1)
    return c < ncc, cc, cl


def _pair_specs(rev, ncc, nc, rows, width, col=0, halo=0):
    per = CHUNK // HALO

    def index(which, count):
        def fn(c):
            ch = _local_chunks(rev, ncc, nc, c)[which]
            if halo == 0:
                return ch
            if halo < 0:
                return jnp.maximum(ch * per - 1, 0)
            return jnp.minimum((ch + 1) * per, count * per - 1)
        return fn

    return [_row_spec(rows, width, index(2, nc - ncc), col), _row_spec(rows, width, index(1, ncc), col)]


def _pick(is_ctx, lat_ref, ctx_ref, s):
    return jnp.where(is_ctx, ctx_ref[s], lat_ref[s])


def _step_ends(rev, ncc, nc, c):
    is_ctx, cc, cl = _local_chunks(rev, ncc, nc, c)
    first = jnp.where(is_ctx, cc == 0, cl == 0)
    last = jnp.where(is_ctx, cc == ncc - 1, cl == nc - ncc - 1)
    return is_ctx, first, last


def _full(shape):
    return pl.BlockSpec(shape, lambda bi, c: (0,) * len(shape))


def _ssd_gates(rev, xs, g, dtbr_ref, dtbc_ref, alr_ref, alc_ref, e_ref):
    lane0 = SSD_HEADS * int(rev)
    lane = lax.broadcasted_iota(jnp.int32, (CHUNK, LANES), 1)
    lmask = (lane >= lane0) & (lane < lane0 + SSD_HEADS)
    dt = jnp.where(lmask, _softplus(g + dtbr_ref[...]), 0.0)
    a = dt * (-jnp.exp(alr_ref[...]))
    gt = g.T
    dt_t = _softplus(gt + dtbc_ref[...])[lane0:lane0 + SSD_HEADS]
    a_t = dt_t * (-jnp.exp(alc_ref[...][lane0:lane0 + SSD_HEADS]))

    cs = _dot_sel_l(_tri(rev).astype(BF16), a)
    cs_t = _dot_sel_r(a_t, _tri(rev, transposed=True).astype(BF16))
    e = e_ref[...]
    dtx = _dot_sel_r(dt, e)
    csx = _dot_sel_r(cs, e)
    end = 0 if rev else CHUNK - 1
    totx = csx[end:end + 1, :]
    ecsx = jnp.exp(csx)
    decx = jnp.exp(totx - csx)
    etotx = jnp.exp(totx)

    xdt = xs * dtx
    return dict(cs=cs, cs_t=cs_t, ecsx=ecsx, etotx=etotx, xdt_b=xdt.astype(BF16), xd_b=(xdt * decx).astype(BF16))


def _ssd_chunks(rev, xbcs, gs_, dtbr_ref, dtbc_ref, alr_ref, alc_ref, e_ref, s_ref):
    lane0 = SSD_HEADS * int(rev)
    keep = _tri(rev)
    half = lax.broadcasted_iota(jnp.int32, (CHUNK, LANES), 1) // SSD_HEADDIM
    n_bc = SSD_GROUPS * SSD_STATE
    hpg = SSD_HEADS // SSD_GROUPS
    gw = hpg * SSD_HEADDIM

    gq = [_ssd_gates(rev, xbc[:, :SSD_WIDTH], g, dtbr_ref, dtbc_ref, alr_ref, alc_ref, e_ref)
          for xbc, g in zip(xbcs, gs_)]
    units = []
    for s, xbc in enumerate(xbcs):
        for grp in range(SSD_GROUPS):
            bm = xbc[:, SSD_WIDTH + grp * SSD_STATE:SSD_WIDTH + (grp + 1) * SSD_STATE]
            cm = xbc[:, SSD_WIDTH + n_bc + grp * SSD_STATE:SSD_WIDTH + n_bc + (grp + 1) * SSD_STATE]
            units.append(dict(s=s, grp=grp, cols=slice(grp * gw, (grp + 1) * gw), bm=bm, bm_b=bm.astype(BF16),
                              cm_b=cm.astype(BF16), s_old=s_ref[s, :, grp * gw:(grp + 1) * gw]))
    for u in units:
        u["cb"] = _dot_nt(u["cm_b"], u["bm_b"])
        u["y_off"] = _dot(u["cm_b"], u["s_old"].astype(BF16)) * gq[u["s"]]["ecsx"][:, u["cols"]]
        u["bt_b"] = u["bm"].T.astype(BF16)
    for u in units:
        q_ = gq[u["s"]]
        masks = []
        for hh in range(hpg):
            h = u["grp"] * hpg + hh
            dl = q_["cs"][:, lane0 + h:lane0 + h + 1] - q_["cs_t"][h:h + 1, :]
            masks.append((u["cb"] * jnp.exp(jnp.where(keep, dl, NEG_BIG))).astype(BF16))
        u["masks"] = masks
    for u in units:
        q_ = gq[u["s"]]
        blocks = []
        for pair in range(hpg // 2):
            blk = u["grp"] * (hpg // 2) + pair
            xj = q_["xdt_b"][:, blk * LANES:(blk + 1) * LANES]
            acc = u["y_off"][:, pair * LANES:(pair + 1) * LANES]
            for q in range(2):
                acc = acc + _dot(u["masks"][pair * 2 + q], jnp.where(half == q, xj, jnp.zeros_like(xj)))
            blocks.append(acc)
        u["y"] = blocks
    for u in units:
        q_ = gq[u["s"]]
        s_new = u["s_old"] * q_["etotx"][:, u["cols"]] + _dot(u["bt_b"], q_["xd_b"][:, u["cols"]])
        s_ref[u["s"], :, u["cols"]] = s_new
    return [jnp.concatenate([blk for u in units if u["s"] == s for blk in u["y"]], axis=1)
            for s in range(len(xbcs))]


def _ssd_rev_kernel(ncc, nc, xml_ref, xmc_ref, xpl_ref, xpc_ref, xnl_ref, xnc_ref, gl_ref, gc_ref, cw_ref, cb_ref,
                    sh_ref, dtbr_ref, dtbc_ref, alr_ref, alc_ref, e_ref, o_ref, xc_ref, s_ref):
    c = pl.program_id(1)
    is_ctx, first, last = _step_ends(True, ncc, nc, c)

    @pl.when(c == 0)
    def _():
        s_ref[...] = jnp.zeros_like(s_ref)

    xbcs = [_conv_silu(_pick(is_ctx, xml_ref, xmc_ref, s), _pick(is_ctx, xpl_ref, xpc_ref, s),
                       _pick(is_ctx, xnl_ref, xnc_ref, s), cw_ref, cb_ref, sh_ref, first, last) for s in range(NB)]
    for s in range(NB):
        xc_ref[s] = xbcs[s].astype(BF16)
    ys = _ssd_chunks(True, xbcs, [_pick(is_ctx, gl_ref, gc_ref, s) for s in range(NB)], dtbr_ref, dtbc_ref,
                     alr_ref, alc_ref, e_ref, s_ref)
    for s in range(NB):
        o_ref[s] = ys[s].astype(o_ref.dtype)


def _ssd_fwd_kernel(ncc, xc_ref, gl_ref, gc_ref, dtbr_ref, dtbc_ref, alr_ref, alc_ref, e_ref, dsk_ref, yb_ref,
                    z_ref, nw_ref, o_ref, s_ref):
    is_ctx = pl.program_id(1) < ncc

    @pl.when(pl.program_id(1) == 0)
    def _():
        s_ref[...] = jnp.zeros_like(s_ref)

    xbcs = [xc_ref[s].astype(F32) for s in range(NB)]
    ys = _ssd_chunks(False, xbcs, [_pick(is_ctx, gl_ref, gc_ref, s) for s in range(NB)], dtbr_ref, dtbc_ref,
                     alr_ref, alc_ref, e_ref, s_ref)
    ys = [(ys[s] + yb_ref[s].astype(F32) + dsk_ref[...] * xbcs[s][:, :SSD_WIDTH]) * _silu(z_ref[s].astype(F32))
          for s in range(NB)]
    scale = [lax.rsqrt(jnp.mean(y * y, axis=-1, keepdims=True) + EPS) for y in ys]
    for s in range(NB):
        o_ref[s] = (ys[s] * scale[s] * nw_ref[...]).astype(o_ref.dtype)


def _ssd_scans(xbc, gates, z, conv_params, gate_params, e_mats, dsk, nw):
    b, seq, _ = xbc[0].shape
    ncc = xbc[1].shape[1] // CHUNK
    nc = seq // CHUNK + ncc
    state = pltpu.VMEM((NB, SSD_STATE, SSD_WIDTH), F32)
    params = pltpu.CompilerParams(dimension_semantics=("arbitrary", "arbitrary"), vmem_limit_bytes=VMEM_LIMIT)
    lat_shape = jax.ShapeDtypeStruct((b, seq, SSD_WIDTH), BF16)

    ceff = functools.partial(_chunk_eff, True, ncc, nc)
    lat = functools.partial(_lat_block, True, ncc, nc)
    pair = functools.partial(_pair_specs, True, ncc, nc)
    rev_params = list(conv_params) + list(gate_params) + [e_mats[1]]
    yb, xbc_act = pl.pallas_call(
        functools.partial(_ssd_rev_kernel, ncc, nc),
        grid=(b // NB, nc),
        in_specs=pair(CHUNK, SSD_XBC) + pair(HALO, SSD_XBC, halo=-1) + pair(HALO, SSD_XBC, halo=1)
        + pair(CHUNK, LANES) + [_full(p.shape) for p in rev_params],
        out_specs=[_row_spec(CHUNK, SSD_WIDTH, lat), _row_spec(CHUNK, SSD_XBC, ceff)],
        out_shape=(lat_shape, jax.ShapeDtypeStruct((b, nc * CHUNK, SSD_XBC), BF16)),
        scratch_shapes=[state],
        compiler_params=params,
        name="ssd_rev",
    )(*xbc, *xbc, *xbc, *gates, *rev_params)

    ceff = functools.partial(_chunk_eff, False, ncc, nc)
    lat = functools.partial(_lat_block, False, ncc, nc)
    fwd_params = list(gate_params) + [e_mats[0], dsk]
    return pl.pallas_call(
        functools.partial(_ssd_fwd_kernel, ncc),
        grid=(b // NB, nc),
        in_specs=[_row_spec(CHUNK, SSD_XBC, ceff)] + _pair_specs(False, ncc, nc, CHUNK, LANES)
        + [_full(p.shape) for p in fwd_params]
        + [_row_spec(CHUNK, SSD_WIDTH, lat), _row_spec(CHUNK, SSD_WIDTH, lat), _full(nw.shape)],
        out_specs=_row_spec(CHUNK, SSD_WIDTH, lat),
        out_shape=lat_shape,
        scratch_shapes=[state],
        compiler_params=params,
        name="ssd_fwd",
    )(xbc_act, *gates, *fwd_params, yb, z, nw)


ML_I_LANE0 = 2 * SSD_HEADS
ML_F_LANE0 = ML_I_LANE0 + 2 * ML_HEADS


def _ml_gates(rev, g, gbr_ref, gbc_ref):
    i_lane0 = ML_I_LANE0 + ML_HEADS * int(rev)
    f_lane0 = ML_F_LANE0 + ML_HEADS * int(rev)
    ga = g + gbr_ref[...]
    lane = lax.broadcasted_iota(jnp.int32, (CHUNK, LANES), 1)
    logf = jnp.where((lane >= f_lane0) & (lane < f_lane0 + ML_HEADS), -_softplus(-ga), 0.0)
    cs = _dot_sel_l(_tri(rev).astype(BF16), logf)
    gt = g.T + gbc_ref[...]
    i_t = gt[ML_I_LANE0:ML_F_LANE0]
    logf_t = -_softplus(-gt[ML_F_LANE0:ML_F_LANE0 + 2 * ML_HEADS])
    cs_t = _dot_sel_r(logf_t, _tri(rev, transposed=True).astype(BF16))
    end = 0 if rev else CHUNK - 1

    u_t = i_t - cs_t
    u_c = ga - pltpu.roll(cs, LANES - (ML_F_LANE0 - ML_I_LANE0), axis=1)
    row = lax.broadcasted_iota(jnp.int32, (CHUNK, LANES), 0)
    pm = u_c
    step = 1
    while step < CHUNK:
        if rev:
            pm = jnp.maximum(pm, jnp.where(row < CHUNK - step, pltpu.roll(pm, CHUNK - step, axis=0), NEG_BIG))
        else:
            pm = jnp.maximum(pm, jnp.where(row >= step, pltpu.roll(pm, step, axis=0), NEG_BIG))
        step *= 2

    heads = []
    for h in range(ML_HEADS):
        r = ML_HEADS * int(rev) + h
        li = i_lane0 + h
        heads.append(dict(
            csc=cs[:, f_lane0 + h:f_lane0 + h + 1], u_col=u_c[:, li:li + 1], u_row=u_t[r:r + 1, :],
            tot=cs_t[r:r + 1, end:end + 1], u_max=pm[end:end + 1, li:li + 1], pm_col=pm[:, li:li + 1]))
    return heads


def _ml_chunks(rev, samples, gbr_ref, gbc_ref, c_ref, n_ref, mx_ref):
    keep = _tri(rev)
    ones_b = jnp.ones((CHUNK, ML_HEADDIM), BF16)
    units = []
    for s, (k, q_b, k_b, v_b, g) in enumerate(samples):
        for h, gq in enumerate(_ml_gates(rev, g, gbr_ref, gbc_ref)):
            sl = slice(h * ML_HEADDIM, (h + 1) * ML_HEADDIM)
            units.append(dict(gq, s=s, h=h, kh=k[:, sl], vh=v_b[:, sl], qh_b=q_b[:, sl], kh_b=k_b[:, sl],
                              m_prev=mx_ref[s, h:h + 1, 0:1], c_prev=c_ref[s * ML_HEADS + h],
                              n_prev=n_ref[s, h:h + 1, :]))

    for u in units:
        u["qk"] = _dot_nt(u["qh_b"], u["kh_b"])
        u["vt_b"] = u["vh"].astype(F32).T.astype(BF16)
    for u in units:
        u["mm"] = jnp.maximum(u["m_prev"], u["pm_col"])
        u["scores_b"] = (u["qk"] * jnp.exp(jnp.where(keep, u["u_row"] - u["mm"], NEG_BIG))).astype(BF16)
    for u in units:
        c_aug = jnp.concatenate([u["c_prev"], jnp.broadcast_to(u["n_prev"], (CHUNK, ML_HEADDIM))], axis=0)
        u["intra"] = _dot(u["scores_b"], jnp.concatenate([u["vh"], ones_b], axis=1))
        u["inter"] = _dot_nt(u["qh_b"], c_aug.astype(BF16))
    outs = [[None] * ML_HEADS for _ in samples]
    for u in units:
        both = u["intra"] + jnp.exp(u["m_prev"] - u["mm"]) * u["inter"]
        den = both[:, ML_HEADDIM:ML_HEADDIM + 1]
        outs[u["s"]][u["h"]] = both[:, :ML_HEADDIM] / jnp.maximum(jnp.abs(den), jnp.exp(-(u["csc"] + u["mm"])))
    for u in units:
        kw = u["kh"] * jnp.exp(u["u_col"] - u["u_max"])
        u["c_loc"] = _dot(u["vt_b"], kw.astype(BF16))
        u["n_loc"] = jnp.sum(kw, axis=0, keepdims=True)
    for u in units:
        s, h, tot, m_prev = u["s"], u["h"], u["tot"], u["m_prev"]
        m_loc = tot + u["u_max"]
        m_new = jnp.maximum(tot + m_prev, m_loc)
        s_prev = jnp.exp(tot + m_prev - m_new)
        s_loc = jnp.exp(m_loc - m_new)
        c_ref[s * ML_HEADS + h] = s_prev * u["c_prev"] + s_loc * u["c_loc"]
        n_ref[s, h:h + 1, :] = s_prev * u["n_prev"] + s_loc * u["n_loc"]
        mx_ref[s, h:h + 1, :] = jnp.broadcast_to(m_new, (1, LANES))
    return outs


def _ml_init_state(c_ref, n_ref, mx_ref):
    c_ref[...] = jnp.zeros_like(c_ref)
    n_ref[...] = jnp.zeros_like(n_ref)
    mx_ref[...] = jnp.full(mx_ref.shape, NEG_BIG, F32)


def _ml_rev_kernel(ncc, nc, xml_ref, xmc_ref, xpl_ref, xpc_ref, xnl_ref, xnc_ref, vl_ref, vc_ref, gl_ref, gc_ref,
                   cw_ref, cb_ref, sh_ref, wq_ref, wk_ref, gbr_ref, gbc_ref, o_ref, xc_ref, q_ref, k_ref, c_ref, n_ref,
                   mx_ref):
    c = pl.program_id(1)
    is_ctx, first, last = _step_ends(True, ncc, nc, c)

    @pl.when(c == 0)
    def _():
        _ml_init_state(c_ref, n_ref, mx_ref)

    samples = []
    for s in range(NB):
        xconv = _conv_silu(_pick(is_ctx, xml_ref, xmc_ref, s), _pick(is_ctx, xpl_ref, xpc_ref, s),
                           _pick(is_ctx, xnl_ref, xnc_ref, s), cw_ref, cb_ref, sh_ref, first, last)
        xc_b = xconv.astype(BF16)
        q = _dot(xc_b, wq_ref[...])
        k = _dot(xc_b, wk_ref[...])
        q_b, k_b = q.astype(BF16), k.astype(BF16)
        xc_ref[s] = xc_b
        q_ref[s] = q_b
        k_ref[s] = k_b
        samples.append((k, q_b, k_b, _pick(is_ctx, vl_ref, vc_ref, s), _pick(is_ctx, gl_ref, gc_ref, s)))
    outs = _ml_chunks(True, samples, gbr_ref, gbc_ref, c_ref, n_ref, mx_ref)
    for s in range(NB):
        o_ref[s] = jnp.concatenate(outs[s], axis=1).astype(o_ref.dtype)


def _ml_fwd_kernel(ncc, xc_ref, q_ref, k_ref, vl_ref, vc_ref, og_ref, gl_ref, gc_ref, gbr_ref, gbc_ref, hb_ref,
                   nw_ref, sk_ref, o_ref, c_ref, n_ref, mx_ref):
    is_ctx = pl.program_id(1) < ncc

    @pl.when(pl.program_id(1) == 0)
    def _():
        _ml_init_state(c_ref, n_ref, mx_ref)

    samples = []
    for s in range(NB):
        q_b, k_b = q_ref[s], k_ref[s]
        samples.append((k_b.astype(F32), q_b, k_b, _pick(is_ctx, vl_ref, vc_ref, s),
                        _pick(is_ctx, gl_ref, gc_ref, s)))
    outs = _ml_chunks(False, samples, gbr_ref, gbc_ref, c_ref, n_ref, mx_ref)
    gated = []
    for s in range(NB):
        for h in range(ML_HEADS):
            sl = slice(h * ML_HEADDIM, (h + 1) * ML_HEADDIM)
            gated.append(jax.nn.sigmoid(og_ref[s, :, sl].astype(F32)) * (outs[s][h] + hb_ref[s, :, sl].astype(F32)))
    scale = [lax.rsqrt(jnp.mean(hh * hh, axis=-1, keepdims=True) + EPS) for hh in gated]
    normed = [hh * sc for hh, sc in zip(gated, scale)]
    for s in range(NB):
        y = jnp.concatenate(normed[s * ML_HEADS:(s + 1) * ML_HEADS], axis=1) * nw_ref[...]
        o_ref[s] = (y + sk_ref[...] * xc_ref[s].astype(F32)).astype(o_ref.dtype)


def _ml_scans(ml, gates, conv_params, proj_params, gate_params, nw, sk):
    b, seq, _ = ml[0].shape
    ncc = ml[1].shape[1] // CHUNK
    nc = seq // CHUNK + ncc
    scratch = [pltpu.VMEM((NB * ML_HEADS, ML_HEADDIM, ML_HEADDIM), F32),
               pltpu.VMEM((NB, 8, ML_HEADDIM), F32),
               pltpu.VMEM((NB, 8, LANES), F32)]
    params = pltpu.CompilerParams(dimension_semantics=("arbitrary", "arbitrary"), vmem_limit_bytes=VMEM_LIMIT)
    lat_shape = jax.ShapeDtypeStruct((b, seq, ML_WIDTH), BF16)
    act_shape = jax.ShapeDtypeStruct((b, nc * CHUNK, ML_WIDTH), BF16)

    ceff = functools.partial(_chunk_eff, True, ncc, nc)
    lat = functools.partial(_lat_block, True, ncc, nc)
    pair = functools.partial(_pair_specs, True, ncc, nc)
    rev_params = list(conv_params) + list(proj_params) + list(gate_params)
    act_spec = _row_spec(CHUNK, ML_WIDTH, ceff)
    hb, xc, q, k = pl.pallas_call(
        functools.partial(_ml_rev_kernel, ncc, nc),
        grid=(b // NB, nc),
        in_specs=pair(CHUNK, ML_WIDTH) + pair(HALO, ML_WIDTH, halo=-1) + pair(HALO, ML_WIDTH, halo=1)
        + pair(CHUNK, ML_WIDTH, col=1) + pair(CHUNK, LANES) + [_full(p.shape) for p in rev_params],
        out_specs=[_row_spec(CHUNK, ML_WIDTH, lat), act_spec, act_spec, act_spec],
        out_shape=(lat_shape, act_shape, act_shape, act_shape),
        scratch_shapes=scratch,
        compiler_params=params,
        name="mlstm_rev",
    )(*ml, *ml, *ml, *ml, *gates, *rev_params)

    ceff = functools.partial(_chunk_eff, False, ncc, nc)
    lat = functools.partial(_lat_block, False, ncc, nc)
    pair = functools.partial(_pair_specs, False, ncc, nc)
    act_spec = _row_spec(CHUNK, ML_WIDTH, ceff)
    return pl.pallas_call(
        functools.partial(_ml_fwd_kernel, ncc),
        grid=(b // NB, nc),
        in_specs=[act_spec, act_spec, act_spec] + pair(CHUNK, ML_WIDTH, col=1)
        + [_row_spec(CHUNK, ML_WIDTH, lat, 2)] + pair(CHUNK, LANES)
        + [_full(p.shape) for p in gate_params]
        + [_row_spec(CHUNK, ML_WIDTH, lat), _full(nw.shape), _full(sk.shape)],
        out_specs=_row_spec(CHUNK, ML_WIDTH, lat),
        out_shape=lat_shape,
        scratch_shapes=scratch,
        compiler_params=params,
        name="mlstm_fwd",
    )(xc, q, k, *ml, ml[0], *gates, *gate_params, hb, nw, sk)


def _out_kernel(x_ref, ys_ref, ym_ref, wo_ref, g1_ref, sh_ref, sc_ref, nw_ref, rw_ref, rb_ref,
                x1_ref, ts_ref, route_ref):
    mix = _dot(ys_ref[...], wo_ref[0:SSD_WIDTH, :]) + _dot(ym_ref[...], wo_ref[SSD_WIDTH:, :])
    x1 = x_ref[...] + g1_ref[...] * mix
    y = x1 * lax.rsqrt(jnp.mean(x1 * x1, axis=-1, keepdims=True) + EPS) * nw_ref[...]
    t = y * (1.0 + sc_ref[...]) + sh_ref[...]
    x1_ref[...] = x1
    for j in range(SLAB):
        ts_ref[pl.ds(j, TM_IN, stride=SLAB), :] = t[:, j * LANES:(j + 1) * LANES]
    lg2 = _dot(t.astype(BF16), rw_ref[...])
    lg = lg2[:, :LANES] + lg2[:, LANES:] + rb_ref[...]

    lane = lax.broadcasted_iota(jnp.int32, lg.shape, 1).astype(F32)
    gmask = lane < MOE_GROUPS
    gl = jnp.where(gmask, lg, NEG_BIG)
    gmax = jnp.max(gl, axis=1, keepdims=True)
    g_sel = jnp.min(jnp.where(gmask & (gl == gmax), lane, 1e9), axis=1, keepdims=True)
    p_group = 1.0 / jnp.sum(jnp.where(gmask, jnp.exp(gl - gmax), 0.0), axis=1, keepdims=True)
    lo = ROUTE_LANE0 + MOE_EPG * g_sel
    emask = (lane >= lo) & (lane < lo + MOE_EPG)
    l1 = jnp.max(jnp.where(emask, lg, NEG_BIG), axis=1, keepdims=True)
    i1 = jnp.min(jnp.where(emask & (lg == l1), lane, 1e9), axis=1, keepdims=True)
    emask2 = emask & (lane != i1)
    l2 = jnp.max(jnp.where(emask2, lg, NEG_BIG), axis=1, keepdims=True)
    i2 = jnp.min(jnp.where(emask2 & (lg == l2), lane, 1e9), axis=1, keepdims=True)
    r = jnp.exp(l2 - l1)
    w1 = p_group / (1.0 + r)
    w2 = p_group * r / (1.0 + r)
    route = (jnp.where(lane == 0, i1, 0.0) + jnp.where(lane == 1, i2, 0.0)
             + jnp.where(lane == 2, w1, 0.0) + jnp.where(lane == 3, w2, 0.0))
    route_ref[...] = route.T[0:8, :]


def _out_proj(x, y_ssd, y_ml, w_out_b, mod3, norm_w, rw2, rb):
    b, seq, d = x.shape
    nt = seq // TM_IN

    def row(j):
        return pl.BlockSpec((None, 1, d), lambda bi, t: (bi * 6 + j, 0, 0))

    def tile(width):
        return pl.BlockSpec((None, TM_IN, width), lambda bi, t: (bi, t, 0))

    return pl.pallas_call(
        _out_kernel,
        grid=(b, nt),
        in_specs=[tile(d), tile(SSD_WIDTH), tile(ML_WIDTH), _full(w_out_b.shape),
                  row(2), row(3), row(4), _full((1, d)), _full(rw2.shape), _full(rb.shape)],
        out_specs=[tile(d),
                   pl.BlockSpec((None, TM_IN * SLAB, LANES), lambda bi, t: (bi, t, 0)),
                   pl.BlockSpec((None, 8, TM_IN), lambda bi, t: (bi, 0, t))],
        out_shape=(jax.ShapeDtypeStruct((b, seq, d), F32),
                   jax.ShapeDtypeStruct((b, seq * SLAB, LANES), F32),
                   jax.ShapeDtypeStruct((b, 8, seq), F32)),
        compiler_params=pltpu.CompilerParams(
            dimension_semantics=("arbitrary", "arbitrary"), vmem_limit_bytes=VMEM_LIMIT),
        name="out_proj_router",
    )(x, y_ssd, y_ml, w_out_b, mod3, mod3, mod3, norm_w.reshape(1, d), rw2, rb)


def _route_tables(route, seq):
    b = route.shape[0]
    n_inst = 2 * seq
    nblk = n_inst // BM + MOE_EXPERTS
    e_flat = (route[:, 0:2, :].astype(jnp.int32) - ROUTE_LANE0).reshape(b, n_inst)
    w_flat = route[:, 2:4, :].reshape(b, n_inst)
    tok_flat = jnp.tile(jnp.arange(seq, dtype=jnp.int32), 2 * b).reshape(b, n_inst)
    _, tok_sorted, w_sorted = lax.sort((e_flat, tok_flat, w_flat), dimension=1, num_keys=1)
    experts = jnp.arange(MOE_EXPERTS, dtype=jnp.int32)
    counts = jnp.sum((e_flat[:, None, :] == experts[None, :, None]).astype(jnp.int32), axis=2)
    nblk_e = (counts + BM - 1) // BM
    blk_end = jnp.cumsum(nblk_e, axis=1)
    blk_start = blk_end - nblk_e
    cnt_start = jnp.cumsum(counts, axis=1) - counts
    nb = blk_end[:, -1:]
    j = jnp.arange(nblk, dtype=jnp.int32)[None, :]
    valid_blk = j < nb
    jj = jnp.minimum(j, nb - 1)
    e_j = jnp.sum((jj[:, :, None] >= blk_end[:, None, :]).astype(jnp.int32), axis=2)
    onehot = (e_j[:, :, None] == experts[None, None, :]).astype(jnp.int32)
    take = lambda tbl: jnp.sum(onehot * tbl[:, None, :], axis=2)
    r = jnp.arange(BM, dtype=jnp.int32)[None, None, :]
    rank = ((jj - take(blk_start)) * BM)[:, :, None] + r
    valid = valid_blk[:, :, None] & (rank < take(counts)[:, :, None])
    sidx = jnp.clip(take(cnt_start)[:, :, None] + rank, 0, n_inst - 1).reshape(b, nblk * BM)
    tok = jnp.take_along_axis(tok_sorted, sidx, axis=1).reshape(b, nblk, BM)
    wslot = jnp.take_along_axis(w_sorted, sidx, axis=1).reshape(b, nblk, BM)
    tok = jnp.where(valid, tok, seq + r) * SLAB
    wslot = jnp.where(valid, wslot, 0.0)
    return (nblk_e.reshape(-1), blk_start.reshape(-1), tok.reshape(b, 1, nblk * BM), wslot)


SEM_T, SEM_X, SEM_O = 0, 1, 3


def _moe_kernel(seq, nbe_ref, bs_ref, tok_ref, ws_ref, wg_ref, wu_ref, wd_ref, g2_ref, fw_ref,
                t_hbm, x1_hbm, o_hbm, t_scr, y_scr, wgb, wub, wdb, xt, ot, xin, stage, sems):
    b = pl.program_id(0)
    e = pl.program_id(1)
    nb = pl.num_programs(0)
    rows = seq * SLAB
    n_fin = seq // TM_FIN

    def t_copy(sample):
        return pltpu.make_async_copy(t_hbm.at[sample], t_scr.at[pl.ds(0, rows)], sems.at[SEM_T])

    def x1_copy(s, slot):
        return pltpu.make_async_copy(x1_hbm.at[b, pl.ds(s * TM_FIN, TM_FIN)], xin.at[slot], sems.at[SEM_X + slot])

    def out_copy(s, slot):
        return pltpu.make_async_copy(stage.at[slot], o_hbm.at[b, pl.ds(s * TM_FIN, TM_FIN)], sems.at[SEM_O + slot])

    @pl.when((e == 0) & (b == 0))
    def _():
        t_copy(b).start()
        y_scr[...] = jnp.zeros_like(y_scr)
        t_scr[pl.ds(rows, BM * SLAB), :] = jnp.zeros((BM * SLAB, LANES), F32)

    @pl.when(e == 0)
    def _():
        t_copy(b).wait()

    n_blocks = nbe_ref[b * MOE_EXPERTS + e]
    blk0 = bs_ref[b * MOE_EXPERTS + e]

    @pl.when(n_blocks > 0)
    def _():
        wgb[...] = wg_ref[...].astype(BF16)
        wub[...] = wu_ref[...].astype(BF16)
        wdb[...] = wd_ref[...].astype(BF16)
        g2 = g2_ref[...]
        diag = (lax.broadcasted_iota(jnp.int32, (BM, BM), 0) == lax.broadcasted_iota(jnp.int32, (BM, BM), 1))

        def block(i, carry):
            blk = blk0 + i
            base = blk * BM

            tok_blk = tok_ref.at[0, pl.ds(base, BM)]

            def slab_rows(r):
                return pl.ds(pl.multiple_of(tok_blk[r], SLAB), SLAB)

            w_col = jnp.sum(jnp.where(diag, ws_ref[pl.ds(blk, 1), :], 0.0), axis=1, keepdims=True)
            parts = range(BLOCK_PARTS)
            for p in parts:
                for r in range(PART):
                    xt[p, pl.ds(r, SLAB, stride=PART_PITCH), :] = t_scr[slab_rows(p * PART + r), :]
            for p in parts:
                x = jnp.concatenate([xt[p, c * PART_PITCH:c * PART_PITCH + PART, :] for c in range(SLAB)],
                                    axis=1).astype(BF16)
                hidden = _silu(_dot(x, wgb[...])) * _dot(x, wub[...]) * w_col[p * PART:(p + 1) * PART]
                out = _dot(hidden.astype(BF16), wdb[...]) * g2
                for c in range(SLAB):
                    ot[p, c * PART_PITCH:c * PART_PITCH + PART, :] = out[:, c * LANES:(c + 1) * LANES]
            for p in parts:
                for r0 in range(0, PART, SCATTER_GROUP):
                    sl = [slab_rows(p * PART + r0 + u) for u in range(SCATTER_GROUP)]
                    vals = [y_scr[sl[u], :] + ot[p, pl.ds(r0 + u, SLAB, stride=PART_PITCH), :]
                            for u in range(SCATTER_GROUP)]
                    for u in range(SCATTER_GROUP):
                        y_scr[sl[u], :] = vals[u]
            return carry

        lax.fori_loop(0, n_blocks, block, 0)

    @pl.when(e == MOE_EXPERTS - 1)
    def _():
        @pl.when(b + 1 < nb)
        def _():
            t_copy(b + 1).start()

        x1_copy(0, 0).start()

        def tile(s, carry):
            slot = s % 2
            x1_copy(s, slot).wait()

            @pl.when(s + 1 < n_fin)
            def _():
                x1_copy(s + 1, 1 - slot).start()

            @pl.when(s >= 2)
            def _():
                out_copy(s - 2, slot).wait()

            base = pl.multiple_of(s * (TM_FIN * SLAB), TM_FIN * SLAB)
            x1 = xin[slot]
            chunks = [x1[:, c * LANES:(c + 1) * LANES] + y_scr[pl.ds(base + c, TM_FIN, stride=SLAB), :]
                      for c in range(SLAB)]
            y_scr[pl.ds(base, TM_FIN * SLAB), :] = jnp.zeros((TM_FIN * SLAB, LANES), F32)
            ssq = chunks[0] * chunks[0]
            for c in range(1, SLAB):
                ssq = ssq + chunks[c] * chunks[c]
            inv = lax.rsqrt(jnp.sum(ssq, axis=1, keepdims=True) * (1.0 / D_MODEL) + EPS)
            for c in range(SLAB):
                stage[slot, :, c * LANES:(c + 1) * LANES] = chunks[c] * inv * fw_ref[:, c * LANES:(c + 1) * LANES]
            out_copy(s, slot).start()
            return carry

        lax.fori_loop(0, n_fin, tile, 0)
        out_copy(n_fin - 2, n_fin % 2).wait()
        out_copy(n_fin - 1, (n_fin - 1) % 2).wait()


def _moe(t_slab, x1, route, wg, wu, wd, mod3, final_w):
    b, seq, d = x1.shape
    nblk_e, blk_start, tok, wslot = _route_tables(route, seq)
    nblk = wslot.shape[1]

    def w_spec(shape):
        return pl.BlockSpec((None,) + shape, lambda bi, e, nbe, bs: (e, 0, 0))

    grid_spec = pltpu.PrefetchScalarGridSpec(
        num_scalar_prefetch=2,
        grid=(b, MOE_EXPERTS),
        in_specs=[
            pl.BlockSpec((None, 1, nblk * BM), lambda bi, e, nbe, bs: (bi, 0, 0), memory_space=pltpu.SMEM),
            pl.BlockSpec((None, nblk, BM), lambda bi, e, nbe, bs: (bi, 0, 0)),
            w_spec((d, D_EXPERT)), w_spec((d, D_EXPERT)), w_spec((D_EXPERT, d)),
            pl.BlockSpec((None, 1, d), lambda bi, e, nbe, bs: (bi * 6 + 5, 0, 0)),
            pl.BlockSpec((1, d), lambda bi, e, nbe, bs: (0, 0)),
            pl.BlockSpec(memory_space=pl.ANY),
            pl.BlockSpec(memory_space=pl.ANY),
        ],
        out_specs=pl.BlockSpec(memory_space=pl.ANY),
        scratch_shapes=[
            pltpu.VMEM(((seq + BM) * SLAB, LANES), F32),
            pltpu.VMEM(((seq + BM) * SLAB, LANES), F32),
            pltpu.VMEM((d, D_EXPERT), BF16),
            pltpu.VMEM((d, D_EXPERT), BF16),
            pltpu.VMEM((D_EXPERT, d), BF16),
            pltpu.VMEM((BLOCK_PARTS, SLAB * PART_PITCH, LANES), F32),
            pltpu.VMEM((BLOCK_PARTS, SLAB * PART_PITCH, LANES), F32),
            pltpu.VMEM((2, TM_FIN, d), F32),
            pltpu.VMEM((2, TM_FIN, d), F32),
            pltpu.SemaphoreType.DMA((5,)),
        ],
    )
    return pl.pallas_call(
        functools.partial(_moe_kernel, seq),
        grid_spec=grid_spec,
        out_shape=jax.ShapeDtypeStruct((b, seq, d), F32),
        compiler_params=pltpu.CompilerParams(
            dimension_semantics=("arbitrary", "arbitrary"), vmem_limit_bytes=VMEM_LIMIT_MOE),
        name="experts_final_norm",
    )(nblk_e, blk_start, tok, wslot, wg, wu, wd, mod3, final_w.reshape(1, d), t_slab, x1)


def _lane_pad(v, offset=0):
    v = v.reshape(-1).astype(F32)
    return jnp.zeros((1, LANES), F32).at[0, offset:offset + v.shape[0]].set(v)


def _to_col_major(t, rows):
    b, length, ch = t.shape
    return t.reshape(b, rows, GRID_W, ch).transpose(0, 2, 1, 3).reshape(b, length, ch)


def _from_col_major(t, rows):
    b, length, ch = t.shape
    return t.reshape(b, GRID_W, rows, ch).transpose(0, 2, 1, 3).reshape(b, length, ch)


def kernel(x, c, ctx, c_ctx, w_mod, b_mod, norm1_w, w_in, ssd_conv_w, ssd_conv_b, ssd_dt_bias, ssd_a_log, ssd_d, ssd_norm_w, ml_conv_w, ml_conv_b, ml_w_qk, ml_gate_b, ml_norm_w, ml_skip, w_out, norm2_w, moe_rg_w, moe_rg_b, moe_re_w, moe_re_b, moe_w_gate, moe_w_up, moe_w_down, final_norm_w):
    b, seq, d = x.shape
    ctx_len = ctx.shape[1]
    rows = seq // GRID_W
    ncc = ctx_len // CHUNK
    assert w_mod.shape[0] == 1 and d == D_MODEL and b + 1 <= 8 and b % NB == 0
    assert seq % TM_IN == 0 and ctx_len % CHUNK == 0 and seq % TM_FIN == 0 and (2 * seq) % BM == 0
    assert seq == rows * GRID_W and GRID_W % W_TILE == 0 and rows % 8 == 0

    c_all = jnp.zeros((8, d), F32).at[:b].set(c).at[b].set(c_ctx)
    mod = _modulation(c_all, w_mod[0], b_mod[0])
    mod3 = mod.reshape(8 * 6, 1, d)

    w = w_in[0]
    ssd_in = SSD_WIDTH + SSD_XBC + 2 * SSD_HEADS
    ml_main = 3 * ML_WIDTH
    n_gate = 2 * SSD_HEADS + 4 * ML_HEADS
    w_cat = jnp.concatenate([
        w[:, :SSD_WIDTH + SSD_XBC], w[:, ssd_in:ssd_in + ml_main],
        w[:, SSD_WIDTH + SSD_XBC:ssd_in], w[:, ssd_in + ml_main:],
        jnp.zeros((d, LANES - n_gate), F32)], axis=1).astype(BF16)
    z, xbc, ml, gates, gates_cm = _in_proj(x, ctx, mod3, norm1_w[0], w_cat)

    e_mats = []
    for direction in range(2):
        lane = jnp.arange(LANES)[:, None]
        head = (jnp.arange(SSD_WIDTH) // SSD_HEADDIM)[None, :]
        e_mats.append((lane == direction * SSD_HEADS + head).astype(BF16))
    cw = jnp.zeros((8, SSD_XBC), F32).at[:CONV_W].set(ssd_conv_w[0])
    dtb = _lane_pad(ssd_dt_bias[0])
    alog = _lane_pad(ssd_a_log[0])
    dsk = jnp.repeat(ssd_d[0], SSD_HEADDIM).reshape(1, SSD_WIDTH)
    shifts = _shift_matrices()
    y_ssd = _ssd_scans(xbc, gates, z, [cw, ssd_conv_b[0].reshape(1, -1), shifts],
                       [dtb, dtb.reshape(LANES, 1), alog, alog.reshape(LANES, 1)], e_mats, dsk,
                       ssd_norm_w[0].reshape(1, -1))

    w_rows = jnp.tile(ml_w_qk[0].reshape(2, ML_WIDTH, ML_QK_BLOCK), (1, 1, ML_WIDTH // ML_QK_BLOCK))
    blk_id = jnp.arange(ML_WIDTH) // ML_QK_BLOCK
    w_bd = jnp.where((blk_id[:, None] == blk_id[None, :])[None], w_rows, 0.0)
    wq = w_bd[0].astype(BF16)
    wk = (w_bd[1] * (ML_HEADDIM ** -0.5)).astype(BF16)
    mcw = jnp.zeros((8, ML_WIDTH), F32).at[:CONV_W].set(ml_conv_w[0])
    gb = _lane_pad(ml_gate_b[0], offset=2 * SSD_HEADS)
    y_ml_cm = _ml_scans(ml, gates_cm, [mcw, ml_conv_b[0].reshape(1, -1), shifts], [wq, wk], [gb, gb.reshape(LANES, 1)],
                        ml_norm_w[0].reshape(1, -1), ml_skip[0].reshape(1, -1))
    y_ml = _from_col_major(y_ml_cm, rows)

    rw = jnp.concatenate([moe_rg_w[0], moe_re_w[0],
                          jnp.zeros((d, LANES - MOE_GROUPS - MOE_EXPERTS), F32)], axis=1)
    rw_hi = rw.astype(BF16)
    rw_lo = (rw - rw_hi.astype(F32)).astype(BF16)
    rb = _lane_pad(jnp.concatenate([moe_rg_b[0], moe_re_b[0]]))
    rw2 = jnp.concatenate([rw_hi, rw_lo], axis=1)
    x1, t_slab, route = _out_proj(x, y_ssd, y_ml, w_out[0].astype(BF16), mod3, norm2_w[0], rw2, rb)

    return _moe(t_slab, x1, route, moe_w_gate[0], moe_w_up[0], moe_w_down[0], mod3, final_norm_w)
```

```python
import functools

import jax
import jax.numpy as jnp
from jax import lax
from jax.experimental import pallas as pl
from jax.experimental.pallas import tpu as pltpu

F32 = jnp.float32
BF16 = jnp.bfloat16
HIGHEST = lax.Precision.HIGHEST

D_MODEL = 1024
GRID_W = 64
EPS = 1e-6
CONV_W = 5
NEG_BIG = -1e30
CHUNK = 128
LANES = 128
HALO = 16
NB = 4

SSD_WIDTH = 512
SSD_HEADS = 8
SSD_HEADDIM = 64
SSD_GROUPS = 2
SSD_STATE = 128
SSD_XBC = SSD_WIDTH + 2 * SSD_GROUPS * SSD_STATE

ML_WIDTH = 512
ML_HEADS = 4
ML_HEADDIM = 128
ML_QK_BLOCK = 4

MOE_GROUPS = 4
MOE_EPG = 8
MOE_EXPERTS = 32
D_EXPERT = 256
ROUTE_LANE0 = MOE_GROUPS

TM_IN = 512
TM_FIN = 256
SLAB = D_MODEL // LANES
BM = 128
BLOCK_PARTS = 1
PART = BM // BLOCK_PARTS
PART_PITCH = PART + 8
SCATTER_GROUP = 4
VMEM_LIMIT = 48 * 1024 * 1024
VMEM_LIMIT_MOE = 56 * 1024 * 1024


def _silu(v):
    return v * jax.nn.sigmoid(v)


def _softplus(v):
    return jnp.maximum(v, 0.0) + jnp.log1p(jnp.exp(-jnp.abs(v)))


def _dot(a, b):
    return jnp.dot(a, b, preferred_element_type=F32)


def _dot_nt(a, b):
    return lax.dot_general(a, b, (((1,), (1,)), ((), ())), preferred_element_type=F32)


def _dot_hi(a, b):
    return jnp.dot(a, b, preferred_element_type=F32, precision=HIGHEST)


def _mod_kernel(c_ref, w_ref, b_ref, o_ref):
    c = c_ref[...]
    o_ref[...] = _dot_hi(_silu(c), w_ref[...]) + b_ref[...]


def _modulation(c_all, w_mod, b_mod):
    n = w_mod.shape[1]
    bn = 1536
    return pl.pallas_call(
        _mod_kernel,
        grid=(n // bn,),
        in_specs=[
            pl.BlockSpec((8, D_MODEL), lambda j: (0, 0)),
            pl.BlockSpec((D_MODEL, bn), lambda j: (0, j)),
            pl.BlockSpec((1, bn), lambda j: (0, j)),
        ],
        out_specs=pl.BlockSpec((8, bn), lambda j: (0, j)),
        out_shape=jax.ShapeDtypeStruct((8, n), F32),
        compiler_params=pltpu.CompilerParams(vmem_limit_bytes=VMEM_LIMIT),
        name="modulation",
    )(c_all, w_mod, b_mod.reshape(1, n))


W_TILE = 16
COL_Z, COL_XBC, COL_ML, COL_G = 0, SSD_WIDTH, SSD_WIDTH + SSD_XBC, SSD_WIDTH + SSD_XBC + 3 * ML_WIDTH
PROJ_CHUNK = 512


def _norm_mod(xin, sh_ref, sc_ref, nw_ref):
    ms = jnp.mean(xin * xin, axis=-1, keepdims=True)
    y = xin * lax.rsqrt(ms + EPS) * nw_ref[...]
    return (y * (1.0 + sc_ref[...]) + sh_ref[...]).astype(BF16)


def _in_lat_kernel(rows, x_ref, sh_ref, sc_ref, nw_ref, w_ref, z_ref, xbc_ref, ml_ref, g_ref, gml_ref, scr):
    tm = rows * W_TILE
    h = _norm_mod(x_ref[...].reshape(tm, D_MODEL), sh_ref, sc_ref, nw_ref)

    def proj(col, width=PROJ_CHUNK):
        return _dot(h, w_ref[:, col:col + width])

    def to_col_major(val, slab0, dst_ref, lo, dtype):
        n_slab = val.shape[1] // LANES
        for k in range(n_slab):
            scr[slab0 + k] = val[:, k * LANES:(k + 1) * LANES]
        for j in range(W_TILE):
            for k in range(n_slab):
                dst_ref[j, :, lo + k * LANES:lo + (k + 1) * LANES] = (
                    scr[slab0 + k, pl.ds(j, rows, stride=W_TILE), :].astype(dtype))

    slabs = PROJ_CHUNK // LANES
    g = proj(COL_G, LANES)
    g_ref[...] = g.reshape(rows, W_TILE, LANES)
    to_col_major(g, 0, gml_ref, 0, F32)
    for j in range(3 * ML_WIDTH // PROJ_CHUNK):
        to_col_major(proj(COL_ML + j * PROJ_CHUNK), 1 + j * slabs, ml_ref, j * PROJ_CHUNK, BF16)
    z_ref[...] = proj(COL_Z).astype(BF16).reshape(rows, W_TILE, PROJ_CHUNK)
    for j in range(SSD_XBC // PROJ_CHUNK):
        lo = j * PROJ_CHUNK
        xbc_ref[:, :, lo:lo + PROJ_CHUNK] = proj(COL_XBC + lo).astype(BF16).reshape(rows, W_TILE, PROJ_CHUNK)


def _in_ctx_kernel(x_ref, sh_ref, sc_ref, nw_ref, w_ref, xbc_ref, ml_ref, g_ref):
    h = _norm_mod(x_ref[...], sh_ref, sc_ref, nw_ref)
    for j in range(SSD_XBC // PROJ_CHUNK):
        lo = j * PROJ_CHUNK
        xbc_ref[:, lo:lo + PROJ_CHUNK] = _dot(h, w_ref[:, COL_XBC + lo:COL_XBC + lo + PROJ_CHUNK]).astype(BF16)
    for j in range(3 * ML_WIDTH // PROJ_CHUNK):
        lo = j * PROJ_CHUNK
        ml_ref[:, lo:lo + PROJ_CHUNK] = _dot(h, w_ref[:, COL_ML + lo:COL_ML + lo + PROJ_CHUNK]).astype(BF16)
    g_ref[...] = _dot(h, w_ref[:, COL_G:COL_G + LANES])


def _in_proj(x, ctx, mod3, norm_w, w_cat):
    b, seq, d = x.shape
    ctx_len = ctx.shape[1]
    rows = seq // GRID_W
    widths = (SSD_WIDTH, SSD_XBC, 3 * ML_WIDTH, LANES, LANES)
    dtypes = (BF16, BF16, BF16, F32, F32)
    params = pltpu.CompilerParams(dimension_semantics=("arbitrary", "arbitrary"), vmem_limit_bytes=VMEM_LIMIT)
    nw = norm_w.reshape(1, d)

    def raster(width):
        return pl.BlockSpec((None, rows, W_TILE, width), lambda bi, wi: (bi, 0, wi, 0))

    def col_major(width):
        return pl.BlockSpec((None, W_TILE, rows, width), lambda bi, wi: (bi, wi, 0, 0))

    outs = pl.pallas_call(
        functools.partial(_in_lat_kernel, rows),
        grid=(b, GRID_W // W_TILE),
        in_specs=[
            raster(d),
            pl.BlockSpec((None, 1, d), lambda bi, wi: (bi * 6, 0, 0)),
            pl.BlockSpec((None, 1, d), lambda bi, wi: (bi * 6 + 1, 0, 0)),
            _full((1, d)), _full(w_cat.shape),
        ],
        out_specs=[raster(SSD_WIDTH), raster(SSD_XBC), col_major(3 * ML_WIDTH), raster(LANES), col_major(LANES)],
        out_shape=[jax.ShapeDtypeStruct((b, GRID_W, rows, w) if cm else (b, rows, GRID_W, w), t)
                   for w, t, cm in zip(widths, dtypes, (False, False, True, False, True))],
        scratch_shapes=[pltpu.VMEM((1 + 3 * ML_WIDTH // LANES, rows * W_TILE, LANES), F32)],
        compiler_params=params,
        name="in_proj",
    )(x.reshape(b, rows, GRID_W, d), mod3, mod3, nw, w_cat)
    z, xbc, ml, gates, gates_cm = [o.reshape(b, seq, w) for o, w in zip(outs, widths)]

    ctx_row = b
    ctx_widths = (SSD_XBC, 3 * ML_WIDTH, LANES)
    xbc_c, ml_c, gates_c = pl.pallas_call(
        _in_ctx_kernel,
        grid=(b, 1),
        in_specs=[
            pl.BlockSpec((None, ctx_len, d), lambda bi, t: (bi, 0, 0)),
            pl.BlockSpec((None, 1, d), lambda bi, t: (ctx_row * 6, 0, 0)),
            pl.BlockSpec((None, 1, d), lambda bi, t: (ctx_row * 6 + 1, 0, 0)),
            _full((1, d)), _full(w_cat.shape),
        ],
        out_specs=[pl.BlockSpec((None, ctx_len, w), lambda bi, t: (bi, 0, 0)) for w in ctx_widths],
        out_shape=[jax.ShapeDtypeStruct((b, ctx_len, w), t) for w, t in zip(ctx_widths, (BF16, BF16, F32))],
        compiler_params=params,
        name="in_proj_ctx",
    )(ctx, mod3, mod3, nw, w_cat)
    return z, (xbc, xbc_c), (ml, ml_c), (gates, gates_c), (gates_cm, gates_c)


def _chunk_eff(rev, ncc, nc, c):
    nl = nc - ncc
    if not rev:
        return jnp.where(c < ncc, nl + c, c - ncc)
    return jnp.where(c < ncc, nc - 1 - c, nl - 1 - (c - ncc))


def _seq_ends(ncc, nc, ceff):
    nl = nc - ncc
    return (ceff == 0) | (ceff == nl), (ceff == nl - 1) | (ceff == nc - 1)


def _lat_block(rev, ncc, nc, c):
    nl = nc - ncc
    if not rev:
        return jnp.maximum(c - ncc, 0)
    return jnp.where(c < ncc, nl - 1, nl - 1 - (c - ncc))


CONV_SIDE_TAPS = tuple(t for t in range(CONV_W) if t != CONV_W // 2)


def _shift_matrices():
    row = jnp.arange(CHUNK)[:, None]
    col = jnp.arange(CHUNK + 2 * HALO)[None, :]
    return jnp.stack([col == row + HALO + tap - CONV_W // 2 for tap in CONV_SIDE_TAPS]).astype(BF16)


def _conv_silu(xm, xp, xn, cw_ref, cb_ref, sh_ref, first, last):
    xp = jnp.where(first, jnp.zeros_like(xp), xp)
    xn = jnp.where(last, jnp.zeros_like(xn), xn)
    ext = jnp.concatenate([xp, xm, xn], axis=0)
    mid = CONV_W // 2
    acc = cb_ref[...] + cw_ref[mid:mid + 1, :] * xm.astype(F32)
    for i, tap in enumerate(CONV_SIDE_TAPS):
        acc = acc + cw_ref[tap:tap + 1, :] * _dot(sh_ref[i], ext)
    return _silu(acc)


def _tri(rev, transposed=False):
    row = lax.broadcasted_iota(jnp.int32, (CHUNK, CHUNK), 0)
    col = lax.broadcasted_iota(jnp.int32, (CHUNK, CHUNK), 1)
    if transposed:
        row, col = col, row
    keep = (col >= row) if rev else (col <= row)
    return keep


def _split3(a):
    a1 = a.astype(BF16)
    r1 = a - a1.astype(F32)
    a2 = r1.astype(BF16)
    a3 = (r1 - a2.astype(F32)).astype(BF16)
    return a1, a2, a3


def _dot_sel_l(m_b, a):
    p = _split3(a)
    return _dot(m_b, p[0]) + _dot(m_b, p[1]) + _dot(m_b, p[2])


def _dot_sel_r(a, m_b):
    p = _split3(a)
    return _dot(p[0], m_b) + _dot(p[1], m_b) + _dot(p[2], m_b)


def _row_spec(rows, width, row_block, col_block=0):
    return pl.BlockSpec((NB, rows, width), lambda bi, c: (bi, row_block(c), col_block))


def _local_chunks(rev, ncc, nc, c):
    nl = nc - ncc
    cc = jnp.clip(ncc - 1 - c if rev else c, 0, ncc - 1)
    cl = jnp.clip(nl - 1 - (c - ncc) if rev else c - ncc, 0, nl - 1)
    return c < ncc, cc, cl


def _pair_specs(rev, ncc, nc, rows, width, col=0, halo=0):
    per = CHUNK // HALO

    def index(which, count):
        def fn(c):
            ch = _local_chunks(rev, ncc, nc, c)[which]
            if halo == 0:
                return ch
            if halo < 0:
                return jnp.maximum(ch * per - 1, 0)
            return jnp.minimum((ch + 1) * per, count * per - 1)
        return fn

    return [_row_spec(rows, width, index(2, nc - ncc), col), _row_spec(rows, width, index(1, ncc), col)]


def _pick(is_ctx, lat_ref, ctx_ref, s):
    return jnp.where(is_ctx, ctx_ref[s], lat_ref[s])


def _step_ends(rev, ncc, nc, c):
    is_ctx, cc, cl = _local_chunks(rev, ncc, nc, c)
    first = jnp.where(is_ctx, cc == 0, cl == 0)
    last = jnp.where(is_ctx, cc == ncc - 1, cl == nc - ncc - 1)
    return is_ctx, first, last


def _full(shape):
    return pl.BlockSpec(shape, lambda bi, c: (0,) * len(shape))


def _ssd_gates(rev, xs, g, dtbr_ref, dtbc_ref, alr_ref, alc_ref, e_ref):
    lane0 = SSD_HEADS * int(rev)
    lane = lax.broadcasted_iota(jnp.int32, (CHUNK, LANES), 1)
    lmask = (lane >= lane0) & (lane < lane0 + SSD_HEADS)
    dt = jnp.where(lmask, _softplus(g + dtbr_ref[...]), 0.0)
    a = dt * (-jnp.exp(alr_ref[...]))
    gt = g.T
    dt_t = _softplus(gt + dtbc_ref[...])[lane0:lane0 + SSD_HEADS]
    a_t = dt_t * (-jnp.exp(alc_ref[...][lane0:lane0 + SSD_HEADS]))

    cs = _dot_sel_l(_tri(rev).astype(BF16), a)
    cs_t = _dot_sel_r(a_t, _tri(rev, transposed=True).astype(BF16))
    e = e_ref[...]
    dtx = _dot_sel_r(dt, e)
    csx = _dot_sel_r(cs, e)
    end = 0 if rev else CHUNK - 1
    totx = csx[end:end + 1, :]
    ecsx = jnp.exp(csx)
    decx = jnp.exp(totx - csx)
    etotx = jnp.exp(totx)

    xdt = xs * dtx
    return dict(cs=cs, cs_t=cs_t, ecsx=ecsx, etotx=etotx, xdt_b=xdt.astype(BF16), xd_b=(xdt * decx).astype(BF16))


def _ssd_chunks(rev, xbcs, gs_, dtbr_ref, dtbc_ref, alr_ref, alc_ref, e_ref, s_ref):
    lane0 = SSD_HEADS * int(rev)
    keep = _tri(rev)
    half = lax.broadcasted_iota(jnp.int32, (CHUNK, LANES), 1) // SSD_HEADDIM
    n_bc = SSD_GROUPS * SSD_STATE
    hpg = SSD_HEADS // SSD_GROUPS
    gw = hpg * SSD_HEADDIM

    gq = [_ssd_gates(rev, xbc[:, :SSD_WIDTH], g, dtbr_ref, dtbc_ref, alr_ref, alc_ref, e_ref)
          for xbc, g in zip(xbcs, gs_)]
    units = []
    for s, xbc in enumerate(xbcs):
        for grp in range(SSD_GROUPS):
            bm = xbc[:, SSD_WIDTH + grp * SSD_STATE:SSD_WIDTH + (grp + 1) * SSD_STATE]
            cm = xbc[:, SSD_WIDTH + n_bc + grp * SSD_STATE:SSD_WIDTH + n_bc + (grp + 1) * SSD_STATE]
            units.append(dict(s=s, grp=grp, cols=slice(grp * gw, (grp + 1) * gw), bm=bm, bm_b=bm.astype(BF16),
                              cm_b=cm.astype(BF16), s_old=s_ref[s, :, grp * gw:(grp + 1) * gw]))
    for u in units:
        u["cb"] = _dot_nt(u["cm_b"], u["bm_b"])
        u["y_off"] = _dot(u["cm_b"], u["s_old"].astype(BF16)) * gq[u["s"]]["ecsx"][:, u["cols"]]
        u["bt_b"] = u["bm"].T.astype(BF16)
    for u in units:
        q_ = gq[u["s"]]
        masks = []
        for hh in range(hpg):
            h = u["grp"] * hpg + hh
            dl = q_["cs"][:, lane0 + h:lane0 + h + 1] - q_["cs_t"][h:h + 1, :]
            masks.append((u["cb"] * jnp.exp(jnp.where(keep, dl, NEG_BIG))).astype(BF16))
        u["masks"] = masks
    for u in units:
        q_ = gq[u["s"]]
        blocks = []
        for pair in range(hpg // 2):
            blk = u["grp"] * (hpg // 2) + pair
            xj = q_["xdt_b"][:, blk * LANES:(blk + 1) * LANES]
            acc = u["y_off"][:, pair * LANES:(pair + 1) * LANES]
            for q in range(2):
                acc = acc + _dot(u["masks"][pair * 2 + q], jnp.where(half == q, xj, jnp.zeros_like(xj)))
            blocks.append(acc)
        u["y"] = blocks
    for u in units:
        q_ = gq[u["s"]]
        s_new = u["s_old"] * q_["etotx"][:, u["cols"]] + _dot(u["bt_b"], q_["xd_b"][:, u["cols"]])
        s_ref[u["s"], :, u["cols"]] = s_new
    return [jnp.concatenate([blk for u in units if u["s"] == s for blk in u["y"]], axis=1)
            for s in range(len(xbcs))]


def _ssd_rev_kernel(ncc, nc, xml_ref, xmc_ref, xpl_ref, xpc_ref, xnl_ref, xnc_ref, gl_ref, gc_ref, cw_ref, cb_ref,
                    sh_ref, dtbr_ref, dtbc_ref, alr_ref, alc_ref, e_ref, o_ref, xc_ref, s_ref):
    c = pl.program_id(1)
    is_ctx, first, last = _step_ends(True, ncc, nc, c)

    @pl.when(c == 0)
    def _():
        s_ref[...] = jnp.zeros_like(s_ref)

    xbcs = [_conv_silu(_pick(is_ctx, xml_ref, xmc_ref, s), _pick(is_ctx, xpl_ref, xpc_ref, s),
                       _pick(is_ctx, xnl_ref, xnc_ref, s), cw_ref, cb_ref, sh_ref, first, last) for s in range(NB)]
    for s in range(NB):
        xc_ref[s] = xbcs[s].astype(BF16)
    ys = _ssd_chunks(True, xbcs, [_pick(is_ctx, gl_ref, gc_ref, s) for s in range(NB)], dtbr_ref, dtbc_ref,
                     alr_ref, alc_ref, e_ref, s_ref)
    for s in range(NB):
        o_ref[s] = ys[s].astype(o_ref.dtype)


def _ssd_fwd_kernel(ncc, xc_ref, gl_ref, gc_ref, dtbr_ref, dtbc_ref, alr_ref, alc_ref, e_ref, dsk_ref, yb_ref,
                    z_ref, nw_ref, o_ref, s_ref):
    is_ctx = pl.program_id(1) < ncc

    @pl.when(pl.program_id(1) == 0)
    def _():
        s_ref[...] = jnp.zeros_like(s_ref)

    xbcs = [xc_ref[s].astype(F32) for s in range(NB)]
    ys = _ssd_chunks(False, xbcs, [_pick(is_ctx, gl_ref, gc_ref, s) for s in range(NB)], dtbr_ref, dtbc_ref,
                     alr_ref, alc_ref, e_ref, s_ref)
    ys = [(ys[s] + yb_ref[s].astype(F32) + dsk_ref[...] * xbcs[s][:, :SSD_WIDTH]) * _silu(z_ref[s].astype(F32))
          for s in range(NB)]
    scale = [lax.rsqrt(jnp.mean(y * y, axis=-1, keepdims=True) + EPS) for y in ys]
    for s in range(NB):
        o_ref[s] = (ys[s] * scale[s] * nw_ref[...]).astype(o_ref.dtype)


def _ssd_scans(xbc, gates, z, conv_params, gate_params, e_mats, dsk, nw):
    b, seq, _ = xbc[0].shape
    ncc = xbc[1].shape[1] // CHUNK
    nc = seq // CHUNK + ncc
    state = pltpu.VMEM((NB, SSD_STATE, SSD_WIDTH), F32)
    params = pltpu.CompilerParams(dimension_semantics=("arbitrary", "arbitrary"), vmem_limit_bytes=VMEM_LIMIT)
    lat_shape = jax.ShapeDtypeStruct((b, seq, SSD_WIDTH), BF16)

    ceff = functools.partial(_chunk_eff, True, ncc, nc)
    lat = functools.partial(_lat_block, True, ncc, nc)
    pair = functools.partial(_pair_specs, True, ncc, nc)
    rev_params = list(conv_params) + list(gate_params) + [e_mats[1]]
    yb, xbc_act = pl.pallas_call(
        functools.partial(_ssd_rev_kernel, ncc, nc),
        grid=(b // NB, nc),
        in_specs=pair(CHUNK, SSD_XBC) + pair(HALO, SSD_XBC, halo=-1) + pair(HALO, SSD_XBC, halo=1)
        + pair(CHUNK, LANES) + [_full(p.shape) for p in rev_params],
        out_specs=[_row_spec(CHUNK, SSD_WIDTH, lat), _row_spec(CHUNK, SSD_XBC, ceff)],
        out_shape=(lat_shape, jax.ShapeDtypeStruct((b, nc * CHUNK, SSD_XBC), BF16)),
        scratch_shapes=[state],
        compiler_params=params,
        name="ssd_rev",
    )(*xbc, *xbc, *xbc, *gates, *rev_params)

    ceff = functools.partial(_chunk_eff, False, ncc, nc)
    lat = functools.partial(_lat_block, False, ncc, nc)
    fwd_params = list(gate_params) + [e_mats[0], dsk]
    return pl.pallas_call(
        functools.partial(_ssd_fwd_kernel, ncc),
        grid=(b // NB, nc),
        in_specs=[_row_spec(CHUNK, SSD_XBC, ceff)] + _pair_specs(False, ncc, nc, CHUNK, LANES)
        + [_full(p.shape) for p in fwd_params]
        + [_row_spec(CHUNK, SSD_WIDTH, lat), _row_spec(CHUNK, SSD_WIDTH, lat), _full(nw.shape)],
        out_specs=_row_spec(CHUNK, SSD_WIDTH, lat),
        out_shape=lat_shape,
        scratch_shapes=[state],
        compiler_params=params,
        name="ssd_fwd",
    )(xbc_act, *gates, *fwd_params, yb, z, nw)


ML_I_LANE0 = 2 * SSD_HEADS
ML_F_LANE0 = ML_I_LANE0 + 2 * ML_HEADS


def _ml_gates(rev, g, gbr_ref, gbc_ref):
    i_lane0 = ML_I_LANE0 + ML_HEADS * int(rev)
    f_lane0 = ML_F_LANE0 + ML_HEADS * int(rev)
    ga = g + gbr_ref[...]
    lane = lax.broadcasted_iota(jnp.int32, (CHUNK, LANES), 1)
    logf = jnp.where((lane >= f_lane0) & (lane < f_lane0 + ML_HEADS), -_softplus(-ga), 0.0)
    cs = _dot_sel_l(_tri(rev).astype(BF16), logf)
    gt = g.T + gbc_ref[...]
    i_t = gt[ML_I_LANE0:ML_F_LANE0]
    logf_t = -_softplus(-gt[ML_F_LANE0:ML_F_LANE0 + 2 * ML_HEADS])
    cs_t = _dot_sel_r(logf_t, _tri(rev, transposed=True).astype(BF16))
    end = 0 if rev else CHUNK - 1

    u_t = i_t - cs_t
    u_c = ga - pltpu.roll(cs, LANES - (ML_F_LANE0 - ML_I_LANE0), axis=1)
    row = lax.broadcasted_iota(jnp.int32, (CHUNK, LANES), 0)
    pm = u_c
    step = 1
    while step < CHUNK:
        if rev:
            pm = jnp.maximum(pm, jnp.where(row < CHUNK - step, pltpu.roll(pm, CHUNK - step, axis=0), NEG_BIG))
        else:
            pm = jnp.maximum(pm, jnp.where(row >= step, pltpu.roll(pm, step, axis=0), NEG_BIG))
        step *= 2

    heads = []
    for h in range(ML_HEADS):
        r = ML_HEADS * int(rev) + h
        li = i_lane0 + h
        heads.append(dict(
            csc=cs[:, f_lane0 + h:f_lane0 + h + 1], u_col=u_c[:, li:li + 1], u_row=u_t[r:r + 1, :],
            tot=cs_t[r:r + 1, end:end + 1], u_max=pm[end:end + 1, li:li + 1], pm_col=pm[:, li:li + 1]))
    return heads


def _ml_chunks(rev, samples, gbr_ref, gbc_ref, c_ref, n_ref, mx_ref):
    keep = _tri(rev)
    ones_b = jnp.ones((CHUNK, ML_HEADDIM), BF16)
    units = []
    for s, (k, q_b, k_b, v_b, g) in enumerate(samples):
        for h, gq in enumerate(_ml_gates(rev, g, gbr_ref, gbc_ref)):
            sl = slice(h * ML_HEADDIM, (h + 1) * ML_HEADDIM)
            units.append(dict(gq, s=s, h=h, kh=k[:, sl], vh=v_b[:, sl], qh_b=q_b[:, sl], kh_b=k_b[:, sl],
                              m_prev=mx_ref[s, h:h + 1, 0:1], c_prev=c_ref[s * ML_HEADS + h],
                              n_prev=n_ref[s, h:h + 1, :]))

    for u in units:
        u["qk"] = _dot_nt(u["qh_b"], u["kh_b"])
        u["vt_b"] = u["vh"].astype(F32).T.astype(BF16)
    for u in units:
        u["mm"] = jnp.maximum(u["m_prev"], u["pm_col"])
        u["scores_b"] = (u["qk"] * jnp.exp(jnp.where(keep, u["u_row"] - u["mm"], NEG_BIG))).astype(BF16)
    for u in units:
        c_aug = jnp.concatenate([u["c_prev"], jnp.broadcast_to(u["n_prev"], (CHUNK, ML_HEADDIM))], axis=0)
        u["intra"] = _dot(u["scores_b"], jnp.concatenate([u["vh"], ones_b], axis=1))
        u["inter"] = _dot_nt(u["qh_b"], c_aug.astype(BF16))
    outs = [[None] * ML_HEADS for _ in samples]
    for u in units:
        both = u["intra"] + jnp.exp(u["m_prev"] - u["mm"]) * u["inter"]
        den = both[:, ML_HEADDIM:ML_HEADDIM + 1]
        outs[u["s"]][u["h"]] = both[:, :ML_HEADDIM] / jnp.maximum(jnp.abs(den), jnp.exp(-(u["csc"] + u["mm"])))
    for u in units:
        kw = u["kh"] * jnp.exp(u["u_col"] - u["u_max"])
        u["c_loc"] = _dot(u["vt_b"], kw.astype(BF16))
        u["n_loc"] = jnp.sum(kw, axis=0, keepdims=True)
    for u in units:
        s, h, tot, m_prev = u["s"], u["h"], u["tot"], u["m_prev"]
        m_loc = tot + u["u_max"]
        m_new = jnp.maximum(tot + m_prev, m_loc)
        s_prev = jnp.exp(tot + m_prev - m_new)
        s_loc = jnp.exp(m_loc - m_new)
        c_ref[s * ML_HEADS + h] = s_prev * u["c_prev"] + s_loc * u["c_loc"]
        n_ref[s, h:h + 1, :] = s_prev * u["n_prev"] + s_loc * u["n_loc"]
        mx_ref[s, h:h + 1, :] = jnp.broadcast_to(m_new, (1, LANES))
    return outs


def _ml_init_state(c_ref, n_ref, mx_ref):
    c_ref[...] = jnp.zeros_like(c_ref)
    n_ref[...] = jnp.zeros_like(n_ref)
    mx_ref[...] = jnp.full(mx_ref.shape, NEG_BIG, F32)


def _ml_rev_kernel(ncc, nc, xml_ref, xmc_ref, xpl_ref, xpc_ref, xnl_ref, xnc_ref, vl_ref, vc_ref, gl_ref, gc_ref,
                   cw_ref, cb_ref, sh_ref, wq_ref, wk_ref, gbr_ref, gbc_ref, o_ref, xc_ref, q_ref, k_ref, c_ref, n_ref,
                   mx_ref):
    c = pl.program_id(1)
    is_ctx, first, last = _step_ends(True, ncc, nc, c)

    @pl.when(c == 0)
    def _():
        _ml_init_state(c_ref, n_ref, mx_ref)

    samples = []
    for s in range(NB):
        xconv = _conv_silu(_pick(is_ctx, xml_ref, xmc_ref, s), _pick(is_ctx, xpl_ref, xpc_ref, s),
                           _pick(is_ctx, xnl_ref, xnc_ref, s), cw_ref, cb_ref, sh_ref, first, last)
        xc_b = xconv.astype(BF16)
        q = _dot(xc_b, wq_ref[...])
        k = _dot(xc_b, wk_ref[...])
        q_b, k_b = q.astype(BF16), k.astype(BF16)
        xc_ref[s] = xc_b
        q_ref[s] = q_b
        k_ref[s] = k_b
        samples.append((k, q_b, k_b, _pick(is_ctx, vl_ref, vc_ref, s), _pick(is_ctx, gl_ref, gc_ref, s)))
    outs = _ml_chunks(True, samples, gbr_ref, gbc_ref, c_ref, n_ref, mx_ref)
    for s in range(NB):
        o_ref[s] = jnp.concatenate(outs[s], axis=1).astype(o_ref.dtype)


def _ml_fwd_kernel(ncc, xc_ref, q_ref, k_ref, vl_ref, vc_ref, og_ref, gl_ref, gc_ref, gbr_ref, gbc_ref, hb_ref,
                   nw_ref, sk_ref, o_ref, c_ref, n_ref, mx_ref):
    is_ctx = pl.program_id(1) < ncc

    @pl.when(pl.program_id(1) == 0)
    def _():
        _ml_init_state(c_ref, n_ref, mx_ref)

    samples = []
    for s in range(NB):
        q_b, k_b = q_ref[s], k_ref[s]
        samples.append((k_b.astype(F32), q_b, k_b, _pick(is_ctx, vl_ref, vc_ref, s),
                        _pick(is_ctx, gl_ref, gc_ref, s)))
    outs = _ml_chunks(False, samples, gbr_ref, gbc_ref, c_ref, n_ref, mx_ref)
    gated = []
    for s in range(NB):
        for h in range(ML_HEADS):
            sl = slice(h * ML_HEADDIM, (h + 1) * ML_HEADDIM)
            gated.append(jax.nn.sigmoid(og_ref[s, :, sl].astype(F32)) * (outs[s][h] + hb_ref[s, :, sl].astype(F32)))
    scale = [lax.rsqrt(jnp.mean(hh * hh, axis=-1, keepdims=True) + EPS) for hh in gated]
    normed = [hh * sc for hh, sc in zip(gated, scale)]
    for s in range(NB):
        y = jnp.concatenate(normed[s * ML_HEADS:(s + 1) * ML_HEADS], axis=1) * nw_ref[...]
        o_ref[s] = (y + sk_ref[...] * xc_ref[s].astype(F32)).astype(o_ref.dtype)


def _ml_scans(ml, gates, conv_params, proj_params, gate_params, nw, sk):
    b, seq, _ = ml[0].shape
    ncc = ml[1].shape[1] // CHUNK
    nc = seq // CHUNK + ncc
    scratch = [pltpu.VMEM((NB * ML_HEADS, ML_HEADDIM, ML_HEADDIM), F32),
               pltpu.VMEM((NB, 8, ML_HEADDIM), F32),
               pltpu.VMEM((NB, 8, LANES), F32)]
    params = pltpu.CompilerParams(dimension_semantics=("arbitrary", "arbitrary"), vmem_limit_bytes=VMEM_LIMIT)
    lat_shape = jax.ShapeDtypeStruct((b, seq, ML_WIDTH), BF16)
    act_shape = jax.ShapeDtypeStruct((b, nc * CHUNK, ML_WIDTH), BF16)

    ceff = functools.partial(_chunk_eff, True, ncc, nc)
    lat = functools.partial(_lat_block, True, ncc, nc)
    pair = functools.partial(_pair_specs, True, ncc, nc)
    rev_params = list(conv_params) + list(proj_params) + list(gate_params)
    act_spec = _row_spec(CHUNK, ML_WIDTH, ceff)
    hb, xc, q, k = pl.pallas_call(
        functools.partial(_ml_rev_kernel, ncc, nc),
        grid=(b // NB, nc),
        in_specs=pair(CHUNK, ML_WIDTH) + pair(HALO, ML_WIDTH, halo=-1) + pair(HALO, ML_WIDTH, halo=1)
        + pair(CHUNK, ML_WIDTH, col=1) + pair(CHUNK, LANES) + [_full(p.shape) for p in rev_params],
        out_specs=[_row_spec(CHUNK, ML_WIDTH, lat), act_spec, act_spec, act_spec],
        out_shape=(lat_shape, act_shape, act_shape, act_shape),
        scratch_shapes=scratch,
        compiler_params=params,
        name="mlstm_rev",
    )(*ml, *ml, *ml, *ml, *gates, *rev_params)

    ceff = functools.partial(_chunk_eff, False, ncc, nc)
    lat = functools.partial(_lat_block, False, ncc, nc)
    pair = functools.partial(_pair_specs, False, ncc, nc)
    act_spec = _row_spec(CHUNK, ML_WIDTH, ceff)
    return pl.pallas_call(
        functools.partial(_ml_fwd_kernel, ncc),
        grid=(b // NB, nc),
        in_specs=[act_spec, act_spec, act_spec] + pair(CHUNK, ML_WIDTH, col=1)
        + [_row_spec(CHUNK, ML_WIDTH, lat, 2)] + pair(CHUNK, LANES)
        + [_full(p.shape) for p in gate_params]
        + [_row_spec(CHUNK, ML_WIDTH, lat), _full(nw.shape), _full(sk.shape)],
        out_specs=_row_spec(CHUNK, ML_WIDTH, lat),
        out_shape=lat_shape,
        scratch_shapes=scratch,
        compiler_params=params,
        name="mlstm_fwd",
    )(xc, q, k, *ml, ml[0], *gates, *gate_params, hb, nw, sk)


def _out_kernel(x_ref, ys_ref, ym_ref, wo_ref, g1_ref, sh_ref, sc_ref, nw_ref, rw_ref, rb_ref,
                x1_ref, ts_ref, route_ref):
    mix = _dot(ys_ref[...], wo_ref[0:SSD_WIDTH, :]) + _dot(ym_ref[...], wo_ref[SSD_WIDTH:, :])
    x1 = x_ref[...] + g1_ref[...] * mix
    y = x1 * lax.rsqrt(jnp.mean(x1 * x1, axis=-1, keepdims=True) + EPS) * nw_ref[...]
    t = y * (1.0 + sc_ref[...]) + sh_ref[...]
    x1_ref[...] = x1
    for j in range(SLAB):
        ts_ref[pl.ds(j, TM_IN, stride=SLAB), :] = t[:, j * LANES:(j + 1) * LANES]
    lg2 = _dot(t.astype(BF16), rw_ref[...])
    lg = lg2[:, :LANES] + lg2[:, LANES:] + rb_ref[...]

    lane = lax.broadcasted_iota(jnp.int32, lg.shape, 1).astype(F32)
    gmask = lane < MOE_GROUPS
    gl = jnp.where(gmask, lg, NEG_BIG)
    gmax = jnp.max(gl, axis=1, keepdims=True)
    g_sel = jnp.min(jnp.where(gmask & (gl == gmax), lane, 1e9), axis=1, keepdims=True)
    p_group = 1.0 / jnp.sum(jnp.where(gmask, jnp.exp(gl - gmax), 0.0), axis=1, keepdims=True)
    lo = ROUTE_LANE0 + MOE_EPG * g_sel
    emask = (lane >= lo) & (lane < lo + MOE_EPG)
    l1 = jnp.max(jnp.where(emask, lg, NEG_BIG), axis=1, keepdims=True)
    i1 = jnp.min(jnp.where(emask & (lg == l1), lane, 1e9), axis=1, keepdims=True)
    emask2 = emask & (lane != i1)
    l2 = jnp.max(jnp.where(emask2, lg, NEG_BIG), axis=1, keepdims=True)
    i2 = jnp.min(jnp.where(emask2 & (lg == l2), lane, 1e9), axis=1, keepdims=True)
    r = jnp.exp(l2 - l1)
    w1 = p_group / (1.0 + r)
    w2 = p_group * r / (1.0 + r)
    route = (jnp.where(lane == 0, i1, 0.0) + jnp.where(lane == 1, i2, 0.0)
             + jnp.where(lane == 2, w1, 0.0) + jnp.where(lane == 3, w2, 0.0))
    route_ref[...] = route.T[0:8, :]


def _out_proj(x, y_ssd, y_ml, w_out_b, mod3, norm_w, rw2, rb):
    b, seq, d = x.shape
    nt = seq // TM_IN

    def row(j):
        return pl.BlockSpec((None, 1, d), lambda bi, t: (bi * 6 + j, 0, 0))

    def tile(width):
        return pl.BlockSpec((None, TM_IN, width), lambda bi, t: (bi, t, 0))

    return pl.pallas_call(
        _out_kernel,
        grid=(b, nt),
        in_specs=[tile(d), tile(SSD_WIDTH), tile(ML_WIDTH), _full(w_out_b.shape),
                  row(2), row(3), row(4), _full((1, d)), _full(rw2.shape), _full(rb.shape)],
        out_specs=[tile(d),
                   pl.BlockSpec((None, TM_IN * SLAB, LANES), lambda bi, t: (bi, t, 0)),
                   pl.BlockSpec((None, 8, TM_IN), lambda bi, t: (bi, 0, t))],
        out_shape=(jax.ShapeDtypeStruct((b, seq, d), F32),
                   jax.ShapeDtypeStruct((b, seq * SLAB, LANES), F32),
                   jax.ShapeDtypeStruct((b, 8, seq), F32)),
        compiler_params=pltpu.CompilerParams(
            dimension_semantics=("arbitrary", "arbitrary"), vmem_limit_bytes=VMEM_LIMIT),
        name="out_proj_router",
    )(x, y_ssd, y_ml, w_out_b, mod3, mod3, mod3, norm_w.reshape(1, d), rw2, rb)


def _route_tables(route, seq):
    b = route.shape[0]
    n_inst = 2 * seq
    nblk = n_inst // BM + MOE_EXPERTS
    e_flat = (route[:, 0:2, :].astype(jnp.int32) - ROUTE_LANE0).reshape(b, n_inst)
    w_flat = route[:, 2:4, :].reshape(b, n_inst)
    tok_flat = jnp.tile(jnp.arange(seq, dtype=jnp.int32), 2 * b).reshape(b, n_inst)
    _, tok_sorted, w_sorted = lax.sort((e_flat, tok_flat, w_flat), dimension=1, num_keys=1)
    experts = jnp.arange(MOE_EXPERTS, dtype=jnp.int32)
    counts = jnp.sum((e_flat[:, None, :] == experts[None, :, None]).astype(jnp.int32), axis=2)
    nblk_e = (counts + BM - 1) // BM
    blk_end = jnp.cumsum(nblk_e, axis=1)
    blk_start = blk_end - nblk_e
    cnt_start = jnp.cumsum(counts, axis=1) - counts
    nb = blk_end[:, -1:]
    j = jnp.arange(nblk, dtype=jnp.int32)[None, :]
    valid_blk = j < nb
    jj = jnp.minimum(j, nb - 1)
    e_j = jnp.sum((jj[:, :, None] >= blk_end[:, None, :]).astype(jnp.int32), axis=2)
    onehot = (e_j[:, :, None] == experts[None, None, :]).astype(jnp.int32)
    take = lambda tbl: jnp.sum(onehot * tbl[:, None, :], axis=2)
    r = jnp.arange(BM, dtype=jnp.int32)[None, None, :]
    rank = ((jj - take(blk_start)) * BM)[:, :, None] + r
    valid = valid_blk[:, :, None] & (rank < take(counts)[:, :, None])
    sidx = jnp.clip(take(cnt_start)[:, :, None] + rank, 0, n_inst - 1).reshape(b, nblk * BM)
    tok = jnp.take_along_axis(tok_sorted, sidx, axis=1).reshape(b, nblk, BM)
    wslot = jnp.take_along_axis(w_sorted, sidx, axis=1).reshape(b, nblk, BM)
    tok = jnp.where(valid, tok, seq + r) * SLAB
    wslot = jnp.where(valid, wslot, 0.0)
    return (nblk_e.reshape(-1), blk_start.reshape(-1), tok.reshape(b, 1, nblk * BM), wslot)


SEM_T, SEM_X, SEM_O = 0, 1, 3


def _moe_kernel(seq, nbe_ref, bs_ref, tok_ref, ws_ref, wg_ref, wu_ref, wd_ref, g2_ref, fw_ref,
                t_hbm, x1_hbm, o_hbm, t_scr, y_scr, xt, ot, xin, stage, sems):
    b = pl.program_id(0)
    e = pl.program_id(1)
    nb = pl.num_programs(0)
    rows = seq * SLAB
    n_fin = seq // TM_FIN

    def t_copy(sample):
        return pltpu.make_async_copy(t_hbm.at[sample], t_scr.at[pl.ds(0, rows)], sems.at[SEM_T])

    def x1_copy(s, slot):
        return pltpu.make_async_copy(x1_hbm.at[b, pl.ds(s * TM_FIN, TM_FIN)], xin.at[slot], sems.at[SEM_X + slot])

    def out_copy(s, slot):
        return pltpu.make_async_copy(stage.at[slot], o_hbm.at[b, pl.ds(s * TM_FIN, TM_FIN)], sems.at[SEM_O + slot])

    @pl.when((e == 0) & (b == 0))
    def _():
        t_copy(b).start()
        y_scr[...] = jnp.zeros_like(y_scr)
        t_scr[pl.ds(rows, BM * SLAB), :] = jnp.zeros((BM * SLAB, LANES), F32)

    @pl.when(e == 0)
    def _():
        t_copy(b).wait()

    n_blocks = nbe_ref[b * MOE_EXPERTS + e]
    blk0 = bs_ref[b * MOE_EXPERTS + e]

    @pl.when(n_blocks > 0)
    def _():
        g2 = g2_ref[...]
        diag = (lax.broadcasted_iota(jnp.int32, (BM, BM), 0) == lax.broadcasted_iota(jnp.int32, (BM, BM), 1))

        def block(i, carry):
            blk = blk0 + i
            base = blk * BM

            tok_blk = tok_ref.at[0, pl.ds(base, BM)]

            def slab_rows(r):
                return pl.ds(pl.multiple_of(tok_blk[r], SLAB), SLAB)

            w_col = jnp.sum(jnp.where(diag, ws_ref[pl.ds(blk, 1), :], 0.0), axis=1, keepdims=True)
            parts = range(BLOCK_PARTS)
            for p in parts:
                for r in range(PART):
                    xt[p, pl.ds(r, SLAB, stride=PART_PITCH), :] = t_scr[slab_rows(p * PART + r), :]
            for p in parts:
                x = jnp.concatenate([xt[p, c * PART_PITCH:c * PART_PITCH + PART, :] for c in range(SLAB)],
                                    axis=1).astype(BF16)
                hidden = _silu(_dot(x, wg_ref[...])) * _dot(x, wu_ref[...]) * w_col[p * PART:(p + 1) * PART]
                out = _dot(hidden.astype(BF16), wd_ref[...]) * g2
                for c in range(SLAB):
                    ot[p, c * PART_PITCH:c * PART_PITCH + PART, :] = out[:, c * LANES:(c + 1) * LANES]
            for p in parts:
                for r0 in range(0, PART, SCATTER_GROUP):
                    sl = [slab_rows(p * PART + r0 + u) for u in range(SCATTER_GROUP)]
                    vals = [y_scr[sl[u], :] + ot[p, pl.ds(r0 + u, SLAB, stride=PART_PITCH), :]
                            for u in range(SCATTER_GROUP)]
                    for u in range(SCATTER_GROUP):
                        y_scr[sl[u], :] = vals[u]
            return carry

        lax.fori_loop(0, n_blocks, block, 0)

    @pl.when(e == MOE_EXPERTS - 1)
    def _():
        @pl.when(b + 1 < nb)
        def _():
            t_copy(b + 1).start()

        x1_copy(0, 0).start()

        def tile(s, carry):
            slot = s % 2
            x1_copy(s, slot).wait()

            @pl.when(s + 1 < n_fin)
            def _():
                x1_copy(s + 1, 1 - slot).start()

            @pl.when(s >= 2)
            def _():
                out_copy(s - 2, slot).wait()

            base = pl.multiple_of(s * (TM_FIN * SLAB), TM_FIN * SLAB)
            x1 = xin[slot]
            chunks = [x1[:, c * LANES:(c + 1) * LANES] + y_scr[pl.ds(base + c, TM_FIN, stride=SLAB), :]
                      for c in range(SLAB)]
            y_scr[pl.ds(base, TM_FIN * SLAB), :] = jnp.zeros((TM_FIN * SLAB, LANES), F32)
            ssq = chunks[0] * chunks[0]
            for c in range(1, SLAB):
                ssq = ssq + chunks[c] * chunks[c]
            inv = lax.rsqrt(jnp.sum(ssq, axis=1, keepdims=True) * (1.0 / D_MODEL) + EPS)
            for c in range(SLAB):
                stage[slot, :, c * LANES:(c + 1) * LANES] = chunks[c] * inv * fw_ref[:, c * LANES:(c + 1) * LANES]
            out_copy(s, slot).start()
            return carry

        lax.fori_loop(0, n_fin, tile, 0)
        out_copy(n_fin - 2, n_fin % 2).wait()
        out_copy(n_fin - 1, (n_fin - 1) % 2).wait()


def _cast_kernel(a_ref, b_ref, c_ref, ao_ref, bo_ref, co_ref):
    ao_ref[...] = a_ref[...].astype(BF16)
    bo_ref[...] = b_ref[...].astype(BF16)
    co_ref[...] = c_ref[...].astype(BF16)


def _expert_weights_bf16(wg, wu, wd):
    specs = [pl.BlockSpec((None,) + w.shape[1:], lambda e: (e, 0, 0)) for w in (wg, wu, wd)]
    return pl.pallas_call(
        _cast_kernel,
        grid=(wg.shape[0],),
        in_specs=specs,
        out_specs=specs,
        out_shape=[jax.ShapeDtypeStruct(w.shape, BF16) for w in (wg, wu, wd)],
        compiler_params=pltpu.CompilerParams(vmem_limit_bytes=VMEM_LIMIT),
        name="expert_weights_bf16",
    )(wg, wu, wd)


def _moe(t_slab, x1, route, wg, wu, wd, mod3, final_w):
    b, seq, d = x1.shape
    wg, wu, wd = _expert_weights_bf16(wg, wu, wd)
    nblk_e, blk_start, tok, wslot = _route_tables(route, seq)
    nblk = wslot.shape[1]

    def w_spec(shape):
        return pl.BlockSpec((None,) + shape, lambda bi, e, nbe, bs: (e, 0, 0))

    grid_spec = pltpu.PrefetchScalarGridSpec(
        num_scalar_prefetch=2,
        grid=(b, MOE_EXPERTS),
        in_specs=[
            pl.BlockSpec((None, 1, nblk * BM), lambda bi, e, nbe, bs: (bi, 0, 0), memory_space=pltpu.SMEM),
            pl.BlockSpec((None, nblk, BM), lambda bi, e, nbe, bs: (bi, 0, 0)),
            w_spec((d, D_EXPERT)), w_spec((d, D_EXPERT)), w_spec((D_EXPERT, d)),
            pl.BlockSpec((None, 1, d), lambda bi, e, nbe, bs: (bi * 6 + 5, 0, 0)),
            pl.BlockSpec((1, d), lambda bi, e, nbe, bs: (0, 0)),
            pl.BlockSpec(memory_space=pl.ANY),
            pl.BlockSpec(memory_space=pl.ANY),
        ],
        out_specs=pl.BlockSpec(memory_space=pl.ANY),
        scratch_shapes=[
            pltpu.VMEM(((seq + BM) * SLAB, LANES), F32),
            pltpu.VMEM(((seq + BM) * SLAB, LANES), F32),
            pltpu.VMEM((BLOCK_PARTS, SLAB * PART_PITCH, LANES), F32),
            pltpu.VMEM((BLOCK_PARTS, SLAB * PART_PITCH, LANES), F32),
            pltpu.VMEM((2, TM_FIN, d), F32),
            pltpu.VMEM((2, TM_FIN, d), F32),
            pltpu.SemaphoreType.DMA((5,)),
        ],
    )
    return pl.pallas_call(
        functools.partial(_moe_kernel, seq),
        grid_spec=grid_spec,
        out_shape=jax.ShapeDtypeStruct((b, seq, d), F32),
        compiler_params=pltpu.CompilerParams(
            dimension_semantics=("arbitrary", "arbitrary"), vmem_limit_bytes=VMEM_LIMIT_MOE),
        name="experts_final_norm",
    )(nblk_e, blk_start, tok, wslot, wg, wu, wd, mod3, final_w.reshape(1, d), t_slab, x1)


def _lane_pad(v, offset=0):
    v = v.reshape(-1).astype(F32)
    return jnp.zeros((1, LANES), F32).at[0, offset:offset + v.shape[0]].set(v)


def _to_col_major(t, rows):
    b, length, ch = t.shape
    return t.reshape(b, rows, GRID_W, ch).transpose(0, 2, 1, 3).reshape(b, length, ch)


def _from_col_major(t, rows):
    b, length, ch = t.shape
    return t.reshape(b, GRID_W, rows, ch).transpose(0, 2, 1, 3).reshape(b, length, ch)


def kernel(x, c, ctx, c_ctx, w_mod, b_mod, norm1_w, w_in, ssd_conv_w, ssd_conv_b, ssd_dt_bias, ssd_a_log, ssd_d, ssd_norm_w, ml_conv_w, ml_conv_b, ml_w_qk, ml_gate_b, ml_norm_w, ml_skip, w_out, norm2_w, moe_rg_w, moe_rg_b, moe_re_w, moe_re_b, moe_w_gate, moe_w_up, moe_w_down, final_norm_w):
    b, seq, d = x.shape
    ctx_len = ctx.shape[1]
    rows = seq // GRID_W
    ncc = ctx_len // CHUNK
    assert w_mod.shape[0] == 1 and d == D_MODEL and b + 1 <= 8 and b % NB == 0
    assert seq % TM_IN == 0 and ctx_len % CHUNK == 0 and seq % TM_FIN == 0 and (2 * seq) % BM == 0
    assert seq == rows * GRID_W and GRID_W % W_TILE == 0 and rows % 8 == 0

    c_all = jnp.zeros((8, d), F32).at[:b].set(c).at[b].set(c_ctx)
    mod = _modulation(c_all, w_mod[0], b_mod[0])
    mod3 = mod.reshape(8 * 6, 1, d)

    w = w_in[0]
    ssd_in = SSD_WIDTH + SSD_XBC + 2 * SSD_HEADS
    ml_main = 3 * ML_WIDTH
    n_gate = 2 * SSD_HEADS + 4 * ML_HEADS
    w_cat = jnp.concatenate([
        w[:, :SSD_WIDTH + SSD_XBC], w[:, ssd_in:ssd_in + ml_main],
        w[:, SSD_WIDTH + SSD_XBC:ssd_in], w[:, ssd_in + ml_main:],
        jnp.zeros((d, LANES - n_gate), F32)], axis=1).astype(BF16)
    z, xbc, ml, gates, gates_cm = _in_proj(x, ctx, mod3, norm1_w[0], w_cat)

    e_mats = []
    for direction in range(2):
        lane = jnp.arange(LANES)[:, None]
        head = (jnp.arange(SSD_WIDTH) // SSD_HEADDIM)[None, :]
        e_mats.append((lane == direction * SSD_HEADS + head).astype(BF16))
    cw = jnp.zeros((8, SSD_XBC), F32).at[:CONV_W].set(ssd_conv_w[0])
    dtb = _lane_pad(ssd_dt_bias[0])
    alog = _lane_pad(ssd_a_log[0])
    dsk = jnp.repeat(ssd_d[0], SSD_HEADDIM).reshape(1, SSD_WIDTH)
    shifts = _shift_matrices()
    y_ssd = _ssd_scans(xbc, gates, z, [cw, ssd_conv_b[0].reshape(1, -1), shifts],
                       [dtb, dtb.reshape(LANES, 1), alog, alog.reshape(LANES, 1)], e_mats, dsk,
                       ssd_norm_w[0].reshape(1, -1))

    w_rows = jnp.tile(ml_w_qk[0].reshape(2, ML_WIDTH, ML_QK_BLOCK), (1, 1, ML_WIDTH // ML_QK_BLOCK))
    blk_id = jnp.arange(ML_WIDTH) // ML_QK_BLOCK
    w_bd = jnp.where((blk_id[:, None] == blk_id[None, :])[None], w_rows, 0.0)
    wq = w_bd[0].astype(BF16)
    wk = (w_bd[1] * (ML_HEADDIM ** -0.5)).astype(BF16)
    mcw = jnp.zeros((8, ML_WIDTH), F32).at[:CONV_W].set(ml_conv_w[0])
    gb = _lane_pad(ml_gate_b[0], offset=2 * SSD_HEADS)
    y_ml_cm = _ml_scans(ml, gates_cm, [mcw, ml_conv_b[0].reshape(1, -1), shifts], [wq, wk], [gb, gb.reshape(LANES, 1)],
                        ml_norm_w[0].reshape(1, -1), ml_skip[0].reshape(1, -1))
    y_ml = _from_col_major(y_ml_cm, rows)

    rw = jnp.concatenate([moe_rg_w[0], moe_re_w[0],
                          jnp.zeros((d, LANES - MOE_GROUPS - MOE_EXPERTS), F32)], axis=1)
    rw_hi = rw.astype(BF16)
    rw_lo = (rw - rw_hi.astype(F32)).astype(BF16)
    rb = _lane_pad(jnp.concatenate([moe_rg_b[0], moe_re_b[0]]))
    rw2 = jnp.concatenate([rw_hi, rw_lo], axis=1)
    x1, t_slab, route = _out_proj(x, y_ssd, y_ml, w_out[0].astype(BF16), mod3, norm2_w[0], rw2, rb)

    return _moe(t_slab, x1, route, moe_w_gate[0], moe_w_up[0], moe_w_down[0], mod3, final_norm_w)
```

```python
import functools

import jax
import jax.numpy as jnp
from jax import lax
from jax.experimental import pallas as pl
from jax.experimental.pallas import tpu as pltpu

F32 = jnp.float32
BF16 = jnp.bfloat16
HIGHEST = lax.Precision.HIGHEST

D_MODEL = 1024
GRID_W = 64
EPS = 1e-6
CONV_W = 5
NEG_BIG = -1e30
CHUNK = 128
LANES = 128
HALO = 16
NB = 4

SSD_WIDTH = 512
SSD_HEADS = 8
SSD_HEADDIM = 64
SSD_GROUPS = 2
SSD_STATE = 128
SSD_XBC = SSD_WIDTH + 2 * SSD_GROUPS * SSD_STATE

ML_WIDTH = 512
ML_HEADS = 4
ML_HEADDIM = 128
ML_QK_BLOCK = 4

MOE_GROUPS = 4
MOE_EPG = 8
MOE_EXPERTS = 32
D_EXPERT = 256
ROUTE_LANE0 = MOE_GROUPS

TM_OUT = 512
TM_FIN = 256
SLAB = D_MODEL // LANES
BM = 128
BLOCK_PARTS = 1
PART = BM // BLOCK_PARTS
PART_PITCH = PART + 8
SCATTER_GROUP = 4
VMEM_LIMIT = 48 * 1024 * 1024
VMEM_LIMIT_MOE = 56 * 1024 * 1024


def _silu(v):
    return v * jax.nn.sigmoid(v)


def _softplus(v):
    return jnp.maximum(v, 0.0) + jnp.log1p(jnp.exp(-jnp.abs(v)))


def _dot(a, b):
    return jnp.dot(a, b, preferred_element_type=F32)


def _dot_nt(a, b):
    return lax.dot_general(a, b, (((1,), (1,)), ((), ())), preferred_element_type=F32)


def _dot_hi(a, b):
    return jnp.dot(a, b, preferred_element_type=F32, precision=HIGHEST)


def _mod_kernel(c_ref, w_ref, b_ref, o_ref):
    c = c_ref[...]
    o_ref[...] = _dot_hi(_silu(c), w_ref[...]) + b_ref[...]


def _modulation(c_all, w_mod, b_mod):
    n = w_mod.shape[1]
    bn = 1536
    return pl.pallas_call(
        _mod_kernel,
        grid=(n // bn,),
        in_specs=[
            pl.BlockSpec((8, D_MODEL), lambda j: (0, 0)),
            pl.BlockSpec((D_MODEL, bn), lambda j: (0, j)),
            pl.BlockSpec((1, bn), lambda j: (0, j)),
        ],
        out_specs=pl.BlockSpec((8, bn), lambda j: (0, j)),
        out_shape=jax.ShapeDtypeStruct((8, n), F32),
        compiler_params=pltpu.CompilerParams(vmem_limit_bytes=VMEM_LIMIT),
        name="modulation",
    )(c_all, w_mod, b_mod.reshape(1, n))


W_TILE = 16
COL_Z, COL_XBC, COL_ML, COL_G = 0, SSD_WIDTH, SSD_WIDTH + SSD_XBC, SSD_WIDTH + SSD_XBC + 3 * ML_WIDTH
PROJ_CHUNK = 512


def _norm_mod(xin, sh_ref, sc_ref, nw_ref):
    ms = jnp.mean(xin * xin, axis=-1, keepdims=True)
    y = xin * lax.rsqrt(ms + EPS) * nw_ref[...]
    return (y * (1.0 + sc_ref[...]) + sh_ref[...]).astype(BF16)


def _in_lat_kernel(rows, x_ref, sh_ref, sc_ref, nw_ref, w_ref, z_ref, xbc_ref, ml_ref, g_ref, gml_ref, scr):
    tm = rows * W_TILE
    h = _norm_mod(x_ref[...].reshape(tm, D_MODEL), sh_ref, sc_ref, nw_ref)

    def proj(col, width=PROJ_CHUNK):
        return _dot(h, w_ref[:, col:col + width])

    def to_col_major(val, slab0, dst_ref, lo, dtype):
        n_slab = val.shape[1] // LANES
        for k in range(n_slab):
            scr[slab0 + k] = val[:, k * LANES:(k + 1) * LANES]
        for j in range(W_TILE):
            for k in range(n_slab):
                dst_ref[j, :, lo + k * LANES:lo + (k + 1) * LANES] = (
                    scr[slab0 + k, pl.ds(j, rows, stride=W_TILE), :].astype(dtype))

    slabs = PROJ_CHUNK // LANES
    g = proj(COL_G, LANES)
    g_ref[...] = g.reshape(rows, W_TILE, LANES)
    to_col_major(g, 0, gml_ref, 0, F32)
    for j in range(3 * ML_WIDTH // PROJ_CHUNK):
        to_col_major(proj(COL_ML + j * PROJ_CHUNK), 1 + j * slabs, ml_ref, j * PROJ_CHUNK, BF16)
    z_ref[...] = proj(COL_Z).astype(BF16).reshape(rows, W_TILE, PROJ_CHUNK)
    for j in range(SSD_XBC // PROJ_CHUNK):
        lo = j * PROJ_CHUNK
        xbc_ref[:, :, lo:lo + PROJ_CHUNK] = proj(COL_XBC + lo).astype(BF16).reshape(rows, W_TILE, PROJ_CHUNK)


def _in_ctx_kernel(x_ref, sh_ref, sc_ref, nw_ref, w_ref, xbc_ref, ml_ref, g_ref):
    h = _norm_mod(x_ref[...], sh_ref, sc_ref, nw_ref)
    for j in range(SSD_XBC // PROJ_CHUNK):
        lo = j * PROJ_CHUNK
        xbc_ref[:, lo:lo + PROJ_CHUNK] = _dot(h, w_ref[:, COL_XBC + lo:COL_XBC + lo + PROJ_CHUNK]).astype(BF16)
    for j in range(3 * ML_WIDTH // PROJ_CHUNK):
        lo = j * PROJ_CHUNK
        ml_ref[:, lo:lo + PROJ_CHUNK] = _dot(h, w_ref[:, COL_ML + lo:COL_ML + lo + PROJ_CHUNK]).astype(BF16)
    g_ref[...] = _dot(h, w_ref[:, COL_G:COL_G + LANES])


def _in_proj(x, ctx, mod3, norm_w, w_cat):
    b, seq, d = x.shape
    ctx_len = ctx.shape[1]
    rows = seq // GRID_W
    widths = (SSD_WIDTH, SSD_XBC, 3 * ML_WIDTH, LANES, LANES)
    dtypes = (BF16, BF16, BF16, F32, F32)
    params = pltpu.CompilerParams(dimension_semantics=("arbitrary", "arbitrary"), vmem_limit_bytes=VMEM_LIMIT)
    nw = norm_w.reshape(1, d)

    def raster(width):
        return pl.BlockSpec((None, rows, W_TILE, width), lambda bi, wi: (bi, 0, wi, 0))

    def col_major(width):
        return pl.BlockSpec((None, W_TILE, rows, width), lambda bi, wi: (bi, wi, 0, 0))

    outs = pl.pallas_call(
        functools.partial(_in_lat_kernel, rows),
        grid=(b, GRID_W // W_TILE),
        in_specs=[
            raster(d),
            pl.BlockSpec((None, 1, d), lambda bi, wi: (bi * 6, 0, 0)),
            pl.BlockSpec((None, 1, d), lambda bi, wi: (bi * 6 + 1, 0, 0)),
            _full((1, d)), _full(w_cat.shape),
        ],
        out_specs=[raster(SSD_WIDTH), raster(SSD_XBC), col_major(3 * ML_WIDTH), raster(LANES), col_major(LANES)],
        out_shape=[jax.ShapeDtypeStruct((b, GRID_W, rows, w) if cm else (b, rows, GRID_W, w), t)
                   for w, t, cm in zip(widths, dtypes, (False, False, True, False, True))],
        scratch_shapes=[pltpu.VMEM((1 + 3 * ML_WIDTH // LANES, rows * W_TILE, LANES), F32)],
        compiler_params=params,
        name="in_proj",
    )(x.reshape(b, rows, GRID_W, d), mod3, mod3, nw, w_cat)
    z, xbc, ml, gates, gates_cm = [o.reshape(b, seq, w) for o, w in zip(outs, widths)]

    ctx_row = b
    ctx_widths = (SSD_XBC, 3 * ML_WIDTH, LANES)
    xbc_c, ml_c, gates_c = pl.pallas_call(
        _in_ctx_kernel,
        grid=(b, 1),
        in_specs=[
            pl.BlockSpec((None, ctx_len, d), lambda bi, t: (bi, 0, 0)),
            pl.BlockSpec((None, 1, d), lambda bi, t: (ctx_row * 6, 0, 0)),
            pl.BlockSpec((None, 1, d), lambda bi, t: (ctx_row * 6 + 1, 0, 0)),
            _full((1, d)), _full(w_cat.shape),
        ],
        out_specs=[pl.BlockSpec((None, ctx_len, w), lambda bi, t: (bi, 0, 0)) for w in ctx_widths],
        out_shape=[jax.ShapeDtypeStruct((b, ctx_len, w), t) for w, t in zip(ctx_widths, (BF16, BF16, F32))],
        compiler_params=params,
        name="in_proj_ctx",
    )(ctx, mod3, mod3, nw, w_cat)
    return z, (xbc, xbc_c), (ml, ml_c), (gates, gates_c), (gates_cm, gates_c)


def _chunk_eff(rev, ncc, nc, c):
    nl = nc - ncc
    if not rev:
        return jnp.where(c < ncc, nl + c, c - ncc)
    return jnp.where(c < ncc, nc - 1 - c, nl - 1 - (c - ncc))


def _lat_block(rev, ncc, nc, c):
    nl = nc - ncc
    if not rev:
        return jnp.maximum(c - ncc, 0)
    return jnp.where(c < ncc, nl - 1, nl - 1 - (c - ncc))


CONV_SIDE_TAPS = tuple(t for t in range(CONV_W) if t != CONV_W // 2)


def _shift_matrices():
    row = jnp.arange(CHUNK)[:, None]
    col = jnp.arange(CHUNK + 2 * HALO)[None, :]
    return jnp.stack([col == row + HALO + tap - CONV_W // 2 for tap in CONV_SIDE_TAPS]).astype(BF16)


def _conv_silu(xm, xp, xn, cw_ref, cb_ref, sh_ref, first, last):
    xp = jnp.where(first, jnp.zeros_like(xp), xp)
    xn = jnp.where(last, jnp.zeros_like(xn), xn)
    ext = jnp.concatenate([xp, xm, xn], axis=0)
    mid = CONV_W // 2
    acc = cb_ref[...] + cw_ref[mid:mid + 1, :] * xm.astype(F32)
    for i, tap in enumerate(CONV_SIDE_TAPS):
        acc = acc + cw_ref[tap:tap + 1, :] * _dot(sh_ref[i], ext)
    return _silu(acc)


def _tri(rev, transposed=False):
    row = lax.broadcasted_iota(jnp.int32, (CHUNK, CHUNK), 0)
    col = lax.broadcasted_iota(jnp.int32, (CHUNK, CHUNK), 1)
    if transposed:
        row, col = col, row
    keep = (col >= row) if rev else (col <= row)
    return keep


def _split3(a):
    a1 = a.astype(BF16)
    r1 = a - a1.astype(F32)
    a2 = r1.astype(BF16)
    a3 = (r1 - a2.astype(F32)).astype(BF16)
    return a1, a2, a3


def _dot_sel_l(m_b, a):
    p = _split3(a)
    return _dot(m_b, p[0]) + _dot(m_b, p[1]) + _dot(m_b, p[2])


def _dot_sel_r(a, m_b):
    p = _split3(a)
    return _dot(p[0], m_b) + _dot(p[1], m_b) + _dot(p[2], m_b)


def _row_spec(rows, width, row_block, col_block=0):
    return pl.BlockSpec((NB, rows, width), lambda bi, c: (bi, row_block(c), col_block))


def _local_chunks(rev, ncc, nc, c):
    nl = nc - ncc
    cc = jnp.clip(ncc - 1 - c if rev else c, 0, ncc - 1)
    cl = jnp.clip(nl - 1 - (c - ncc) if rev else c - ncc, 0, nl - 1)
    return c < ncc, cc, cl


def _pair_specs(rev, ncc, nc, rows, width, col=0, halo=0):
    per = CHUNK // HALO

    def index(which, count):
        def fn(c):
            ch = _local_chunks(rev, ncc, nc, c)[which]
            if halo == 0:
                return ch
            if halo < 0:
                return jnp.maximum(ch * per - 1, 0)
            return jnp.minimum((ch + 1) * per, count * per - 1)
        return fn

    return [_row_spec(rows, width, index(2, nc - ncc), col), _row_spec(rows, width, index(1, ncc), col)]


def _pick(is_ctx, lat_ref, ctx_ref, s):
    return jnp.where(is_ctx, ctx_ref[s], lat_ref[s])


def _step_ends(rev, ncc, nc, c):
    is_ctx, cc, cl = _local_chunks(rev, ncc, nc, c)
    first = jnp.where(is_ctx, cc == 0, cl == 0)
    last = jnp.where(is_ctx, cc == ncc - 1, cl == nc - ncc - 1)
    return is_ctx, first, last


def _full(shape):
    return pl.BlockSpec(shape, lambda bi, c: (0,) * len(shape))


def _ssd_gates(rev, xs, g, dtbr_ref, dtbc_ref, alr_ref, alc_ref, e_ref):
    lane0 = SSD_HEADS * int(rev)
    lane = lax.broadcasted_iota(jnp.int32, (CHUNK, LANES), 1)
    lmask = (lane >= lane0) & (lane < lane0 + SSD_HEADS)
    dt = jnp.where(lmask, _softplus(g + dtbr_ref[...]), 0.0)
    a = dt * (-jnp.exp(alr_ref[...]))
    gt = g.T
    dt_t = _softplus(gt + dtbc_ref[...])[lane0:lane0 + SSD_HEADS]
    a_t = dt_t * (-jnp.exp(alc_ref[...][lane0:lane0 + SSD_HEADS]))

    cs = _dot_sel_l(_tri(rev).astype(BF16), a)
    cs_t = _dot_sel_r(a_t, _tri(rev, transposed=True).astype(BF16))
    e = e_ref[...]
    dtx = _dot_sel_r(dt, e)
    csx = _dot_sel_r(cs, e)
    end = 0 if rev else CHUNK - 1
    totx = csx[end:end + 1, :]
    ecsx = jnp.exp(csx)
    decx = jnp.exp(totx - csx)
    etotx = jnp.exp(totx)

    xdt = xs * dtx
    return dict(cs=cs, cs_t=cs_t, ecsx=ecsx, etotx=etotx, xdt_b=xdt.astype(BF16), xd_b=(xdt * decx).astype(BF16))


def _ssd_chunks(rev, xbcs, gs_, dtbr_ref, dtbc_ref, alr_ref, alc_ref, e_ref, s_ref):
    lane0 = SSD_HEADS * int(rev)
    keep = _tri(rev)
    half = lax.broadcasted_iota(jnp.int32, (CHUNK, LANES), 1) // SSD_HEADDIM
    n_bc = SSD_GROUPS * SSD_STATE
    hpg = SSD_HEADS // SSD_GROUPS
    gw = hpg * SSD_HEADDIM

    gq = [_ssd_gates(rev, xbc[:, :SSD_WIDTH], g, dtbr_ref, dtbc_ref, alr_ref, alc_ref, e_ref)
          for xbc, g in zip(xbcs, gs_)]
    units = []
    for s, xbc in enumerate(xbcs):
        for grp in range(SSD_GROUPS):
            bm = xbc[:, SSD_WIDTH + grp * SSD_STATE:SSD_WIDTH + (grp + 1) * SSD_STATE]
            cm = xbc[:, SSD_WIDTH + n_bc + grp * SSD_STATE:SSD_WIDTH + n_bc + (grp + 1) * SSD_STATE]
            units.append(dict(s=s, grp=grp, cols=slice(grp * gw, (grp + 1) * gw), bm=bm, bm_b=bm.astype(BF16),
                              cm_b=cm.astype(BF16), s_old=s_ref[s, :, grp * gw:(grp + 1) * gw]))
    for u in units:
        u["cb"] = _dot_nt(u["cm_b"], u["bm_b"])
        u["y_off"] = _dot(u["cm_b"], u["s_old"].astype(BF16)) * gq[u["s"]]["ecsx"][:, u["cols"]]
        u["bt_b"] = u["bm"].T.astype(BF16)
    for u in units:
        q_ = gq[u["s"]]
        masks = []
        for hh in range(hpg):
            h = u["grp"] * hpg + hh
            dl = q_["cs"][:, lane0 + h:lane0 + h + 1] - q_["cs_t"][h:h + 1, :]
            masks.append((u["cb"] * jnp.exp(jnp.where(keep, dl, NEG_BIG))).astype(BF16))
        u["masks"] = masks
    for u in units:
        q_ = gq[u["s"]]
        blocks = []
        for pair in range(hpg // 2):
            blk = u["grp"] * (hpg // 2) + pair
            xj = q_["xdt_b"][:, blk * LANES:(blk + 1) * LANES]
            acc = u["y_off"][:, pair * LANES:(pair + 1) * LANES]
            for q in range(2):
                acc = acc + _dot(u["masks"][pair * 2 + q], jnp.where(half == q, xj, jnp.zeros_like(xj)))
            blocks.append(acc)
        u["y"] = blocks
    for u in units:
        q_ = gq[u["s"]]
        s_new = u["s_old"] * q_["etotx"][:, u["cols"]] + _dot(u["bt_b"], q_["xd_b"][:, u["cols"]])
        s_ref[u["s"], :, u["cols"]] = s_new
    return [jnp.concatenate([blk for u in units if u["s"] == s for blk in u["y"]], axis=1)
            for s in range(len(xbcs))]


def _ssd_rev_kernel(ncc, nc, xml_ref, xmc_ref, xpl_ref, xpc_ref, xnl_ref, xnc_ref, gl_ref, gc_ref, cw_ref, cb_ref,
                    sh_ref, dtbr_ref, dtbc_ref, alr_ref, alc_ref, e_ref, o_ref, xc_ref, s_ref):
    c = pl.program_id(1)
    is_ctx, first, last = _step_ends(True, ncc, nc, c)

    @pl.when(c == 0)
    def _():
        s_ref[...] = jnp.zeros_like(s_ref)

    xbcs = [_conv_silu(_pick(is_ctx, xml_ref, xmc_ref, s), _pick(is_ctx, xpl_ref, xpc_ref, s),
                       _pick(is_ctx, xnl_ref, xnc_ref, s), cw_ref, cb_ref, sh_ref, first, last) for s in range(NB)]
    for s in range(NB):
        xc_ref[s] = xbcs[s].astype(BF16)
    ys = _ssd_chunks(True, xbcs, [_pick(is_ctx, gl_ref, gc_ref, s) for s in range(NB)], dtbr_ref, dtbc_ref,
                     alr_ref, alc_ref, e_ref, s_ref)
    for s in range(NB):
        o_ref[s] = ys[s].astype(o_ref.dtype)


def _ssd_fwd_kernel(ncc, xc_ref, gl_ref, gc_ref, dtbr_ref, dtbc_ref, alr_ref, alc_ref, e_ref, dsk_ref, yb_ref,
                    z_ref, nw_ref, o_ref, s_ref):
    is_ctx = pl.program_id(1) < ncc

    @pl.when(pl.program_id(1) == 0)
    def _():
        s_ref[...] = jnp.zeros_like(s_ref)

    xbcs = [xc_ref[s].astype(F32) for s in range(NB)]
    ys = _ssd_chunks(False, xbcs, [_pick(is_ctx, gl_ref, gc_ref, s) for s in range(NB)], dtbr_ref, dtbc_ref,
                     alr_ref, alc_ref, e_ref, s_ref)
    ys = [(ys[s] + yb_ref[s].astype(F32) + dsk_ref[...] * xbcs[s][:, :SSD_WIDTH]) * _silu(z_ref[s].astype(F32))
          for s in range(NB)]
    scale = [lax.rsqrt(jnp.mean(y * y, axis=-1, keepdims=True) + EPS) for y in ys]
    for s in range(NB):
        o_ref[s] = (ys[s] * scale[s] * nw_ref[...]).astype(o_ref.dtype)


def _ssd_scans(xbc, gates, z, conv_params, gate_params, e_mats, dsk, nw):
    b, seq, _ = xbc[0].shape
    ncc = xbc[1].shape[1] // CHUNK
    nc = seq // CHUNK + ncc
    state = pltpu.VMEM((NB, SSD_STATE, SSD_WIDTH), F32)
    params = pltpu.CompilerParams(dimension_semantics=("arbitrary", "arbitrary"), vmem_limit_bytes=VMEM_LIMIT)
    lat_shape = jax.ShapeDtypeStruct((b, seq, SSD_WIDTH), BF16)

    ceff = functools.partial(_chunk_eff, True, ncc, nc)
    lat = functools.partial(_lat_block, True, ncc, nc)
    pair = functools.partial(_pair_specs, True, ncc, nc)
    rev_params = list(conv_params) + list(gate_params) + [e_mats[1]]
    yb, xbc_act = pl.pallas_call(
        functools.partial(_ssd_rev_kernel, ncc, nc),
        grid=(b // NB, nc),
        in_specs=pair(CHUNK, SSD_XBC) + pair(HALO, SSD_XBC, halo=-1) + pair(HALO, SSD_XBC, halo=1)
        + pair(CHUNK, LANES) + [_full(p.shape) for p in rev_params],
        out_specs=[_row_spec(CHUNK, SSD_WIDTH, lat), _row_spec(CHUNK, SSD_XBC, ceff)],
        out_shape=(lat_shape, jax.ShapeDtypeStruct((b, nc * CHUNK, SSD_XBC), BF16)),
        scratch_shapes=[state],
        compiler_params=params,
        name="ssd_rev",
    )(*xbc, *xbc, *xbc, *gates, *rev_params)

    ceff = functools.partial(_chunk_eff, False, ncc, nc)
    lat = functools.partial(_lat_block, False, ncc, nc)
    fwd_params = list(gate_params) + [e_mats[0], dsk]
    return pl.pallas_call(
        functools.partial(_ssd_fwd_kernel, ncc),
        grid=(b // NB, nc),
        in_specs=[_row_spec(CHUNK, SSD_XBC, ceff)] + _pair_specs(False, ncc, nc, CHUNK, LANES)
        + [_full(p.shape) for p in fwd_params]
        + [_row_spec(CHUNK, SSD_WIDTH, lat), _row_spec(CHUNK, SSD_WIDTH, lat), _full(nw.shape)],
        out_specs=_row_spec(CHUNK, SSD_WIDTH, lat),
        out_shape=lat_shape,
        scratch_shapes=[state],
        compiler_params=params,
        name="ssd_fwd",
    )(xbc_act, *gates, *fwd_params, yb, z, nw)


ML_I_LANE0 = 2 * SSD_HEADS
ML_F_LANE0 = ML_I_LANE0 + 2 * ML_HEADS


def _ml_gates(rev, g, gbr_ref, gbc_ref):
    i_lane0 = ML_I_LANE0 + ML_HEADS * int(rev)
    f_lane0 = ML_F_LANE0 + ML_HEADS * int(rev)
    ga = g + gbr_ref[...]
    lane = lax.broadcasted_iota(jnp.int32, (CHUNK, LANES), 1)
    logf = jnp.where((lane >= f_lane0) & (lane < f_lane0 + ML_HEADS), -_softplus(-ga), 0.0)
    cs = _dot_sel_l(_tri(rev).astype(BF16), logf)
    gt = g.T + gbc_ref[...]
    i_t = gt[ML_I_LANE0:ML_F_LANE0]
    logf_t = -_softplus(-gt[ML_F_LANE0:ML_F_LANE0 + 2 * ML_HEADS])
    cs_t = _dot_sel_r(logf_t, _tri(rev, transposed=True).astype(BF16))
    end = 0 if rev else CHUNK - 1

    u_t = i_t - cs_t
    u_c = ga - pltpu.roll(cs, LANES - (ML_F_LANE0 - ML_I_LANE0), axis=1)
    row = lax.broadcasted_iota(jnp.int32, (CHUNK, LANES), 0)
    pm = u_c
    step = 1
    while step < CHUNK:
        if rev:
            pm = jnp.maximum(pm, jnp.where(row < CHUNK - step, pltpu.roll(pm, CHUNK - step, axis=0), NEG_BIG))
        else:
            pm = jnp.maximum(pm, jnp.where(row >= step, pltpu.roll(pm, step, axis=0), NEG_BIG))
        step *= 2

    heads = []
    for h in range(ML_HEADS):
        r = ML_HEADS * int(rev) + h
        li = i_lane0 + h
        heads.append(dict(
            csc=cs[:, f_lane0 + h:f_lane0 + h + 1], u_col=u_c[:, li:li + 1], u_row=u_t[r:r + 1, :],
            tot=cs_t[r:r + 1, end:end + 1], u_max=pm[end:end + 1, li:li + 1], pm_col=pm[:, li:li + 1]))
    return heads


def _ml_chunks(rev, samples, gbr_ref, gbc_ref, c_ref, n_ref, mx_ref):
    keep = _tri(rev)
    ones_b = jnp.ones((CHUNK, ML_HEADDIM), BF16)
    units = []
    for s, (k, q_b, k_b, v_b, g) in enumerate(samples):
        for h, gq in enumerate(_ml_gates(rev, g, gbr_ref, gbc_ref)):
            sl = slice(h * ML_HEADDIM, (h + 1) * ML_HEADDIM)
            units.append(dict(gq, s=s, h=h, kh=k[:, sl], vh=v_b[:, sl], qh_b=q_b[:, sl], kh_b=k_b[:, sl],
                              m_prev=mx_ref[s, h:h + 1, 0:1], c_prev=c_ref[s * ML_HEADS + h],
                              n_prev=n_ref[s, h:h + 1, :]))

    for u in units:
        u["qk"] = _dot_nt(u["qh_b"], u["kh_b"])
        u["vt_b"] = u["vh"].astype(F32).T.astype(BF16)
    for u in units:
        u["mm"] = jnp.maximum(u["m_prev"], u["pm_col"])
        u["scores_b"] = (u["qk"] * jnp.exp(jnp.where(keep, u["u_row"] - u["mm"], NEG_BIG))).astype(BF16)
    for u in units:
        c_aug = jnp.concatenate([u["c_prev"], jnp.broadcast_to(u["n_prev"], (CHUNK, ML_HEADDIM))], axis=0)
        u["intra"] = _dot(u["scores_b"], jnp.concatenate([u["vh"], ones_b], axis=1))
        u["inter"] = _dot_nt(u["qh_b"], c_aug.astype(BF16))
    outs = [[None] * ML_HEADS for _ in samples]
    for u in units:
        both = u["intra"] + jnp.exp(u["m_prev"] - u["mm"]) * u["inter"]
        den = both[:, ML_HEADDIM:ML_HEADDIM + 1]
        outs[u["s"]][u["h"]] = both[:, :ML_HEADDIM] / jnp.maximum(jnp.abs(den), jnp.exp(-(u["csc"] + u["mm"])))
    for u in units:
        kw = u["kh"] * jnp.exp(u["u_col"] - u["u_max"])
        u["c_loc"] = _dot(u["vt_b"], kw.astype(BF16))
        u["n_loc"] = jnp.sum(kw, axis=0, keepdims=True)
    for u in units:
        s, h, tot, m_prev = u["s"], u["h"], u["tot"], u["m_prev"]
        m_loc = tot + u["u_max"]
        m_new = jnp.maximum(tot + m_prev, m_loc)
        s_prev = jnp.exp(tot + m_prev - m_new)
        s_loc = jnp.exp(m_loc - m_new)
        c_ref[s * ML_HEADS + h] = s_prev * u["c_prev"] + s_loc * u["c_loc"]
        n_ref[s, h:h + 1, :] = s_prev * u["n_prev"] + s_loc * u["n_loc"]
        mx_ref[s, h:h + 1, :] = jnp.broadcast_to(m_new, (1, LANES))
    return outs


def _ml_init_state(c_ref, n_ref, mx_ref):
    c_ref[...] = jnp.zeros_like(c_ref)
    n_ref[...] = jnp.zeros_like(n_ref)
    mx_ref[...] = jnp.full(mx_ref.shape, NEG_BIG, F32)


def _ml_rev_kernel(ncc, nc, xml_ref, xmc_ref, xpl_ref, xpc_ref, xnl_ref, xnc_ref, vl_ref, vc_ref, gl_ref, gc_ref,
                   cw_ref, cb_ref, sh_ref, wq_ref, wk_ref, gbr_ref, gbc_ref, o_ref, xc_ref, q_ref, k_ref, c_ref, n_ref,
                   mx_ref):
    c = pl.program_id(1)
    is_ctx, first, last = _step_ends(True, ncc, nc, c)

    @pl.when(c == 0)
    def _():
        _ml_init_state(c_ref, n_ref, mx_ref)

    samples = []
    for s in range(NB):
        xconv = _conv_silu(_pick(is_ctx, xml_ref, xmc_ref, s), _pick(is_ctx, xpl_ref, xpc_ref, s),
                           _pick(is_ctx, xnl_ref, xnc_ref, s), cw_ref, cb_ref, sh_ref, first, last)
        xc_b = xconv.astype(BF16)
        q = _dot(xc_b, wq_ref[...])
        k = _dot(xc_b, wk_ref[...])
        q_b, k_b = q.astype(BF16), k.astype(BF16)
        xc_ref[s] = xc_b
        q_ref[s] = q_b
        k_ref[s] = k_b
        samples.append((k, q_b, k_b, _pick(is_ctx, vl_ref, vc_ref, s), _pick(is_ctx, gl_ref, gc_ref, s)))
    outs = _ml_chunks(True, samples, gbr_ref, gbc_ref, c_ref, n_ref, mx_ref)
    for s in range(NB):
        o_ref[s] = jnp.concatenate(outs[s], axis=1).astype(o_ref.dtype)


def _ml_fwd_kernel(ncc, xc_ref, q_ref, k_ref, vl_ref, vc_ref, og_ref, gl_ref, gc_ref, gbr_ref, gbc_ref, hb_ref,
                   nw_ref, sk_ref, o_ref, c_ref, n_ref, mx_ref):
    is_ctx = pl.program_id(1) < ncc

    @pl.when(pl.program_id(1) == 0)
    def _():
        _ml_init_state(c_ref, n_ref, mx_ref)

    samples = []
    for s in range(NB):
        q_b, k_b = q_ref[s], k_ref[s]
        samples.append((k_b.astype(F32), q_b, k_b, _pick(is_ctx, vl_ref, vc_ref, s),
                        _pick(is_ctx, gl_ref, gc_ref, s)))
    outs = _ml_chunks(False, samples, gbr_ref, gbc_ref, c_ref, n_ref, mx_ref)
    gated = []
    for s in range(NB):
        for h in range(ML_HEADS):
            sl = slice(h * ML_HEADDIM, (h + 1) * ML_HEADDIM)
            gated.append(jax.nn.sigmoid(og_ref[s, :, sl].astype(F32)) * (outs[s][h] + hb_ref[s, :, sl].astype(F32)))
    scale = [lax.rsqrt(jnp.mean(hh * hh, axis=-1, keepdims=True) + EPS) for hh in gated]
    normed = [hh * sc for hh, sc in zip(gated, scale)]
    for s in range(NB):
        y = jnp.concatenate(normed[s * ML_HEADS:(s + 1) * ML_HEADS], axis=1) * nw_ref[...]
        o_ref[s] = (y + sk_ref[...] * xc_ref[s].astype(F32)).astype(o_ref.dtype)


def _ml_scans(ml, gates, conv_params, proj_params, gate_params, nw, sk):
    b, seq, _ = ml[0].shape
    ncc = ml[1].shape[1] // CHUNK
    nc = seq // CHUNK + ncc
    scratch = [pltpu.VMEM((NB * ML_HEADS, ML_HEADDIM, ML_HEADDIM), F32),
               pltpu.VMEM((NB, 8, ML_HEADDIM), F32),
               pltpu.VMEM((NB, 8, LANES), F32)]
    params = pltpu.CompilerParams(dimension_semantics=("arbitrary", "arbitrary"), vmem_limit_bytes=VMEM_LIMIT)
    lat_shape = jax.ShapeDtypeStruct((b, seq, ML_WIDTH), BF16)
    act_shape = jax.ShapeDtypeStruct((b, nc * CHUNK, ML_WIDTH), BF16)

    ceff = functools.partial(_chunk_eff, True, ncc, nc)
    lat = functools.partial(_lat_block, True, ncc, nc)
    pair = functools.partial(_pair_specs, True, ncc, nc)
    rev_params = list(conv_params) + list(proj_params) + list(gate_params)
    act_spec = _row_spec(CHUNK, ML_WIDTH, ceff)
    hb, xc, q, k = pl.pallas_call(
        functools.partial(_ml_rev_kernel, ncc, nc),
        grid=(b // NB, nc),
        in_specs=pair(CHUNK, ML_WIDTH) + pair(HALO, ML_WIDTH, halo=-1) + pair(HALO, ML_WIDTH, halo=1)
        + pair(CHUNK, ML_WIDTH, col=1) + pair(CHUNK, LANES) + [_full(p.shape) for p in rev_params],
        out_specs=[_row_spec(CHUNK, ML_WIDTH, lat), act_spec, act_spec, act_spec],
        out_shape=(lat_shape, act_shape, act_shape, act_shape),
        scratch_shapes=scratch,
        compiler_params=params,
        name="mlstm_rev",
    )(*ml, *ml, *ml, *ml, *gates, *rev_params)

    ceff = functools.partial(_chunk_eff, False, ncc, nc)
    lat = functools.partial(_lat_block, False, ncc, nc)
    pair = functools.partial(_pair_specs, False, ncc, nc)
    act_spec = _row_spec(CHUNK, ML_WIDTH, ceff)
    return pl.pallas_call(
        functools.partial(_ml_fwd_kernel, ncc),
        grid=(b // NB, nc),
        in_specs=[act_spec, act_spec, act_spec] + pair(CHUNK, ML_WIDTH, col=1)
        + [_row_spec(CHUNK, ML_WIDTH, lat, 2)] + pair(CHUNK, LANES)
        + [_full(p.shape) for p in gate_params]
        + [_row_spec(CHUNK, ML_WIDTH, lat), _full(nw.shape), _full(sk.shape)],
        out_specs=_row_spec(CHUNK, ML_WIDTH, lat),
        out_shape=lat_shape,
        scratch_shapes=scratch,
        compiler_params=params,
        name="mlstm_fwd",
    )(xc, q, k, *ml, ml[0], *gates, *gate_params, hb, nw, sk)


def _out_kernel(x_ref, ys_ref, ym_ref, wo_ref, g1_ref, sh_ref, sc_ref, nw_ref, rw_ref, rb_ref,
                x1_ref, ts_ref, route_ref):
    mix = _dot(ys_ref[...], wo_ref[0:SSD_WIDTH, :]) + _dot(ym_ref[...], wo_ref[SSD_WIDTH:, :])
    x1 = x_ref[...] + g1_ref[...] * mix
    y = x1 * lax.rsqrt(jnp.mean(x1 * x1, axis=-1, keepdims=True) + EPS) * nw_ref[...]
    t = y * (1.0 + sc_ref[...]) + sh_ref[...]
    x1_ref[...] = x1
    for j in range(SLAB):
        ts_ref[pl.ds(j, TM_OUT, stride=SLAB), :] = t[:, j * LANES:(j + 1) * LANES]
    lg2 = _dot(t.astype(BF16), rw_ref[...])
    lg = lg2[:, :LANES] + lg2[:, LANES:] + rb_ref[...]

    lane = lax.broadcasted_iota(jnp.int32, lg.shape, 1).astype(F32)
    gmask = lane < MOE_GROUPS
    gl = jnp.where(gmask, lg, NEG_BIG)
    gmax = jnp.max(gl, axis=1, keepdims=True)
    g_sel = jnp.min(jnp.where(gmask & (gl == gmax), lane, 1e9), axis=1, keepdims=True)
    p_group = 1.0 / jnp.sum(jnp.where(gmask, jnp.exp(gl - gmax), 0.0), axis=1, keepdims=True)
    lo = ROUTE_LANE0 + MOE_EPG * g_sel
    emask = (lane >= lo) & (lane < lo + MOE_EPG)
    l1 = jnp.max(jnp.where(emask, lg, NEG_BIG), axis=1, keepdims=True)
    i1 = jnp.min(jnp.where(emask & (lg == l1), lane, 1e9), axis=1, keepdims=True)
    emask2 = emask & (lane != i1)
    l2 = jnp.max(jnp.where(emask2, lg, NEG_BIG), axis=1, keepdims=True)
    i2 = jnp.min(jnp.where(emask2 & (lg == l2), lane, 1e9), axis=1, keepdims=True)
    r = jnp.exp(l2 - l1)
    w1 = p_group / (1.0 + r)
    w2 = p_group * r / (1.0 + r)
    route = (jnp.where(lane == 0, i1, 0.0) + jnp.where(lane == 1, i2, 0.0)
             + jnp.where(lane == 2, w1, 0.0) + jnp.where(lane == 3, w2, 0.0))
    route_ref[...] = route.T[0:8, :]


def _out_proj(x, y_ssd, y_ml, w_out_b, mod3, norm_w, rw2, rb):
    b, seq, d = x.shape
    nt = seq // TM_OUT

    def row(j):
        return pl.BlockSpec((None, 1, d), lambda bi, t: (bi * 6 + j, 0, 0))

    def tile(width):
        return pl.BlockSpec((None, TM_OUT, width), lambda bi, t: (bi, t, 0))

    return pl.pallas_call(
        _out_kernel,
        grid=(b, nt),
        in_specs=[tile(d), tile(SSD_WIDTH), tile(ML_WIDTH), _full(w_out_b.shape),
                  row(2), row(3), row(4), _full((1, d)), _full(rw2.shape), _full(rb.shape)],
        out_specs=[tile(d),
                   pl.BlockSpec((None, TM_OUT * SLAB, LANES), lambda bi, t: (bi, t, 0)),
                   pl.BlockSpec((None, 8, TM_OUT), lambda bi, t: (bi, 0, t))],
        out_shape=(jax.ShapeDtypeStruct((b, seq, d), F32),
                   jax.ShapeDtypeStruct((b, seq * SLAB, LANES), F32),
                   jax.ShapeDtypeStruct((b, 8, seq), F32)),
        compiler_params=pltpu.CompilerParams(
            dimension_semantics=("arbitrary", "arbitrary"), vmem_limit_bytes=VMEM_LIMIT),
        name="out_proj_router",
    )(x, y_ssd, y_ml, w_out_b, mod3, mod3, mod3, norm_w.reshape(1, d), rw2, rb)


def _route_tables(route, seq):
    b = route.shape[0]
    n_inst = 2 * seq
    nblk = n_inst // BM + MOE_EXPERTS
    e_flat = (route[:, 0:2, :].astype(jnp.int32) - ROUTE_LANE0).reshape(b, n_inst)
    w_flat = route[:, 2:4, :].reshape(b, n_inst)
    tok_flat = jnp.tile(jnp.arange(seq, dtype=jnp.int32), 2 * b).reshape(b, n_inst)
    _, tok_sorted, w_sorted = lax.sort((e_flat, tok_flat, w_flat), dimension=1, num_keys=1)
    experts = jnp.arange(MOE_EXPERTS, dtype=jnp.int32)
    counts = jnp.sum((e_flat[:, None, :] == experts[None, :, None]).astype(jnp.int32), axis=2)
    nblk_e = (counts + BM - 1) // BM
    blk_end = jnp.cumsum(nblk_e, axis=1)
    blk_start = blk_end - nblk_e
    cnt_start = jnp.cumsum(counts, axis=1) - counts
    nb = blk_end[:, -1:]
    j = jnp.arange(nblk, dtype=jnp.int32)[None, :]
    valid_blk = j < nb
    jj = jnp.minimum(j, nb - 1)
    e_j = jnp.sum((jj[:, :, None] >= blk_end[:, None, :]).astype(jnp.int32), axis=2)
    onehot = (e_j[:, :, None] == experts[None, None, :]).astype(jnp.int32)
    take = lambda tbl: jnp.sum(onehot * tbl[:, None, :], axis=2)
    r = jnp.arange(BM, dtype=jnp.int32)[None, None, :]
    rank = ((jj - take(blk_start)) * BM)[:, :, None] + r
    valid = valid_blk[:, :, None] & (rank < take(counts)[:, :, None])
    sidx = jnp.clip(take(cnt_start)[:, :, None] + rank, 0, n_inst - 1).reshape(b, nblk * BM)
    tok = jnp.take_along_axis(tok_sorted, sidx, axis=1).reshape(b, nblk, BM)
    wslot = jnp.take_along_axis(w_sorted, sidx, axis=1).reshape(b, nblk, BM)
    tok = jnp.where(valid, tok, seq + r) * SLAB
    wslot = jnp.where(valid, wslot, 0.0)
    return (nblk_e.reshape(-1), blk_start.reshape(-1), tok.reshape(b, 1, nblk * BM), wslot)


SEM_T, SEM_X, SEM_O = 0, 1, 3


def _moe_kernel(seq, nbe_ref, bs_ref, tok_ref, ws_ref, wg_ref, wu_ref, wd_ref, g2_ref, fw_ref,
                t_hbm, x1_hbm, o_hbm, t_scr, y_scr, wgb, wub, wdb, xt, ot, xin, stage, sems):
    b = pl.program_id(0)
    e = pl.program_id(1)
    nb = pl.num_programs(0)
    rows = seq * SLAB
    n_fin = seq // TM_FIN

    def t_copy(sample):
        return pltpu.make_async_copy(t_hbm.at[sample], t_scr.at[pl.ds(0, rows)], sems.at[SEM_T])

    def x1_copy(s, slot):
        return pltpu.make_async_copy(x1_hbm.at[b, pl.ds(s * TM_FIN, TM_FIN)], xin.at[slot], sems.at[SEM_X + slot])

    def out_copy(s, slot):
        return pltpu.make_async_copy(stage.at[slot], o_hbm.at[b, pl.ds(s * TM_FIN, TM_FIN)], sems.at[SEM_O + slot])

    @pl.when((e == 0) & (b == 0))
    def _():
        t_copy(b).start()
        y_scr[...] = jnp.zeros_like(y_scr)
        t_scr[pl.ds(rows, BM * SLAB), :] = jnp.zeros((BM * SLAB, LANES), F32)

    @pl.when(e == 0)
    def _():
        t_copy(b).wait()

    n_blocks = nbe_ref[b * MOE_EXPERTS + e]
    blk0 = bs_ref[b * MOE_EXPERTS + e]

    @pl.when(n_blocks > 0)
    def _():
        wgb[...] = wg_ref[...].astype(BF16)
        wub[...] = wu_ref[...].astype(BF16)
        wdb[...] = wd_ref[...].astype(BF16)
        g2 = g2_ref[...]
        diag = (lax.broadcasted_iota(jnp.int32, (BM, BM), 0) == lax.broadcasted_iota(jnp.int32, (BM, BM), 1))

        def block(i, carry):
            blk = blk0 + i
            base = blk * BM

            tok_blk = tok_ref.at[0, pl.ds(base, BM)]

            def slab_rows(r):
                return pl.ds(pl.multiple_of(tok_blk[r], SLAB), SLAB)

            w_col = jnp.sum(jnp.where(diag, ws_ref[pl.ds(blk, 1), :], 0.0), axis=1, keepdims=True)
            parts = range(BLOCK_PARTS)
            for p in parts:
                for r in range(PART):
                    xt[p, pl.ds(r, SLAB, stride=PART_PITCH), :] = t_scr[slab_rows(p * PART + r), :]
            for p in parts:
                x = jnp.concatenate([xt[p, c * PART_PITCH:c * PART_PITCH + PART, :] for c in range(SLAB)],
                                    axis=1).astype(BF16)
                hidden = _silu(_dot(x, wgb[...])) * _dot(x, wub[...]) * w_col[p * PART:(p + 1) * PART]
                out = _dot(hidden.astype(BF16), wdb[...]) * g2
                for c in range(SLAB):
                    ot[p, c * PART_PITCH:c * PART_PITCH + PART, :] = out[:, c * LANES:(c + 1) * LANES]
            for p in parts:
                for r0 in range(0, PART, SCATTER_GROUP):
                    sl = [slab_rows(p * PART + r0 + u) for u in range(SCATTER_GROUP)]
                    vals = [y_scr[sl[u], :] + ot[p, pl.ds(r0 + u, SLAB, stride=PART_PITCH), :]
                            for u in range(SCATTER_GROUP)]
                    for u in range(SCATTER_GROUP):
                        y_scr[sl[u], :] = vals[u]
            return carry

        lax.fori_loop(0, n_blocks, block, 0)

    @pl.when(e == MOE_EXPERTS - 1)
    def _():
        @pl.when(b + 1 < nb)
        def _():
            t_copy(b + 1).start()

        x1_copy(0, 0).start()

        def tile(s, carry):
            slot = s % 2
            x1_copy(s, slot).wait()

            @pl.when(s + 1 < n_fin)
            def _():
                x1_copy(s + 1, 1 - slot).start()

            @pl.when(s >= 2)
            def _():
                out_copy(s - 2, slot).wait()

            base = pl.multiple_of(s * (TM_FIN * SLAB), TM_FIN * SLAB)
            x1 = xin[slot]
            chunks = [x1[:, c * LANES:(c + 1) * LANES] + y_scr[pl.ds(base + c, TM_FIN, stride=SLAB), :]
                      for c in range(SLAB)]
            y_scr[pl.ds(base, TM_FIN * SLAB), :] = jnp.zeros((TM_FIN * SLAB, LANES), F32)
            ssq = chunks[0] * chunks[0]
            for c in range(1, SLAB):
                ssq = ssq + chunks[c] * chunks[c]
            inv = lax.rsqrt(jnp.sum(ssq, axis=1, keepdims=True) * (1.0 / D_MODEL) + EPS)
            for c in range(SLAB):
                stage[slot, :, c * LANES:(c + 1) * LANES] = chunks[c] * inv * fw_ref[:, c * LANES:(c + 1) * LANES]
            out_copy(s, slot).start()
            return carry

        lax.fori_loop(0, n_fin, tile, 0)
        out_copy(n_fin - 2, n_fin % 2).wait()
        out_copy(n_fin - 1, (n_fin - 1) % 2).wait()


def _moe(t_slab, x1, route, wg, wu, wd, mod3, final_w):
    b, seq, d = x1.shape
    nblk_e, blk_start, tok, wslot = _route_tables(route, seq)
    nblk = wslot.shape[1]

    def w_spec(shape):
        return pl.BlockSpec((None,) + shape, lambda bi, e, nbe, bs: (e, 0, 0))

    grid_spec = pltpu.PrefetchScalarGridSpec(
        num_scalar_prefetch=2,
        grid=(b, MOE_EXPERTS),
        in_specs=[
            pl.BlockSpec((None, 1, nblk * BM), lambda bi, e, nbe, bs: (bi, 0, 0), memory_space=pltpu.SMEM),
            pl.BlockSpec((None, nblk, BM), lambda bi, e, nbe, bs: (bi, 0, 0)),
            w_spec((d, D_EXPERT)), w_spec((d, D_EXPERT)), w_spec((D_EXPERT, d)),
            pl.BlockSpec((None, 1, d), lambda bi, e, nbe, bs: (bi * 6 + 5, 0, 0)),
            pl.BlockSpec((1, d), lambda bi, e, nbe, bs: (0, 0)),
            pl.BlockSpec(memory_space=pl.ANY),
            pl.BlockSpec(memory_space=pl.ANY),
        ],
        out_specs=pl.BlockSpec(memory_space=pl.ANY),
        scratch_shapes=[
            pltpu.VMEM(((seq + BM) * SLAB, LANES), F32),
            pltpu.VMEM(((seq + BM) * SLAB, LANES), F32),
            pltpu.VMEM((d, D_EXPERT), BF16),
            pltpu.VMEM((d, D_EXPERT), BF16),
            pltpu.VMEM((D_EXPERT, d), BF16),
            pltpu.VMEM((BLOCK_PARTS, SLAB * PART_PITCH, LANES), F32),
            pltpu.VMEM((BLOCK_PARTS, SLAB * PART_PITCH, LANES), F32),
            pltpu.VMEM((2, TM_FIN, d), F32),
            pltpu.VMEM((2, TM_FIN, d), F32),
            pltpu.SemaphoreType.DMA((5,)),
        ],
    )
    return pl.pallas_call(
        functools.partial(_moe_kernel, seq),
        grid_spec=grid_spec,
        out_shape=jax.ShapeDtypeStruct((b, seq, d), F32),
        compiler_params=pltpu.CompilerParams(
            dimension_semantics=("arbitrary", "arbitrary"), vmem_limit_bytes=VMEM_LIMIT_MOE),
        name="experts_final_norm",
    )(nblk_e, blk_start, tok, wslot, wg, wu, wd, mod3, final_w.reshape(1, d), t_slab, x1)


def _lane_pad(v, offset=0):
    v = v.reshape(-1).astype(F32)
    return jnp.zeros((1, LANES), F32).at[0, offset:offset + v.shape[0]].set(v)


def _from_col_major(t, rows):
    b, length, ch = t.shape
    return t.reshape(b, GRID_W, rows, ch).transpose(0, 2, 1, 3).reshape(b, length, ch)


def kernel(x, c, ctx, c_ctx, w_mod, b_mod, norm1_w, w_in, ssd_conv_w, ssd_conv_b, ssd_dt_bias, ssd_a_log, ssd_d, ssd_norm_w, ml_conv_w, ml_conv_b, ml_w_qk, ml_gate_b, ml_norm_w, ml_skip, w_out, norm2_w, moe_rg_w, moe_rg_b, moe_re_w, moe_re_b, moe_w_gate, moe_w_up, moe_w_down, final_norm_w):
    b, seq, d = x.shape
    ctx_len = ctx.shape[1]
    rows = seq // GRID_W
    assert w_mod.shape[0] == 1 and d == D_MODEL and b + 1 <= 8 and b % NB == 0
    assert seq % TM_OUT == 0 and ctx_len % CHUNK == 0 and seq % TM_FIN == 0 and (2 * seq) % BM == 0
    assert seq == rows * GRID_W and GRID_W % W_TILE == 0 and rows % 8 == 0

    c_all = jnp.zeros((8, d), F32).at[:b].set(c).at[b].set(c_ctx)
    mod = _modulation(c_all, w_mod[0], b_mod[0])
    mod3 = mod.reshape(8 * 6, 1, d)

    w = w_in[0]
    ssd_in = SSD_WIDTH + SSD_XBC + 2 * SSD_HEADS
    ml_main = 3 * ML_WIDTH
    n_gate = 2 * SSD_HEADS + 4 * ML_HEADS
    w_cat = jnp.concatenate([
        w[:, :SSD_WIDTH + SSD_XBC], w[:, ssd_in:ssd_in + ml_main],
        w[:, SSD_WIDTH + SSD_XBC:ssd_in], w[:, ssd_in + ml_main:],
        jnp.zeros((d, LANES - n_gate), F32)], axis=1).astype(BF16)
    z, xbc, ml, gates, gates_cm = _in_proj(x, ctx, mod3, norm1_w[0], w_cat)

    e_mats = []
    for direction in range(2):
        lane = jnp.arange(LANES)[:, None]
        head = (jnp.arange(SSD_WIDTH) // SSD_HEADDIM)[None, :]
        e_mats.append((lane == direction * SSD_HEADS + head).astype(BF16))
    cw = jnp.zeros((8, SSD_XBC), F32).at[:CONV_W].set(ssd_conv_w[0])
    dtb = _lane_pad(ssd_dt_bias[0])
    alog = _lane_pad(ssd_a_log[0])
    dsk = jnp.repeat(ssd_d[0], SSD_HEADDIM).reshape(1, SSD_WIDTH)
    shifts = _shift_matrices()
    y_ssd = _ssd_scans(xbc, gates, z, [cw, ssd_conv_b[0].reshape(1, -1), shifts],
                       [dtb, dtb.reshape(LANES, 1), alog, alog.reshape(LANES, 1)], e_mats, dsk,
                       ssd_norm_w[0].reshape(1, -1))

    w_rows = jnp.tile(ml_w_qk[0].reshape(2, ML_WIDTH, ML_QK_BLOCK), (1, 1, ML_WIDTH // ML_QK_BLOCK))
    blk_id = jnp.arange(ML_WIDTH) // ML_QK_BLOCK
    w_bd = jnp.where((blk_id[:, None] == blk_id[None, :])[None], w_rows, 0.0)
    wq = w_bd[0].astype(BF16)
    wk = (w_bd[1] * (ML_HEADDIM ** -0.5)).astype(BF16)
    mcw = jnp.zeros((8, ML_WIDTH), F32).at[:CONV_W].set(ml_conv_w[0])
    gb = _lane_pad(ml_gate_b[0], offset=2 * SSD_HEADS)
    y_ml_cm = _ml_scans(ml, gates_cm, [mcw, ml_conv_b[0].reshape(1, -1), shifts], [wq, wk], [gb, gb.reshape(LANES, 1)],
                        ml_norm_w[0].reshape(1, -1), ml_skip[0].reshape(1, -1))
    y_ml = _from_col_major(y_ml_cm, rows)

    rw = jnp.concatenate([moe_rg_w[0], moe_re_w[0],
                          jnp.zeros((d, LANES - MOE_GROUPS - MOE_EXPERTS), F32)], axis=1)
    rw_hi = rw.astype(BF16)
    rw_lo = (rw - rw_hi.astype(F32)).astype(BF16)
    rb = _lane_pad(jnp.concatenate([moe_rg_b[0], moe_re_b[0]]))
    rw2 = jnp.concatenate([rw_hi, rw_lo], axis=1)
    x1, t_slab, route = _out_proj(x, y_ssd, y_ml, w_out[0].astype(BF16), mod3, norm2_w[0], rw2, rb)

    return _moe(t_slab, x1, route, moe_w_gate[0], moe_w_up[0], moe_w_down[0], mod3, final_norm_w)
```

```python
import functools

import jax
import jax.numpy as jnp
from jax import lax
from jax.experimental import pallas as pl
from jax.experimental.pallas import tpu as pltpu

F32 = jnp.float32
BF16 = jnp.bfloat16
HIGHEST = lax.Precision.HIGHEST

D_MODEL = 1024
GRID_W = 64
EPS = 1e-6
CONV_W = 5
NEG_BIG = -1e30
CHUNK = 128
LANES = 128
HALO = 16
NB = 4

SSD_WIDTH = 512
SSD_HEADS = 8
SSD_HEADDIM = 64
SSD_GROUPS = 2
SSD_STATE = 128
SSD_XBC = SSD_WIDTH + 2 * SSD_GROUPS * SSD_STATE

ML_WIDTH = 512
ML_HEADS = 4
ML_HEADDIM = 128
ML_QK_BLOCK = 4

MOE_GROUPS = 4
MOE_EPG = 8
MOE_EXPERTS = 32
D_EXPERT = 256
ROUTE_LANE0 = MOE_GROUPS

TM_OUT = 512
TM_FIN = 512
SLAB = D_MODEL // LANES
BM = 128
BLOCK_PARTS = 1
PART = BM // BLOCK_PARTS
PART_PITCH = PART + 8
SCATTER_GROUP = 4
VMEM_LIMIT = 48 * 1024 * 1024
VMEM_LIMIT_MOE = 56 * 1024 * 1024


def _silu(v):
    return v * jax.nn.sigmoid(v)


def _softplus(v):
    return jnp.maximum(v, 0.0) + jnp.log1p(jnp.exp(-jnp.abs(v)))


def _dot(a, b):
    return jnp.dot(a, b, preferred_element_type=F32)


def _dot_nt(a, b):
    return lax.dot_general(a, b, (((1,), (1,)), ((), ())), preferred_element_type=F32)


def _dot_hi(a, b):
    return jnp.dot(a, b, preferred_element_type=F32, precision=HIGHEST)


def _mod_kernel(c_ref, w_ref, b_ref, o_ref):
    c = c_ref[...]
    o_ref[...] = _dot_hi(_silu(c), w_ref[...]) + b_ref[...]


def _modulation(c_all, w_mod, b_mod):
    n = w_mod.shape[1]
    bn = 1536
    return pl.pallas_call(
        _mod_kernel,
        grid=(n // bn,),
        in_specs=[
            pl.BlockSpec((8, D_MODEL), lambda j: (0, 0)),
            pl.BlockSpec((D_MODEL, bn), lambda j: (0, j)),
            pl.BlockSpec((1, bn), lambda j: (0, j)),
        ],
        out_specs=pl.BlockSpec((8, bn), lambda j: (0, j)),
        out_shape=jax.ShapeDtypeStruct((8, n), F32),
        compiler_params=pltpu.CompilerParams(vmem_limit_bytes=VMEM_LIMIT),
        name="modulation",
    )(c_all, w_mod, b_mod.reshape(1, n))


W_TILE = 16
COL_Z, COL_XBC, COL_ML, COL_G = 0, SSD_WIDTH, SSD_WIDTH + SSD_XBC, SSD_WIDTH + SSD_XBC + 3 * ML_WIDTH
PROJ_CHUNK = 512


def _norm_mod(xin, sh_ref, sc_ref, nw_ref):
    ms = jnp.mean(xin * xin, axis=-1, keepdims=True)
    y = xin * lax.rsqrt(ms + EPS) * nw_ref[...]
    return (y * (1.0 + sc_ref[...]) + sh_ref[...]).astype(BF16)


def _in_lat_kernel(rows, x_ref, sh_ref, sc_ref, nw_ref, w_ref, z_ref, xbc_ref, ml_ref, g_ref, gml_ref, scr):
    tm = rows * W_TILE
    h = _norm_mod(x_ref[...].reshape(tm, D_MODEL), sh_ref, sc_ref, nw_ref)

    def proj(col, width=PROJ_CHUNK):
        return _dot(h, w_ref[:, col:col + width])

    def to_col_major(val, slab0, dst_ref, lo, dtype):
        n_slab = val.shape[1] // LANES
        for k in range(n_slab):
            scr[slab0 + k] = val[:, k * LANES:(k + 1) * LANES]
        for j in range(W_TILE):
            for k in range(n_slab):
                dst_ref[j, :, lo + k * LANES:lo + (k + 1) * LANES] = (
                    scr[slab0 + k, pl.ds(j, rows, stride=W_TILE), :].astype(dtype))

    slabs = PROJ_CHUNK // LANES
    g = proj(COL_G, LANES)
    g_ref[...] = g.reshape(rows, W_TILE, LANES)
    to_col_major(g, 0, gml_ref, 0, F32)
    for j in range(3 * ML_WIDTH // PROJ_CHUNK):
        to_col_major(proj(COL_ML + j * PROJ_CHUNK), 1 + j * slabs, ml_ref, j * PROJ_CHUNK, BF16)
    z_ref[...] = proj(COL_Z).astype(BF16).reshape(rows, W_TILE, PROJ_CHUNK)
    for j in range(SSD_XBC // PROJ_CHUNK):
        lo = j * PROJ_CHUNK
        xbc_ref[:, :, lo:lo + PROJ_CHUNK] = proj(COL_XBC + lo).astype(BF16).reshape(rows, W_TILE, PROJ_CHUNK)


def _in_ctx_kernel(x_ref, sh_ref, sc_ref, nw_ref, w_ref, xbc_ref, ml_ref, g_ref):
    h = _norm_mod(x_ref[...], sh_ref, sc_ref, nw_ref)
    for j in range(SSD_XBC // PROJ_CHUNK):
        lo = j * PROJ_CHUNK
        xbc_ref[:, lo:lo + PROJ_CHUNK] = _dot(h, w_ref[:, COL_XBC + lo:COL_XBC + lo + PROJ_CHUNK]).astype(BF16)
    for j in range(3 * ML_WIDTH // PROJ_CHUNK):
        lo = j * PROJ_CHUNK
        ml_ref[:, lo:lo + PROJ_CHUNK] = _dot(h, w_ref[:, COL_ML + lo:COL_ML + lo + PROJ_CHUNK]).astype(BF16)
    g_ref[...] = _dot(h, w_ref[:, COL_G:COL_G + LANES])


def _in_proj(x, ctx, mod3, norm_w, w_cat):
    b, seq, d = x.shape
    ctx_len = ctx.shape[1]
    rows = seq // GRID_W
    widths = (SSD_WIDTH, SSD_XBC, 3 * ML_WIDTH, LANES, LANES)
    dtypes = (BF16, BF16, BF16, F32, F32)
    params = pltpu.CompilerParams(dimension_semantics=("arbitrary", "arbitrary"), vmem_limit_bytes=VMEM_LIMIT)
    nw = norm_w.reshape(1, d)

    def raster(width):
        return pl.BlockSpec((None, rows, W_TILE, width), lambda bi, wi: (bi, 0, wi, 0))

    def col_major(width):
        return pl.BlockSpec((None, W_TILE, rows, width), lambda bi, wi: (bi, wi, 0, 0))

    outs = pl.pallas_call(
        functools.partial(_in_lat_kernel, rows),
        grid=(b, GRID_W // W_TILE),
        in_specs=[
            raster(d),
            pl.BlockSpec((None, 1, d), lambda bi, wi: (bi * 6, 0, 0)),
            pl.BlockSpec((None, 1, d), lambda bi, wi: (bi * 6 + 1, 0, 0)),
            _full((1, d)), _full(w_cat.shape),
        ],
        out_specs=[raster(SSD_WIDTH), raster(SSD_XBC), col_major(3 * ML_WIDTH), raster(LANES), col_major(LANES)],
        out_shape=[jax.ShapeDtypeStruct((b, GRID_W, rows, w) if cm else (b, rows, GRID_W, w), t)
                   for w, t, cm in zip(widths, dtypes, (False, False, True, False, True))],
        scratch_shapes=[pltpu.VMEM((1 + 3 * ML_WIDTH // LANES, rows * W_TILE, LANES), F32)],
        compiler_params=params,
        name="in_proj",
    )(x.reshape(b, rows, GRID_W, d), mod3, mod3, nw, w_cat)
    z, xbc, ml, gates, gates_cm = [o.reshape(b, seq, w) for o, w in zip(outs, widths)]

    ctx_row = b
    ctx_widths = (SSD_XBC, 3 * ML_WIDTH, LANES)
    xbc_c, ml_c, gates_c = pl.pallas_call(
        _in_ctx_kernel,
        grid=(b, 1),
        in_specs=[
            pl.BlockSpec((None, ctx_len, d), lambda bi, t: (bi, 0, 0)),
            pl.BlockSpec((None, 1, d), lambda bi, t: (ctx_row * 6, 0, 0)),
            pl.BlockSpec((None, 1, d), lambda bi, t: (ctx_row * 6 + 1, 0, 0)),
            _full((1, d)), _full(w_cat.shape),
        ],
        out_specs=[pl.BlockSpec((None, ctx_len, w), lambda bi, t: (bi, 0, 0)) for w in ctx_widths],
        out_shape=[jax.ShapeDtypeStruct((b, ctx_len, w), t) for w, t in zip(ctx_widths, (BF16, BF16, F32))],
        compiler_params=params,
        name="in_proj_ctx",
    )(ctx, mod3, mod3, nw, w_cat)
    return z, (xbc, xbc_c), (ml, ml_c), (gates, gates_c), (gates_cm, gates_c)


def _chunk_eff(rev, ncc, nc, c):
    nl = nc - ncc
    if not rev:
        return jnp.where(c < ncc, nl + c, c - ncc)
    return jnp.where(c < ncc, nc - 1 - c, nl - 1 - (c - ncc))


def _lat_block(rev, ncc, nc, c):
    nl = nc - ncc
    if not rev:
        return jnp.maximum(c - ncc, 0)
    return jnp.where(c < ncc, nl - 1, nl - 1 - (c - ncc))


CONV_SIDE_TAPS = tuple(t for t in range(CONV_W) if t != CONV_W // 2)


def _shift_matrices():
    row = jnp.arange(CHUNK)[:, None]
    col = jnp.arange(CHUNK + 2 * HALO)[None, :]
    return jnp.stack([col == row + HALO + tap - CONV_W // 2 for tap in CONV_SIDE_TAPS]).astype(BF16)


def _conv_silu(xm, xp, xn, cw_ref, cb_ref, sh_ref, first, last):
    xp = jnp.where(first, jnp.zeros_like(xp), xp)
    xn = jnp.where(last, jnp.zeros_like(xn), xn)
    ext = jnp.concatenate([xp, xm, xn], axis=0)
    mid = CONV_W // 2
    acc = cb_ref[...] + cw_ref[mid:mid + 1, :] * xm.astype(F32)
    for i, tap in enumerate(CONV_SIDE_TAPS):
        acc = acc + cw_ref[tap:tap + 1, :] * _dot(sh_ref[i], ext)
    return _silu(acc)


def _tri(rev, transposed=False):
    row = lax.broadcasted_iota(jnp.int32, (CHUNK, CHUNK), 0)
    col = lax.broadcasted_iota(jnp.int32, (CHUNK, CHUNK), 1)
    if transposed:
        row, col = col, row
    keep = (col >= row) if rev else (col <= row)
    return keep


def _split3(a):
    a1 = a.astype(BF16)
    r1 = a - a1.astype(F32)
    a2 = r1.astype(BF16)
    a3 = (r1 - a2.astype(F32)).astype(BF16)
    return a1, a2, a3


def _dot_sel_l(m_b, a):
    p = _split3(a)
    return _dot(m_b, p[0]) + _dot(m_b, p[1]) + _dot(m_b, p[2])


def _dot_sel_r(a, m_b):
    p = _split3(a)
    return _dot(p[0], m_b) + _dot(p[1], m_b) + _dot(p[2], m_b)


def _row_spec(rows, width, row_block, col_block=0):
    return pl.BlockSpec((NB, rows, width), lambda bi, c: (bi, row_block(c), col_block))


def _local_chunks(rev, ncc, nc, c):
    nl = nc - ncc
    cc = jnp.clip(ncc - 1 - c if rev else c, 0, ncc - 1)
    cl = jnp.clip(nl - 1 - (c - ncc) if rev else c - ncc, 0, nl - 1)
    return c < ncc, cc, cl


def _pair_specs(rev, ncc, nc, rows, width, col=0, halo=0):
    per = CHUNK // HALO

    def index(which, count):
        def fn(c):
            ch = _local_chunks(rev, ncc, nc, c)[which]
            if halo == 0:
                return ch
            if halo < 0:
                return jnp.maximum(ch * per - 1, 0)
            return jnp.minimum((ch + 1) * per, count * per - 1)
        return fn

    return [_row_spec(rows, width, index(2, nc - ncc), col), _row_spec(rows, width, index(1, ncc), col)]


def _pick(is_ctx, lat_ref, ctx_ref, s):
    return jnp.where(is_ctx, ctx_ref[s], lat_ref[s])


def _step_ends(rev, ncc, nc, c):
    is_ctx, cc, cl = _local_chunks(rev, ncc, nc, c)
    first = jnp.where(is_ctx, cc == 0, cl == 0)
    last = jnp.where(is_ctx, cc == ncc - 1, cl == nc - ncc - 1)
    return is_ctx, first, last


def _full(shape):
    return pl.BlockSpec(shape, lambda bi, c: (0,) * len(shape))


def _ssd_gates(rev, xs, g, dtbr_ref, dtbc_ref, alr_ref, alc_ref, e_ref):
    lane0 = SSD_HEADS * int(rev)
    lane = lax.broadcasted_iota(jnp.int32, (CHUNK, LANES), 1)
    lmask = (lane >= lane0) & (lane < lane0 + SSD_HEADS)
    dt = jnp.where(lmask, _softplus(g + dtbr_ref[...]), 0.0)
    a = dt * (-jnp.exp(alr_ref[...]))
    gt = g.T
    dt_t = _softplus(gt + dtbc_ref[...])[lane0:lane0 + SSD_HEADS]
    a_t = dt_t * (-jnp.exp(alc_ref[...][lane0:lane0 + SSD_HEADS]))

    cs = _dot_sel_l(_tri(rev).astype(BF16), a)
    cs_t = _dot_sel_r(a_t, _tri(rev, transposed=True).astype(BF16))
    e = e_ref[...]
    dtx = _dot_sel_r(dt, e)
    csx = _dot_sel_r(cs, e)
    end = 0 if rev else CHUNK - 1
    totx = csx[end:end + 1, :]
    ecsx = jnp.exp(csx)
    decx = jnp.exp(totx - csx)
    etotx = jnp.exp(totx)

    xdt = xs * dtx
    return dict(cs=cs, cs_t=cs_t, ecsx=ecsx, etotx=etotx, xdt_b=xdt.astype(BF16), xd_b=(xdt * decx).astype(BF16))


def _ssd_chunks(rev, xbcs, gs_, dtbr_ref, dtbc_ref, alr_ref, alc_ref, e_ref, s_ref):
    lane0 = SSD_HEADS * int(rev)
    keep = _tri(rev)
    half = lax.broadcasted_iota(jnp.int32, (CHUNK, LANES), 1) // SSD_HEADDIM
    n_bc = SSD_GROUPS * SSD_STATE
    hpg = SSD_HEADS // SSD_GROUPS
    gw = hpg * SSD_HEADDIM

    gq = [_ssd_gates(rev, xbc[:, :SSD_WIDTH], g, dtbr_ref, dtbc_ref, alr_ref, alc_ref, e_ref)
          for xbc, g in zip(xbcs, gs_)]
    units = []
    for s, xbc in enumerate(xbcs):
        for grp in range(SSD_GROUPS):
            bm = xbc[:, SSD_WIDTH + grp * SSD_STATE:SSD_WIDTH + (grp + 1) * SSD_STATE]
            cm = xbc[:, SSD_WIDTH + n_bc + grp * SSD_STATE:SSD_WIDTH + n_bc + (grp + 1) * SSD_STATE]
            units.append(dict(s=s, grp=grp, cols=slice(grp * gw, (grp + 1) * gw), bm=bm, bm_b=bm.astype(BF16),
                              cm_b=cm.astype(BF16), s_old=s_ref[s, :, grp * gw:(grp + 1) * gw]))
    for u in units:
        u["cb"] = _dot_nt(u["cm_b"], u["bm_b"])
        u["y_off"] = _dot(u["cm_b"], u["s_old"].astype(BF16)) * gq[u["s"]]["ecsx"][:, u["cols"]]
        u["bt_b"] = u["bm"].T.astype(BF16)
    for u in units:
        q_ = gq[u["s"]]
        masks = []
        for hh in range(hpg):
            h = u["grp"] * hpg + hh
            dl = q_["cs"][:, lane0 + h:lane0 + h + 1] - q_["cs_t"][h:h + 1, :]
            masks.append((u["cb"] * jnp.exp(jnp.where(keep, dl, NEG_BIG))).astype(BF16))
        u["masks"] = masks
    for u in units:
        q_ = gq[u["s"]]
        blocks = []
        for pair in range(hpg // 2):
            blk = u["grp"] * (hpg // 2) + pair
            xj = q_["xdt_b"][:, blk * LANES:(blk + 1) * LANES]
            acc = u["y_off"][:, pair * LANES:(pair + 1) * LANES]
            for q in range(2):
                acc = acc + _dot(u["masks"][pair * 2 + q], jnp.where(half == q, xj, jnp.zeros_like(xj)))
            blocks.append(acc)
        u["y"] = blocks
    for u in units:
        q_ = gq[u["s"]]
        s_new = u["s_old"] * q_["etotx"][:, u["cols"]] + _dot(u["bt_b"], q_["xd_b"][:, u["cols"]])
        s_ref[u["s"], :, u["cols"]] = s_new
    return [jnp.concatenate([blk for u in units if u["s"] == s for blk in u["y"]], axis=1)
            for s in range(len(xbcs))]


def _ssd_rev_kernel(ncc, nc, xml_ref, xmc_ref, xpl_ref, xpc_ref, xnl_ref, xnc_ref, gl_ref, gc_ref, cw_ref, cb_ref,
                    sh_ref, dtbr_ref, dtbc_ref, alr_ref, alc_ref, e_ref, o_ref, xc_ref, s_ref):
    c = pl.program_id(1)
    is_ctx, first, last = _step_ends(True, ncc, nc, c)

    @pl.when(c == 0)
    def _():
        s_ref[...] = jnp.zeros_like(s_ref)

    xbcs = [_conv_silu(_pick(is_ctx, xml_ref, xmc_ref, s), _pick(is_ctx, xpl_ref, xpc_ref, s),
                       _pick(is_ctx, xnl_ref, xnc_ref, s), cw_ref, cb_ref, sh_ref, first, last) for s in range(NB)]
    for s in range(NB):
        xc_ref[s] = xbcs[s].astype(BF16)
    ys = _ssd_chunks(True, xbcs, [_pick(is_ctx, gl_ref, gc_ref, s) for s in range(NB)], dtbr_ref, dtbc_ref,
                     alr_ref, alc_ref, e_ref, s_ref)
    for s in range(NB):
        o_ref[s] = ys[s].astype(o_ref.dtype)


def _ssd_fwd_kernel(ncc, xc_ref, gl_ref, gc_ref, dtbr_ref, dtbc_ref, alr_ref, alc_ref, e_ref, dsk_ref, yb_ref,
                    z_ref, nw_ref, o_ref, s_ref):
    is_ctx = pl.program_id(1) < ncc

    @pl.when(pl.program_id(1) == 0)
    def _():
        s_ref[...] = jnp.zeros_like(s_ref)

    xbcs = [xc_ref[s].astype(F32) for s in range(NB)]
    ys = _ssd_chunks(False, xbcs, [_pick(is_ctx, gl_ref, gc_ref, s) for s in range(NB)], dtbr_ref, dtbc_ref,
                     alr_ref, alc_ref, e_ref, s_ref)
    ys = [(ys[s] + yb_ref[s].astype(F32) + dsk_ref[...] * xbcs[s][:, :SSD_WIDTH]) * _silu(z_ref[s].astype(F32))
          for s in range(NB)]
    scale = [lax.rsqrt(jnp.mean(y * y, axis=-1, keepdims=True) + EPS) for y in ys]
    for s in range(NB):
        o_ref[s] = (ys[s] * scale[s] * nw_ref[...]).astype(o_ref.dtype)


def _ssd_scans(xbc, gates, z, conv_params, gate_params, e_mats, dsk, nw):
    b, seq, _ = xbc[0].shape
    ncc = xbc[1].shape[1] // CHUNK
    nc = seq // CHUNK + ncc
    state = pltpu.VMEM((NB, SSD_STATE, SSD_WIDTH), F32)
    params = pltpu.CompilerParams(dimension_semantics=("arbitrary", "arbitrary"), vmem_limit_bytes=VMEM_LIMIT)
    lat_shape = jax.ShapeDtypeStruct((b, seq, SSD_WIDTH), BF16)

    ceff = functools.partial(_chunk_eff, True, ncc, nc)
    lat = functools.partial(_lat_block, True, ncc, nc)
    pair = functools.partial(_pair_specs, True, ncc, nc)
    rev_params = list(conv_params) + list(gate_params) + [e_mats[1]]
    yb, xbc_act = pl.pallas_call(
        functools.partial(_ssd_rev_kernel, ncc, nc),
        grid=(b // NB, nc),
        in_specs=pair(CHUNK, SSD_XBC) + pair(HALO, SSD_XBC, halo=-1) + pair(HALO, SSD_XBC, halo=1)
        + pair(CHUNK, LANES) + [_full(p.shape) for p in rev_params],
        out_specs=[_row_spec(CHUNK, SSD_WIDTH, lat), _row_spec(CHUNK, SSD_XBC, ceff)],
        out_shape=(lat_shape, jax.ShapeDtypeStruct((b, nc * CHUNK, SSD_XBC), BF16)),
        scratch_shapes=[state],
        compiler_params=params,
        name="ssd_rev",
    )(*xbc, *xbc, *xbc, *gates, *rev_params)

    ceff = functools.partial(_chunk_eff, False, ncc, nc)
    lat = functools.partial(_lat_block, False, ncc, nc)
    fwd_params = list(gate_params) + [e_mats[0], dsk]
    return pl.pallas_call(
        functools.partial(_ssd_fwd_kernel, ncc),
        grid=(b // NB, nc),
        in_specs=[_row_spec(CHUNK, SSD_XBC, ceff)] + _pair_specs(False, ncc, nc, CHUNK, LANES)
        + [_full(p.shape) for p in fwd_params]
        + [_row_spec(CHUNK, SSD_WIDTH, lat), _row_spec(CHUNK, SSD_WIDTH, lat), _full(nw.shape)],
        out_specs=_row_spec(CHUNK, SSD_WIDTH, lat),
        out_shape=lat_shape,
        scratch_shapes=[state],
        compiler_params=params,
        name="ssd_fwd",
    )(xbc_act, *gates, *fwd_params, yb, z, nw)


ML_I_LANE0 = 2 * SSD_HEADS
ML_F_LANE0 = ML_I_LANE0 + 2 * ML_HEADS


def _ml_gates(rev, g, gbr_ref, gbc_ref):
    i_lane0 = ML_I_LANE0 + ML_HEADS * int(rev)
    f_lane0 = ML_F_LANE0 + ML_HEADS * int(rev)
    ga = g + gbr_ref[...]
    lane = lax.broadcasted_iota(jnp.int32, (CHUNK, LANES), 1)
    logf = jnp.where((lane >= f_lane0) & (lane < f_lane0 + ML_HEADS), -_softplus(-ga), 0.0)
    cs = _dot_sel_l(_tri(rev).astype(BF16), logf)
    gt = g.T + gbc_ref[...]
    i_t = gt[ML_I_LANE0:ML_F_LANE0]
    logf_t = -_softplus(-gt[ML_F_LANE0:ML_F_LANE0 + 2 * ML_HEADS])
    cs_t = _dot_sel_r(logf_t, _tri(rev, transposed=True).astype(BF16))
    end = 0 if rev else CHUNK - 1

    u_t = i_t - cs_t
    u_c = ga - pltpu.roll(cs, LANES - (ML_F_LANE0 - ML_I_LANE0), axis=1)
    row = lax.broadcasted_iota(jnp.int32, (CHUNK, LANES), 0)
    pm = u_c
    step = 1
    while step < CHUNK:
        if rev:
            pm = jnp.maximum(pm, jnp.where(row < CHUNK - step, pltpu.roll(pm, CHUNK - step, axis=0), NEG_BIG))
        else:
            pm = jnp.maximum(pm, jnp.where(row >= step, pltpu.roll(pm, step, axis=0), NEG_BIG))
        step *= 2

    heads = []
    for h in range(ML_HEADS):
        r = ML_HEADS * int(rev) + h
        li = i_lane0 + h
        heads.append(dict(
            csc=cs[:, f_lane0 + h:f_lane0 + h + 1], u_col=u_c[:, li:li + 1], u_row=u_t[r:r + 1, :],
            tot=cs_t[r:r + 1, end:end + 1], u_max=pm[end:end + 1, li:li + 1], pm_col=pm[:, li:li + 1]))
    return heads


def _ml_chunks(rev, samples, gbr_ref, gbc_ref, c_ref, n_ref, mx_ref, consume):
    keep = _tri(rev)
    ones_b = jnp.ones((CHUNK, ML_HEADDIM), BF16)
    units = []
    for s, (k, q_b, k_b, v_b, g) in enumerate(samples):
        for h, gq in enumerate(_ml_gates(rev, g, gbr_ref, gbc_ref)):
            sl = slice(h * ML_HEADDIM, (h + 1) * ML_HEADDIM)
            units.append(dict(gq, s=s, h=h, kh=k[:, sl], vh=v_b[:, sl], qh_b=q_b[:, sl], kh_b=k_b[:, sl],
                              m_prev=mx_ref[s, h:h + 1, 0:1], c_prev=c_ref[s * ML_HEADS + h],
                              n_prev=n_ref[s, h:h + 1, :]))

    for u in units:
        u["qk"] = _dot_nt(u["qh_b"], u["kh_b"])
        u["vt_b"] = u["vh"].astype(F32).T.astype(BF16)
    for u in units:
        u["mm"] = jnp.maximum(u["m_prev"], u["pm_col"])
        u["scores_b"] = (u["qk"] * jnp.exp(jnp.where(keep, u["u_row"] - u["mm"], NEG_BIG))).astype(BF16)
    for u in units:
        c_aug = jnp.concatenate([u["c_prev"], jnp.broadcast_to(u["n_prev"], (CHUNK, ML_HEADDIM))], axis=0)
        u["intra"] = _dot(u["scores_b"], jnp.concatenate([u["vh"], ones_b], axis=1))
        u["inter"] = _dot_nt(u["qh_b"], c_aug.astype(BF16))
    outs = [[None] * ML_HEADS for _ in samples]
    for u in units:
        both = u["intra"] + jnp.exp(u["m_prev"] - u["mm"]) * u["inter"]
        den = both[:, ML_HEADDIM:ML_HEADDIM + 1]
        outs[u["s"]][u["h"]] = both[:, :ML_HEADDIM] / jnp.maximum(jnp.abs(den), jnp.exp(-(u["csc"] + u["mm"])))
    consume(outs)
    for u in units:
        kw = u["kh"] * jnp.exp(u["u_col"] - u["u_max"])
        u["c_loc"] = _dot(u["vt_b"], kw.astype(BF16))
        u["n_loc"] = jnp.sum(kw, axis=0, keepdims=True)
    for u in units:
        s, h, tot, m_prev = u["s"], u["h"], u["tot"], u["m_prev"]
        m_loc = tot + u["u_max"]
        m_new = jnp.maximum(tot + m_prev, m_loc)
        s_prev = jnp.exp(tot + m_prev - m_new)
        s_loc = jnp.exp(m_loc - m_new)
        c_ref[s * ML_HEADS + h] = s_prev * u["c_prev"] + s_loc * u["c_loc"]
        n_ref[s, h:h + 1, :] = s_prev * u["n_prev"] + s_loc * u["n_loc"]
        mx_ref[s, h:h + 1, :] = jnp.broadcast_to(m_new, (1, LANES))


def _ml_init_state(c_ref, n_ref, mx_ref):
    c_ref[...] = jnp.zeros_like(c_ref)
    n_ref[...] = jnp.zeros_like(n_ref)
    mx_ref[...] = jnp.full(mx_ref.shape, NEG_BIG, F32)


def _ml_rev_kernel(ncc, nc, xml_ref, xmc_ref, xpl_ref, xpc_ref, xnl_ref, xnc_ref, vl_ref, vc_ref, gl_ref, gc_ref,
                   cw_ref, cb_ref, sh_ref, wq_ref, wk_ref, gbr_ref, gbc_ref, o_ref, xc_ref, q_ref, k_ref, c_ref, n_ref,
                   mx_ref):
    c = pl.program_id(1)
    is_ctx, first, last = _step_ends(True, ncc, nc, c)

    @pl.when(c == 0)
    def _():
        _ml_init_state(c_ref, n_ref, mx_ref)

    samples = []
    for s in range(NB):
        xconv = _conv_silu(_pick(is_ctx, xml_ref, xmc_ref, s), _pick(is_ctx, xpl_ref, xpc_ref, s),
                           _pick(is_ctx, xnl_ref, xnc_ref, s), cw_ref, cb_ref, sh_ref, first, last)
        xc_b = xconv.astype(BF16)
        q = _dot(xc_b, wq_ref[...])
        k = _dot(xc_b, wk_ref[...])
        q_b, k_b = q.astype(BF16), k.astype(BF16)
        xc_ref[s] = xc_b
        q_ref[s] = q_b
        k_ref[s] = k_b
        samples.append((k, q_b, k_b, _pick(is_ctx, vl_ref, vc_ref, s), _pick(is_ctx, gl_ref, gc_ref, s)))
    def store(outs):
        for s in range(NB):
            o_ref[s] = jnp.concatenate(outs[s], axis=1).astype(o_ref.dtype)

    _ml_chunks(True, samples, gbr_ref, gbc_ref, c_ref, n_ref, mx_ref, store)


def _ml_fwd_kernel(ncc, xc_ref, q_ref, k_ref, vl_ref, vc_ref, og_ref, gl_ref, gc_ref, gbr_ref, gbc_ref, hb_ref,
                   nw_ref, sk_ref, o_ref, c_ref, n_ref, mx_ref):
    is_ctx = pl.program_id(1) < ncc

    @pl.when(pl.program_id(1) == 0)
    def _():
        _ml_init_state(c_ref, n_ref, mx_ref)

    samples = []
    for s in range(NB):
        q_b, k_b = q_ref[s], k_ref[s]
        samples.append((k_b.astype(F32), q_b, k_b, _pick(is_ctx, vl_ref, vc_ref, s),
                        _pick(is_ctx, gl_ref, gc_ref, s)))
    def finish(outs):
        gated = []
        for s in range(NB):
            for h in range(ML_HEADS):
                sl = slice(h * ML_HEADDIM, (h + 1) * ML_HEADDIM)
                gated.append(jax.nn.sigmoid(og_ref[s, :, sl].astype(F32))
                             * (outs[s][h] + hb_ref[s, :, sl].astype(F32)))
        scale = [lax.rsqrt(jnp.mean(hh * hh, axis=-1, keepdims=True) + EPS) for hh in gated]
        normed = [hh * sc for hh, sc in zip(gated, scale)]
        for s in range(NB):
            y = jnp.concatenate(normed[s * ML_HEADS:(s + 1) * ML_HEADS], axis=1) * nw_ref[...]
            o_ref[s] = (y + sk_ref[...] * xc_ref[s].astype(F32)).astype(o_ref.dtype)

    _ml_chunks(False, samples, gbr_ref, gbc_ref, c_ref, n_ref, mx_ref, finish)


def _ml_scans(ml, gates, conv_params, proj_params, gate_params, nw, sk):
    b, seq, _ = ml[0].shape
    ncc = ml[1].shape[1] // CHUNK
    nc = seq // CHUNK + ncc
    scratch = [pltpu.VMEM((NB * ML_HEADS, ML_HEADDIM, ML_HEADDIM), F32),
               pltpu.VMEM((NB, 8, ML_HEADDIM), F32),
               pltpu.VMEM((NB, 8, LANES), F32)]
    params = pltpu.CompilerParams(dimension_semantics=("arbitrary", "arbitrary"), vmem_limit_bytes=VMEM_LIMIT)
    lat_shape = jax.ShapeDtypeStruct((b, seq, ML_WIDTH), BF16)
    act_shape = jax.ShapeDtypeStruct((b, nc * CHUNK, ML_WIDTH), BF16)

    ceff = functools.partial(_chunk_eff, True, ncc, nc)
    lat = functools.partial(_lat_block, True, ncc, nc)
    pair = functools.partial(_pair_specs, True, ncc, nc)
    rev_params = list(conv_params) + list(proj_params) + list(gate_params)
    act_spec = _row_spec(CHUNK, ML_WIDTH, ceff)
    hb, xc, q, k = pl.pallas_call(
        functools.partial(_ml_rev_kernel, ncc, nc),
        grid=(b // NB, nc),
        in_specs=pair(CHUNK, ML_WIDTH) + pair(HALO, ML_WIDTH, halo=-1) + pair(HALO, ML_WIDTH, halo=1)
        + pair(CHUNK, ML_WIDTH, col=1) + pair(CHUNK, LANES) + [_full(p.shape) for p in rev_params],
        out_specs=[_row_spec(CHUNK, ML_WIDTH, lat), act_spec, act_spec, act_spec],
        out_shape=(lat_shape, act_shape, act_shape, act_shape),
        scratch_shapes=scratch,
        compiler_params=params,
        name="mlstm_rev",
    )(*ml, *ml, *ml, *ml, *gates, *rev_params)

    ceff = functools.partial(_chunk_eff, False, ncc, nc)
    lat = functools.partial(_lat_block, False, ncc, nc)
    pair = functools.partial(_pair_specs, False, ncc, nc)
    act_spec = _row_spec(CHUNK, ML_WIDTH, ceff)
    return pl.pallas_call(
        functools.partial(_ml_fwd_kernel, ncc),
        grid=(b // NB, nc),
        in_specs=[act_spec, act_spec, act_spec] + pair(CHUNK, ML_WIDTH, col=1)
        + [_row_spec(CHUNK, ML_WIDTH, lat, 2)] + pair(CHUNK, LANES)
        + [_full(p.shape) for p in gate_params]
        + [_row_spec(CHUNK, ML_WIDTH, lat), _full(nw.shape), _full(sk.shape)],
        out_specs=_row_spec(CHUNK, ML_WIDTH, lat),
        out_shape=lat_shape,
        scratch_shapes=scratch,
        compiler_params=params,
        name="mlstm_fwd",
    )(xc, q, k, *ml, ml[0], *gates, *gate_params, hb, nw, sk)


def _out_kernel(x_ref, ys_ref, ym_ref, wo_ref, g1_ref, sh_ref, sc_ref, nw_ref, rw_ref, rb_ref,
                x1_ref, ts_ref, route_ref):
    mix = _dot(ys_ref[...], wo_ref[0:SSD_WIDTH, :]) + _dot(ym_ref[...], wo_ref[SSD_WIDTH:, :])
    x1 = x_ref[...] + g1_ref[...] * mix
    y = x1 * lax.rsqrt(jnp.mean(x1 * x1, axis=-1, keepdims=True) + EPS) * nw_ref[...]
    t = y * (1.0 + sc_ref[...]) + sh_ref[...]
    x1_ref[...] = x1
    for j in range(SLAB):
        ts_ref[pl.ds(j, TM_OUT, stride=SLAB), :] = t[:, j * LANES:(j + 1) * LANES]
    lg2 = _dot(t.astype(BF16), rw_ref[...])
    lg = lg2[:, :LANES] + lg2[:, LANES:] + rb_ref[...]

    lane = lax.broadcasted_iota(jnp.int32, lg.shape, 1).astype(F32)
    gmask = lane < MOE_GROUPS
    gl = jnp.where(gmask, lg, NEG_BIG)
    gmax = jnp.max(gl, axis=1, keepdims=True)
    g_sel = jnp.min(jnp.where(gmask & (gl == gmax), lane, 1e9), axis=1, keepdims=True)
    p_group = 1.0 / jnp.sum(jnp.where(gmask, jnp.exp(gl - gmax), 0.0), axis=1, keepdims=True)
    lo = ROUTE_LANE0 + MOE_EPG * g_sel
    emask = (lane >= lo) & (lane < lo + MOE_EPG)
    l1 = jnp.max(jnp.where(emask, lg, NEG_BIG), axis=1, keepdims=True)
    i1 = jnp.min(jnp.where(emask & (lg == l1), lane, 1e9), axis=1, keepdims=True)
    emask2 = emask & (lane != i1)
    l2 = jnp.max(jnp.where(emask2, lg, NEG_BIG), axis=1, keepdims=True)
    i2 = jnp.min(jnp.where(emask2 & (lg == l2), lane, 1e9), axis=1, keepdims=True)
    r = jnp.exp(l2 - l1)
    w1 = p_group / (1.0 + r)
    w2 = p_group * r / (1.0 + r)
    route = (jnp.where(lane == 0, i1, 0.0) + jnp.where(lane == 1, i2, 0.0)
             + jnp.where(lane == 2, w1, 0.0) + jnp.where(lane == 3, w2, 0.0))
    route_ref[...] = route.T[0:8, :]


def _out_proj(x, y_ssd, y_ml, w_out_b, mod3, norm_w, rw2, rb):
    b, seq, d = x.shape
    nt = seq // TM_OUT

    def row(j):
        return pl.BlockSpec((None, 1, d), lambda bi, t: (bi * 6 + j, 0, 0))

    def tile(width):
        return pl.BlockSpec((None, TM_OUT, width), lambda bi, t: (bi, t, 0))

    return pl.pallas_call(
        _out_kernel,
        grid=(b, nt),
        in_specs=[tile(d), tile(SSD_WIDTH), tile(ML_WIDTH), _full(w_out_b.shape),
                  row(2), row(3), row(4), _full((1, d)), _full(rw2.shape), _full(rb.shape)],
        out_specs=[tile(d),
                   pl.BlockSpec((None, TM_OUT * SLAB, LANES), lambda bi, t: (bi, t, 0)),
                   pl.BlockSpec((None, 8, TM_OUT), lambda bi, t: (bi, 0, t))],
        out_shape=(jax.ShapeDtypeStruct((b, seq, d), F32),
                   jax.ShapeDtypeStruct((b, seq * SLAB, LANES), F32),
                   jax.ShapeDtypeStruct((b, 8, seq), F32)),
        compiler_params=pltpu.CompilerParams(
            dimension_semantics=("arbitrary", "arbitrary"), vmem_limit_bytes=VMEM_LIMIT),
        name="out_proj_router",
    )(x, y_ssd, y_ml, w_out_b, mod3, mod3, mod3, norm_w.reshape(1, d), rw2, rb)


def _route_tables(route, seq):
    b = route.shape[0]
    n_inst = 2 * seq
    nblk = n_inst // BM + MOE_EXPERTS
    e_flat = (route[:, 0:2, :].astype(jnp.int32) - ROUTE_LANE0).reshape(b, n_inst)
    w_flat = route[:, 2:4, :].reshape(b, n_inst)
    tok_flat = jnp.tile(jnp.arange(seq, dtype=jnp.int32), 2 * b).reshape(b, n_inst)
    _, tok_sorted, w_sorted = lax.sort((e_flat, tok_flat, w_flat), dimension=1, num_keys=1)
    experts = jnp.arange(MOE_EXPERTS, dtype=jnp.int32)
    counts = jnp.sum((e_flat[:, None, :] == experts[None, :, None]).astype(jnp.int32), axis=2)
    nblk_e = (counts + BM - 1) // BM
    blk_end = jnp.cumsum(nblk_e, axis=1)
    blk_start = blk_end - nblk_e
    cnt_start = jnp.cumsum(counts, axis=1) - counts
    nb = blk_end[:, -1:]
    j = jnp.arange(nblk, dtype=jnp.int32)[None, :]
    valid_blk = j < nb
    jj = jnp.minimum(j, nb - 1)
    e_j = jnp.sum((jj[:, :, None] >= blk_end[:, None, :]).astype(jnp.int32), axis=2)
    onehot = (e_j[:, :, None] == experts[None, None, :]).astype(jnp.int32)
    take = lambda tbl: jnp.sum(onehot * tbl[:, None, :], axis=2)
    r = jnp.arange(BM, dtype=jnp.int32)[None, None, :]
    rank = ((jj - take(blk_start)) * BM)[:, :, None] + r
    valid = valid_blk[:, :, None] & (rank < take(counts)[:, :, None])
    sidx = jnp.clip(take(cnt_start)[:, :, None] + rank, 0, n_inst - 1).reshape(b, nblk * BM)
    tok = jnp.take_along_axis(tok_sorted, sidx, axis=1).reshape(b, nblk, BM)
    wslot = jnp.take_along_axis(w_sorted, sidx, axis=1).reshape(b, nblk, BM)
    tok = jnp.where(valid, tok, seq + r) * SLAB
    wslot = jnp.where(valid, wslot, 0.0)
    return (nblk_e.reshape(-1), blk_start.reshape(-1), tok.reshape(b, 1, nblk * BM), wslot)


SEM_T, SEM_X, SEM_O = 0, 1, 3


def _moe_kernel(seq, nbe_ref, bs_ref, tok_ref, ws_ref, wg_ref, wu_ref, wd_ref, g2_ref, fw_ref,
                t_hbm, x1_hbm, o_hbm, t_scr, y_scr, wgb, wub, wdb, xt, ot, xin, stage, sems):
    b = pl.program_id(0)
    e = pl.program_id(1)
    nb = pl.num_programs(0)
    rows = seq * SLAB
    n_fin = seq // TM_FIN

    def t_copy(sample):
        return pltpu.make_async_copy(t_hbm.at[sample], t_scr.at[pl.ds(0, rows)], sems.at[SEM_T])

    def x1_copy(s, slot):
        return pltpu.make_async_copy(x1_hbm.at[b, pl.ds(s * TM_FIN, TM_FIN)], xin.at[slot], sems.at[SEM_X + slot])

    def out_copy(s, slot):
        return pltpu.make_async_copy(stage.at[slot], o_hbm.at[b, pl.ds(s * TM_FIN, TM_FIN)], sems.at[SEM_O + slot])

    @pl.when((e == 0) & (b == 0))
    def _():
        t_copy(b).start()
        y_scr[...] = jnp.zeros_like(y_scr)
        t_scr[pl.ds(rows, BM * SLAB), :] = jnp.zeros((BM * SLAB, LANES), F32)

    @pl.when(e == 0)
    def _():
        t_copy(b).wait()

    n_blocks = nbe_ref[b * MOE_EXPERTS + e]
    blk0 = bs_ref[b * MOE_EXPERTS + e]

    @pl.when(n_blocks > 0)
    def _():
        wgb[...] = wg_ref[...].astype(BF16)
        wub[...] = wu_ref[...].astype(BF16)
        wdb[...] = wd_ref[...].astype(BF16)
        g2 = g2_ref[...]
        diag = (lax.broadcasted_iota(jnp.int32, (BM, BM), 0) == lax.broadcasted_iota(jnp.int32, (BM, BM), 1))

        def block(i, carry):
            blk = blk0 + i
            base = blk * BM

            tok_blk = tok_ref.at[0, pl.ds(base, BM)]

            def slab_rows(r):
                return pl.ds(pl.multiple_of(tok_blk[r], SLAB), SLAB)

            w_col = jnp.sum(jnp.where(diag, ws_ref[pl.ds(blk, 1), :], 0.0), axis=1, keepdims=True)
            parts = range(BLOCK_PARTS)
            for p in parts:
                for r in range(PART):
                    xt[p, pl.ds(r, SLAB, stride=PART_PITCH), :] = t_scr[slab_rows(p * PART + r), :]
            for p in parts:
                x = jnp.concatenate([xt[p, c * PART_PITCH:c * PART_PITCH + PART, :] for c in range(SLAB)],
                                    axis=1).astype(BF16)
                hidden = _silu(_dot(x, wgb[...])) * _dot(x, wub[...]) * w_col[p * PART:(p + 1) * PART]
                out = _dot(hidden.astype(BF16), wdb[...]) * g2
                for c in range(SLAB):
                    ot[p, c * PART_PITCH:c * PART_PITCH + PART, :] = out[:, c * LANES:(c + 1) * LANES]
            for p in parts:
                for r0 in range(0, PART, SCATTER_GROUP):
                    sl = [slab_rows(p * PART + r0 + u) for u in range(SCATTER_GROUP)]
                    vals = [y_scr[sl[u], :] + ot[p, pl.ds(r0 + u, SLAB, stride=PART_PITCH), :]
                            for u in range(SCATTER_GROUP)]
                    for u in range(SCATTER_GROUP):
                        y_scr[sl[u], :] = vals[u]
            return carry

        lax.fori_loop(0, n_blocks, block, 0)

    @pl.when(e == MOE_EXPERTS - 1)
    def _():
        @pl.when(b + 1 < nb)
        def _():
            t_copy(b + 1).start()

        x1_copy(0, 0).start()

        def tile(s, carry):
            slot = s % 2
            x1_copy(s, slot).wait()

            @pl.when(s + 1 < n_fin)
            def _():
                x1_copy(s + 1, 1 - slot).start()

            @pl.when(s >= 2)
            def _():
                out_copy(s - 2, slot).wait()

            base = pl.multiple_of(s * (TM_FIN * SLAB), TM_FIN * SLAB)
            x1 = xin[slot]
            chunks = [x1[:, c * LANES:(c + 1) * LANES] + y_scr[pl.ds(base + c, TM_FIN, stride=SLAB), :]
                      for c in range(SLAB)]
            y_scr[pl.ds(base, TM_FIN * SLAB), :] = jnp.zeros((TM_FIN * SLAB, LANES), F32)
            ssq = chunks[0] * chunks[0]
            for c in range(1, SLAB):
                ssq = ssq + chunks[c] * chunks[c]
            inv = lax.rsqrt(jnp.sum(ssq, axis=1, keepdims=True) * (1.0 / D_MODEL) + EPS)
            for c in range(SLAB):
                stage[slot, :, c * LANES:(c + 1) * LANES] = chunks[c] * inv * fw_ref[:, c * LANES:(c + 1) * LANES]
            out_copy(s, slot).start()
            return carry

        lax.fori_loop(0, n_fin, tile, 0)
        out_copy(n_fin - 2, n_fin % 2).wait()
        out_copy(n_fin - 1, (n_fin - 1) % 2).wait()


def _moe(t_slab, x1, route, wg, wu, wd, mod3, final_w):
    b, seq, d = x1.shape
    nblk_e, blk_start, tok, wslot = _route_tables(route, seq)
    nblk = wslot.shape[1]

    def w_spec(shape):
        return pl.BlockSpec((None,) + shape, lambda bi, e, nbe, bs: (e, 0, 0))

    grid_spec = pltpu.PrefetchScalarGridSpec(
        num_scalar_prefetch=2,
        grid=(b, MOE_EXPERTS),
        in_specs=[
            pl.BlockSpec((None, 1, nblk * BM), lambda bi, e, nbe, bs: (bi, 0, 0), memory_space=pltpu.SMEM),
            pl.BlockSpec((None, nblk, BM), lambda bi, e, nbe, bs: (bi, 0, 0)),
            w_spec((d, D_EXPERT)), w_spec((d, D_EXPERT)), w_spec((D_EXPERT, d)),
            pl.BlockSpec((None, 1, d), lambda bi, e, nbe, bs: (bi * 6 + 5, 0, 0)),
            pl.BlockSpec((1, d), lambda bi, e, nbe, bs: (0, 0)),
            pl.BlockSpec(memory_space=pl.ANY),
            pl.BlockSpec(memory_space=pl.ANY),
        ],
        out_specs=pl.BlockSpec(memory_space=pl.ANY),
        scratch_shapes=[
            pltpu.VMEM(((seq + BM) * SLAB, LANES), F32),
            pltpu.VMEM(((seq + BM) * SLAB, LANES), F32),
            pltpu.VMEM((d, D_EXPERT), BF16),
            pltpu.VMEM((d, D_EXPERT), BF16),
            pltpu.VMEM((D_EXPERT, d), BF16),
            pltpu.VMEM((BLOCK_PARTS, SLAB * PART_PITCH, LANES), F32),
            pltpu.VMEM((BLOCK_PARTS, SLAB * PART_PITCH, LANES), F32),
            pltpu.VMEM((2, TM_FIN, d), F32),
            pltpu.VMEM((2, TM_FIN, d), F32),
            pltpu.SemaphoreType.DMA((5,)),
        ],
    )
    return pl.pallas_call(
        functools.partial(_moe_kernel, seq),
        grid_spec=grid_spec,
        out_shape=jax.ShapeDtypeStruct((b, seq, d), F32),
        compiler_params=pltpu.CompilerParams(
            dimension_semantics=("arbitrary", "arbitrary"), vmem_limit_bytes=VMEM_LIMIT_MOE),
        name="experts_final_norm",
    )(nblk_e, blk_start, tok, wslot, wg, wu, wd, mod3, final_w.reshape(1, d), t_slab, x1)


def _lane_pad(v, offset=0):
    v = v.reshape(-1).astype(F32)
    return jnp.zeros((1, LANES), F32).at[0, offset:offset + v.shape[0]].set(v)


def _from_col_major(t, rows):
    b, length, ch = t.shape
    return t.reshape(b, GRID_W, rows, ch).transpose(0, 2, 1, 3).reshape(b, length, ch)


def kernel(x, c, ctx, c_ctx, w_mod, b_mod, norm1_w, w_in, ssd_conv_w, ssd_conv_b, ssd_dt_bias, ssd_a_log, ssd_d, ssd_norm_w, ml_conv_w, ml_conv_b, ml_w_qk, ml_gate_b, ml_norm_w, ml_skip, w_out, norm2_w, moe_rg_w, moe_rg_b, moe_re_w, moe_re_b, moe_w_gate, moe_w_up, moe_w_down, final_norm_w):
    b, seq, d = x.shape
    ctx_len = ctx.shape[1]
    rows = seq // GRID_W
    assert w_mod.shape[0] == 1 and d == D_MODEL and b + 1 <= 8 and b % NB == 0
    assert seq % TM_OUT == 0 and ctx_len % CHUNK == 0 and seq % TM_FIN == 0 and (2 * seq) % BM == 0
    assert seq == rows * GRID_W and GRID_W % W_TILE == 0 and rows % 8 == 0 and seq // TM_FIN >= 2

    c_all = jnp.zeros((8, d), F32).at[:b].set(c).at[b].set(c_ctx)
    mod = _modulation(c_all, w_mod[0], b_mod[0])
    mod3 = mod.reshape(8 * 6, 1, d)

    w = w_in[0]
    ssd_in = SSD_WIDTH + SSD_XBC + 2 * SSD_HEADS
    ml_main = 3 * ML_WIDTH
    n_gate = 2 * SSD_HEADS + 4 * ML_HEADS
    w_cat = jnp.concatenate([
        w[:, :SSD_WIDTH + SSD_XBC], w[:, ssd_in:ssd_in + ml_main],
        w[:, SSD_WIDTH + SSD_XBC:ssd_in], w[:, ssd_in + ml_main:],
        jnp.zeros((d, LANES - n_gate), F32)], axis=1).astype(BF16)
    z, xbc, ml, gates, gates_cm = _in_proj(x, ctx, mod3, norm1_w[0], w_cat)

    e_mats = []
    for direction in range(2):
        lane = jnp.arange(LANES)[:, None]
        head = (jnp.arange(SSD_WIDTH) // SSD_HEADDIM)[None, :]
        e_mats.append((lane == direction * SSD_HEADS + head).astype(BF16))
    cw = jnp.zeros((8, SSD_XBC), F32).at[:CONV_W].set(ssd_conv_w[0])
    dtb = _lane_pad(ssd_dt_bias[0])
    alog = _lane_pad(ssd_a_log[0])
    dsk = jnp.repeat(ssd_d[0], SSD_HEADDIM).reshape(1, SSD_WIDTH)
    shifts = _shift_matrices()
    y_ssd = _ssd_scans(xbc, gates, z, [cw, ssd_conv_b[0].reshape(1, -1), shifts],
                       [dtb, dtb.reshape(LANES, 1), alog, alog.reshape(LANES, 1)], e_mats, dsk,
                       ssd_norm_w[0].reshape(1, -1))

    w_rows = jnp.tile(ml_w_qk[0].reshape(2, ML_WIDTH, ML_QK_BLOCK), (1, 1, ML_WIDTH // ML_QK_BLOCK))
    blk_id = jnp.arange(ML_WIDTH) // ML_QK_BLOCK
    w_bd = jnp.where((blk_id[:, None] == blk_id[None, :])[None], w_rows, 0.0)
    wq = w_bd[0].astype(BF16)
    wk = (w_bd[1] * (ML_HEADDIM ** -0.5)).astype(BF16)
    mcw = jnp.zeros((8, ML_WIDTH), F32).at[:CONV_W].set(ml_conv_w[0])
    gb = _lane_pad(ml_gate_b[0], offset=2 * SSD_HEADS)
    y_ml_cm = _ml_scans(ml, gates_cm, [mcw, ml_conv_b[0].reshape(1, -1), shifts], [wq, wk], [gb, gb.reshape(LANES, 1)],
                        ml_norm_w[0].reshape(1, -1), ml_skip[0].reshape(1, -1))
    y_ml = _from_col_major(y_ml_cm, rows)

    rw = jnp.concatenate([moe_rg_w[0], moe_re_w[0],
                          jnp.zeros((d, LANES - MOE_GROUPS - MOE_EXPERTS), F32)], axis=1)
    rw_hi = rw.astype(BF16)
    rw_lo = (rw - rw_hi.astype(F32)).astype(BF16)
    rb = _lane_pad(jnp.concatenate([moe_rg_b[0], moe_re_b[0]]))
    rw2 = jnp.concatenate([rw_hi, rw_lo], axis=1)
    x1, t_slab, route = _out_proj(x, y_ssd, y_ml, w_out[0].astype(BF16), mod3, norm2_w[0], rw2, rb)

    return _moe(t_slab, x1, route, moe_w_gate[0], moe_w_up[0], moe_w_down[0], mod3, final_norm_w)
```

```python
import functools

import jax
import jax.numpy as jnp
from jax import lax
from jax.experimental import pallas as pl
from jax.experimental.pallas import tpu as pltpu

F32 = jnp.float32
BF16 = jnp.bfloat16
HIGHEST = lax.Precision.HIGHEST

D_MODEL = 1024
GRID_W = 64
EPS = 1e-6
CONV_W = 5
NEG_BIG = -1e30
CHUNK = 128
LANES = 128
HALO = 16
NB = 4

SSD_WIDTH = 512
SSD_HEADS = 8
SSD_HEADDIM = 64
SSD_GROUPS = 2
SSD_STATE = 128
SSD_XBC = SSD_WIDTH + 2 * SSD_GROUPS * SSD_STATE

ML_WIDTH = 512
ML_HEADS = 4
ML_HEADDIM = 128
ML_QK_BLOCK = 4

MOE_GROUPS = 4
MOE_EPG = 8
MOE_EXPERTS = 32
D_EXPERT = 256
ROUTE_LANE0 = MOE_GROUPS

TM_OUT = 512
TM_FIN = 512
SLAB = D_MODEL // LANES
BM = 128
BLOCK_PARTS = 1
PART = BM // BLOCK_PARTS
PART_PITCH = PART + 8
SCATTER_GROUP = 4
VMEM_LIMIT = 48 * 1024 * 1024
VMEM_LIMIT_MOE = 56 * 1024 * 1024


def _silu(v):
    return v * jax.nn.sigmoid(v)


def _softplus(v):
    return jnp.maximum(v, 0.0) + jnp.log1p(jnp.exp(-jnp.abs(v)))


def _dot(a, b):
    return jnp.dot(a, b, preferred_element_type=F32)


def _dot_nt(a, b):
    return lax.dot_general(a, b, (((1,), (1,)), ((), ())), preferred_element_type=F32)


def _dot_hi(a, b):
    return jnp.dot(a, b, preferred_element_type=F32, precision=HIGHEST)


def _mod_kernel(c_ref, w_ref, b_ref, o_ref):
    c = c_ref[...]
    o_ref[...] = _dot_hi(_silu(c), w_ref[...]) + b_ref[...]


def _modulation(c_all, w_mod, b_mod):
    n = w_mod.shape[1]
    bn = 1536
    return pl.pallas_call(
        _mod_kernel,
        grid=(n // bn,),
        in_specs=[
            pl.BlockSpec((8, D_MODEL), lambda j: (0, 0)),
            pl.BlockSpec((D_MODEL, bn), lambda j: (0, j)),
            pl.BlockSpec((1, bn), lambda j: (0, j)),
        ],
        out_specs=pl.BlockSpec((8, bn), lambda j: (0, j)),
        out_shape=jax.ShapeDtypeStruct((8, n), F32),
        compiler_params=pltpu.CompilerParams(vmem_limit_bytes=VMEM_LIMIT),
        name="modulation",
    )(c_all, w_mod, b_mod.reshape(1, n))


W_TILE = 16
COL_Z, COL_XBC, COL_ML, COL_G = 0, SSD_WIDTH, SSD_WIDTH + SSD_XBC, SSD_WIDTH + SSD_XBC + 3 * ML_WIDTH
PROJ_CHUNK = 512


def _norm_mod(xin, sh_ref, sc_ref, nw_ref):
    ms = jnp.mean(xin * xin, axis=-1, keepdims=True)
    y = xin * lax.rsqrt(ms + EPS) * nw_ref[...]
    return (y * (1.0 + sc_ref[...]) + sh_ref[...]).astype(BF16)


def _in_lat_kernel(rows, x_ref, sh_ref, sc_ref, nw_ref, w_ref, z_ref, xbc_ref, ml_ref, g_ref, gml_ref, scr):
    tm = rows * W_TILE
    h = _norm_mod(x_ref[...].reshape(tm, D_MODEL), sh_ref, sc_ref, nw_ref)

    def proj(col, width=PROJ_CHUNK):
        return _dot(h, w_ref[:, col:col + width])

    def to_col_major(val, slab0, dst_ref, lo, dtype):
        n_slab = val.shape[1] // LANES
        for k in range(n_slab):
            scr[slab0 + k] = val[:, k * LANES:(k + 1) * LANES]
        for j in range(W_TILE):
            for k in range(n_slab):
                dst_ref[j, :, lo + k * LANES:lo + (k + 1) * LANES] = (
                    scr[slab0 + k, pl.ds(j, rows, stride=W_TILE), :].astype(dtype))

    slabs = PROJ_CHUNK // LANES
    g = proj(COL_G, LANES)
    g_ref[...] = g.reshape(rows, W_TILE, LANES)
    to_col_major(g, 0, gml_ref, 0, F32)
    for j in range(3 * ML_WIDTH // PROJ_CHUNK):
        to_col_major(proj(COL_ML + j * PROJ_CHUNK), 1 + j * slabs, ml_ref, j * PROJ_CHUNK, BF16)
    z_ref[...] = proj(COL_Z).astype(BF16).reshape(rows, W_TILE, PROJ_CHUNK)
    for j in range(SSD_XBC // PROJ_CHUNK):
        lo = j * PROJ_CHUNK
        xbc_ref[:, :, lo:lo + PROJ_CHUNK] = proj(COL_XBC + lo).astype(BF16).reshape(rows, W_TILE, PROJ_CHUNK)


def _in_ctx_kernel(x_ref, sh_ref, sc_ref, nw_ref, w_ref, xbc_ref, ml_ref, g_ref):
    h = _norm_mod(x_ref[...], sh_ref, sc_ref, nw_ref)
    for j in range(SSD_XBC // PROJ_CHUNK):
        lo = j * PROJ_CHUNK
        xbc_ref[:, lo:lo + PROJ_CHUNK] = _dot(h, w_ref[:, COL_XBC + lo:COL_XBC + lo + PROJ_CHUNK]).astype(BF16)
    for j in range(3 * ML_WIDTH // PROJ_CHUNK):
        lo = j * PROJ_CHUNK
        ml_ref[:, lo:lo + PROJ_CHUNK] = _dot(h, w_ref[:, COL_ML + lo:COL_ML + lo + PROJ_CHUNK]).astype(BF16)
    g_ref[...] = _dot(h, w_ref[:, COL_G:COL_G + LANES])


def _in_proj(x, ctx, mod3, norm_w, w_cat):
    b, seq, d = x.shape
    ctx_len = ctx.shape[1]
    rows = seq // GRID_W
    widths = (SSD_WIDTH, SSD_XBC, 3 * ML_WIDTH, LANES, LANES)
    dtypes = (BF16, BF16, BF16, F32, F32)
    params = pltpu.CompilerParams(dimension_semantics=("arbitrary", "arbitrary"), vmem_limit_bytes=VMEM_LIMIT)
    nw = norm_w.reshape(1, d)

    def raster(width):
        return pl.BlockSpec((None, rows, W_TILE, width), lambda bi, wi: (bi, 0, wi, 0))

    def col_major(width):
        return pl.BlockSpec((None, W_TILE, rows, width), lambda bi, wi: (bi, wi, 0, 0))

    outs = pl.pallas_call(
        functools.partial(_in_lat_kernel, rows),
        grid=(b, GRID_W // W_TILE),
        in_specs=[
            raster(d),
            pl.BlockSpec((None, 1, d), lambda bi, wi: (bi * 6, 0, 0)),
            pl.BlockSpec((None, 1, d), lambda bi, wi: (bi * 6 + 1, 0, 0)),
            _full((1, d)), _full(w_cat.shape),
        ],
        out_specs=[raster(SSD_WIDTH), raster(SSD_XBC), col_major(3 * ML_WIDTH), raster(LANES), col_major(LANES)],
        out_shape=[jax.ShapeDtypeStruct((b, GRID_W, rows, w) if cm else (b, rows, GRID_W, w), t)
                   for w, t, cm in zip(widths, dtypes, (False, False, True, False, True))],
        scratch_shapes=[pltpu.VMEM((1 + 3 * ML_WIDTH // LANES, rows * W_TILE, LANES), F32)],
        compiler_params=params,
        name="in_proj",
    )(x.reshape(b, rows, GRID_W, d), mod3, mod3, nw, w_cat)
    z, xbc, ml, gates, gates_cm = [o.reshape(b, seq, w) for o, w in zip(outs, widths)]

    ctx_row = b
    ctx_widths = (SSD_XBC, 3 * ML_WIDTH, LANES)
    xbc_c, ml_c, gates_c = pl.pallas_call(
        _in_ctx_kernel,
        grid=(b, 1),
        in_specs=[
            pl.BlockSpec((None, ctx_len, d), lambda bi, t: (bi, 0, 0)),
            pl.BlockSpec((None, 1, d), lambda bi, t: (ctx_row * 6, 0, 0)),
            pl.BlockSpec((None, 1, d), lambda bi, t: (ctx_row * 6 + 1, 0, 0)),
            _full((1, d)), _full(w_cat.shape),
        ],
        out_specs=[pl.BlockSpec((None, ctx_len, w), lambda bi, t: (bi, 0, 0)) for w in ctx_widths],
        out_shape=[jax.ShapeDtypeStruct((b, ctx_len, w), t) for w, t in zip(ctx_widths, (BF16, BF16, F32))],
        compiler_params=params,
        name="in_proj_ctx",
    )(ctx, mod3, mod3, nw, w_cat)
    return z, (xbc, xbc_c), (ml, ml_c), (gates, gates_c), (gates_cm, gates_c)


def _chunk_eff(rev, ncc, nc, c):
    nl = nc - ncc
    if not rev:
        return jnp.where(c < ncc, nl + c, c - ncc)
    return jnp.where(c < ncc, nc - 1 - c, nl - 1 - (c - ncc))


def _lat_block(rev, ncc, nc, c):
    nl = nc - ncc
    if not rev:
        return jnp.maximum(c - ncc, 0)
    return jnp.where(c < ncc, nl - 1, nl - 1 - (c - ncc))


CONV_SIDE_TAPS = tuple(t for t in range(CONV_W) if t != CONV_W // 2)


def _shift_matrices():
    row = jnp.arange(CHUNK)[:, None]
    col = jnp.arange(CHUNK + 2 * HALO)[None, :]
    return jnp.stack([col == row + HALO + tap - CONV_W // 2 for tap in CONV_SIDE_TAPS]).astype(BF16)


def _conv_silu(xm, xp, xn, cw_ref, cb_ref, sh_ref, first, last):
    xp = jnp.where(first, jnp.zeros_like(xp), xp)
    xn = jnp.where(last, jnp.zeros_like(xn), xn)
    ext = jnp.concatenate([xp, xm, xn], axis=0)
    mid = CONV_W // 2
    acc = cb_ref[...] + cw_ref[mid:mid + 1, :] * xm.astype(F32)
    for i, tap in enumerate(CONV_SIDE_TAPS):
        acc = acc + cw_ref[tap:tap + 1, :] * _dot(sh_ref[i], ext)
    return _silu(acc)


def _tri(rev, transposed=False):
    row = lax.broadcasted_iota(jnp.int32, (CHUNK, CHUNK), 0)
    col = lax.broadcasted_iota(jnp.int32, (CHUNK, CHUNK), 1)
    if transposed:
        row, col = col, row
    keep = (col >= row) if rev else (col <= row)
    return keep


def _split3(a):
    a1 = a.astype(BF16)
    r1 = a - a1.astype(F32)
    a2 = r1.astype(BF16)
    a3 = (r1 - a2.astype(F32)).astype(BF16)
    return a1, a2, a3


def _dot_sel_l(m_b, a):
    p = _split3(a)
    return _dot(m_b, p[0]) + _dot(m_b, p[1]) + _dot(m_b, p[2])


def _dot_sel_r(a, m_b):
    p = _split3(a)
    return _dot(p[0], m_b) + _dot(p[1], m_b) + _dot(p[2], m_b)


def _row_spec(rows, width, row_block, col_block=0):
    return pl.BlockSpec((NB, rows, width), lambda bi, c: (bi, row_block(c), col_block))


def _local_chunks(rev, ncc, nc, c):
    nl = nc - ncc
    cc = jnp.clip(ncc - 1 - c if rev else c, 0, ncc - 1)
    cl = jnp.clip(nl - 1 - (c - ncc) if rev else c - ncc, 0, nl - 1)
    return c < ncc, cc, cl


def _pair_specs(rev, ncc, nc, rows, width, col=0, halo=0):
    per = CHUNK // HALO

    def index(which, count):
        def fn(c):
            ch = _local_chunks(rev, ncc, nc, c)[which]
            if halo == 0:
                return ch
            if halo < 0:
                return jnp.maximum(ch * per - 1, 0)
            return jnp.minimum((ch + 1) * per, count * per - 1)
        return fn

    return [_row_spec(rows, width, index(2, nc - ncc), col), _row_spec(rows, width, index(1, ncc), col)]


def _pick(is_ctx, lat_ref, ctx_ref, s):
    return jnp.where(is_ctx, ctx_ref[s], lat_ref[s])


def _step_ends(rev, ncc, nc, c):
    is_ctx, cc, cl = _local_chunks(rev, ncc, nc, c)
    first = jnp.where(is_ctx, cc == 0, cl == 0)
    last = jnp.where(is_ctx, cc == ncc - 1, cl == nc - ncc - 1)
    return is_ctx, first, last


def _full(shape):
    return pl.BlockSpec(shape, lambda bi, c: (0,) * len(shape))


def _ssd_gates(rev, xs, g, dtbr_ref, dtbc_ref, alr_ref, alc_ref, e_ref):
    lane0 = SSD_HEADS * int(rev)
    lane = lax.broadcasted_iota(jnp.int32, (CHUNK, LANES), 1)
    lmask = (lane >= lane0) & (lane < lane0 + SSD_HEADS)
    dt = jnp.where(lmask, _softplus(g + dtbr_ref[...]), 0.0)
    a = dt * (-jnp.exp(alr_ref[...]))
    gt = g.T
    dt_t = _softplus(gt + dtbc_ref[...])[lane0:lane0 + SSD_HEADS]
    a_t = dt_t * (-jnp.exp(alc_ref[...][lane0:lane0 + SSD_HEADS]))

    cs = _dot_sel_l(_tri(rev).astype(BF16), a)
    cs_t = _dot_sel_r(a_t, _tri(rev, transposed=True).astype(BF16))
    e = e_ref[...]
    dtx = _dot_sel_r(dt, e)
    csx = _dot_sel_r(cs, e)
    end = 0 if rev else CHUNK - 1
    totx = csx[end:end + 1, :]
    ecsx = jnp.exp(csx)
    decx = jnp.exp(totx - csx)
    etotx = jnp.exp(totx)

    xdt = xs * dtx
    return dict(cs=cs, cs_t=cs_t, ecsx=ecsx, etotx=etotx, xdt_b=xdt.astype(BF16), xd_b=(xdt * decx).astype(BF16))


def _ssd_chunks(rev, xbcs, gs_, dtbr_ref, dtbc_ref, alr_ref, alc_ref, e_ref, s_ref, consume):
    lane0 = SSD_HEADS * int(rev)
    keep = _tri(rev)
    half = lax.broadcasted_iota(jnp.int32, (CHUNK, LANES), 1) // SSD_HEADDIM
    n_bc = SSD_GROUPS * SSD_STATE
    hpg = SSD_HEADS // SSD_GROUPS
    gw = hpg * SSD_HEADDIM

    gq = [_ssd_gates(rev, xbc[:, :SSD_WIDTH], g, dtbr_ref, dtbc_ref, alr_ref, alc_ref, e_ref)
          for xbc, g in zip(xbcs, gs_)]
    units = []
    for s, xbc in enumerate(xbcs):
        for grp in range(SSD_GROUPS):
            bm = xbc[:, SSD_WIDTH + grp * SSD_STATE:SSD_WIDTH + (grp + 1) * SSD_STATE]
            cm = xbc[:, SSD_WIDTH + n_bc + grp * SSD_STATE:SSD_WIDTH + n_bc + (grp + 1) * SSD_STATE]
            units.append(dict(s=s, grp=grp, cols=slice(grp * gw, (grp + 1) * gw), bm=bm, bm_b=bm.astype(BF16),
                              cm_b=cm.astype(BF16), s_old=s_ref[s, :, grp * gw:(grp + 1) * gw]))
    for u in units:
        u["cb"] = _dot_nt(u["cm_b"], u["bm_b"])
        u["y_off"] = _dot(u["cm_b"], u["s_old"].astype(BF16)) * gq[u["s"]]["ecsx"][:, u["cols"]]
        u["bt_b"] = u["bm"].T.astype(BF16)
    for u in units:
        q_ = gq[u["s"]]
        masks = []
        for hh in range(hpg):
            h = u["grp"] * hpg + hh
            dl = q_["cs"][:, lane0 + h:lane0 + h + 1] - q_["cs_t"][h:h + 1, :]
            masks.append((u["cb"] * jnp.exp(jnp.where(keep, dl, NEG_BIG))).astype(BF16))
        u["masks"] = masks
    for u in units:
        q_ = gq[u["s"]]
        blocks = []
        for pair in range(hpg // 2):
            blk = u["grp"] * (hpg // 2) + pair
            xj = q_["xdt_b"][:, blk * LANES:(blk + 1) * LANES]
            acc = u["y_off"][:, pair * LANES:(pair + 1) * LANES]
            for q in range(2):
                acc = acc + _dot(u["masks"][pair * 2 + q], jnp.where(half == q, xj, jnp.zeros_like(xj)))
            blocks.append(acc)
        u["y"] = blocks
    consume([jnp.concatenate([blk for u in units if u["s"] == s for blk in u["y"]], axis=1)
             for s in range(len(xbcs))])
    for u in units:
        q_ = gq[u["s"]]
        s_new = u["s_old"] * q_["etotx"][:, u["cols"]] + _dot(u["bt_b"], q_["xd_b"][:, u["cols"]])
        s_ref[u["s"], :, u["cols"]] = s_new


def _ssd_rev_kernel(ncc, nc, xml_ref, xmc_ref, xpl_ref, xpc_ref, xnl_ref, xnc_ref, gl_ref, gc_ref, cw_ref, cb_ref,
                    sh_ref, dtbr_ref, dtbc_ref, alr_ref, alc_ref, e_ref, o_ref, xc_ref, s_ref):
    c = pl.program_id(1)
    is_ctx, first, last = _step_ends(True, ncc, nc, c)

    @pl.when(c == 0)
    def _():
        s_ref[...] = jnp.zeros_like(s_ref)

    xbcs = [_conv_silu(_pick(is_ctx, xml_ref, xmc_ref, s), _pick(is_ctx, xpl_ref, xpc_ref, s),
                       _pick(is_ctx, xnl_ref, xnc_ref, s), cw_ref, cb_ref, sh_ref, first, last) for s in range(NB)]
    for s in range(NB):
        xc_ref[s] = xbcs[s].astype(BF16)
    def store(ys):
        for s in range(NB):
            o_ref[s] = ys[s].astype(o_ref.dtype)

    _ssd_chunks(True, xbcs, [_pick(is_ctx, gl_ref, gc_ref, s) for s in range(NB)], dtbr_ref, dtbc_ref,
                alr_ref, alc_ref, e_ref, s_ref, store)


def _ssd_fwd_kernel(ncc, xc_ref, gl_ref, gc_ref, dtbr_ref, dtbc_ref, alr_ref, alc_ref, e_ref, dsk_ref, yb_ref,
                    z_ref, nw_ref, o_ref, s_ref):
    is_ctx = pl.program_id(1) < ncc

    @pl.when(pl.program_id(1) == 0)
    def _():
        s_ref[...] = jnp.zeros_like(s_ref)

    xbcs = [xc_ref[s].astype(F32) for s in range(NB)]
    def finish(ys):
        ys = [(ys[s] + yb_ref[s].astype(F32) + dsk_ref[...] * xbcs[s][:, :SSD_WIDTH]) * _silu(z_ref[s].astype(F32))
              for s in range(NB)]
        scale = [lax.rsqrt(jnp.mean(y * y, axis=-1, keepdims=True) + EPS) for y in ys]
        for s in range(NB):
            o_ref[s] = (ys[s] * scale[s] * nw_ref[...]).astype(o_ref.dtype)

    _ssd_chunks(False, xbcs, [_pick(is_ctx, gl_ref, gc_ref, s) for s in range(NB)], dtbr_ref, dtbc_ref,
                alr_ref, alc_ref, e_ref, s_ref, finish)


def _ssd_scans(xbc, gates, z, conv_params, gate_params, e_mats, dsk, nw):
    b, seq, _ = xbc[0].shape
    ncc = xbc[1].shape[1] // CHUNK
    nc = seq // CHUNK + ncc
    state = pltpu.VMEM((NB, SSD_STATE, SSD_WIDTH), F32)
    params = pltpu.CompilerParams(dimension_semantics=("arbitrary", "arbitrary"), vmem_limit_bytes=VMEM_LIMIT)
    lat_shape = jax.ShapeDtypeStruct((b, seq, SSD_WIDTH), BF16)

    ceff = functools.partial(_chunk_eff, True, ncc, nc)
    lat = functools.partial(_lat_block, True, ncc, nc)
    pair = functools.partial(_pair_specs, True, ncc, nc)
    rev_params = list(conv_params) + list(gate_params) + [e_mats[1]]
    yb, xbc_act = pl.pallas_call(
        functools.partial(_ssd_rev_kernel, ncc, nc),
        grid=(b // NB, nc),
        in_specs=pair(CHUNK, SSD_XBC) + pair(HALO, SSD_XBC, halo=-1) + pair(HALO, SSD_XBC, halo=1)
        + pair(CHUNK, LANES) + [_full(p.shape) for p in rev_params],
        out_specs=[_row_spec(CHUNK, SSD_WIDTH, lat), _row_spec(CHUNK, SSD_XBC, ceff)],
        out_shape=(lat_shape, jax.ShapeDtypeStruct((b, nc * CHUNK, SSD_XBC), BF16)),
        scratch_shapes=[state],
        compiler_params=params,
        name="ssd_rev",
    )(*xbc, *xbc, *xbc, *gates, *rev_params)

    ceff = functools.partial(_chunk_eff, False, ncc, nc)
    lat = functools.partial(_lat_block, False, ncc, nc)
    fwd_params = list(gate_params) + [e_mats[0], dsk]
    return pl.pallas_call(
        functools.partial(_ssd_fwd_kernel, ncc),
        grid=(b // NB, nc),
        in_specs=[_row_spec(CHUNK, SSD_XBC, ceff)] + _pair_specs(False, ncc, nc, CHUNK, LANES)
        + [_full(p.shape) for p in fwd_params]
        + [_row_spec(CHUNK, SSD_WIDTH, lat), _row_spec(CHUNK, SSD_WIDTH, lat), _full(nw.shape)],
        out_specs=_row_spec(CHUNK, SSD_WIDTH, lat),
        out_shape=lat_shape,
        scratch_shapes=[state],
        compiler_params=params,
        name="ssd_fwd",
    )(xbc_act, *gates, *fwd_params, yb, z, nw)


ML_I_LANE0 = 2 * SSD_HEADS
ML_F_LANE0 = ML_I_LANE0 + 2 * ML_HEADS


def _ml_gates(rev, g, gbr_ref, gbc_ref):
    i_lane0 = ML_I_LANE0 + ML_HEADS * int(rev)
    f_lane0 = ML_F_LANE0 + ML_HEADS * int(rev)
    ga = g + gbr_ref[...]
    lane = lax.broadcasted_iota(jnp.int32, (CHUNK, LANES), 1)
    logf = jnp.where((lane >= f_lane0) & (lane < f_lane0 + ML_HEADS), -_softplus(-ga), 0.0)
    cs = _dot_sel_l(_tri(rev).astype(BF16), logf)
    gt = g.T + gbc_ref[...]
    i_t = gt[ML_I_LANE0:ML_F_LANE0]
    logf_t = -_softplus(-gt[ML_F_LANE0:ML_F_LANE0 + 2 * ML_HEADS])
    cs_t = _dot_sel_r(logf_t, _tri(rev, transposed=True).astype(BF16))
    end = 0 if rev else CHUNK - 1

    u_t = i_t - cs_t
    u_c = ga - pltpu.roll(cs, LANES - (ML_F_LANE0 - ML_I_LANE0), axis=1)
    row = lax.broadcasted_iota(jnp.int32, (CHUNK, LANES), 0)
    pm = u_c
    step = 1
    while step < CHUNK:
        if rev:
            pm = jnp.maximum(pm, jnp.where(row < CHUNK - step, pltpu.roll(pm, CHUNK - step, axis=0), NEG_BIG))
        else:
            pm = jnp.maximum(pm, jnp.where(row >= step, pltpu.roll(pm, step, axis=0), NEG_BIG))
        step *= 2

    heads = []
    for h in range(ML_HEADS):
        r = ML_HEADS * int(rev) + h
        li = i_lane0 + h
        heads.append(dict(
            csc=cs[:, f_lane0 + h:f_lane0 + h + 1], u_col=u_c[:, li:li + 1], u_row=u_t[r:r + 1, :],
            tot=cs_t[r:r + 1, end:end + 1], u_max=pm[end:end + 1, li:li + 1], pm_col=pm[:, li:li + 1]))
    return heads


def _ml_chunks(rev, samples, gbr_ref, gbc_ref, c_ref, n_ref, mx_ref, consume):
    keep = _tri(rev)
    ones_b = jnp.ones((CHUNK, ML_HEADDIM), BF16)
    units = []
    for s, (k, q_b, k_b, v_b, g) in enumerate(samples):
        for h, gq in enumerate(_ml_gates(rev, g, gbr_ref, gbc_ref)):
            sl = slice(h * ML_HEADDIM, (h + 1) * ML_HEADDIM)
            units.append(dict(gq, s=s, h=h, kh=k[:, sl], vh=v_b[:, sl], qh_b=q_b[:, sl], kh_b=k_b[:, sl],
                              m_prev=mx_ref[s, h:h + 1, 0:1], c_prev=c_ref[s * ML_HEADS + h],
                              n_prev=n_ref[s, h:h + 1, :]))

    for u in units:
        u["qk"] = _dot_nt(u["qh_b"], u["kh_b"])
        u["vt_b"] = u["vh"].astype(F32).T.astype(BF16)
    for u in units:
        u["mm"] = jnp.maximum(u["m_prev"], u["pm_col"])
        u["scores_b"] = (u["qk"] * jnp.exp(jnp.where(keep, u["u_row"] - u["mm"], NEG_BIG))).astype(BF16)
    for u in units:
        c_aug = jnp.concatenate([u["c_prev"], jnp.broadcast_to(u["n_prev"], (CHUNK, ML_HEADDIM))], axis=0)
        u["intra"] = _dot(u["scores_b"], jnp.concatenate([u["vh"], ones_b], axis=1))
        u["inter"] = _dot_nt(u["qh_b"], c_aug.astype(BF16))
    outs = [[None] * ML_HEADS for _ in samples]
    for u in units:
        both = u["intra"] + jnp.exp(u["m_prev"] - u["mm"]) * u["inter"]
        den = both[:, ML_HEADDIM:ML_HEADDIM + 1]
        outs[u["s"]][u["h"]] = both[:, :ML_HEADDIM] / jnp.maximum(jnp.abs(den), jnp.exp(-(u["csc"] + u["mm"])))
    consume(outs)
    for u in units:
        kw = u["kh"] * jnp.exp(u["u_col"] - u["u_max"])
        u["c_loc"] = _dot(u["vt_b"], kw.astype(BF16))
        u["n_loc"] = jnp.sum(kw, axis=0, keepdims=True)
    for u in units:
        s, h, tot, m_prev = u["s"], u["h"], u["tot"], u["m_prev"]
        m_loc = tot + u["u_max"]
        m_new = jnp.maximum(tot + m_prev, m_loc)
        s_prev = jnp.exp(tot + m_prev - m_new)
        s_loc = jnp.exp(m_loc - m_new)
        c_ref[s * ML_HEADS + h] = s_prev * u["c_prev"] + s_loc * u["c_loc"]
        n_ref[s, h:h + 1, :] = s_prev * u["n_prev"] + s_loc * u["n_loc"]
        mx_ref[s, h:h + 1, :] = jnp.broadcast_to(m_new, (1, LANES))


def _ml_init_state(c_ref, n_ref, mx_ref):
    c_ref[...] = jnp.zeros_like(c_ref)
    n_ref[...] = jnp.zeros_like(n_ref)
    mx_ref[...] = jnp.full(mx_ref.shape, NEG_BIG, F32)


def _ml_rev_kernel(ncc, nc, xml_ref, xmc_ref, xpl_ref, xpc_ref, xnl_ref, xnc_ref, vl_ref, vc_ref, gl_ref, gc_ref,
                   cw_ref, cb_ref, sh_ref, wq_ref, wk_ref, gbr_ref, gbc_ref, o_ref, xc_ref, q_ref, k_ref, c_ref, n_ref,
                   mx_ref):
    c = pl.program_id(1)
    is_ctx, first, last = _step_ends(True, ncc, nc, c)

    @pl.when(c == 0)
    def _():
        _ml_init_state(c_ref, n_ref, mx_ref)

    samples = []
    for s in range(NB):
        xconv = _conv_silu(_pick(is_ctx, xml_ref, xmc_ref, s), _pick(is_ctx, xpl_ref, xpc_ref, s),
                           _pick(is_ctx, xnl_ref, xnc_ref, s), cw_ref, cb_ref, sh_ref, first, last)
        xc_b = xconv.astype(BF16)
        q = _dot(xc_b, wq_ref[...])
        k = _dot(xc_b, wk_ref[...])
        q_b, k_b = q.astype(BF16), k.astype(BF16)
        xc_ref[s] = xc_b
        q_ref[s] = q_b
        k_ref[s] = k_b
        samples.append((k, q_b, k_b, _pick(is_ctx, vl_ref, vc_ref, s), _pick(is_ctx, gl_ref, gc_ref, s)))
    def store(outs):
        for s in range(NB):
            o_ref[s] = jnp.concatenate(outs[s], axis=1).astype(o_ref.dtype)

    _ml_chunks(True, samples, gbr_ref, gbc_ref, c_ref, n_ref, mx_ref, store)


def _ml_fwd_kernel(ncc, xc_ref, q_ref, k_ref, vl_ref, vc_ref, og_ref, gl_ref, gc_ref, gbr_ref, gbc_ref, hb_ref,
                   nw_ref, sk_ref, o_ref, c_ref, n_ref, mx_ref):
    is_ctx = pl.program_id(1) < ncc

    @pl.when(pl.program_id(1) == 0)
    def _():
        _ml_init_state(c_ref, n_ref, mx_ref)

    samples = []
    for s in range(NB):
        q_b, k_b = q_ref[s], k_ref[s]
        samples.append((k_b.astype(F32), q_b, k_b, _pick(is_ctx, vl_ref, vc_ref, s),
                        _pick(is_ctx, gl_ref, gc_ref, s)))
    def finish(outs):
        gated = []
        for s in range(NB):
            for h in range(ML_HEADS):
                sl = slice(h * ML_HEADDIM, (h + 1) * ML_HEADDIM)
                gated.append(jax.nn.sigmoid(og_ref[s, :, sl].astype(F32))
                             * (outs[s][h] + hb_ref[s, :, sl].astype(F32)))
        scale = [lax.rsqrt(jnp.mean(hh * hh, axis=-1, keepdims=True) + EPS) for hh in gated]
        normed = [hh * sc for hh, sc in zip(gated, scale)]
        for s in range(NB):
            y = jnp.concatenate(normed[s * ML_HEADS:(s + 1) * ML_HEADS], axis=1) * nw_ref[...]
            o_ref[s] = (y + sk_ref[...] * xc_ref[s].astype(F32)).astype(o_ref.dtype)

    _ml_chunks(False, samples, gbr_ref, gbc_ref, c_ref, n_ref, mx_ref, finish)


def _ml_scans(ml, gates, conv_params, proj_params, gate_params, nw, sk):
    b, seq, _ = ml[0].shape
    ncc = ml[1].shape[1] // CHUNK
    nc = seq // CHUNK + ncc
    scratch = [pltpu.VMEM((NB * ML_HEADS, ML_HEADDIM, ML_HEADDIM), F32),
               pltpu.VMEM((NB, 8, ML_HEADDIM), F32),
               pltpu.VMEM((NB, 8, LANES), F32)]
    params = pltpu.CompilerParams(dimension_semantics=("arbitrary", "arbitrary"), vmem_limit_bytes=VMEM_LIMIT)
    lat_shape = jax.ShapeDtypeStruct((b, seq, ML_WIDTH), BF16)
    act_shape = jax.ShapeDtypeStruct((b, nc * CHUNK, ML_WIDTH), BF16)

    ceff = functools.partial(_chunk_eff, True, ncc, nc)
    lat = functools.partial(_lat_block, True, ncc, nc)
    pair = functools.partial(_pair_specs, True, ncc, nc)
    rev_params = list(conv_params) + list(proj_params) + list(gate_params)
    act_spec = _row_spec(CHUNK, ML_WIDTH, ceff)
    hb, xc, q, k = pl.pallas_call(
        functools.partial(_ml_rev_kernel, ncc, nc),
        grid=(b // NB, nc),
        in_specs=pair(CHUNK, ML_WIDTH) + pair(HALO, ML_WIDTH, halo=-1) + pair(HALO, ML_WIDTH, halo=1)
        + pair(CHUNK, ML_WIDTH, col=1) + pair(CHUNK, LANES) + [_full(p.shape) for p in rev_params],
        out_specs=[_row_spec(CHUNK, ML_WIDTH, lat), act_spec, act_spec, act_spec],
        out_shape=(lat_shape, act_shape, act_shape, act_shape),
        scratch_shapes=scratch,
        compiler_params=params,
        name="mlstm_rev",
    )(*ml, *ml, *ml, *ml, *gates, *rev_params)

    ceff = functools.partial(_chunk_eff, False, ncc, nc)
    lat = functools.partial(_lat_block, False, ncc, nc)
    pair = functools.partial(_pair_specs, False, ncc, nc)
    act_spec = _row_spec(CHUNK, ML_WIDTH, ceff)
    return pl.pallas_call(
        functools.partial(_ml_fwd_kernel, ncc),
        grid=(b // NB, nc),
        in_specs=[act_spec, act_spec, act_spec] + pair(CHUNK, ML_WIDTH, col=1)
        + [_row_spec(CHUNK, ML_WIDTH, lat, 2)] + pair(CHUNK, LANES)
        + [_full(p.shape) for p in gate_params]
        + [_row_spec(CHUNK, ML_WIDTH, lat), _full(nw.shape), _full(sk.shape)],
        out_specs=_row_spec(CHUNK, ML_WIDTH, lat),
        out_shape=lat_shape,
        scratch_shapes=scratch,
        compiler_params=params,
        name="mlstm_fwd",
    )(xc, q, k, *ml, ml[0], *gates, *gate_params, hb, nw, sk)


def _out_kernel(x_ref, ys_ref, ym_ref, wo_ref, g1_ref, sh_ref, sc_ref, nw_ref, rw_ref, rb_ref,
                x1_ref, ts_ref, route_ref):
    mix = _dot(ys_ref[...], wo_ref[0:SSD_WIDTH, :]) + _dot(ym_ref[...], wo_ref[SSD_WIDTH:, :])
    x1 = x_ref[...] + g1_ref[...] * mix
    y = x1 * lax.rsqrt(jnp.mean(x1 * x1, axis=-1, keepdims=True) + EPS) * nw_ref[...]
    t = y * (1.0 + sc_ref[...]) + sh_ref[...]
    x1_ref[...] = x1
    for j in range(SLAB):
        ts_ref[pl.ds(j, TM_OUT, stride=SLAB), :] = t[:, j * LANES:(j + 1) * LANES]
    lg2 = _dot(t.astype(BF16), rw_ref[...])
    lg = lg2[:, :LANES] + lg2[:, LANES:] + rb_ref[...]

    lane = lax.broadcasted_iota(jnp.int32, lg.shape, 1).astype(F32)
    gmask = lane < MOE_GROUPS
    gl = jnp.where(gmask, lg, NEG_BIG)
    gmax = jnp.max(gl, axis=1, keepdims=True)
    g_sel = jnp.min(jnp.where(gmask & (gl == gmax), lane, 1e9), axis=1, keepdims=True)
    p_group = 1.0 / jnp.sum(jnp.where(gmask, jnp.exp(gl - gmax), 0.0), axis=1, keepdims=True)
    lo = ROUTE_LANE0 + MOE_EPG * g_sel
    emask = (lane >= lo) & (lane < lo + MOE_EPG)
    l1 = jnp.max(jnp.where(emask, lg, NEG_BIG), axis=1, keepdims=True)
    i1 = jnp.min(jnp.where(emask & (lg == l1), lane, 1e9), axis=1, keepdims=True)
    emask2 = emask & (lane != i1)
    l2 = jnp.max(jnp.where(emask2, lg, NEG_BIG), axis=1, keepdims=True)
    i2 = jnp.min(jnp.where(emask2 & (lg == l2), lane, 1e9), axis=1, keepdims=True)
    r = jnp.exp(l2 - l1)
    w1 = p_group / (1.0 + r)
    w2 = p_group * r / (1.0 + r)
    route = (jnp.where(lane == 0, i1, 0.0) + jnp.where(lane == 1, i2, 0.0)
             + jnp.where(lane == 2, w1, 0.0) + jnp.where(lane == 3, w2, 0.0))
    route_ref[...] = route.T[0:8, :]


def _out_proj(x, y_ssd, y_ml, w_out_b, mod3, norm_w, rw2, rb):
    b, seq, d = x.shape
    nt = seq // TM_OUT

    def row(j):
        return pl.BlockSpec((None, 1, d), lambda bi, t: (bi * 6 + j, 0, 0))

    def tile(width):
        return pl.BlockSpec((None, TM_OUT, width), lambda bi, t: (bi, t, 0))

    return pl.pallas_call(
        _out_kernel,
        grid=(b, nt),
        in_specs=[tile(d), tile(SSD_WIDTH), tile(ML_WIDTH), _full(w_out_b.shape),
                  row(2), row(3), row(4), _full((1, d)), _full(rw2.shape), _full(rb.shape)],
        out_specs=[tile(d),
                   pl.BlockSpec((None, TM_OUT * SLAB, LANES), lambda bi, t: (bi, t, 0)),
                   pl.BlockSpec((None, 8, TM_OUT), lambda bi, t: (bi, 0, t))],
        out_shape=(jax.ShapeDtypeStruct((b, seq, d), F32),
                   jax.ShapeDtypeStruct((b, seq * SLAB, LANES), F32),
                   jax.ShapeDtypeStruct((b, 8, seq), F32)),
        compiler_params=pltpu.CompilerParams(
            dimension_semantics=("arbitrary", "arbitrary"), vmem_limit_bytes=VMEM_LIMIT),
        name="out_proj_router",
    )(x, y_ssd, y_ml, w_out_b, mod3, mod3, mod3, norm_w.reshape(1, d), rw2, rb)


def _route_tables(route, seq):
    b = route.shape[0]
    n_inst = 2 * seq
    nblk = n_inst // BM + MOE_EXPERTS
    e_flat = (route[:, 0:2, :].astype(jnp.int32) - ROUTE_LANE0).reshape(b, n_inst)
    w_flat = route[:, 2:4, :].reshape(b, n_inst)
    tok_flat = jnp.tile(jnp.arange(seq, dtype=jnp.int32), 2 * b).reshape(b, n_inst)
    _, tok_sorted, w_sorted = lax.sort((e_flat, tok_flat, w_flat), dimension=1, num_keys=1)
    experts = jnp.arange(MOE_EXPERTS, dtype=jnp.int32)
    counts = jnp.sum((e_flat[:, None, :] == experts[None, :, None]).astype(jnp.int32), axis=2)
    nblk_e = (counts + BM - 1) // BM
    blk_end = jnp.cumsum(nblk_e, axis=1)
    blk_start = blk_end - nblk_e
    cnt_start = jnp.cumsum(counts, axis=1) - counts
    nb = blk_end[:, -1:]
    j = jnp.arange(nblk, dtype=jnp.int32)[None, :]
    valid_blk = j < nb
    jj = jnp.minimum(j, nb - 1)
    e_j = jnp.sum((jj[:, :, None] >= blk_end[:, None, :]).astype(jnp.int32), axis=2)
    onehot = (e_j[:, :, None] == experts[None, None, :]).astype(jnp.int32)
    take = lambda tbl: jnp.sum(onehot * tbl[:, None, :], axis=2)
    r = jnp.arange(BM, dtype=jnp.int32)[None, None, :]
    rank = ((jj - take(blk_start)) * BM)[:, :, None] + r
    valid = valid_blk[:, :, None] & (rank < take(counts)[:, :, None])
    sidx = jnp.clip(take(cnt_start)[:, :, None] + rank, 0, n_inst - 1).reshape(b, nblk * BM)
    tok = jnp.take_along_axis(tok_sorted, sidx, axis=1).reshape(b, nblk, BM)
    wslot = jnp.take_along_axis(w_sorted, sidx, axis=1).reshape(b, nblk, BM)
    tok = jnp.where(valid, tok, seq + r) * SLAB
    wslot = jnp.where(valid, wslot, 0.0)
    return (nblk_e.reshape(-1), blk_start.reshape(-1), tok.reshape(b, 1, nblk * BM), wslot)


SEM_T, SEM_X, SEM_O = 0, 1, 3


def _moe_kernel(seq, nbe_ref, bs_ref, tok_ref, ws_ref, wg_ref, wu_ref, wd_ref, g2_ref, fw_ref,
                t_hbm, x1_hbm, o_hbm, t_scr, y_scr, wgb, wub, wdb, xt, ot, xin, stage, sems):
    b = pl.program_id(0)
    e = pl.program_id(1)
    nb = pl.num_programs(0)
    rows = seq * SLAB
    n_fin = seq // TM_FIN

    def t_copy(sample):
        return pltpu.make_async_copy(t_hbm.at[sample], t_scr.at[pl.ds(0, rows)], sems.at[SEM_T])

    def x1_copy(s, slot):
        return pltpu.make_async_copy(x1_hbm.at[b, pl.ds(s * TM_FIN, TM_FIN)], xin.at[slot], sems.at[SEM_X + slot])

    def out_copy(s, slot):
        return pltpu.make_async_copy(stage.at[slot], o_hbm.at[b, pl.ds(s * TM_FIN, TM_FIN)], sems.at[SEM_O + slot])

    @pl.when((e == 0) & (b == 0))
    def _():
        t_copy(b).start()
        y_scr[...] = jnp.zeros_like(y_scr)
        t_scr[pl.ds(rows, BM * SLAB), :] = jnp.zeros((BM * SLAB, LANES), F32)

    @pl.when(e == 0)
    def _():
        t_copy(b).wait()

    n_blocks = nbe_ref[b * MOE_EXPERTS + e]
    blk0 = bs_ref[b * MOE_EXPERTS + e]

    @pl.when(n_blocks > 0)
    def _():
        wgb[...] = wg_ref[...].astype(BF16)
        wub[...] = wu_ref[...].astype(BF16)
        wdb[...] = wd_ref[...].astype(BF16)
        g2 = g2_ref[...]
        diag = (lax.broadcasted_iota(jnp.int32, (BM, BM), 0) == lax.broadcasted_iota(jnp.int32, (BM, BM), 1))

        def block(i, carry):
            blk = blk0 + i
            base = blk * BM

            tok_blk = tok_ref.at[0, pl.ds(base, BM)]

            def slab_rows(r):
                return pl.ds(pl.multiple_of(tok_blk[r], SLAB), SLAB)

            w_col = jnp.sum(jnp.where(diag, ws_ref[pl.ds(blk, 1), :], 0.0), axis=1, keepdims=True)
            parts = range(BLOCK_PARTS)
            for p in parts:
                for r in range(PART):
                    xt[p, pl.ds(r, SLAB, stride=PART_PITCH), :] = t_scr[slab_rows(p * PART + r), :]
            for p in parts:
                x = jnp.concatenate([xt[p, c * PART_PITCH:c * PART_PITCH + PART, :] for c in range(SLAB)],
                                    axis=1).astype(BF16)
                hidden = _silu(_dot(x, wgb[...])) * _dot(x, wub[...]) * w_col[p * PART:(p + 1) * PART]
                out = _dot(hidden.astype(BF16), wdb[...]) * g2
                for c in range(SLAB):
                    ot[p, c * PART_PITCH:c * PART_PITCH + PART, :] = out[:, c * LANES:(c + 1) * LANES]
            for p in parts:
                for r0 in range(0, PART, SCATTER_GROUP):
                    sl = [slab_rows(p * PART + r0 + u) for u in range(SCATTER_GROUP)]
                    vals = [y_scr[sl[u], :] + ot[p, pl.ds(r0 + u, SLAB, stride=PART_PITCH), :]
                            for u in range(SCATTER_GROUP)]
                    for u in range(SCATTER_GROUP):
                        y_scr[sl[u], :] = vals[u]
            return carry

        lax.fori_loop(0, n_blocks, block, 0)

    @pl.when(e == MOE_EXPERTS - 1)
    def _():
        @pl.when(b + 1 < nb)
        def _():
            t_copy(b + 1).start()

        x1_copy(0, 0).start()

        def tile(s, carry):
            slot = s % 2
            x1_copy(s, slot).wait()

            @pl.when(s + 1 < n_fin)
            def _():
                x1_copy(s + 1, 1 - slot).start()

            @pl.when(s >= 2)
            def _():
                out_copy(s - 2, slot).wait()

            base = pl.multiple_of(s * (TM_FIN * SLAB), TM_FIN * SLAB)
            x1 = xin[slot]
            chunks = [x1[:, c * LANES:(c + 1) * LANES] + y_scr[pl.ds(base + c, TM_FIN, stride=SLAB), :]
                      for c in range(SLAB)]
            y_scr[pl.ds(base, TM_FIN * SLAB), :] = jnp.zeros((TM_FIN * SLAB, LANES), F32)
            ssq = chunks[0] * chunks[0]
            for c in range(1, SLAB):
                ssq = ssq + chunks[c] * chunks[c]
            inv = lax.rsqrt(jnp.sum(ssq, axis=1, keepdims=True) * (1.0 / D_MODEL) + EPS)
            for c in range(SLAB):
                stage[slot, :, c * LANES:(c + 1) * LANES] = chunks[c] * inv * fw_ref[:, c * LANES:(c + 1) * LANES]
            out_copy(s, slot).start()
            return carry

        lax.fori_loop(0, n_fin, tile, 0)
        out_copy(n_fin - 2, n_fin % 2).wait()
        out_copy(n_fin - 1, (n_fin - 1) % 2).wait()


def _moe(t_slab, x1, route, wg, wu, wd, mod3, final_w):
    b, seq, d = x1.shape
    nblk_e, blk_start, tok, wslot = _route_tables(route, seq)
    nblk = wslot.shape[1]

    def w_spec(shape):
        return pl.BlockSpec((None,) + shape, lambda bi, e, nbe, bs: (e, 0, 0))

    grid_spec = pltpu.PrefetchScalarGridSpec(
        num_scalar_prefetch=2,
        grid=(b, MOE_EXPERTS),
        in_specs=[
            pl.BlockSpec((None, 1, nblk * BM), lambda bi, e, nbe, bs: (bi, 0, 0), memory_space=pltpu.SMEM),
            pl.BlockSpec((None, nblk, BM), lambda bi, e, nbe, bs: (bi, 0, 0)),
            w_spec((d, D_EXPERT)), w_spec((d, D_EXPERT)), w_spec((D_EXPERT, d)),
            pl.BlockSpec((None, 1, d), lambda bi, e, nbe, bs: (bi * 6 + 5, 0, 0)),
            pl.BlockSpec((1, d), lambda bi, e, nbe, bs: (0, 0)),
            pl.BlockSpec(memory_space=pl.ANY),
            pl.BlockSpec(memory_space=pl.ANY),
        ],
        out_specs=pl.BlockSpec(memory_space=pl.ANY),
        scratch_shapes=[
            pltpu.VMEM(((seq + BM) * SLAB, LANES), F32),
            pltpu.VMEM(((seq + BM) * SLAB, LANES), F32),
            pltpu.VMEM((d, D_EXPERT), BF16),
            pltpu.VMEM((d, D_EXPERT), BF16),
            pltpu.VMEM((D_EXPERT, d), BF16),
            pltpu.VMEM((BLOCK_PARTS, SLAB * PART_PITCH, LANES), F32),
            pltpu.VMEM((BLOCK_PARTS, SLAB * PART_PITCH, LANES), F32),
            pltpu.VMEM((2, TM_FIN, d), F32),
            pltpu.VMEM((2, TM_FIN, d), F32),
            pltpu.SemaphoreType.DMA((5,)),
        ],
    )
    return pl.pallas_call(
        functools.partial(_moe_kernel, seq),
        grid_spec=grid_spec,
        out_shape=jax.ShapeDtypeStruct((b, seq, d), F32),
        compiler_params=pltpu.CompilerParams(
            dimension_semantics=("arbitrary", "arbitrary"), vmem_limit_bytes=VMEM_LIMIT_MOE),
        name="experts_final_norm",
    )(nblk_e, blk_start, tok, wslot, wg, wu, wd, mod3, final_w.reshape(1, d), t_slab, x1)


def _lane_pad(v, offset=0):
    v = v.reshape(-1).astype(F32)
    return jnp.zeros((1, LANES), F32).at[0, offset:offset + v.shape[0]].set(v)


def _from_col_major(t, rows):
    b, length, ch = t.shape
    return t.reshape(b, GRID_W, rows, ch).transpose(0, 2, 1, 3).reshape(b, length, ch)


def kernel(x, c, ctx, c_ctx, w_mod, b_mod, norm1_w, w_in, ssd_conv_w, ssd_conv_b, ssd_dt_bias, ssd_a_log, ssd_d, ssd_norm_w, ml_conv_w, ml_conv_b, ml_w_qk, ml_gate_b, ml_norm_w, ml_skip, w_out, norm2_w, moe_rg_w, moe_rg_b, moe_re_w, moe_re_b, moe_w_gate, moe_w_up, moe_w_down, final_norm_w):
    b, seq, d = x.shape
    ctx_len = ctx.shape[1]
    rows = seq // GRID_W
    assert w_mod.shape[0] == 1 and d == D_MODEL and b + 1 <= 8 and b % NB == 0
    assert seq % TM_OUT == 0 and ctx_len % CHUNK == 0 and seq % TM_FIN == 0 and (2 * seq) % BM == 0
    assert seq == rows * GRID_W and GRID_W % W_TILE == 0 and rows % 8 == 0 and seq // TM_FIN >= 2

    c_all = jnp.zeros((8, d), F32).at[:b].set(c).at[b].set(c_ctx)
    mod = _modulation(c_all, w_mod[0], b_mod[0])
    mod3 = mod.reshape(8 * 6, 1, d)

    w = w_in[0]
    ssd_in = SSD_WIDTH + SSD_XBC + 2 * SSD_HEADS
    ml_main = 3 * ML_WIDTH
    n_gate = 2 * SSD_HEADS + 4 * ML_HEADS
    w_cat = jnp.concatenate([
        w[:, :SSD_WIDTH + SSD_XBC], w[:, ssd_in:ssd_in + ml_main],
        w[:, SSD_WIDTH + SSD_XBC:ssd_in], w[:, ssd_in + ml_main:],
        jnp.zeros((d, LANES - n_gate), F32)], axis=1).astype(BF16)
    z, xbc, ml, gates, gates_cm = _in_proj(x, ctx, mod3, norm1_w[0], w_cat)

    e_mats = []
    for direction in range(2):
        lane = jnp.arange(LANES)[:, None]
        head = (jnp.arange(SSD_WIDTH) // SSD_HEADDIM)[None, :]
        e_mats.append((lane == direction * SSD_HEADS + head).astype(BF16))
    cw = jnp.zeros((8, SSD_XBC), F32).at[:CONV_W].set(ssd_conv_w[0])
    dtb = _lane_pad(ssd_dt_bias[0])
    alog = _lane_pad(ssd_a_log[0])
    dsk = jnp.repeat(ssd_d[0], SSD_HEADDIM).reshape(1, SSD_WIDTH)
    shifts = _shift_matrices()
    y_ssd = _ssd_scans(xbc, gates, z, [cw, ssd_conv_b[0].reshape(1, -1), shifts],
                       [dtb, dtb.reshape(LANES, 1), alog, alog.reshape(LANES, 1)], e_mats, dsk,
                       ssd_norm_w[0].reshape(1, -1))

    w_rows = jnp.tile(ml_w_qk[0].reshape(2, ML_WIDTH, ML_QK_BLOCK), (1, 1, ML_WIDTH // ML_QK_BLOCK))
    blk_id = jnp.arange(ML_WIDTH) // ML_QK_BLOCK
    w_bd = jnp.where((blk_id[:, None] == blk_id[None, :])[None], w_rows, 0.0)
    wq = w_bd[0].astype(BF16)
    wk = (w_bd[1] * (ML_HEADDIM ** -0.5)).astype(BF16)
    mcw = jnp.zeros((8, ML_WIDTH), F32).at[:CONV_W].set(ml_conv_w[0])
    gb = _lane_pad(ml_gate_b[0], offset=2 * SSD_HEADS)
    y_ml_cm = _ml_scans(ml, gates_cm, [mcw, ml_conv_b[0].reshape(1, -1), shifts], [wq, wk], [gb, gb.reshape(LANES, 1)],
                        ml_norm_w[0].reshape(1, -1), ml_skip[0].reshape(1, -1))
    y_ml = _from_col_major(y_ml_cm, rows)

    rw = jnp.concatenate([moe_rg_w[0], moe_re_w[0],
                          jnp.zeros((d, LANES - MOE_GROUPS - MOE_EXPERTS), F32)], axis=1)
    rw_hi = rw.astype(BF16)
    rw_lo = (rw - rw_hi.astype(F32)).astype(BF16)
    rb = _lane_pad(jnp.concatenate([moe_rg_b[0], moe_re_b[0]]))
    rw2 = jnp.concatenate([rw_hi, rw_lo], axis=1)
    x1, t_slab, route = _out_proj(x, y_ssd, y_ml, w_out[0].astype(BF16), mod3, norm2_w[0], rw2, rb)

    return _moe(t_slab, x1, route, moe_w_gate[0], moe_w_up[0], moe_w_down[0], mod3, final_norm_w)
```

```python
import functools

import jax
import jax.numpy as jnp
from jax import lax
from jax.experimental import pallas as pl
from jax.experimental.pallas import tpu as pltpu

F32 = jnp.float32
BF16 = jnp.bfloat16
HIGHEST = lax.Precision.HIGHEST

D_MODEL = 1024
GRID_W = 64
EPS = 1e-6
CONV_W = 5
NEG_BIG = -1e30
CHUNK = 128
LANES = 128
HALO = 16
NB = 4

SSD_WIDTH = 512
SSD_HEADS = 8
SSD_HEADDIM = 64
SSD_GROUPS = 2
SSD_STATE = 128
SSD_XBC = SSD_WIDTH + 2 * SSD_GROUPS * SSD_STATE

ML_WIDTH = 512
ML_HEADS = 4
ML_HEADDIM = 128
ML_QK_BLOCK = 4

MOE_GROUPS = 4
MOE_EPG = 8
MOE_EXPERTS = 32
D_EXPERT = 256
ROUTE_LANE0 = MOE_GROUPS

TM_OUT = 512
TM_FIN = 512
SLAB = D_MODEL // LANES
BM = 128
BLOCK_PARTS = 1
PART = BM // BLOCK_PARTS
PART_PITCH = PART + 8
SCATTER_GROUP = 8
VMEM_LIMIT = 48 * 1024 * 1024
VMEM_LIMIT_MOE = 56 * 1024 * 1024


def _silu(v):
    return v * jax.nn.sigmoid(v)


def _softplus(v):
    return jnp.maximum(v, 0.0) + jnp.log1p(jnp.exp(-jnp.abs(v)))


def _dot(a, b):
    return jnp.dot(a, b, preferred_element_type=F32)


def _dot_nt(a, b):
    return lax.dot_general(a, b, (((1,), (1,)), ((), ())), preferred_element_type=F32)


def _dot_hi(a, b):
    return jnp.dot(a, b, preferred_element_type=F32, precision=HIGHEST)


def _mod_kernel(c_ref, w_ref, b_ref, o_ref):
    c = c_ref[...]
    o_ref[...] = _dot_hi(_silu(c), w_ref[...]) + b_ref[...]


def _modulation(c_all, w_mod, b_mod):
    n = w_mod.shape[1]
    bn = 1536
    return pl.pallas_call(
        _mod_kernel,
        grid=(n // bn,),
        in_specs=[
            pl.BlockSpec((8, D_MODEL), lambda j: (0, 0)),
            pl.BlockSpec((D_MODEL, bn), lambda j: (0, j)),
            pl.BlockSpec((1, bn), lambda j: (0, j)),
        ],
        out_specs=pl.BlockSpec((8, bn), lambda j: (0, j)),
        out_shape=jax.ShapeDtypeStruct((8, n), F32),
        compiler_params=pltpu.CompilerParams(vmem_limit_bytes=VMEM_LIMIT),
        name="modulation",
    )(c_all, w_mod, b_mod.reshape(1, n))


W_TILE = 16
COL_Z, COL_XBC, COL_ML, COL_G = 0, SSD_WIDTH, SSD_WIDTH + SSD_XBC, SSD_WIDTH + SSD_XBC + 3 * ML_WIDTH
PROJ_CHUNK = 512


def _norm_mod(xin, sh_ref, sc_ref, nw_ref):
    ms = jnp.mean(xin * xin, axis=-1, keepdims=True)
    y = xin * lax.rsqrt(ms + EPS) * nw_ref[...]
    return (y * (1.0 + sc_ref[...]) + sh_ref[...]).astype(BF16)


def _in_lat_kernel(rows, x_ref, sh_ref, sc_ref, nw_ref, w_ref, z_ref, xbc_ref, ml_ref, g_ref, gml_ref, scr):
    tm = rows * W_TILE
    h = _norm_mod(x_ref[...].reshape(tm, D_MODEL), sh_ref, sc_ref, nw_ref)

    def proj(col, width=PROJ_CHUNK):
        return _dot(h, w_ref[:, col:col + width])

    def to_col_major(val, slab0, dst_ref, lo, dtype):
        n_slab = val.shape[1] // LANES
        for k in range(n_slab):
            scr[slab0 + k] = val[:, k * LANES:(k + 1) * LANES]
        for j in range(W_TILE):
            for k in range(n_slab):
                dst_ref[j, :, lo + k * LANES:lo + (k + 1) * LANES] = (
                    scr[slab0 + k, pl.ds(j, rows, stride=W_TILE), :].astype(dtype))

    slabs = PROJ_CHUNK // LANES
    g = proj(COL_G, LANES)
    g_ref[...] = g.reshape(rows, W_TILE, LANES)
    to_col_major(g, 0, gml_ref, 0, F32)
    for j in range(3 * ML_WIDTH // PROJ_CHUNK):
        to_col_major(proj(COL_ML + j * PROJ_CHUNK), 1 + j * slabs, ml_ref, j * PROJ_CHUNK, BF16)
    z_ref[...] = proj(COL_Z).astype(BF16).reshape(rows, W_TILE, PROJ_CHUNK)
    for j in range(SSD_XBC // PROJ_CHUNK):
        lo = j * PROJ_CHUNK
        xbc_ref[:, :, lo:lo + PROJ_CHUNK] = proj(COL_XBC + lo).astype(BF16).reshape(rows, W_TILE, PROJ_CHUNK)


def _in_ctx_kernel(x_ref, sh_ref, sc_ref, nw_ref, w_ref, xbc_ref, ml_ref, g_ref):
    h = _norm_mod(x_ref[...], sh_ref, sc_ref, nw_ref)
    for j in range(SSD_XBC // PROJ_CHUNK):
        lo = j * PROJ_CHUNK
        xbc_ref[:, lo:lo + PROJ_CHUNK] = _dot(h, w_ref[:, COL_XBC + lo:COL_XBC + lo + PROJ_CHUNK]).astype(BF16)
    for j in range(3 * ML_WIDTH // PROJ_CHUNK):
        lo = j * PROJ_CHUNK
        ml_ref[:, lo:lo + PROJ_CHUNK] = _dot(h, w_ref[:, COL_ML + lo:COL_ML + lo + PROJ_CHUNK]).astype(BF16)
    g_ref[...] = _dot(h, w_ref[:, COL_G:COL_G + LANES])


def _in_proj(x, ctx, mod3, norm_w, w_cat):
    b, seq, d = x.shape
    ctx_len = ctx.shape[1]
    rows = seq // GRID_W
    widths = (SSD_WIDTH, SSD_XBC, 3 * ML_WIDTH, LANES, LANES)
    dtypes = (BF16, BF16, BF16, F32, F32)
    params = pltpu.CompilerParams(dimension_semantics=("arbitrary", "arbitrary"), vmem_limit_bytes=VMEM_LIMIT)
    nw = norm_w.reshape(1, d)

    def raster(width):
        return pl.BlockSpec((None, rows, W_TILE, width), lambda bi, wi: (bi, 0, wi, 0))

    def col_major(width):
        return pl.BlockSpec((None, W_TILE, rows, width), lambda bi, wi: (bi, wi, 0, 0))

    outs = pl.pallas_call(
        functools.partial(_in_lat_kernel, rows),
        grid=(b, GRID_W // W_TILE),
        in_specs=[
            raster(d),
            pl.BlockSpec((None, 1, d), lambda bi, wi: (bi * 6, 0, 0)),
            pl.BlockSpec((None, 1, d), lambda bi, wi: (bi * 6 + 1, 0, 0)),
            _full((1, d)), _full(w_cat.shape),
        ],
        out_specs=[raster(SSD_WIDTH), raster(SSD_XBC), col_major(3 * ML_WIDTH), raster(LANES), col_major(LANES)],
        out_shape=[jax.ShapeDtypeStruct((b, GRID_W, rows, w) if cm else (b, rows, GRID_W, w), t)
                   for w, t, cm in zip(widths, dtypes, (False, False, True, False, True))],
        scratch_shapes=[pltpu.VMEM((1 + 3 * ML_WIDTH // LANES, rows * W_TILE, LANES), F32)],
        compiler_params=params,
        name="in_proj",
    )(x.reshape(b, rows, GRID_W, d), mod3, mod3, nw, w_cat)
    z, xbc, ml, gates, gates_cm = [o.reshape(b, seq, w) for o, w in zip(outs, widths)]

    ctx_row = b
    ctx_widths = (SSD_XBC, 3 * ML_WIDTH, LANES)
    xbc_c, ml_c, gates_c = pl.pallas_call(
        _in_ctx_kernel,
        grid=(b, 1),
        in_specs=[
            pl.BlockSpec((None, ctx_len, d), lambda bi, t: (bi, 0, 0)),
            pl.BlockSpec((None, 1, d), lambda bi, t: (ctx_row * 6, 0, 0)),
            pl.BlockSpec((None, 1, d), lambda bi, t: (ctx_row * 6 + 1, 0, 0)),
            _full((1, d)), _full(w_cat.shape),
        ],
        out_specs=[pl.BlockSpec((None, ctx_len, w), lambda bi, t: (bi, 0, 0)) for w in ctx_widths],
        out_shape=[jax.ShapeDtypeStruct((b, ctx_len, w), t) for w, t in zip(ctx_widths, (BF16, BF16, F32))],
        compiler_params=params,
        name="in_proj_ctx",
    )(ctx, mod3, mod3, nw, w_cat)
    return z, (xbc, xbc_c), (ml, ml_c), (gates, gates_c), (gates_cm, gates_c)


def _chunk_eff(rev, ncc, nc, c):
    nl = nc - ncc
    if not rev:
        return jnp.where(c < ncc, nl + c, c - ncc)
    return jnp.where(c < ncc, nc - 1 - c, nl - 1 - (c - ncc))


def _lat_block(rev, ncc, nc, c):
    nl = nc - ncc
    if not rev:
        return jnp.maximum(c - ncc, 0)
    return jnp.where(c < ncc, nl - 1, nl - 1 - (c - ncc))


CONV_SIDE_TAPS = tuple(t for t in range(CONV_W) if t != CONV_W // 2)


def _shift_matrices():
    row = jnp.arange(CHUNK)[:, None]
    col = jnp.arange(CHUNK + 2 * HALO)[None, :]
    return jnp.stack([col == row + HALO + tap - CONV_W // 2 for tap in CONV_SIDE_TAPS]).astype(BF16)


def _conv_silu(xm, xp, xn, cw_ref, cb_ref, sh_ref, first, last):
    xp = jnp.where(first, jnp.zeros_like(xp), xp)
    xn = jnp.where(last, jnp.zeros_like(xn), xn)
    ext = jnp.concatenate([xp, xm, xn], axis=0)
    mid = CONV_W // 2
    acc = cb_ref[...] + cw_ref[mid:mid + 1, :] * xm.astype(F32)
    for i, tap in enumerate(CONV_SIDE_TAPS):
        acc = acc + cw_ref[tap:tap + 1, :] * _dot(sh_ref[i], ext)
    return _silu(acc)


def _tri(rev, transposed=False):
    row = lax.broadcasted_iota(jnp.int32, (CHUNK, CHUNK), 0)
    col = lax.broadcasted_iota(jnp.int32, (CHUNK, CHUNK), 1)
    if transposed:
        row, col = col, row
    keep = (col >= row) if rev else (col <= row)
    return keep


def _split3(a):
    a1 = a.astype(BF16)
    r1 = a - a1.astype(F32)
    a2 = r1.astype(BF16)
    a3 = (r1 - a2.astype(F32)).astype(BF16)
    return a1, a2, a3


def _dot_sel_l(m_b, a):
    p = _split3(a)
    return _dot(m_b, p[0]) + _dot(m_b, p[1]) + _dot(m_b, p[2])


def _dot_sel_r(a, m_b):
    p = _split3(a)
    return _dot(p[0], m_b) + _dot(p[1], m_b) + _dot(p[2], m_b)


def _row_spec(rows, width, row_block, col_block=0):
    return pl.BlockSpec((NB, rows, width), lambda bi, c: (bi, row_block(c), col_block))


def _local_chunks(rev, ncc, nc, c):
    nl = nc - ncc
    cc = jnp.clip(ncc - 1 - c if rev else c, 0, ncc - 1)
    cl = jnp.clip(nl - 1 - (c - ncc) if rev else c - ncc, 0, nl - 1)
    return c < ncc, cc, cl


def _pair_specs(rev, ncc, nc, rows, width, col=0, halo=0):
    per = CHUNK // HALO

    def index(which, count):
        def fn(c):
            ch = _local_chunks(rev, ncc, nc, c)[which]
            if halo == 0:
                return ch
            if halo < 0:
                return jnp.maximum(ch * per - 1, 0)
            return jnp.minimum((ch + 1) * per, count * per - 1)
        return fn

    return [_row_spec(rows, width, index(2, nc - ncc), col), _row_spec(rows, width, index(1, ncc), col)]


def _pick(is_ctx, lat_ref, ctx_ref, s):
    return jnp.where(is_ctx, ctx_ref[s], lat_ref[s])


def _step_ends(rev, ncc, nc, c):
    is_ctx, cc, cl = _local_chunks(rev, ncc, nc, c)
    first = jnp.where(is_ctx, cc == 0, cl == 0)
    last = jnp.where(is_ctx, cc == ncc - 1, cl == nc - ncc - 1)
    return is_ctx, first, last


def _full(shape):
    return pl.BlockSpec(shape, lambda bi, c: (0,) * len(shape))


def _ssd_gates(rev, xs, g, dtbr_ref, dtbc_ref, alr_ref, alc_ref, e_ref):
    lane0 = SSD_HEADS * int(rev)
    lane = lax.broadcasted_iota(jnp.int32, (CHUNK, LANES), 1)
    lmask = (lane >= lane0) & (lane < lane0 + SSD_HEADS)
    dt = jnp.where(lmask, _softplus(g + dtbr_ref[...]), 0.0)
    a = dt * (-jnp.exp(alr_ref[...]))
    gt = g.T
    dt_t = _softplus(gt + dtbc_ref[...])[lane0:lane0 + SSD_HEADS]
    a_t = dt_t * (-jnp.exp(alc_ref[...][lane0:lane0 + SSD_HEADS]))

    cs = _dot_sel_l(_tri(rev).astype(BF16), a)
    cs_t = _dot_sel_r(a_t, _tri(rev, transposed=True).astype(BF16))
    e = e_ref[...]
    dtx = _dot_sel_r(dt, e)
    csx = _dot_sel_r(cs, e)
    end = 0 if rev else CHUNK - 1
    totx = csx[end:end + 1, :]
    ecsx = jnp.exp(csx)
    decx = jnp.exp(totx - csx)
    etotx = jnp.exp(totx)

    xdt = xs * dtx
    return dict(cs=cs, cs_t=cs_t, ecsx=ecsx, etotx=etotx, xdt_b=xdt.astype(BF16), xd_b=(xdt * decx).astype(BF16))


def _ssd_chunks(rev, xbcs, gs_, dtbr_ref, dtbc_ref, alr_ref, alc_ref, e_ref, s_ref):
    lane0 = SSD_HEADS * int(rev)
    keep = _tri(rev)
    half = lax.broadcasted_iota(jnp.int32, (CHUNK, LANES), 1) // SSD_HEADDIM
    n_bc = SSD_GROUPS * SSD_STATE
    hpg = SSD_HEADS // SSD_GROUPS
    gw = hpg * SSD_HEADDIM

    gq = [_ssd_gates(rev, xbc[:, :SSD_WIDTH], g, dtbr_ref, dtbc_ref, alr_ref, alc_ref, e_ref)
          for xbc, g in zip(xbcs, gs_)]
    units = []
    for s, xbc in enumerate(xbcs):
        for grp in range(SSD_GROUPS):
            bm = xbc[:, SSD_WIDTH + grp * SSD_STATE:SSD_WIDTH + (grp + 1) * SSD_STATE]
            cm = xbc[:, SSD_WIDTH + n_bc + grp * SSD_STATE:SSD_WIDTH + n_bc + (grp + 1) * SSD_STATE]
            units.append(dict(s=s, grp=grp, cols=slice(grp * gw, (grp + 1) * gw), bm=bm, bm_b=bm.astype(BF16),
                              cm_b=cm.astype(BF16), s_old=s_ref[s, :, grp * gw:(grp + 1) * gw]))
    for u in units:
        u["cb"] = _dot_nt(u["cm_b"], u["bm_b"])
        u["y_off"] = _dot(u["cm_b"], u["s_old"].astype(BF16)) * gq[u["s"]]["ecsx"][:, u["cols"]]
        u["bt_b"] = u["bm"].T.astype(BF16)
    for u in units:
        q_ = gq[u["s"]]
        masks = []
        for hh in range(hpg):
            h = u["grp"] * hpg + hh
            dl = q_["cs"][:, lane0 + h:lane0 + h + 1] - q_["cs_t"][h:h + 1, :]
            masks.append((u["cb"] * jnp.exp(jnp.where(keep, dl, NEG_BIG))).astype(BF16))
        u["masks"] = masks
    for u in units:
        q_ = gq[u["s"]]
        blocks = []
        for pair in range(hpg // 2):
            blk = u["grp"] * (hpg // 2) + pair
            xj = q_["xdt_b"][:, blk * LANES:(blk + 1) * LANES]
            acc = u["y_off"][:, pair * LANES:(pair + 1) * LANES]
            for q in range(2):
                acc = acc + _dot(u["masks"][pair * 2 + q], jnp.where(half == q, xj, jnp.zeros_like(xj)))
            blocks.append(acc)
        u["y"] = blocks
    for u in units:
        q_ = gq[u["s"]]
        s_new = u["s_old"] * q_["etotx"][:, u["cols"]] + _dot(u["bt_b"], q_["xd_b"][:, u["cols"]])
        s_ref[u["s"], :, u["cols"]] = s_new
    return [jnp.concatenate([blk for u in units if u["s"] == s for blk in u["y"]], axis=1)
            for s in range(len(xbcs))]


def _ssd_rev_kernel(ncc, nc, xml_ref, xmc_ref, xpl_ref, xpc_ref, xnl_ref, xnc_ref, gl_ref, gc_ref, cw_ref, cb_ref,
                    sh_ref, dtbr_ref, dtbc_ref, alr_ref, alc_ref, e_ref, o_ref, xc_ref, s_ref):
    c = pl.program_id(1)
    is_ctx, first, last = _step_ends(True, ncc, nc, c)

    @pl.when(c == 0)
    def _():
        s_ref[...] = jnp.zeros_like(s_ref)

    xbcs = [_conv_silu(_pick(is_ctx, xml_ref, xmc_ref, s), _pick(is_ctx, xpl_ref, xpc_ref, s),
                       _pick(is_ctx, xnl_ref, xnc_ref, s), cw_ref, cb_ref, sh_ref, first, last) for s in range(NB)]
    for s in range(NB):
        xc_ref[s] = xbcs[s].astype(BF16)
    ys = _ssd_chunks(True, xbcs, [_pick(is_ctx, gl_ref, gc_ref, s) for s in range(NB)], dtbr_ref, dtbc_ref,
                     alr_ref, alc_ref, e_ref, s_ref)
    for s in range(NB):
        o_ref[s] = ys[s].astype(o_ref.dtype)


def _ssd_fwd_kernel(ncc, xc_ref, gl_ref, gc_ref, dtbr_ref, dtbc_ref, alr_ref, alc_ref, e_ref, dsk_ref, yb_ref,
                    z_ref, nw_ref, o_ref, s_ref):
    is_ctx = pl.program_id(1) < ncc

    @pl.when(pl.program_id(1) == 0)
    def _():
        s_ref[...] = jnp.zeros_like(s_ref)

    xbcs = [xc_ref[s].astype(F32) for s in range(NB)]
    ys = _ssd_chunks(False, xbcs, [_pick(is_ctx, gl_ref, gc_ref, s) for s in range(NB)], dtbr_ref, dtbc_ref,
                     alr_ref, alc_ref, e_ref, s_ref)
    ys = [(ys[s] + yb_ref[s].astype(F32) + dsk_ref[...] * xbcs[s][:, :SSD_WIDTH]) * _silu(z_ref[s].astype(F32))
          for s in range(NB)]
    scale = [lax.rsqrt(jnp.mean(y * y, axis=-1, keepdims=True) + EPS) for y in ys]
    for s in range(NB):
        o_ref[s] = (ys[s] * scale[s] * nw_ref[...]).astype(o_ref.dtype)


def _ssd_scans(xbc, gates, z, conv_params, gate_params, e_mats, dsk, nw):
    b, seq, _ = xbc[0].shape
    ncc = xbc[1].shape[1] // CHUNK
    nc = seq // CHUNK + ncc
    state = pltpu.VMEM((NB, SSD_STATE, SSD_WIDTH), F32)
    params = pltpu.CompilerParams(dimension_semantics=("arbitrary", "arbitrary"), vmem_limit_bytes=VMEM_LIMIT)
    lat_shape = jax.ShapeDtypeStruct((b, seq, SSD_WIDTH), BF16)

    ceff = functools.partial(_chunk_eff, True, ncc, nc)
    lat = functools.partial(_lat_block, True, ncc, nc)
    pair = functools.partial(_pair_specs, True, ncc, nc)
    rev_params = list(conv_params) + list(gate_params) + [e_mats[1]]
    yb, xbc_act = pl.pallas_call(
        functools.partial(_ssd_rev_kernel, ncc, nc),
        grid=(b // NB, nc),
        in_specs=pair(CHUNK, SSD_XBC) + pair(HALO, SSD_XBC, halo=-1) + pair(HALO, SSD_XBC, halo=1)
        + pair(CHUNK, LANES) + [_full(p.shape) for p in rev_params],
        out_specs=[_row_spec(CHUNK, SSD_WIDTH, lat), _row_spec(CHUNK, SSD_XBC, ceff)],
        out_shape=(lat_shape, jax.ShapeDtypeStruct((b, nc * CHUNK, SSD_XBC), BF16)),
        scratch_shapes=[state],
        compiler_params=params,
        name="ssd_rev",
    )(*xbc, *xbc, *xbc, *gates, *rev_params)

    ceff = functools.partial(_chunk_eff, False, ncc, nc)
    lat = functools.partial(_lat_block, False, ncc, nc)
    fwd_params = list(gate_params) + [e_mats[0], dsk]
    return pl.pallas_call(
        functools.partial(_ssd_fwd_kernel, ncc),
        grid=(b // NB, nc),
        in_specs=[_row_spec(CHUNK, SSD_XBC, ceff)] + _pair_specs(False, ncc, nc, CHUNK, LANES)
        + [_full(p.shape) for p in fwd_params]
        + [_row_spec(CHUNK, SSD_WIDTH, lat), _row_spec(CHUNK, SSD_WIDTH, lat), _full(nw.shape)],
        out_specs=_row_spec(CHUNK, SSD_WIDTH, lat),
        out_shape=lat_shape,
        scratch_shapes=[state],
        compiler_params=params,
        name="ssd_fwd",
    )(xbc_act, *gates, *fwd_params, yb, z, nw)


ML_I_LANE0 = 2 * SSD_HEADS
ML_F_LANE0 = ML_I_LANE0 + 2 * ML_HEADS


def _ml_gates(rev, g, gbr_ref, gbc_ref):
    i_lane0 = ML_I_LANE0 + ML_HEADS * int(rev)
    f_lane0 = ML_F_LANE0 + ML_HEADS * int(rev)
    ga = g + gbr_ref[...]
    lane = lax.broadcasted_iota(jnp.int32, (CHUNK, LANES), 1)
    logf = jnp.where((lane >= f_lane0) & (lane < f_lane0 + ML_HEADS), -_softplus(-ga), 0.0)
    cs = _dot_sel_l(_tri(rev).astype(BF16), logf)
    gt = g.T + gbc_ref[...]
    i_t = gt[ML_I_LANE0:ML_F_LANE0]
    logf_t = -_softplus(-gt[ML_F_LANE0:ML_F_LANE0 + 2 * ML_HEADS])
    cs_t = _dot_sel_r(logf_t, _tri(rev, transposed=True).astype(BF16))
    end = 0 if rev else CHUNK - 1

    u_t = i_t - cs_t
    u_c = ga - pltpu.roll(cs, LANES - (ML_F_LANE0 - ML_I_LANE0), axis=1)
    row = lax.broadcasted_iota(jnp.int32, (CHUNK, LANES), 0)
    pm = u_c
    step = 1
    while step < CHUNK:
        if rev:
            pm = jnp.maximum(pm, jnp.where(row < CHUNK - step, pltpu.roll(pm, CHUNK - step, axis=0), NEG_BIG))
        else:
            pm = jnp.maximum(pm, jnp.where(row >= step, pltpu.roll(pm, step, axis=0), NEG_BIG))
        step *= 2

    heads = []
    for h in range(ML_HEADS):
        r = ML_HEADS * int(rev) + h
        li = i_lane0 + h
        heads.append(dict(
            csc=cs[:, f_lane0 + h:f_lane0 + h + 1], u_col=u_c[:, li:li + 1], u_row=u_t[r:r + 1, :],
            tot=cs_t[r:r + 1, end:end + 1], u_max=pm[end:end + 1, li:li + 1], pm_col=pm[:, li:li + 1]))
    return heads


def _ml_chunks(rev, samples, gbr_ref, gbc_ref, c_ref, n_ref, mx_ref, consume):
    keep = _tri(rev)
    ones_b = jnp.ones((CHUNK, ML_HEADDIM), BF16)
    units = []
    for s, (k, q_b, k_b, v_b, g) in enumerate(samples):
        for h, gq in enumerate(_ml_gates(rev, g, gbr_ref, gbc_ref)):
            sl = slice(h * ML_HEADDIM, (h + 1) * ML_HEADDIM)
            units.append(dict(gq, s=s, h=h, kh=k[:, sl], vh=v_b[:, sl], qh_b=q_b[:, sl], kh_b=k_b[:, sl],
                              m_prev=mx_ref[s, h:h + 1, 0:1], c_prev=c_ref[s * ML_HEADS + h],
                              n_prev=n_ref[s, h:h + 1, :]))

    for u in units:
        u["qk"] = _dot_nt(u["qh_b"], u["kh_b"])
        u["vt_b"] = u["vh"].astype(F32).T.astype(BF16)
    for u in units:
        u["mm"] = jnp.maximum(u["m_prev"], u["pm_col"])
        u["scores_b"] = (u["qk"] * jnp.exp(jnp.where(keep, u["u_row"] - u["mm"], NEG_BIG))).astype(BF16)
    for u in units:
        c_aug = jnp.concatenate([u["c_prev"], jnp.broadcast_to(u["n_prev"], (CHUNK, ML_HEADDIM))], axis=0)
        u["intra"] = _dot(u["scores_b"], jnp.concatenate([u["vh"], ones_b], axis=1))
        u["inter"] = _dot_nt(u["qh_b"], c_aug.astype(BF16))
    outs = [[None] * ML_HEADS for _ in samples]
    for u in units:
        both = u["intra"] + jnp.exp(u["m_prev"] - u["mm"]) * u["inter"]
        den = both[:, ML_HEADDIM:ML_HEADDIM + 1]
        outs[u["s"]][u["h"]] = both[:, :ML_HEADDIM] / jnp.maximum(jnp.abs(den), jnp.exp(-(u["csc"] + u["mm"])))
    consume(outs)
    for u in units:
        kw = u["kh"] * jnp.exp(u["u_col"] - u["u_max"])
        u["c_loc"] = _dot(u["vt_b"], kw.astype(BF16))
        u["n_loc"] = jnp.sum(kw, axis=0, keepdims=True)
    for u in units:
        s, h, tot, m_prev = u["s"], u["h"], u["tot"], u["m_prev"]
        m_loc = tot + u["u_max"]
        m_new = jnp.maximum(tot + m_prev, m_loc)
        s_prev = jnp.exp(tot + m_prev - m_new)
        s_loc = jnp.exp(m_loc - m_new)
        c_ref[s * ML_HEADS + h] = s_prev * u["c_prev"] + s_loc * u["c_loc"]
        n_ref[s, h:h + 1, :] = s_prev * u["n_prev"] + s_loc * u["n_loc"]
        mx_ref[s, h:h + 1, :] = jnp.broadcast_to(m_new, (1, LANES))


def _ml_init_state(c_ref, n_ref, mx_ref):
    c_ref[...] = jnp.zeros_like(c_ref)
    n_ref[...] = jnp.zeros_like(n_ref)
    mx_ref[...] = jnp.full(mx_ref.shape, NEG_BIG, F32)


def _ml_rev_kernel(ncc, nc, xml_ref, xmc_ref, xpl_ref, xpc_ref, xnl_ref, xnc_ref, vl_ref, vc_ref, gl_ref, gc_ref,
                   cw_ref, cb_ref, sh_ref, wq_ref, wk_ref, gbr_ref, gbc_ref, o_ref, xc_ref, q_ref, k_ref, c_ref, n_ref,
                   mx_ref):
    c = pl.program_id(1)
    is_ctx, first, last = _step_ends(True, ncc, nc, c)

    @pl.when(c == 0)
    def _():
        _ml_init_state(c_ref, n_ref, mx_ref)

    samples = []
    for s in range(NB):
        xconv = _conv_silu(_pick(is_ctx, xml_ref, xmc_ref, s), _pick(is_ctx, xpl_ref, xpc_ref, s),
                           _pick(is_ctx, xnl_ref, xnc_ref, s), cw_ref, cb_ref, sh_ref, first, last)
        xc_b = xconv.astype(BF16)
        q = _dot(xc_b, wq_ref[...])
        k = _dot(xc_b, wk_ref[...])
        q_b, k_b = q.astype(BF16), k.astype(BF16)
        xc_ref[s] = xc_b
        q_ref[s] = q_b
        k_ref[s] = k_b
        samples.append((k, q_b, k_b, _pick(is_ctx, vl_ref, vc_ref, s), _pick(is_ctx, gl_ref, gc_ref, s)))
    def store(outs):
        for s in range(NB):
            o_ref[s] = jnp.concatenate(outs[s], axis=1).astype(o_ref.dtype)

    _ml_chunks(True, samples, gbr_ref, gbc_ref, c_ref, n_ref, mx_ref, store)


def _ml_fwd_kernel(ncc, xc_ref, q_ref, k_ref, vl_ref, vc_ref, og_ref, gl_ref, gc_ref, gbr_ref, gbc_ref, hb_ref,
                   nw_ref, sk_ref, o_ref, c_ref, n_ref, mx_ref):
    is_ctx = pl.program_id(1) < ncc

    @pl.when(pl.program_id(1) == 0)
    def _():
        _ml_init_state(c_ref, n_ref, mx_ref)

    samples = []
    for s in range(NB):
        q_b, k_b = q_ref[s], k_ref[s]
        samples.append((k_b.astype(F32), q_b, k_b, _pick(is_ctx, vl_ref, vc_ref, s),
                        _pick(is_ctx, gl_ref, gc_ref, s)))
    def finish(outs):
        gated = []
        for s in range(NB):
            for h in range(ML_HEADS):
                sl = slice(h * ML_HEADDIM, (h + 1) * ML_HEADDIM)
                gated.append(jax.nn.sigmoid(og_ref[s, :, sl].astype(F32))
                             * (outs[s][h] + hb_ref[s, :, sl].astype(F32)))
        scale = [lax.rsqrt(jnp.mean(hh * hh, axis=-1, keepdims=True) + EPS) for hh in gated]
        normed = [hh * sc for hh, sc in zip(gated, scale)]
        for s in range(NB):
            y = jnp.concatenate(normed[s * ML_HEADS:(s + 1) * ML_HEADS], axis=1) * nw_ref[...]
            o_ref[s] = (y + sk_ref[...] * xc_ref[s].astype(F32)).astype(o_ref.dtype)

    _ml_chunks(False, samples, gbr_ref, gbc_ref, c_ref, n_ref, mx_ref, finish)


def _ml_scans(ml, gates, conv_params, proj_params, gate_params, nw, sk):
    b, seq, _ = ml[0].shape
    ncc = ml[1].shape[1] // CHUNK
    nc = seq // CHUNK + ncc
    scratch = [pltpu.VMEM((NB * ML_HEADS, ML_HEADDIM, ML_HEADDIM), F32),
               pltpu.VMEM((NB, 8, ML_HEADDIM), F32),
               pltpu.VMEM((NB, 8, LANES), F32)]
    params = pltpu.CompilerParams(dimension_semantics=("arbitrary", "arbitrary"), vmem_limit_bytes=VMEM_LIMIT)
    lat_shape = jax.ShapeDtypeStruct((b, seq, ML_WIDTH), BF16)
    act_shape = jax.ShapeDtypeStruct((b, nc * CHUNK, ML_WIDTH), BF16)

    ceff = functools.partial(_chunk_eff, True, ncc, nc)
    lat = functools.partial(_lat_block, True, ncc, nc)
    pair = functools.partial(_pair_specs, True, ncc, nc)
    rev_params = list(conv_params) + list(proj_params) + list(gate_params)
    act_spec = _row_spec(CHUNK, ML_WIDTH, ceff)
    hb, xc, q, k = pl.pallas_call(
        functools.partial(_ml_rev_kernel, ncc, nc),
        grid=(b // NB, nc),
        in_specs=pair(CHUNK, ML_WIDTH) + pair(HALO, ML_WIDTH, halo=-1) + pair(HALO, ML_WIDTH, halo=1)
        + pair(CHUNK, ML_WIDTH, col=1) + pair(CHUNK, LANES) + [_full(p.shape) for p in rev_params],
        out_specs=[_row_spec(CHUNK, ML_WIDTH, lat), act_spec, act_spec, act_spec],
        out_shape=(lat_shape, act_shape, act_shape, act_shape),
        scratch_shapes=scratch,
        compiler_params=params,
        name="mlstm_rev",
    )(*ml, *ml, *ml, *ml, *gates, *rev_params)

    ceff = functools.partial(_chunk_eff, False, ncc, nc)
    lat = functools.partial(_lat_block, False, ncc, nc)
    pair = functools.partial(_pair_specs, False, ncc, nc)
    act_spec = _row_spec(CHUNK, ML_WIDTH, ceff)
    return pl.pallas_call(
        functools.partial(_ml_fwd_kernel, ncc),
        grid=(b // NB, nc),
        in_specs=[act_spec, act_spec, act_spec] + pair(CHUNK, ML_WIDTH, col=1)
        + [_row_spec(CHUNK, ML_WIDTH, lat, 2)] + pair(CHUNK, LANES)
        + [_full(p.shape) for p in gate_params]
        + [_row_spec(CHUNK, ML_WIDTH, lat), _full(nw.shape), _full(sk.shape)],
        out_specs=_row_spec(CHUNK, ML_WIDTH, lat),
        out_shape=lat_shape,
        scratch_shapes=scratch,
        compiler_params=params,
        name="mlstm_fwd",
    )(xc, q, k, *ml, ml[0], *gates, *gate_params, hb, nw, sk)


def _out_kernel(x_ref, ys_ref, ym_ref, wo_ref, g1_ref, sh_ref, sc_ref, nw_ref, rw_ref, rb_ref,
                x1_ref, ts_ref, route_ref):
    mix = _dot(ys_ref[...], wo_ref[0:SSD_WIDTH, :]) + _dot(ym_ref[...], wo_ref[SSD_WIDTH:, :])
    x1 = x_ref[...] + g1_ref[...] * mix
    y = x1 * lax.rsqrt(jnp.mean(x1 * x1, axis=-1, keepdims=True) + EPS) * nw_ref[...]
    t = y * (1.0 + sc_ref[...]) + sh_ref[...]
    x1_ref[...] = x1
    for j in range(SLAB):
        ts_ref[pl.ds(j, TM_OUT, stride=SLAB), :] = t[:, j * LANES:(j + 1) * LANES]
    lg2 = _dot(t.astype(BF16), rw_ref[...])
    lg = lg2[:, :LANES] + lg2[:, LANES:] + rb_ref[...]

    lane = lax.broadcasted_iota(jnp.int32, lg.shape, 1).astype(F32)
    gmask = lane < MOE_GROUPS
    gl = jnp.where(gmask, lg, NEG_BIG)
    gmax = jnp.max(gl, axis=1, keepdims=True)
    g_sel = jnp.min(jnp.where(gmask & (gl == gmax), lane, 1e9), axis=1, keepdims=True)
    p_group = 1.0 / jnp.sum(jnp.where(gmask, jnp.exp(gl - gmax), 0.0), axis=1, keepdims=True)
    lo = ROUTE_LANE0 + MOE_EPG * g_sel
    emask = (lane >= lo) & (lane < lo + MOE_EPG)
    l1 = jnp.max(jnp.where(emask, lg, NEG_BIG), axis=1, keepdims=True)
    i1 = jnp.min(jnp.where(emask & (lg == l1), lane, 1e9), axis=1, keepdims=True)
    emask2 = emask & (lane != i1)
    l2 = jnp.max(jnp.where(emask2, lg, NEG_BIG), axis=1, keepdims=True)
    i2 = jnp.min(jnp.where(emask2 & (lg == l2), lane, 1e9), axis=1, keepdims=True)
    r = jnp.exp(l2 - l1)
    w1 = p_group / (1.0 + r)
    w2 = p_group * r / (1.0 + r)
    route = (jnp.where(lane == 0, i1, 0.0) + jnp.where(lane == 1, i2, 0.0)
             + jnp.where(lane == 2, w1, 0.0) + jnp.where(lane == 3, w2, 0.0))
    route_ref[...] = route.T[0:8, :]


def _out_proj(x, y_ssd, y_ml, w_out_b, mod3, norm_w, rw2, rb):
    b, seq, d = x.shape
    nt = seq // TM_OUT

    def row(j):
        return pl.BlockSpec((None, 1, d), lambda bi, t: (bi * 6 + j, 0, 0))

    def tile(width):
        return pl.BlockSpec((None, TM_OUT, width), lambda bi, t: (bi, t, 0))

    return pl.pallas_call(
        _out_kernel,
        grid=(b, nt),
        in_specs=[tile(d), tile(SSD_WIDTH), tile(ML_WIDTH), _full(w_out_b.shape),
                  row(2), row(3), row(4), _full((1, d)), _full(rw2.shape), _full(rb.shape)],
        out_specs=[tile(d),
                   pl.BlockSpec((None, TM_OUT * SLAB, LANES), lambda bi, t: (bi, t, 0)),
                   pl.BlockSpec((None, 8, TM_OUT), lambda bi, t: (bi, 0, t))],
        out_shape=(jax.ShapeDtypeStruct((b, seq, d), F32),
                   jax.ShapeDtypeStruct((b, seq * SLAB, LANES), F32),
                   jax.ShapeDtypeStruct((b, 8, seq), F32)),
        compiler_params=pltpu.CompilerParams(
            dimension_semantics=("arbitrary", "arbitrary"), vmem_limit_bytes=VMEM_LIMIT),
        name="out_proj_router",
    )(x, y_ssd, y_ml, w_out_b, mod3, mod3, mod3, norm_w.reshape(1, d), rw2, rb)


def _route_tables(route, seq):
    b = route.shape[0]
    n_inst = 2 * seq
    nblk = n_inst // BM + MOE_EXPERTS
    e_flat = (route[:, 0:2, :].astype(jnp.int32) - ROUTE_LANE0).reshape(b, n_inst)
    w_flat = route[:, 2:4, :].reshape(b, n_inst)
    tok_flat = jnp.tile(jnp.arange(seq, dtype=jnp.int32), 2 * b).reshape(b, n_inst)
    _, tok_sorted, w_sorted = lax.sort((e_flat, tok_flat, w_flat), dimension=1, num_keys=1)
    experts = jnp.arange(MOE_EXPERTS, dtype=jnp.int32)
    counts = jnp.sum((e_flat[:, None, :] == experts[None, :, None]).astype(jnp.int32), axis=2)
    nblk_e = (counts + BM - 1) // BM
    blk_end = jnp.cumsum(nblk_e, axis=1)
    blk_start = blk_end - nblk_e
    cnt_start = jnp.cumsum(counts, axis=1) - counts
    nb = blk_end[:, -1:]
    j = jnp.arange(nblk, dtype=jnp.int32)[None, :]
    valid_blk = j < nb
    jj = jnp.minimum(j, nb - 1)
    e_j = jnp.sum((jj[:, :, None] >= blk_end[:, None, :]).astype(jnp.int32), axis=2)
    onehot = (e_j[:, :, None] == experts[None, None, :]).astype(jnp.int32)
    take = lambda tbl: jnp.sum(onehot * tbl[:, None, :], axis=2)
    r = jnp.arange(BM, dtype=jnp.int32)[None, None, :]
    rank = ((jj - take(blk_start)) * BM)[:, :, None] + r
    valid = valid_blk[:, :, None] & (rank < take(counts)[:, :, None])
    sidx = jnp.clip(take(cnt_start)[:, :, None] + rank, 0, n_inst - 1).reshape(b, nblk * BM)
    tok = jnp.take_along_axis(tok_sorted, sidx, axis=1).reshape(b, nblk, BM)
    wslot = jnp.take_along_axis(w_sorted, sidx, axis=1).reshape(b, nblk, BM)
    tok = jnp.where(valid, tok, seq + r) * SLAB
    wslot = jnp.where(valid, wslot, 0.0)
    return (nblk_e.reshape(-1), blk_start.reshape(-1), tok.reshape(b, 1, nblk * BM), wslot)


SEM_T, SEM_X, SEM_O = 0, 1, 3


def _moe_kernel(seq, nbe_ref, bs_ref, tok_ref, ws_ref, wg_ref, wu_ref, wd_ref, g2_ref, fw_ref,
                t_hbm, x1_hbm, o_hbm, t_scr, y_scr, wgb, wub, wdb, xt, ot, xin, stage, sems):
    b = pl.program_id(0)
    e = pl.program_id(1)
    nb = pl.num_programs(0)
    rows = seq * SLAB
    n_fin = seq // TM_FIN

    def t_copy(sample):
        return pltpu.make_async_copy(t_hbm.at[sample], t_scr.at[pl.ds(0, rows)], sems.at[SEM_T])

    def x1_copy(s, slot):
        return pltpu.make_async_copy(x1_hbm.at[b, pl.ds(s * TM_FIN, TM_FIN)], xin.at[slot], sems.at[SEM_X + slot])

    def out_copy(s, slot):
        return pltpu.make_async_copy(stage.at[slot], o_hbm.at[b, pl.ds(s * TM_FIN, TM_FIN)], sems.at[SEM_O + slot])

    @pl.when((e == 0) & (b == 0))
    def _():
        t_copy(b).start()
        y_scr[...] = jnp.zeros_like(y_scr)
        t_scr[pl.ds(rows, BM * SLAB), :] = jnp.zeros((BM * SLAB, LANES), F32)

    @pl.when(e == 0)
    def _():
        t_copy(b).wait()

    n_blocks = nbe_ref[b * MOE_EXPERTS + e]
    blk0 = bs_ref[b * MOE_EXPERTS + e]

    @pl.when(n_blocks > 0)
    def _():
        wgb[...] = wg_ref[...].astype(BF16)
        wub[...] = wu_ref[...].astype(BF16)
        wdb[...] = wd_ref[...].astype(BF16)
        g2 = g2_ref[...]
        diag = (lax.broadcasted_iota(jnp.int32, (BM, BM), 0) == lax.broadcasted_iota(jnp.int32, (BM, BM), 1))

        def block(i, carry):
            blk = blk0 + i
            base = blk * BM

            tok_blk = tok_ref.at[0, pl.ds(base, BM)]

            def slab_rows(r):
                return pl.ds(pl.multiple_of(tok_blk[r], SLAB), SLAB)

            w_col = jnp.sum(jnp.where(diag, ws_ref[pl.ds(blk, 1), :], 0.0), axis=1, keepdims=True)
            parts = range(BLOCK_PARTS)
            for p in parts:
                for r in range(PART):
                    xt[p, pl.ds(r, SLAB, stride=PART_PITCH), :] = t_scr[slab_rows(p * PART + r), :]
            for p in parts:
                x = jnp.concatenate([xt[p, c * PART_PITCH:c * PART_PITCH + PART, :] for c in range(SLAB)],
                                    axis=1).astype(BF16)
                hidden = _silu(_dot(x, wgb[...])) * _dot(x, wub[...]) * w_col[p * PART:(p + 1) * PART]
                out = _dot(hidden.astype(BF16), wdb[...]) * g2
                for c in range(SLAB):
                    ot[p, c * PART_PITCH:c * PART_PITCH + PART, :] = out[:, c * LANES:(c + 1) * LANES]
            for p in parts:
                for r0 in range(0, PART, SCATTER_GROUP):
                    sl = [slab_rows(p * PART + r0 + u) for u in range(SCATTER_GROUP)]
                    vals = [y_scr[sl[u], :] + ot[p, pl.ds(r0 + u, SLAB, stride=PART_PITCH), :]
                            for u in range(SCATTER_GROUP)]
                    for u in range(SCATTER_GROUP):
                        y_scr[sl[u], :] = vals[u]
            return carry

        lax.fori_loop(0, n_blocks, block, 0)

    @pl.when(e == MOE_EXPERTS - 1)
    def _():
        @pl.when(b + 1 < nb)
        def _():
            t_copy(b + 1).start()

        x1_copy(0, 0).start()

        def tile(s, carry):
            slot = s % 2
            x1_copy(s, slot).wait()

            @pl.when(s + 1 < n_fin)
            def _():
                x1_copy(s + 1, 1 - slot).start()

            @pl.when(s >= 2)
            def _():
                out_copy(s - 2, slot).wait()

            base = pl.multiple_of(s * (TM_FIN * SLAB), TM_FIN * SLAB)
            x1 = xin[slot]
            chunks = [x1[:, c * LANES:(c + 1) * LANES] + y_scr[pl.ds(base + c, TM_FIN, stride=SLAB), :]
                      for c in range(SLAB)]
            y_scr[pl.ds(base, TM_FIN * SLAB), :] = jnp.zeros((TM_FIN * SLAB, LANES), F32)
            ssq = chunks[0] * chunks[0]
            for c in range(1, SLAB):
                ssq = ssq + chunks[c] * chunks[c]
            inv = lax.rsqrt(jnp.sum(ssq, axis=1, keepdims=True) * (1.0 / D_MODEL) + EPS)
            for c in range(SLAB):
                stage[slot, :, c * LANES:(c + 1) * LANES] = chunks[c] * inv * fw_ref[:, c * LANES:(c + 1) * LANES]
            out_copy(s, slot).start()
            return carry

        lax.fori_loop(0, n_fin, tile, 0)
        out_copy(n_fin - 2, n_fin % 2).wait()
        out_copy(n_fin - 1, (n_fin - 1) % 2).wait()


def _moe(t_slab, x1, route, wg, wu, wd, mod3, final_w):
    b, seq, d = x1.shape
    nblk_e, blk_start, tok, wslot = _route_tables(route, seq)
    nblk = wslot.shape[1]

    def w_spec(shape):
        return pl.BlockSpec((None,) + shape, lambda bi, e, nbe, bs: (e, 0, 0))

    grid_spec = pltpu.PrefetchScalarGridSpec(
        num_scalar_prefetch=2,
        grid=(b, MOE_EXPERTS),
        in_specs=[
            pl.BlockSpec((None, 1, nblk * BM), lambda bi, e, nbe, bs: (bi, 0, 0), memory_space=pltpu.SMEM),
            pl.BlockSpec((None, nblk, BM), lambda bi, e, nbe, bs: (bi, 0, 0)),
            w_spec((d, D_EXPERT)), w_spec((d, D_EXPERT)), w_spec((D_EXPERT, d)),
            pl.BlockSpec((None, 1, d), lambda bi, e, nbe, bs: (bi * 6 + 5, 0, 0)),
            pl.BlockSpec((1, d), lambda bi, e, nbe, bs: (0, 0)),
            pl.BlockSpec(memory_space=pl.ANY),
            pl.BlockSpec(memory_space=pl.ANY),
        ],
        out_specs=pl.BlockSpec(memory_space=pl.ANY),
        scratch_shapes=[
            pltpu.VMEM(((seq + BM) * SLAB, LANES), F32),
            pltpu.VMEM(((seq + BM) * SLAB, LANES), F32),
            pltpu.VMEM((d, D_EXPERT), BF16),
            pltpu.VMEM((d, D_EXPERT), BF16),
            pltpu.VMEM((D_EXPERT, d), BF16),
            pltpu.VMEM((BLOCK_PARTS, SLAB * PART_PITCH, LANES), F32),
            pltpu.VMEM((BLOCK_PARTS, SLAB * PART_PITCH, LANES), F32),
            pltpu.VMEM((2, TM_FIN, d), F32),
            pltpu.VMEM((2, TM_FIN, d), F32),
            pltpu.SemaphoreType.DMA((5,)),
        ],
    )
    return pl.pallas_call(
        functools.partial(_moe_kernel, seq),
        grid_spec=grid_spec,
        out_shape=jax.ShapeDtypeStruct((b, seq, d), F32),
        compiler_params=pltpu.CompilerParams(
            dimension_semantics=("arbitrary", "arbitrary"), vmem_limit_bytes=VMEM_LIMIT_MOE),
        name="experts_final_norm",
    )(nblk_e, blk_start, tok, wslot, wg, wu, wd, mod3, final_w.reshape(1, d), t_slab, x1)


def _lane_pad(v, offset=0):
    v = v.reshape(-1).astype(F32)
    return jnp.zeros((1, LANES), F32).at[0, offset:offset + v.shape[0]].set(v)


def _from_col_major(t, rows):
    b, length, ch = t.shape
    return t.reshape(b, GRID_W, rows, ch).transpose(0, 2, 1, 3).reshape(b, length, ch)


def kernel(x, c, ctx, c_ctx, w_mod, b_mod, norm1_w, w_in, ssd_conv_w, ssd_conv_b, ssd_dt_bias, ssd_a_log, ssd_d, ssd_norm_w, ml_conv_w, ml_conv_b, ml_w_qk, ml_gate_b, ml_norm_w, ml_skip, w_out, norm2_w, moe_rg_w, moe_rg_b, moe_re_w, moe_re_b, moe_w_gate, moe_w_up, moe_w_down, final_norm_w):
    b, seq, d = x.shape
    ctx_len = ctx.shape[1]
    rows = seq // GRID_W
    assert w_mod.shape[0] == 1 and d == D_MODEL and b + 1 <= 8 and b % NB == 0
    assert seq % TM_OUT == 0 and ctx_len % CHUNK == 0 and seq % TM_FIN == 0 and (2 * seq) % BM == 0
    assert seq == rows * GRID_W and GRID_W % W_TILE == 0 and rows % 8 == 0 and seq // TM_FIN >= 2

    c_all = jnp.zeros((8, d), F32).at[:b].set(c).at[b].set(c_ctx)
    mod = _modulation(c_all, w_mod[0], b_mod[0])
    mod3 = mod.reshape(8 * 6, 1, d)

    w = w_in[0]
    ssd_in = SSD_WIDTH + SSD_XBC + 2 * SSD_HEADS
    ml_main = 3 * ML_WIDTH
    n_gate = 2 * SSD_HEADS + 4 * ML_HEADS
    w_cat = jnp.concatenate([
        w[:, :SSD_WIDTH + SSD_XBC], w[:, ssd_in:ssd_in + ml_main],
        w[:, SSD_WIDTH + SSD_XBC:ssd_in], w[:, ssd_in + ml_main:],
        jnp.zeros((d, LANES - n_gate), F32)], axis=1).astype(BF16)
    z, xbc, ml, gates, gates_cm = _in_proj(x, ctx, mod3, norm1_w[0], w_cat)

    e_mats = []
    for direction in range(2):
        lane = jnp.arange(LANES)[:, None]
        head = (jnp.arange(SSD_WIDTH) // SSD_HEADDIM)[None, :]
        e_mats.append((lane == direction * SSD_HEADS + head).astype(BF16))
    cw = jnp.zeros((8, SSD_XBC), F32).at[:CONV_W].set(ssd_conv_w[0])
    dtb = _lane_pad(ssd_dt_bias[0])
    alog = _lane_pad(ssd_a_log[0])
    dsk = jnp.repeat(ssd_d[0], SSD_HEADDIM).reshape(1, SSD_WIDTH)
    shifts = _shift_matrices()
    y_ssd = _ssd_scans(xbc, gates, z, [cw, ssd_conv_b[0].reshape(1, -1), shifts],
                       [dtb, dtb.reshape(LANES, 1), alog, alog.reshape(LANES, 1)], e_mats, dsk,
                       ssd_norm_w[0].reshape(1, -1))

    w_rows = jnp.tile(ml_w_qk[0].reshape(2, ML_WIDTH, ML_QK_BLOCK), (1, 1, ML_WIDTH // ML_QK_BLOCK))
    blk_id = jnp.arange(ML_WIDTH) // ML_QK_BLOCK
    w_bd = jnp.where((blk_id[:, None] == blk_id[None, :])[None], w_rows, 0.0)
    wq = w_bd[0].astype(BF16)
    wk = (w_bd[1] * (ML_HEADDIM ** -0.5)).astype(BF16)
    mcw = jnp.zeros((8, ML_WIDTH), F32).at[:CONV_W].set(ml_conv_w[0])
    gb = _lane_pad(ml_gate_b[0], offset=2 * SSD_HEADS)
    y_ml_cm = _ml_scans(ml, gates_cm, [mcw, ml_conv_b[0].reshape(1, -1), shifts], [wq, wk], [gb, gb.reshape(LANES, 1)],
                        ml_norm_w[0].reshape(1, -1), ml_skip[0].reshape(1, -1))
    y_ml = _from_col_major(y_ml_cm, rows)

    rw = jnp.concatenate([moe_rg_w[0], moe_re_w[0],
                          jnp.zeros((d, LANES - MOE_GROUPS - MOE_EXPERTS), F32)], axis=1)
    rw_hi = rw.astype(BF16)
    rw_lo = (rw - rw_hi.astype(F32)).astype(BF16)
    rb = _lane_pad(jnp.concatenate([moe_rg_b[0], moe_re_b[0]]))
    rw2 = jnp.concatenate([rw_hi, rw_lo], axis=1)
    x1, t_slab, route = _out_proj(x, y_ssd, y_ml, w_out[0].astype(BF16), mod3, norm2_w[0], rw2, rb)

    return _moe(t_slab, x1, route, moe_w_gate[0], moe_w_up[0], moe_w_down[0], mod3, final_norm_w)
```

```python
import functools

import jax
import jax.numpy as jnp
from jax import lax
from jax.experimental import pallas as pl
from jax.experimental.pallas import tpu as pltpu

F32 = jnp.float32
BF16 = jnp.bfloat16
HIGHEST = lax.Precision.HIGHEST

D_MODEL = 1024
GRID_W = 64
EPS = 1e-6
CONV_W = 5
NEG_BIG = -1e30
CHUNK = 128
LANES = 128
HALO = 16
NB = 4

SSD_WIDTH = 512
SSD_HEADS = 8
SSD_HEADDIM = 64
SSD_GROUPS = 2
SSD_STATE = 128
SSD_XBC = SSD_WIDTH + 2 * SSD_GROUPS * SSD_STATE

ML_WIDTH = 512
ML_HEADS = 4
ML_HEADDIM = 128
ML_QK_BLOCK = 4

MOE_GROUPS = 4
MOE_EPG = 8
MOE_EXPERTS = 32
D_EXPERT = 256
ROUTE_LANE0 = MOE_GROUPS

TM_OUT = 512
TM_FIN = 512
SLAB = D_MODEL // LANES
BM = 128
BLOCK_PARTS = 1
PART = BM // BLOCK_PARTS
PART_PITCH = PART + 8
SCATTER_GROUP = 16
VMEM_LIMIT = 48 * 1024 * 1024
VMEM_LIMIT_MOE = 56 * 1024 * 1024


def _silu(v):
    return v * jax.nn.sigmoid(v)


def _softplus(v):
    return jnp.maximum(v, 0.0) + jnp.log1p(jnp.exp(-jnp.abs(v)))


def _dot(a, b):
    return jnp.dot(a, b, preferred_element_type=F32)


def _dot_nt(a, b):
    return lax.dot_general(a, b, (((1,), (1,)), ((), ())), preferred_element_type=F32)


def _dot_hi(a, b):
    return jnp.dot(a, b, preferred_element_type=F32, precision=HIGHEST)


def _mod_kernel(c_ref, w_ref, b_ref, o_ref):
    c = c_ref[...]
    o_ref[...] = _dot_hi(_silu(c), w_ref[...]) + b_ref[...]


def _modulation(c_all, w_mod, b_mod):
    n = w_mod.shape[1]
    bn = 1536
    return pl.pallas_call(
        _mod_kernel,
        grid=(n // bn,),
        in_specs=[
            pl.BlockSpec((8, D_MODEL), lambda j: (0, 0)),
            pl.BlockSpec((D_MODEL, bn), lambda j: (0, j)),
            pl.BlockSpec((1, bn), lambda j: (0, j)),
        ],
        out_specs=pl.BlockSpec((8, bn), lambda j: (0, j)),
        out_shape=jax.ShapeDtypeStruct((8, n), F32),
        compiler_params=pltpu.CompilerParams(vmem_limit_bytes=VMEM_LIMIT),
        name="modulation",
    )(c_all, w_mod, b_mod.reshape(1, n))


W_TILE = 16
COL_Z, COL_XBC, COL_ML, COL_G = 0, SSD_WIDTH, SSD_WIDTH + SSD_XBC, SSD_WIDTH + SSD_XBC + 3 * ML_WIDTH
PROJ_CHUNK = 512


def _norm_mod(xin, sh_ref, sc_ref, nw_ref):
    ms = jnp.mean(xin * xin, axis=-1, keepdims=True)
    y = xin * lax.rsqrt(ms + EPS) * nw_ref[...]
    return (y * (1.0 + sc_ref[...]) + sh_ref[...]).astype(BF16)


def _in_lat_kernel(rows, x_ref, sh_ref, sc_ref, nw_ref, w_ref, z_ref, xbc_ref, ml_ref, g_ref, gml_ref, scr):
    tm = rows * W_TILE
    h = _norm_mod(x_ref[...].reshape(tm, D_MODEL), sh_ref, sc_ref, nw_ref)

    def proj(col, width=PROJ_CHUNK):
        return _dot(h, w_ref[:, col:col + width])

    def to_col_major(val, slab0, dst_ref, lo, dtype):
        n_slab = val.shape[1] // LANES
        for k in range(n_slab):
            scr[slab0 + k] = val[:, k * LANES:(k + 1) * LANES]
        for j in range(W_TILE):
            for k in range(n_slab):
                dst_ref[j, :, lo + k * LANES:lo + (k + 1) * LANES] = (
                    scr[slab0 + k, pl.ds(j, rows, stride=W_TILE), :].astype(dtype))

    slabs = PROJ_CHUNK // LANES
    g = proj(COL_G, LANES)
    g_ref[...] = g.reshape(rows, W_TILE, LANES)
    to_col_major(g, 0, gml_ref, 0, F32)
    for j in range(3 * ML_WIDTH // PROJ_CHUNK):
        to_col_major(proj(COL_ML + j * PROJ_CHUNK), 1 + j * slabs, ml_ref, j * PROJ_CHUNK, BF16)
    z_ref[...] = proj(COL_Z).astype(BF16).reshape(rows, W_TILE, PROJ_CHUNK)
    for j in range(SSD_XBC // PROJ_CHUNK):
        lo = j * PROJ_CHUNK
        xbc_ref[:, :, lo:lo + PROJ_CHUNK] = proj(COL_XBC + lo).astype(BF16).reshape(rows, W_TILE, PROJ_CHUNK)


def _in_ctx_kernel(x_ref, sh_ref, sc_ref, nw_ref, w_ref, xbc_ref, ml_ref, g_ref):
    h = _norm_mod(x_ref[...], sh_ref, sc_ref, nw_ref)
    for j in range(SSD_XBC // PROJ_CHUNK):
        lo = j * PROJ_CHUNK
        xbc_ref[:, lo:lo + PROJ_CHUNK] = _dot(h, w_ref[:, COL_XBC + lo:COL_XBC + lo + PROJ_CHUNK]).astype(BF16)
    for j in range(3 * ML_WIDTH // PROJ_CHUNK):
        lo = j * PROJ_CHUNK
        ml_ref[:, lo:lo + PROJ_CHUNK] = _dot(h, w_ref[:, COL_ML + lo:COL_ML + lo + PROJ_CHUNK]).astype(BF16)
    g_ref[...] = _dot(h, w_ref[:, COL_G:COL_G + LANES])


def _in_proj(x, ctx, mod3, norm_w, w_cat):
    b, seq, d = x.shape
    ctx_len = ctx.shape[1]
    rows = seq // GRID_W
    widths = (SSD_WIDTH, SSD_XBC, 3 * ML_WIDTH, LANES, LANES)
    dtypes = (BF16, BF16, BF16, F32, F32)
    params = pltpu.CompilerParams(dimension_semantics=("arbitrary", "arbitrary"), vmem_limit_bytes=VMEM_LIMIT)
    nw = norm_w.reshape(1, d)

    def raster(width):
        return pl.BlockSpec((None, rows, W_TILE, width), lambda bi, wi: (bi, 0, wi, 0))

    def col_major(width):
        return pl.BlockSpec((None, W_TILE, rows, width), lambda bi, wi: (bi, wi, 0, 0))

    outs = pl.pallas_call(
        functools.partial(_in_lat_kernel, rows),
        grid=(b, GRID_W // W_TILE),
        in_specs=[
            raster(d),
            pl.BlockSpec((None, 1, d), lambda bi, wi: (bi * 6, 0, 0)),
            pl.BlockSpec((None, 1, d), lambda bi, wi: (bi * 6 + 1, 0, 0)),
            _full((1, d)), _full(w_cat.shape),
        ],
        out_specs=[raster(SSD_WIDTH), raster(SSD_XBC), col_major(3 * ML_WIDTH), raster(LANES), col_major(LANES)],
        out_shape=[jax.ShapeDtypeStruct((b, GRID_W, rows, w) if cm else (b, rows, GRID_W, w), t)
                   for w, t, cm in zip(widths, dtypes, (False, False, True, False, True))],
        scratch_shapes=[pltpu.VMEM((1 + 3 * ML_WIDTH // LANES, rows * W_TILE, LANES), F32)],
        compiler_params=params,
        name="in_proj",
    )(x.reshape(b, rows, GRID_W, d), mod3, mod3, nw, w_cat)
    z, xbc, ml, gates, gates_cm = [o.reshape(b, seq, w) for o, w in zip(outs, widths)]

    ctx_row = b
    ctx_widths = (SSD_XBC, 3 * ML_WIDTH, LANES)
    xbc_c, ml_c, gates_c = pl.pallas_call(
        _in_ctx_kernel,
        grid=(b, 1),
        in_specs=[
            pl.BlockSpec((None, ctx_len, d), lambda bi, t: (bi, 0, 0)),
            pl.BlockSpec((None, 1, d), lambda bi, t: (ctx_row * 6, 0, 0)),
            pl.BlockSpec((None, 1, d), lambda bi, t: (ctx_row * 6 + 1, 0, 0)),
            _full((1, d)), _full(w_cat.shape),
        ],
        out_specs=[pl.BlockSpec((None, ctx_len, w), lambda bi, t: (bi, 0, 0)) for w in ctx_widths],
        out_shape=[jax.ShapeDtypeStruct((b, ctx_len, w), t) for w, t in zip(ctx_widths, (BF16, BF16, F32))],
        compiler_params=params,
        name="in_proj_ctx",
    )(ctx, mod3, mod3, nw, w_cat)
    return z, (xbc, xbc_c), (ml, ml_c), (gates, gates_c), (gates_cm, gates_c)


def _chunk_eff(rev, ncc, nc, c):
    nl = nc - ncc
    if not rev:
        return jnp.where(c < ncc, nl + c, c - ncc)
    return jnp.where(c < ncc, nc - 1 - c, nl - 1 - (c - ncc))


def _lat_block(rev, ncc, nc, c):
    nl = nc - ncc
    if not rev:
        return jnp.maximum(c - ncc, 0)
    return jnp.where(c < ncc, nl - 1, nl - 1 - (c - ncc))


CONV_SIDE_TAPS = tuple(t for t in range(CONV_W) if t != CONV_W // 2)


def _shift_matrices():
    row = jnp.arange(CHUNK)[:, None]
    col = jnp.arange(CHUNK + 2 * HALO)[None, :]
    return jnp.stack([col == row + HALO + tap - CONV_W // 2 for tap in CONV_SIDE_TAPS]).astype(BF16)


def _conv_silu(xm, xp, xn, cw_ref, cb_ref, sh_ref, first, last):
    xp = jnp.where(first, jnp.zeros_like(xp), xp)
    xn = jnp.where(last, jnp.zeros_like(xn), xn)
    ext = jnp.concatenate([xp, xm, xn], axis=0)
    mid = CONV_W // 2
    acc = cb_ref[...] + cw_ref[mid:mid + 1, :] * xm.astype(F32)
    for i, tap in enumerate(CONV_SIDE_TAPS):
        acc = acc + cw_ref[tap:tap + 1, :] * _dot(sh_ref[i], ext)
    return _silu(acc)


def _tri(rev, transposed=False):
    row = lax.broadcasted_iota(jnp.int32, (CHUNK, CHUNK), 0)
    col = lax.broadcasted_iota(jnp.int32, (CHUNK, CHUNK), 1)
    if transposed:
        row, col = col, row
    keep = (col >= row) if rev else (col <= row)
    return keep


def _split3(a):
    a1 = a.astype(BF16)
    r1 = a - a1.astype(F32)
    a2 = r1.astype(BF16)
    a3 = (r1 - a2.astype(F32)).astype(BF16)
    return a1, a2, a3


def _dot_sel_l(m_b, a):
    p = _split3(a)
    return _dot(m_b, p[0]) + _dot(m_b, p[1]) + _dot(m_b, p[2])


def _dot_sel_r(a, m_b):
    p = _split3(a)
    return _dot(p[0], m_b) + _dot(p[1], m_b) + _dot(p[2], m_b)


def _row_spec(rows, width, row_block, col_block=0):
    return pl.BlockSpec((NB, rows, width), lambda bi, c: (bi, row_block(c), col_block))


def _local_chunks(rev, ncc, nc, c):
    nl = nc - ncc
    cc = jnp.clip(ncc - 1 - c if rev else c, 0, ncc - 1)
    cl = jnp.clip(nl - 1 - (c - ncc) if rev else c - ncc, 0, nl - 1)
    return c < ncc, cc, cl


def _pair_specs(rev, ncc, nc, rows, width, col=0, halo=0):
    per = CHUNK // HALO

    def index(which, count):
        def fn(c):
            ch = _local_chunks(rev, ncc, nc, c)[which]
            if halo == 0:
                return ch
            if halo < 0:
                return jnp.maximum(ch * per - 1, 0)
            return jnp.minimum((ch + 1) * per, count * per - 1)
        return fn

    return [_row_spec(rows, width, index(2, nc - ncc), col), _row_spec(rows, width, index(1, ncc), col)]


def _pick(is_ctx, lat_ref, ctx_ref, s):
    return jnp.where(is_ctx, ctx_ref[s], lat_ref[s])


def _step_ends(rev, ncc, nc, c):
    is_ctx, cc, cl = _local_chunks(rev, ncc, nc, c)
    first = jnp.where(is_ctx, cc == 0, cl == 0)
    last = jnp.where(is_ctx, cc == ncc - 1, cl == nc - ncc - 1)
    return is_ctx, first, last


def _full(shape):
    return pl.BlockSpec(shape, lambda bi, c: (0,) * len(shape))


def _ssd_gates(rev, xs, g, dtbr_ref, dtbc_ref, alr_ref, alc_ref, e_ref):
    lane0 = SSD_HEADS * int(rev)
    lane = lax.broadcasted_iota(jnp.int32, (CHUNK, LANES), 1)
    lmask = (lane >= lane0) & (lane < lane0 + SSD_HEADS)
    dt = jnp.where(lmask, _softplus(g + dtbr_ref[...]), 0.0)
    a = dt * (-jnp.exp(alr_ref[...]))
    gt = g.T
    dt_t = _softplus(gt + dtbc_ref[...])[lane0:lane0 + SSD_HEADS]
    a_t = dt_t * (-jnp.exp(alc_ref[...][lane0:lane0 + SSD_HEADS]))

    cs = _dot_sel_l(_tri(rev).astype(BF16), a)
    cs_t = _dot_sel_r(a_t, _tri(rev, transposed=True).astype(BF16))
    e = e_ref[...]
    dtx = _dot_sel_r(dt, e)
    csx = _dot_sel_r(cs, e)
    end = 0 if rev else CHUNK - 1
    totx = csx[end:end + 1, :]
    ecsx = jnp.exp(csx)
    decx = jnp.exp(totx - csx)
    etotx = jnp.exp(totx)

    xdt = xs * dtx
    return dict(cs=cs, cs_t=cs_t, ecsx=ecsx, etotx=etotx, xdt_b=xdt.astype(BF16), xd_b=(xdt * decx).astype(BF16))


def _ssd_chunks(rev, xbcs, gs_, dtbr_ref, dtbc_ref, alr_ref, alc_ref, e_ref, s_ref):
    lane0 = SSD_HEADS * int(rev)
    keep = _tri(rev)
    half = lax.broadcasted_iota(jnp.int32, (CHUNK, LANES), 1) // SSD_HEADDIM
    n_bc = SSD_GROUPS * SSD_STATE
    hpg = SSD_HEADS // SSD_GROUPS
    gw = hpg * SSD_HEADDIM

    gq = [_ssd_gates(rev, xbc[:, :SSD_WIDTH], g, dtbr_ref, dtbc_ref, alr_ref, alc_ref, e_ref)
          for xbc, g in zip(xbcs, gs_)]
    units = []
    for s, xbc in enumerate(xbcs):
        for grp in range(SSD_GROUPS):
            bm = xbc[:, SSD_WIDTH + grp * SSD_STATE:SSD_WIDTH + (grp + 1) * SSD_STATE]
            cm = xbc[:, SSD_WIDTH + n_bc + grp * SSD_STATE:SSD_WIDTH + n_bc + (grp + 1) * SSD_STATE]
            units.append(dict(s=s, grp=grp, cols=slice(grp * gw, (grp + 1) * gw), bm=bm, bm_b=bm.astype(BF16),
                              cm_b=cm.astype(BF16), s_old=s_ref[s, :, grp * gw:(grp + 1) * gw]))
    for u in units:
        u["cb"] = _dot_nt(u["cm_b"], u["bm_b"])
        u["y_off"] = _dot(u["cm_b"], u["s_old"].astype(BF16)) * gq[u["s"]]["ecsx"][:, u["cols"]]
        u["bt_b"] = u["bm"].T.astype(BF16)
    for u in units:
        q_ = gq[u["s"]]
        masks = []
        for hh in range(hpg):
            h = u["grp"] * hpg + hh
            dl = q_["cs"][:, lane0 + h:lane0 + h + 1] - q_["cs_t"][h:h + 1, :]
            masks.append((u["cb"] * jnp.exp(jnp.where(keep, dl, NEG_BIG))).astype(BF16))
        u["masks"] = masks
    for u in units:
        q_ = gq[u["s"]]
        blocks = []
        for pair in range(hpg // 2):
            blk = u["grp"] * (hpg // 2) + pair
            xj = q_["xdt_b"][:, blk * LANES:(blk + 1) * LANES]
            acc = u["y_off"][:, pair * LANES:(pair + 1) * LANES]
            for q in range(2):
                acc = acc + _dot(u["masks"][pair * 2 + q], jnp.where(half == q, xj, jnp.zeros_like(xj)))
            blocks.append(acc)
        u["y"] = blocks
    for u in units:
        q_ = gq[u["s"]]
        s_new = u["s_old"] * q_["etotx"][:, u["cols"]] + _dot(u["bt_b"], q_["xd_b"][:, u["cols"]])
        s_ref[u["s"], :, u["cols"]] = s_new
    return [jnp.concatenate([blk for u in units if u["s"] == s for blk in u["y"]], axis=1)
            for s in range(len(xbcs))]


def _ssd_rev_kernel(ncc, nc, xml_ref, xmc_ref, xpl_ref, xpc_ref, xnl_ref, xnc_ref, gl_ref, gc_ref, cw_ref, cb_ref,
                    sh_ref, dtbr_ref, dtbc_ref, alr_ref, alc_ref, e_ref, o_ref, xc_ref, s_ref):
    c = pl.program_id(1)
    is_ctx, first, last = _step_ends(True, ncc, nc, c)

    @pl.when(c == 0)
    def _():
        s_ref[...] = jnp.zeros_like(s_ref)

    xbcs = [_conv_silu(_pick(is_ctx, xml_ref, xmc_ref, s), _pick(is_ctx, xpl_ref, xpc_ref, s),
                       _pick(is_ctx, xnl_ref, xnc_ref, s), cw_ref, cb_ref, sh_ref, first, last) for s in range(NB)]
    for s in range(NB):
        xc_ref[s] = xbcs[s].astype(BF16)
    ys = _ssd_chunks(True, xbcs, [_pick(is_ctx, gl_ref, gc_ref, s) for s in range(NB)], dtbr_ref, dtbc_ref,
                     alr_ref, alc_ref, e_ref, s_ref)
    for s in range(NB):
        o_ref[s] = ys[s].astype(o_ref.dtype)


def _ssd_fwd_kernel(ncc, xc_ref, gl_ref, gc_ref, dtbr_ref, dtbc_ref, alr_ref, alc_ref, e_ref, dsk_ref, yb_ref,
                    z_ref, nw_ref, o_ref, s_ref):
    is_ctx = pl.program_id(1) < ncc

    @pl.when(pl.program_id(1) == 0)
    def _():
        s_ref[...] = jnp.zeros_like(s_ref)

    xbcs = [xc_ref[s].astype(F32) for s in range(NB)]
    ys = _ssd_chunks(False, xbcs, [_pick(is_ctx, gl_ref, gc_ref, s) for s in range(NB)], dtbr_ref, dtbc_ref,
                     alr_ref, alc_ref, e_ref, s_ref)
    ys = [(ys[s] + yb_ref[s].astype(F32) + dsk_ref[...] * xbcs[s][:, :SSD_WIDTH]) * _silu(z_ref[s].astype(F32))
          for s in range(NB)]
    scale = [lax.rsqrt(jnp.mean(y * y, axis=-1, keepdims=True) + EPS) for y in ys]
    for s in range(NB):
        o_ref[s] = (ys[s] * scale[s] * nw_ref[...]).astype(o_ref.dtype)


def _ssd_scans(xbc, gates, z, conv_params, gate_params, e_mats, dsk, nw):
    b, seq, _ = xbc[0].shape
    ncc = xbc[1].shape[1] // CHUNK
    nc = seq // CHUNK + ncc
    state = pltpu.VMEM((NB, SSD_STATE, SSD_WIDTH), F32)
    params = pltpu.CompilerParams(dimension_semantics=("arbitrary", "arbitrary"), vmem_limit_bytes=VMEM_LIMIT)
    lat_shape = jax.ShapeDtypeStruct((b, seq, SSD_WIDTH), BF16)

    ceff = functools.partial(_chunk_eff, True, ncc, nc)
    lat = functools.partial(_lat_block, True, ncc, nc)
    pair = functools.partial(_pair_specs, True, ncc, nc)
    rev_params = list(conv_params) + list(gate_params) + [e_mats[1]]
    yb, xbc_act = pl.pallas_call(
        functools.partial(_ssd_rev_kernel, ncc, nc),
        grid=(b // NB, nc),
        in_specs=pair(CHUNK, SSD_XBC) + pair(HALO, SSD_XBC, halo=-1) + pair(HALO, SSD_XBC, halo=1)
        + pair(CHUNK, LANES) + [_full(p.shape) for p in rev_params],
        out_specs=[_row_spec(CHUNK, SSD_WIDTH, lat), _row_spec(CHUNK, SSD_XBC, ceff)],
        out_shape=(lat_shape, jax.ShapeDtypeStruct((b, nc * CHUNK, SSD_XBC), BF16)),
        scratch_shapes=[state],
        compiler_params=params,
        name="ssd_rev",
    )(*xbc, *xbc, *xbc, *gates, *rev_params)

    ceff = functools.partial(_chunk_eff, False, ncc, nc)
    lat = functools.partial(_lat_block, False, ncc, nc)
    fwd_params = list(gate_params) + [e_mats[0], dsk]
    return pl.pallas_call(
        functools.partial(_ssd_fwd_kernel, ncc),
        grid=(b // NB, nc),
        in_specs=[_row_spec(CHUNK, SSD_XBC, ceff)] + _pair_specs(False, ncc, nc, CHUNK, LANES)
        + [_full(p.shape) for p in fwd_params]
        + [_row_spec(CHUNK, SSD_WIDTH, lat), _row_spec(CHUNK, SSD_WIDTH, lat), _full(nw.shape)],
        out_specs=_row_spec(CHUNK, SSD_WIDTH, lat),
        out_shape=lat_shape,
        scratch_shapes=[state],
        compiler_params=params,
        name="ssd_fwd",
    )(xbc_act, *gates, *fwd_params, yb, z, nw)


ML_I_LANE0 = 2 * SSD_HEADS
ML_F_LANE0 = ML_I_LANE0 + 2 * ML_HEADS


def _ml_gates(rev, g, gbr_ref, gbc_ref):
    i_lane0 = ML_I_LANE0 + ML_HEADS * int(rev)
    f_lane0 = ML_F_LANE0 + ML_HEADS * int(rev)
    ga = g + gbr_ref[...]
    lane = lax.broadcasted_iota(jnp.int32, (CHUNK, LANES), 1)
    logf = jnp.where((lane >= f_lane0) & (lane < f_lane0 + ML_HEADS), -_softplus(-ga), 0.0)
    cs = _dot_sel_l(_tri(rev).astype(BF16), logf)
    gt = g.T + gbc_ref[...]
    i_t = gt[ML_I_LANE0:ML_F_LANE0]
    logf_t = -_softplus(-gt[ML_F_LANE0:ML_F_LANE0 + 2 * ML_HEADS])
    cs_t = _dot_sel_r(logf_t, _tri(rev, transposed=True).astype(BF16))
    end = 0 if rev else CHUNK - 1

    u_t = i_t - cs_t
    u_c = ga - pltpu.roll(cs, LANES - (ML_F_LANE0 - ML_I_LANE0), axis=1)
    row = lax.broadcasted_iota(jnp.int32, (CHUNK, LANES), 0)
    pm = u_c
    step = 1
    while step < CHUNK:
        if rev:
            pm = jnp.maximum(pm, jnp.where(row < CHUNK - step, pltpu.roll(pm, CHUNK - step, axis=0), NEG_BIG))
        else:
            pm = jnp.maximum(pm, jnp.where(row >= step, pltpu.roll(pm, step, axis=0), NEG_BIG))
        step *= 2

    heads = []
    for h in range(ML_HEADS):
        r = ML_HEADS * int(rev) + h
        li = i_lane0 + h
        heads.append(dict(
            csc=cs[:, f_lane0 + h:f_lane0 + h + 1], u_col=u_c[:, li:li + 1], u_row=u_t[r:r + 1, :],
            tot=cs_t[r:r + 1, end:end + 1], u_max=pm[end:end + 1, li:li + 1], pm_col=pm[:, li:li + 1]))
    return heads


def _ml_chunks(rev, samples, gbr_ref, gbc_ref, c_ref, n_ref, mx_ref, consume):
    keep = _tri(rev)
    ones_b = jnp.ones((CHUNK, ML_HEADDIM), BF16)
    units = []
    for s, (k, q_b, k_b, v_b, g) in enumerate(samples):
        for h, gq in enumerate(_ml_gates(rev, g, gbr_ref, gbc_ref)):
            sl = slice(h * ML_HEADDIM, (h + 1) * ML_HEADDIM)
            units.append(dict(gq, s=s, h=h, kh=k[:, sl], vh=v_b[:, sl], qh_b=q_b[:, sl], kh_b=k_b[:, sl],
                              m_prev=mx_ref[s, h:h + 1, 0:1], c_prev=c_ref[s * ML_HEADS + h],
                              n_prev=n_ref[s, h:h + 1, :]))

    for u in units:
        u["qk"] = _dot_nt(u["qh_b"], u["kh_b"])
        u["vt_b"] = u["vh"].astype(F32).T.astype(BF16)
    for u in units:
        u["mm"] = jnp.maximum(u["m_prev"], u["pm_col"])
        u["scores_b"] = (u["qk"] * jnp.exp(jnp.where(keep, u["u_row"] - u["mm"], NEG_BIG))).astype(BF16)
    for u in units:
        c_aug = jnp.concatenate([u["c_prev"], jnp.broadcast_to(u["n_prev"], (CHUNK, ML_HEADDIM))], axis=0)
        u["intra"] = _dot(u["scores_b"], jnp.concatenate([u["vh"], ones_b], axis=1))
        u["inter"] = _dot_nt(u["qh_b"], c_aug.astype(BF16))
    outs = [[None] * ML_HEADS for _ in samples]
    for u in units:
        both = u["intra"] + jnp.exp(u["m_prev"] - u["mm"]) * u["inter"]
        den = both[:, ML_HEADDIM:ML_HEADDIM + 1]
        outs[u["s"]][u["h"]] = both[:, :ML_HEADDIM] / jnp.maximum(jnp.abs(den), jnp.exp(-(u["csc"] + u["mm"])))
    consume(outs)
    for u in units:
        kw = u["kh"] * jnp.exp(u["u_col"] - u["u_max"])
        u["c_loc"] = _dot(u["vt_b"], kw.astype(BF16))
        u["n_loc"] = jnp.sum(kw, axis=0, keepdims=True)
    for u in units:
        s, h, tot, m_prev = u["s"], u["h"], u["tot"], u["m_prev"]
        m_loc = tot + u["u_max"]
        m_new = jnp.maximum(tot + m_prev, m_loc)
        s_prev = jnp.exp(tot + m_prev - m_new)
        s_loc = jnp.exp(m_loc - m_new)
        c_ref[s * ML_HEADS + h] = s_prev * u["c_prev"] + s_loc * u["c_loc"]
        n_ref[s, h:h + 1, :] = s_prev * u["n_prev"] + s_loc * u["n_loc"]
        mx_ref[s, h:h + 1, :] = jnp.broadcast_to(m_new, (1, LANES))


def _ml_init_state(c_ref, n_ref, mx_ref):
    c_ref[...] = jnp.zeros_like(c_ref)
    n_ref[...] = jnp.zeros_like(n_ref)
    mx_ref[...] = jnp.full(mx_ref.shape, NEG_BIG, F32)


def _ml_rev_kernel(ncc, nc, xml_ref, xmc_ref, xpl_ref, xpc_ref, xnl_ref, xnc_ref, vl_ref, vc_ref, gl_ref, gc_ref,
                   cw_ref, cb_ref, sh_ref, wq_ref, wk_ref, gbr_ref, gbc_ref, o_ref, xc_ref, q_ref, k_ref, c_ref, n_ref,
                   mx_ref):
    c = pl.program_id(1)
    is_ctx, first, last = _step_ends(True, ncc, nc, c)

    @pl.when(c == 0)
    def _():
        _ml_init_state(c_ref, n_ref, mx_ref)

    samples = []
    for s in range(NB):
        xconv = _conv_silu(_pick(is_ctx, xml_ref, xmc_ref, s), _pick(is_ctx, xpl_ref, xpc_ref, s),
                           _pick(is_ctx, xnl_ref, xnc_ref, s), cw_ref, cb_ref, sh_ref, first, last)
        xc_b = xconv.astype(BF16)
        q = _dot(xc_b, wq_ref[...])
        k = _dot(xc_b, wk_ref[...])
        q_b, k_b = q.astype(BF16), k.astype(BF16)
        xc_ref[s] = xc_b
        q_ref[s] = q_b
        k_ref[s] = k_b
        samples.append((k, q_b, k_b, _pick(is_ctx, vl_ref, vc_ref, s), _pick(is_ctx, gl_ref, gc_ref, s)))
    def store(outs):
        for s in range(NB):
            o_ref[s] = jnp.concatenate(outs[s], axis=1).astype(o_ref.dtype)

    _ml_chunks(True, samples, gbr_ref, gbc_ref, c_ref, n_ref, mx_ref, store)


def _ml_fwd_kernel(ncc, xc_ref, q_ref, k_ref, vl_ref, vc_ref, og_ref, gl_ref, gc_ref, gbr_ref, gbc_ref, hb_ref,
                   nw_ref, sk_ref, o_ref, c_ref, n_ref, mx_ref):
    is_ctx = pl.program_id(1) < ncc

    @pl.when(pl.program_id(1) == 0)
    def _():
        _ml_init_state(c_ref, n_ref, mx_ref)

    samples = []
    for s in range(NB):
        q_b, k_b = q_ref[s], k_ref[s]
        samples.append((k_b.astype(F32), q_b, k_b, _pick(is_ctx, vl_ref, vc_ref, s),
                        _pick(is_ctx, gl_ref, gc_ref, s)))
    def finish(outs):
        gated = []
        for s in range(NB):
            for h in range(ML_HEADS):
                sl = slice(h * ML_HEADDIM, (h + 1) * ML_HEADDIM)
                gated.append(jax.nn.sigmoid(og_ref[s, :, sl].astype(F32))
                             * (outs[s][h] + hb_ref[s, :, sl].astype(F32)))
        scale = [lax.rsqrt(jnp.mean(hh * hh, axis=-1, keepdims=True) + EPS) for hh in gated]
        normed = [hh * sc for hh, sc in zip(gated, scale)]
        for s in range(NB):
            y = jnp.concatenate(normed[s * ML_HEADS:(s + 1) * ML_HEADS], axis=1) * nw_ref[...]
            o_ref[s] = (y + sk_ref[...] * xc_ref[s].astype(F32)).astype(o_ref.dtype)

    _ml_chunks(False, samples, gbr_ref, gbc_ref, c_ref, n_ref, mx_ref, finish)


def _ml_scans(ml, gates, conv_params, proj_params, gate_params, nw, sk):
    b, seq, _ = ml[0].shape
    ncc = ml[1].shape[1] // CHUNK
    nc = seq // CHUNK + ncc
    scratch = [pltpu.VMEM((NB * ML_HEADS, ML_HEADDIM, ML_HEADDIM), F32),
               pltpu.VMEM((NB, 8, ML_HEADDIM), F32),
               pltpu.VMEM((NB, 8, LANES), F32)]
    params = pltpu.CompilerParams(dimension_semantics=("arbitrary", "arbitrary"), vmem_limit_bytes=VMEM_LIMIT)
    lat_shape = jax.ShapeDtypeStruct((b, seq, ML_WIDTH), BF16)
    act_shape = jax.ShapeDtypeStruct((b, nc * CHUNK, ML_WIDTH), BF16)

    ceff = functools.partial(_chunk_eff, True, ncc, nc)
    lat = functools.partial(_lat_block, True, ncc, nc)
    pair = functools.partial(_pair_specs, True, ncc, nc)
    rev_params = list(conv_params) + list(proj_params) + list(gate_params)
    act_spec = _row_spec(CHUNK, ML_WIDTH, ceff)
    hb, xc, q, k = pl.pallas_call(
        functools.partial(_ml_rev_kernel, ncc, nc),
        grid=(b // NB, nc),
        in_specs=pair(CHUNK, ML_WIDTH) + pair(HALO, ML_WIDTH, halo=-1) + pair(HALO, ML_WIDTH, halo=1)
        + pair(CHUNK, ML_WIDTH, col=1) + pair(CHUNK, LANES) + [_full(p.shape) for p in rev_params],
        out_specs=[_row_spec(CHUNK, ML_WIDTH, lat), act_spec, act_spec, act_spec],
        out_shape=(lat_shape, act_shape, act_shape, act_shape),
        scratch_shapes=scratch,
        compiler_params=params,
        name="mlstm_rev",
    )(*ml, *ml, *ml, *ml, *gates, *rev_params)

    ceff = functools.partial(_chunk_eff, False, ncc, nc)
    lat = functools.partial(_lat_block, False, ncc, nc)
    pair = functools.partial(_pair_specs, False, ncc, nc)
    act_spec = _row_spec(CHUNK, ML_WIDTH, ceff)
    return pl.pallas_call(
        functools.partial(_ml_fwd_kernel, ncc),
        grid=(b // NB, nc),
        in_specs=[act_spec, act_spec, act_spec] + pair(CHUNK, ML_WIDTH, col=1)
        + [_row_spec(CHUNK, ML_WIDTH, lat, 2)] + pair(CHUNK, LANES)
        + [_full(p.shape) for p in gate_params]
        + [_row_spec(CHUNK, ML_WIDTH, lat), _full(nw.shape), _full(sk.shape)],
        out_specs=_row_spec(CHUNK, ML_WIDTH, lat),
        out_shape=lat_shape,
        scratch_shapes=scratch,
        compiler_params=params,
        name="mlstm_fwd",
    )(xc, q, k, *ml, ml[0], *gates, *gate_params, hb, nw, sk)


def _out_kernel(x_ref, ys_ref, ym_ref, wo_ref, g1_ref, sh_ref, sc_ref, nw_ref, rw_ref, rb_ref,
                x1_ref, ts_ref, route_ref):
    mix = _dot(ys_ref[...], wo_ref[0:SSD_WIDTH, :]) + _dot(ym_ref[...], wo_ref[SSD_WIDTH:, :])
    x1 = x_ref[...] + g1_ref[...] * mix
    y = x1 * lax.rsqrt(jnp.mean(x1 * x1, axis=-1, keepdims=True) + EPS) * nw_ref[...]
    t = y * (1.0 + sc_ref[...]) + sh_ref[...]
    x1_ref[...] = x1
    for j in range(SLAB):
        ts_ref[pl.ds(j, TM_OUT, stride=SLAB), :] = t[:, j * LANES:(j + 1) * LANES]
    lg2 = _dot(t.astype(BF16), rw_ref[...])
    lg = lg2[:, :LANES] + lg2[:, LANES:] + rb_ref[...]

    lane = lax.broadcasted_iota(jnp.int32, lg.shape, 1).astype(F32)
    gmask = lane < MOE_GROUPS
    gl = jnp.where(gmask, lg, NEG_BIG)
    gmax = jnp.max(gl, axis=1, keepdims=True)
    g_sel = jnp.min(jnp.where(gmask & (gl == gmax), lane, 1e9), axis=1, keepdims=True)
    p_group = 1.0 / jnp.sum(jnp.where(gmask, jnp.exp(gl - gmax), 0.0), axis=1, keepdims=True)
    lo = ROUTE_LANE0 + MOE_EPG * g_sel
    emask = (lane >= lo) & (lane < lo + MOE_EPG)
    l1 = jnp.max(jnp.where(emask, lg, NEG_BIG), axis=1, keepdims=True)
    i1 = jnp.min(jnp.where(emask & (lg == l1), lane, 1e9), axis=1, keepdims=True)
    emask2 = emask & (lane != i1)
    l2 = jnp.max(jnp.where(emask2, lg, NEG_BIG), axis=1, keepdims=True)
    i2 = jnp.min(jnp.where(emask2 & (lg == l2), lane, 1e9), axis=1, keepdims=True)
    r = jnp.exp(l2 - l1)
    w1 = p_group / (1.0 + r)
    w2 = p_group * r / (1.0 + r)
    route = (jnp.where(lane == 0, i1, 0.0) + jnp.where(lane == 1, i2, 0.0)
             + jnp.where(lane == 2, w1, 0.0) + jnp.where(lane == 3, w2, 0.0))
    route_ref[...] = route.T[0:8, :]


def _out_proj(x, y_ssd, y_ml, w_out_b, mod3, norm_w, rw2, rb):
    b, seq, d = x.shape
    nt = seq // TM_OUT

    def row(j):
        return pl.BlockSpec((None, 1, d), lambda bi, t: (bi * 6 + j, 0, 0))

    def tile(width):
        return pl.BlockSpec((None, TM_OUT, width), lambda bi, t: (bi, t, 0))

    return pl.pallas_call(
        _out_kernel,
        grid=(b, nt),
        in_specs=[tile(d), tile(SSD_WIDTH), tile(ML_WIDTH), _full(w_out_b.shape),
                  row(2), row(3), row(4), _full((1, d)), _full(rw2.shape), _full(rb.shape)],
        out_specs=[tile(d),
                   pl.BlockSpec((None, TM_OUT * SLAB, LANES), lambda bi, t: (bi, t, 0)),
                   pl.BlockSpec((None, 8, TM_OUT), lambda bi, t: (bi, 0, t))],
        out_shape=(jax.ShapeDtypeStruct((b, seq, d), F32),
                   jax.ShapeDtypeStruct((b, seq * SLAB, LANES), F32),
                   jax.ShapeDtypeStruct((b, 8, seq), F32)),
        compiler_params=pltpu.CompilerParams(
            dimension_semantics=("arbitrary", "arbitrary"), vmem_limit_bytes=VMEM_LIMIT),
        name="out_proj_router",
    )(x, y_ssd, y_ml, w_out_b, mod3, mod3, mod3, norm_w.reshape(1, d), rw2, rb)


def _route_tables(route, seq):
    b = route.shape[0]
    n_inst = 2 * seq
    nblk = n_inst // BM + MOE_EXPERTS
    e_flat = (route[:, 0:2, :].astype(jnp.int32) - ROUTE_LANE0).reshape(b, n_inst)
    w_flat = route[:, 2:4, :].reshape(b, n_inst)
    tok_flat = jnp.tile(jnp.arange(seq, dtype=jnp.int32), 2 * b).reshape(b, n_inst)
    _, tok_sorted, w_sorted = lax.sort((e_flat, tok_flat, w_flat), dimension=1, num_keys=1)
    experts = jnp.arange(MOE_EXPERTS, dtype=jnp.int32)
    counts = jnp.sum((e_flat[:, None, :] == experts[None, :, None]).astype(jnp.int32), axis=2)
    nblk_e = (counts + BM - 1) // BM
    blk_end = jnp.cumsum(nblk_e, axis=1)
    blk_start = blk_end - nblk_e
    cnt_start = jnp.cumsum(counts, axis=1) - counts
    nb = blk_end[:, -1:]
    j = jnp.arange(nblk, dtype=jnp.int32)[None, :]
    valid_blk = j < nb
    jj = jnp.minimum(j, nb - 1)
    e_j = jnp.sum((jj[:, :, None] >= blk_end[:, None, :]).astype(jnp.int32), axis=2)
    onehot = (e_j[:, :, None] == experts[None, None, :]).astype(jnp.int32)
    take = lambda tbl: jnp.sum(onehot * tbl[:, None, :], axis=2)
    r = jnp.arange(BM, dtype=jnp.int32)[None, None, :]
    rank = ((jj - take(blk_start)) * BM)[:, :, None] + r
    valid = valid_blk[:, :, None] & (rank < take(counts)[:, :, None])
    sidx = jnp.clip(take(cnt_start)[:, :, None] + rank, 0, n_inst - 1).reshape(b, nblk * BM)
    tok = jnp.take_along_axis(tok_sorted, sidx, axis=1).reshape(b, nblk, BM)
    wslot = jnp.take_along_axis(w_sorted, sidx, axis=1).reshape(b, nblk, BM)
    tok = jnp.where(valid, tok, seq + r) * SLAB
    wslot = jnp.where(valid, wslot, 0.0)
    return (nblk_e.reshape(-1), blk_start.reshape(-1), tok.reshape(b, 1, nblk * BM), wslot)


SEM_T, SEM_X, SEM_O = 0, 1, 3


def _moe_kernel(seq, nbe_ref, bs_ref, tok_ref, ws_ref, wg_ref, wu_ref, wd_ref, g2_ref, fw_ref,
                t_hbm, x1_hbm, o_hbm, t_scr, y_scr, wgb, wub, wdb, xt, ot, xin, stage, sems):
    b = pl.program_id(0)
    e = pl.program_id(1)
    nb = pl.num_programs(0)
    rows = seq * SLAB
    n_fin = seq // TM_FIN

    def t_copy(sample):
        return pltpu.make_async_copy(t_hbm.at[sample], t_scr.at[pl.ds(0, rows)], sems.at[SEM_T])

    def x1_copy(s, slot):
        return pltpu.make_async_copy(x1_hbm.at[b, pl.ds(s * TM_FIN, TM_FIN)], xin.at[slot], sems.at[SEM_X + slot])

    def out_copy(s, slot):
        return pltpu.make_async_copy(stage.at[slot], o_hbm.at[b, pl.ds(s * TM_FIN, TM_FIN)], sems.at[SEM_O + slot])

    @pl.when((e == 0) & (b == 0))
    def _():
        t_copy(b).start()
        y_scr[...] = jnp.zeros_like(y_scr)
        t_scr[pl.ds(rows, BM * SLAB), :] = jnp.zeros((BM * SLAB, LANES), F32)

    @pl.when(e == 0)
    def _():
        t_copy(b).wait()

    n_blocks = nbe_ref[b * MOE_EXPERTS + e]
    blk0 = bs_ref[b * MOE_EXPERTS + e]

    @pl.when(n_blocks > 0)
    def _():
        wgb[...] = wg_ref[...].astype(BF16)
        wub[...] = wu_ref[...].astype(BF16)
        wdb[...] = wd_ref[...].astype(BF16)
        g2 = g2_ref[...]
        diag = (lax.broadcasted_iota(jnp.int32, (BM, BM), 0) == lax.broadcasted_iota(jnp.int32, (BM, BM), 1))

        def block(i, carry):
            blk = blk0 + i
            base = blk * BM

            tok_blk = tok_ref.at[0, pl.ds(base, BM)]

            def slab_rows(r):
                return pl.ds(pl.multiple_of(tok_blk[r], SLAB), SLAB)

            w_col = jnp.sum(jnp.where(diag, ws_ref[pl.ds(blk, 1), :], 0.0), axis=1, keepdims=True)
            parts = range(BLOCK_PARTS)
            for p in parts:
                for r in range(PART):
                    xt[p, pl.ds(r, SLAB, stride=PART_PITCH), :] = t_scr[slab_rows(p * PART + r), :]
            for p in parts:
                x = jnp.concatenate([xt[p, c * PART_PITCH:c * PART_PITCH + PART, :] for c in range(SLAB)],
                                    axis=1).astype(BF16)
                hidden = _silu(_dot(x, wgb[...])) * _dot(x, wub[...]) * w_col[p * PART:(p + 1) * PART]
                out = _dot(hidden.astype(BF16), wdb[...]) * g2
                for c in range(SLAB):
                    ot[p, c * PART_PITCH:c * PART_PITCH + PART, :] = out[:, c * LANES:(c + 1) * LANES]
            for p in parts:
                for r0 in range(0, PART, SCATTER_GROUP):
                    sl = [slab_rows(p * PART + r0 + u) for u in range(SCATTER_GROUP)]
                    vals = [y_scr[sl[u], :] + ot[p, pl.ds(r0 + u, SLAB, stride=PART_PITCH), :]
                            for u in range(SCATTER_GROUP)]
                    for u in range(SCATTER_GROUP):
                        y_scr[sl[u], :] = vals[u]
            return carry

        lax.fori_loop(0, n_blocks, block, 0)

    @pl.when(e == MOE_EXPERTS - 1)
    def _():
        @pl.when(b + 1 < nb)
        def _():
            t_copy(b + 1).start()

        x1_copy(0, 0).start()

        def tile(s, carry):
            slot = s % 2
            x1_copy(s, slot).wait()

            @pl.when(s + 1 < n_fin)
            def _():
                x1_copy(s + 1, 1 - slot).start()

            @pl.when(s >= 2)
            def _():
                out_copy(s - 2, slot).wait()

            base = pl.multiple_of(s * (TM_FIN * SLAB), TM_FIN * SLAB)
            x1 = xin[slot]
            chunks = [x1[:, c * LANES:(c + 1) * LANES] + y_scr[pl.ds(base + c, TM_FIN, stride=SLAB), :]
                      for c in range(SLAB)]
            y_scr[pl.ds(base, TM_FIN * SLAB), :] = jnp.zeros((TM_FIN * SLAB, LANES), F32)
            ssq = chunks[0] * chunks[0]
            for c in range(1, SLAB):
                ssq = ssq + chunks[c] * chunks[c]
            inv = lax.rsqrt(jnp.sum(ssq, axis=1, keepdims=True) * (1.0 / D_MODEL) + EPS)
            for c in range(SLAB):
                stage[slot, :, c * LANES:(c + 1) * LANES] = chunks[c] * inv * fw_ref[:, c * LANES:(c + 1) * LANES]
            out_copy(s, slot).start()
            return carry

        lax.fori_loop(0, n_fin, tile, 0)
        out_copy(n_fin - 2, n_fin % 2).wait()
        out_copy(n_fin - 1, (n_fin - 1) % 2).wait()


def _moe(t_slab, x1, route, wg, wu, wd, mod3, final_w):
    b, seq, d = x1.shape
    nblk_e, blk_start, tok, wslot = _route_tables(route, seq)
    nblk = wslot.shape[1]

    def w_spec(shape):
        return pl.BlockSpec((None,) + shape, lambda bi, e, nbe, bs: (e, 0, 0))

    grid_spec = pltpu.PrefetchScalarGridSpec(
        num_scalar_prefetch=2,
        grid=(b, MOE_EXPERTS),
        in_specs=[
            pl.BlockSpec((None, 1, nblk * BM), lambda bi, e, nbe, bs: (bi, 0, 0), memory_space=pltpu.SMEM),
            pl.BlockSpec((None, nblk, BM), lambda bi, e, nbe, bs: (bi, 0, 0)),
            w_spec((d, D_EXPERT)), w_spec((d, D_EXPERT)), w_spec((D_EXPERT, d)),
            pl.BlockSpec((None, 1, d), lambda bi, e, nbe, bs: (bi * 6 + 5, 0, 0)),
            pl.BlockSpec((1, d), lambda bi, e, nbe, bs: (0, 0)),
            pl.BlockSpec(memory_space=pl.ANY),
            pl.BlockSpec(memory_space=pl.ANY),
        ],
        out_specs=pl.BlockSpec(memory_space=pl.ANY),
        scratch_shapes=[
            pltpu.VMEM(((seq + BM) * SLAB, LANES), F32),
            pltpu.VMEM(((seq + BM) * SLAB, LANES), F32),
            pltpu.VMEM((d, D_EXPERT), BF16),
            pltpu.VMEM((d, D_EXPERT), BF16),
            pltpu.VMEM((D_EXPERT, d), BF16),
            pltpu.VMEM((BLOCK_PARTS, SLAB * PART_PITCH, LANES), F32),
            pltpu.VMEM((BLOCK_PARTS, SLAB * PART_PITCH, LANES), F32),
            pltpu.VMEM((2, TM_FIN, d), F32),
            pltpu.VMEM((2, TM_FIN, d), F32),
            pltpu.SemaphoreType.DMA((5,)),
        ],
    )
    return pl.pallas_call(
        functools.partial(_moe_kernel, seq),
        grid_spec=grid_spec,
        out_shape=jax.ShapeDtypeStruct((b, seq, d), F32),
        compiler_params=pltpu.CompilerParams(
            dimension_semantics=("arbitrary", "arbitrary"), vmem_limit_bytes=VMEM_LIMIT_MOE),
        name="experts_final_norm",
    )(nblk_e, blk_start, tok, wslot, wg, wu, wd, mod3, final_w.reshape(1, d), t_slab, x1)


def _lane_pad(v, offset=0):
    v = v.reshape(-1).astype(F32)
    return jnp.zeros((1, LANES), F32).at[0, offset:offset + v.shape[0]].set(v)


def _from_col_major(t, rows):
    b, length, ch = t.shape
    return t.reshape(b, GRID_W, rows, ch).transpose(0, 2, 1, 3).reshape(b, length, ch)


def kernel(x, c, ctx, c_ctx, w_mod, b_mod, norm1_w, w_in, ssd_conv_w, ssd_conv_b, ssd_dt_bias, ssd_a_log, ssd_d, ssd_norm_w, ml_conv_w, ml_conv_b, ml_w_qk, ml_gate_b, ml_norm_w, ml_skip, w_out, norm2_w, moe_rg_w, moe_rg_b, moe_re_w, moe_re_b, moe_w_gate, moe_w_up, moe_w_down, final_norm_w):
    b, seq, d = x.shape
    ctx_len = ctx.shape[1]
    rows = seq // GRID_W
    assert w_mod.shape[0] == 1 and d == D_MODEL and b + 1 <= 8 and b % NB == 0
    assert seq % TM_OUT == 0 and ctx_len % CHUNK == 0 and seq % TM_FIN == 0 and (2 * seq) % BM == 0
    assert seq == rows * GRID_W and GRID_W % W_TILE == 0 and rows % 8 == 0 and seq // TM_FIN >= 2

    c_all = jnp.zeros((8, d), F32).at[:b].set(c).at[b].set(c_ctx)
    mod = _modulation(c_all, w_mod[0], b_mod[0])
    mod3 = mod.reshape(8 * 6, 1, d)

    w = w_in[0]
    ssd_in = SSD_WIDTH + SSD_XBC + 2 * SSD_HEADS
    ml_main = 3 * ML_WIDTH
    n_gate = 2 * SSD_HEADS + 4 * ML_HEADS
    w_cat = jnp.concatenate([
        w[:, :SSD_WIDTH + SSD_XBC], w[:, ssd_in:ssd_in + ml_main],
        w[:, SSD_WIDTH + SSD_XBC:ssd_in], w[:, ssd_in + ml_main:],
        jnp.zeros((d, LANES - n_gate), F32)], axis=1).astype(BF16)
    z, xbc, ml, gates, gates_cm = _in_proj(x, ctx, mod3, norm1_w[0], w_cat)

    e_mats = []
    for direction in range(2):
        lane = jnp.arange(LANES)[:, None]
        head = (jnp.arange(SSD_WIDTH) // SSD_HEADDIM)[None, :]
        e_mats.append((lane == direction * SSD_HEADS + head).astype(BF16))
    cw = jnp.zeros((8, SSD_XBC), F32).at[:CONV_W].set(ssd_conv_w[0])
    dtb = _lane_pad(ssd_dt_bias[0])
    alog = _lane_pad(ssd_a_log[0])
    dsk = jnp.repeat(ssd_d[0], SSD_HEADDIM).reshape(1, SSD_WIDTH)
    shifts = _shift_matrices()
    y_ssd = _ssd_scans(xbc, gates, z, [cw, ssd_conv_b[0].reshape(1, -1), shifts],
                       [dtb, dtb.reshape(LANES, 1), alog, alog.reshape(LANES, 1)], e_mats, dsk,
                       ssd_norm_w[0].reshape(1, -1))

    w_rows = jnp.tile(ml_w_qk[0].reshape(2, ML_WIDTH, ML_QK_BLOCK), (1, 1, ML_WIDTH // ML_QK_BLOCK))
    blk_id = jnp.arange(ML_WIDTH) // ML_QK_BLOCK
    w_bd = jnp.where((blk_id[:, None] == blk_id[None, :])[None], w_rows, 0.0)
    wq = w_bd[0].astype(BF16)
    wk = (w_bd[1] * (ML_HEADDIM ** -0.5)).astype(BF16)
    mcw = jnp.zeros((8, ML_WIDTH), F32).at[:CONV_W].set(ml_conv_w[0])
    gb = _lane_pad(ml_gate_b[0], offset=2 * SSD_HEADS)
    y_ml_cm = _ml_scans(ml, gates_cm, [mcw, ml_conv_b[0].reshape(1, -1), shifts], [wq, wk], [gb, gb.reshape(LANES, 1)],
                        ml_norm_w[0].reshape(1, -1), ml_skip[0].reshape(1, -1))
    y_ml = _from_col_major(y_ml_cm, rows)

    rw = jnp.concatenate([moe_rg_w[0], moe_re_w[0],
                          jnp.zeros((d, LANES - MOE_GROUPS - MOE_EXPERTS), F32)], axis=1)
    rw_hi = rw.astype(BF16)
    rw_lo = (rw - rw_hi.astype(F32)).astype(BF16)
    rb = _lane_pad(jnp.concatenate([moe_rg_b[0], moe_re_b[0]]))
    rw2 = jnp.concatenate([rw_hi, rw_lo], axis=1)
    x1, t_slab, route = _out_proj(x, y_ssd, y_ml, w_out[0].astype(BF16), mod3, norm2_w[0], rw2, rb)

    return _moe(t_slab, x1, route, moe_w_gate[0], moe_w_up[0], moe_w_down[0], mod3, final_norm_w)
```
